```python
import jax, jax.numpy as jnp
from jax import lax
import numpy as np

D_MODEL = 1024
BATCH = 8
SEQ = 8192
DEPTH = 4

SB_HEADS = 8
SB_HEAD_DIM = 64
SB_WIDTH = SB_HEADS * SB_HEAD_DIM
SB_BLOCK = 128
POOL_WINDOWS = (2, 4, 8, 16)
POOL_GROUPS = 4
POOL_GROUP_DIM = 64
POOL_WIDTH = POOL_GROUPS * POOL_GROUP_DIM
GM_GROUPS = 4
GM_GROUP_DIM = 64
GM_WIDTH = GM_GROUPS * GM_GROUP_DIM
GM_CHUNK = 128
N_BRANCH = 3
D_FF = 4 * D_MODEL
RMS_EPS = 1e-6
IN_SIZES = (SB_WIDTH, SB_WIDTH, SB_WIDTH, POOL_WIDTH, GM_WIDTH, GM_WIDTH, N_BRANCH * D_MODEL)
D_IN = sum(IN_SIZES)
IN_SPLITS = tuple(int(s) for s in np.cumsum(IN_SIZES)[:-1])

kernel_name = "hybrid_stickbreak_pool_gmlp_block"


def rms_norm(x, gain):
    xf = x.astype(jnp.float32)
    y = xf * lax.rsqrt(jnp.mean(xf * xf, axis=-1, keepdims=True) + RMS_EPS)
    return (y * gain.astype(jnp.float32)).astype(x.dtype)


def stick_breaking_attention(q, k, v):
    B, S, H, Dh = q.shape
    scale = Dh ** -0.5
    outs = []
    for blk in range(S // SB_BLOCK):
        q0 = blk * SB_BLOCK
        q1 = q0 + SB_BLOCK
        qb = q[:, q0:q1]
        kb = k[:, :q1]
        vb = v[:, :q1]
        z = jnp.einsum('bthd,bshd->bhts', qb, kb).astype(jnp.float32) * scale
        t_idx = q0 + jnp.arange(SB_BLOCK)[:, None]
        s_idx = jnp.arange(q1)[None, :]
        strict = s_idx < t_idx
        log_not = jnp.where(strict, jax.nn.log_sigmoid(-z), 0.0)
        suffix = lax.cumsum(log_not, axis=3, reverse=True) - log_not
        a = jnp.where(strict, jnp.exp(jax.nn.log_sigmoid(z) + suffix), 0.0).astype(v.dtype)
        outs.append(jnp.einsum('bhts,bshd->bthd', a, vb))
    return jnp.concatenate(outs, axis=1)


def multiscale_pool(p, w_pool, pool_scale):
    B, S, _ = p.shape
    pg = p.reshape(B, S, POOL_GROUPS, POOL_GROUP_DIM)
    csum = jnp.cumsum(pg.astype(jnp.float32), axis=1)
    pos = jnp.arange(S, dtype=jnp.float32)
    pooled = []
    for g, w in enumerate(POOL_WINDOWS):
        cg = csum[:, :, g]
        shifted = jnp.pad(cg, ((0, 0), (w, 0), (0, 0)))[:, :S]
        count = jnp.minimum(pos + 1.0, float(w))[None, :, None]
        pooled.append((cg - shifted) / count - pg[:, :, g].astype(jnp.float32))
    pooled = jnp.stack(pooled, axis=2).astype(p.dtype)
    mixed = jnp.einsum('bsgc,gcd->bsgd', pooled, w_pool)
    return mixed.reshape(B, S, POOL_WIDTH) * pool_scale


def chunked_spatial_gating(u, v, gm_gain, w_spatial, b_spatial):
    B, S, _ = u.shape
    u = jax.nn.gelu(u)
    v = rms_norm(jax.nn.gelu(v), gm_gain)
    n_chunks = S // GM_CHUNK
    vc = v.reshape(B, n_chunks, GM_CHUNK, GM_GROUPS, GM_GROUP_DIM)
    causal = jnp.tril(jnp.ones((GM_CHUNK, GM_CHUNK), dtype=bool))
    ws = jnp.where(causal[None], w_spatial, 0.0).astype(v.dtype)
    mixed = jnp.einsum('gtp,bnpgc->bntgc', ws, vc) + b_spatial.T[:, :, None]
    return u * mixed.reshape(B, S, GM_WIDTH)


def _fwd_setup_inputs(seed: int = 0) -> dict:
    key = jax.random.key(seed)
    ks = jax.random.split(key, 18)

    def nrm(k, shape, scale):
        return jax.random.normal(k, shape, jnp.float32) * scale

    def gain(k, shape):
        return 1.0 + 0.05 * jax.random.normal(k, shape, jnp.float32)

    return {
        "x": nrm(ks[0], (BATCH, SEQ, D_MODEL), 1.0),
        "w_in": nrm(ks[1], (DEPTH, D_MODEL, D_IN), D_MODEL ** -0.5),
        "w_pool": nrm(ks[2], (DEPTH, POOL_GROUPS, POOL_GROUP_DIM, POOL_GROUP_DIM), POOL_GROUP_DIM ** -0.5),
        "pool_scale": gain(ks[3], (DEPTH, POOL_WIDTH)),
        "gm_gain": gain(ks[4], (DEPTH, GM_WIDTH)),
        "w_spatial": nrm(ks[5], (DEPTH, GM_GROUPS, GM_CHUNK, GM_CHUNK), GM_CHUNK ** -0.5),
        "b_spatial": gain(ks[6], (DEPTH, GM_GROUPS, GM_CHUNK)),
        "w_br_sb": nrm(ks[7], (DEPTH, SB_WIDTH, D_MODEL), SB_WIDTH ** -0.5),
        "w_br_pool": nrm(ks[8], (DEPTH, POOL_WIDTH, D_MODEL), POOL_WIDTH ** -0.5),
        "w_br_gm": nrm(ks[9], (DEPTH, GM_WIDTH, D_MODEL), GM_WIDTH ** -0.5),
        "w_out": nrm(ks[10], (DEPTH, D_MODEL, D_MODEL), D_MODEL ** -0.5),
        "g_mix_pre": gain(ks[11], (DEPTH, D_MODEL)),
        "g_mix_post": gain(ks[12], (DEPTH, D_MODEL)),
        "g_ff_pre": gain(ks[13], (DEPTH, D_MODEL)),
        "g_ff_post": gain(ks[14], (DEPTH, D_MODEL)),
        "w_ff_in": nrm(ks[15], (DEPTH, D_MODEL, D_FF), D_MODEL ** -0.5),
        "w_ff_out": nrm(ks[16], (DEPTH, D_FF, D_MODEL), D_FF ** -0.5),
    }


def _fwd_reference(x, w_in, w_pool, pool_scale, gm_gain, w_spatial, b_spatial, w_br_sb, w_br_pool,
              w_br_gm, w_out, g_mix_pre, g_mix_post, g_ff_pre, g_ff_post, w_ff_in, w_ff_out):
    B, S, D = x.shape
    for l in range(DEPTH):
        h = rms_norm(x, g_mix_pre[l])
        proj = h @ w_in[l]
        q, k, v, p_in, gm_u, gm_v, gate_in = jnp.split(proj, IN_SPLITS, axis=-1)
        o_sb = stick_breaking_attention(q.reshape(B, S, SB_HEADS, SB_HEAD_DIM),
                                        k.reshape(B, S, SB_HEADS, SB_HEAD_DIM),
                                        v.reshape(B, S, SB_HEADS, SB_HEAD_DIM)).reshape(B, S, SB_WIDTH)
        o_pool = multiscale_pool(p_in, w_pool[l], pool_scale[l])
        o_gm = chunked_spatial_gating(gm_u, gm_v, gm_gain[l], w_spatial[l], b_spatial[l])
        gates = jax.nn.sigmoid(gate_in.reshape(B, S, N_BRANCH, D))
        merged = (gates[:, :, 0] * (o_sb @ w_br_sb[l])
                  + gates[:, :, 1] * (o_pool @ w_br_pool[l])
                  + gates[:, :, 2] * (o_gm @ w_br_gm[l]))
        x = x + rms_norm(merged @ w_out[l], g_mix_post[l])
        h = rms_norm(x, g_ff_pre[l])
        ff = jnp.square(jax.nn.relu(h @ w_ff_in[l])) @ w_ff_out[l]
        x = x + rms_norm(ff, g_ff_post[l])
    return x


import jax as _jax
import jax.numpy as _jnp

TWIN_FORMAT = 'train_step'
FWD_PARAMS = ['x', 'w_in', 'w_pool', 'pool_scale', 'gm_gain', 'w_spatial', 'b_spatial', 'w_br_sb', 'w_br_pool', 'w_br_gm', 'w_out', 'g_mix_pre', 'g_mix_post', 'g_ff_pre', 'g_ff_post', 'w_ff_in', 'w_ff_out']
TWIN_WEIGHTS = ['w_in', 'w_pool', 'pool_scale', 'gm_gain', 'w_spatial', 'b_spatial', 'w_br_sb', 'w_br_pool', 'w_br_gm', 'w_out', 'g_mix_pre', 'g_mix_post', 'g_ff_pre', 'g_ff_post', 'w_ff_in', 'w_ff_out']
TWIN_DIFF_INPUT = 'x'
TWIN_INPUTS = ['x', 'w_in', 'w_pool', 'pool_scale', 'gm_gain', 'w_spatial', 'b_spatial', 'w_br_sb', 'w_br_pool', 'w_br_gm', 'w_out', 'g_mix_pre', 'g_mix_post', 'g_ff_pre', 'g_ff_post', 'w_ff_in', 'w_ff_out', 'loss_target', 'm_w_in', 'm_w_pool', 'm_pool_scale', 'm_gm_gain', 'm_w_spatial', 'm_b_spatial', 'm_w_br_sb', 'm_w_br_pool', 'm_w_br_gm', 'm_w_out', 'm_g_mix_pre', 'm_g_mix_post', 'm_g_ff_pre', 'm_g_ff_post', 'm_w_ff_in', 'm_w_ff_out', 'v_w_in', 'v_w_pool', 'v_pool_scale', 'v_gm_gain', 'v_w_spatial', 'v_b_spatial', 'v_w_br_sb', 'v_w_br_pool', 'v_w_br_gm', 'v_w_out', 'v_g_mix_pre', 'v_g_mix_post', 'v_g_ff_pre', 'v_g_ff_post', 'v_w_ff_in', 'v_w_ff_out']
TWIN_OUTPUTS = ['loss', 'grad_x', 'grad_w_in', 'grad_w_pool', 'grad_pool_scale', 'grad_gm_gain', 'grad_w_spatial', 'grad_b_spatial', 'grad_w_br_sb', 'grad_w_br_pool', 'grad_w_br_gm', 'grad_w_out', 'grad_g_mix_pre', 'grad_g_mix_post', 'grad_g_ff_pre', 'grad_g_ff_post', 'grad_w_ff_in', 'grad_w_ff_out', 'delta_w_in', 'delta_w_pool', 'delta_pool_scale', 'delta_gm_gain', 'delta_w_spatial', 'delta_b_spatial', 'delta_w_br_sb', 'delta_w_br_pool', 'delta_w_br_gm', 'delta_w_out', 'delta_g_mix_pre', 'delta_g_mix_post', 'delta_g_ff_pre', 'delta_g_ff_post', 'delta_w_ff_in', 'delta_w_ff_out', 'new_m_w_in', 'new_m_w_pool', 'new_m_pool_scale', 'new_m_gm_gain', 'new_m_w_spatial', 'new_m_b_spatial', 'new_m_w_br_sb', 'new_m_w_br_pool', 'new_m_w_br_gm', 'new_m_w_out', 'new_m_g_mix_pre', 'new_m_g_mix_post', 'new_m_g_ff_pre', 'new_m_g_ff_post', 'new_m_w_ff_in', 'new_m_w_ff_out', 'new_v_w_in', 'new_v_w_pool', 'new_v_pool_scale', 'new_v_gm_gain', 'new_v_w_spatial', 'new_v_b_spatial', 'new_v_w_br_sb', 'new_v_w_br_pool', 'new_v_w_br_gm', 'new_v_w_out', 'new_v_g_mix_pre', 'new_v_g_mix_post', 'new_v_g_ff_pre', 'new_v_g_ff_post', 'new_v_w_ff_in', 'new_v_w_ff_out']
TWIN_LEAF_KINDS = {'loss': 'loss', 'grad_x': 'grad_x', 'grad_w_in': 'grad_w', 'grad_w_pool': 'grad_w', 'grad_pool_scale': 'grad_w', 'grad_gm_gain': 'grad_w', 'grad_w_spatial': 'grad_w', 'grad_b_spatial': 'grad_w', 'grad_w_br_sb': 'grad_w', 'grad_w_br_pool': 'grad_w', 'grad_w_br_gm': 'grad_w', 'grad_w_out': 'grad_w', 'grad_g_mix_pre': 'grad_w', 'grad_g_mix_post': 'grad_w', 'grad_g_ff_pre': 'grad_w', 'grad_g_ff_post': 'grad_w', 'grad_w_ff_in': 'grad_w', 'grad_w_ff_out': 'grad_w', 'delta_w_in': 'delta_w', 'delta_w_pool': 'delta_w', 'delta_pool_scale': 'delta_w', 'delta_gm_gain': 'delta_w', 'delta_w_spatial': 'delta_w', 'delta_b_spatial': 'delta_w', 'delta_w_br_sb': 'delta_w', 'delta_w_br_pool': 'delta_w', 'delta_w_br_gm': 'delta_w', 'delta_w_out': 'delta_w', 'delta_g_mix_pre': 'delta_w', 'delta_g_mix_post': 'delta_w', 'delta_g_ff_pre': 'delta_w', 'delta_g_ff_post': 'delta_w', 'delta_w_ff_in': 'delta_w', 'delta_w_ff_out': 'delta_w', 'new_m_w_in': 'new_m', 'new_m_w_pool': 'new_m', 'new_m_pool_scale': 'new_m', 'new_m_gm_gain': 'new_m', 'new_m_w_spatial': 'new_m', 'new_m_b_spatial': 'new_m', 'new_m_w_br_sb': 'new_m', 'new_m_w_br_pool': 'new_m', 'new_m_w_br_gm': 'new_m', 'new_m_w_out': 'new_m', 'new_m_g_mix_pre': 'new_m', 'new_m_g_mix_post': 'new_m', 'new_m_g_ff_pre': 'new_m', 'new_m_g_ff_post': 'new_m', 'new_m_w_ff_in': 'new_m', 'new_m_w_ff_out': 'new_m', 'new_v_w_in': 'new_v', 'new_v_w_pool': 'new_v', 'new_v_pool_scale': 'new_v', 'new_v_gm_gain': 'new_v', 'new_v_w_spatial': 'new_v', 'new_v_b_spatial': 'new_v', 'new_v_w_br_sb': 'new_v', 'new_v_w_br_pool': 'new_v', 'new_v_w_br_gm': 'new_v', 'new_v_w_out': 'new_v', 'new_v_g_mix_pre': 'new_v', 'new_v_g_mix_post': 'new_v', 'new_v_g_ff_pre': 'new_v', 'new_v_g_ff_post': 'new_v', 'new_v_w_ff_in': 'new_v', 'new_v_w_ff_out': 'new_v'}


def _forward(args):
    return _fwd_reference(*[args[k] for k in FWD_PARAMS])


def _output_shape():
    def fwd():
        inp = _fwd_setup_inputs(0)
        return _fwd_reference(*[inp[k] for k in FWD_PARAMS])
    out = _jax.eval_shape(fwd)
    return out.shape, out.dtype

N_MICROBATCH = 1
ADAM_LR = 0.001
ADAM_B1 = 0.9
ADAM_B2 = 0.999
ADAM_EPS = 1e-08
ADAM_WD = 0.01
ADAM_STEP = 10
PER_EXAMPLE_BATCH_AXIS = {'x': 0, 'loss_target': 0}
SHARED_INPUTS = []
_WEIGHT_DTYPES = {'w_in': _jnp.float32, 'w_pool': _jnp.float32, 'pool_scale': _jnp.float32, 'gm_gain': _jnp.float32, 'w_spatial': _jnp.float32, 'b_spatial': _jnp.float32, 'w_br_sb': _jnp.float32, 'w_br_pool': _jnp.float32, 'w_br_gm': _jnp.float32, 'w_out': _jnp.float32, 'g_mix_pre': _jnp.float32, 'g_mix_post': _jnp.float32, 'g_ff_pre': _jnp.float32, 'g_ff_post': _jnp.float32, 'w_ff_in': _jnp.float32, 'w_ff_out': _jnp.float32}
MOMENT_SCALE = {'w_in': 6.256771e+00, 'w_pool': 6.501154e+00, 'pool_scale': 7.759302e+00, 'gm_gain': 1.324228e+00, 'w_spatial': 7.261580e-01, 'b_spatial': 1.406876e+00, 'w_br_sb': 1.336421e+01, 'w_br_pool': 3.648277e+00, 'w_br_gm': 1.697364e+01, 'w_out': 2.155822e+01, 'g_mix_pre': 1.449037e+01, 'g_mix_post': 6.813400e+01, 'g_ff_pre': 1.327898e+01, 'g_ff_post': 7.439430e+01, 'w_ff_in': 6.478838e+00, 'w_ff_out': 3.245962e+01}


def _to_microbatches(a, axis):
    t = _jnp.moveaxis(a, axis, 0)
    t = t.reshape((N_MICROBATCH, t.shape[0] // N_MICROBATCH) + t.shape[1:])
    return _jnp.moveaxis(t, 1, axis + 1)


def setup_inputs(seed: int = 0) -> dict:
    inp = _fwd_setup_inputs(seed)
    key = _jax.random.fold_in(_jax.random.key(seed), 7919)
    shape, _ = _output_shape()
    out = dict(inp)
    out["loss_target"] = _jax.random.normal(_jax.random.fold_in(key, 0), shape, _jnp.float32)
    for i, name in enumerate(TWIN_WEIGHTS):
        w = inp[name].astype(_jnp.float32)
        if MOMENT_SCALE is None:
            s = _jnp.sqrt(_jnp.mean(_jnp.square(w)) + 1e-30)
        else:
            s = MOMENT_SCALE[name]
        km, kv = _jax.random.split(_jax.random.fold_in(key, i + 1))
        out[name] = w
        out["m_" + name] = s * _jax.random.normal(km, w.shape, _jnp.float32)
        out["v_" + name] = (s * s) * _jax.random.uniform(kv, w.shape, _jnp.float32, 0.5, 1.5)
    if N_MICROBATCH > 1:
        for name, axis in PER_EXAMPLE_BATCH_AXIS.items():
            out[name] = _to_microbatches(out[name], axis)
    return {'x': out['x'], 'w_in': out['w_in'], 'w_pool': out['w_pool'], 'pool_scale': out['pool_scale'], 'gm_gain': out['gm_gain'], 'w_spatial': out['w_spatial'], 'b_spatial': out['b_spatial'], 'w_br_sb': out['w_br_sb'], 'w_br_pool': out['w_br_pool'], 'w_br_gm': out['w_br_gm'], 'w_out': out['w_out'], 'g_mix_pre': out['g_mix_pre'], 'g_mix_post': out['g_mix_post'], 'g_ff_pre': out['g_ff_pre'], 'g_ff_post': out['g_ff_post'], 'w_ff_in': out['w_ff_in'], 'w_ff_out': out['w_ff_out'], 'loss_target': out['loss_target'], 'm_w_in': out['m_w_in'], 'm_w_pool': out['m_w_pool'], 'm_pool_scale': out['m_pool_scale'], 'm_gm_gain': out['m_gm_gain'], 'm_w_spatial': out['m_w_spatial'], 'm_b_spatial': out['m_b_spatial'], 'm_w_br_sb': out['m_w_br_sb'], 'm_w_br_pool': out['m_w_br_pool'], 'm_w_br_gm': out['m_w_br_gm'], 'm_w_out': out['m_w_out'], 'm_g_mix_pre': out['m_g_mix_pre'], 'm_g_mix_post': out['m_g_mix_post'], 'm_g_ff_pre': out['m_g_ff_pre'], 'm_g_ff_post': out['m_g_ff_post'], 'm_w_ff_in': out['m_w_ff_in'], 'm_w_ff_out': out['m_w_ff_out'], 'v_w_in': out['v_w_in'], 'v_w_pool': out['v_w_pool'], 'v_pool_scale': out['v_pool_scale'], 'v_gm_gain': out['v_gm_gain'], 'v_w_spatial': out['v_w_spatial'], 'v_b_spatial': out['v_b_spatial'], 'v_w_br_sb': out['v_w_br_sb'], 'v_w_br_pool': out['v_w_br_pool'], 'v_w_br_gm': out['v_w_br_gm'], 'v_w_out': out['v_w_out'], 'v_g_mix_pre': out['v_g_mix_pre'], 'v_g_mix_post': out['v_g_mix_post'], 'v_g_ff_pre': out['v_g_ff_pre'], 'v_g_ff_post': out['v_g_ff_post'], 'v_w_ff_in': out['v_w_ff_in'], 'v_w_ff_out': out['v_w_ff_out']}


def _loss(weights, diff, rest, loss_target):
    with _jax.named_scope("forward"):
        args = {**rest, TWIN_DIFF_INPUT: diff, **{k: w.astype(_WEIGHT_DTYPES[k]) for k, w in weights.items()}}
        y = _forward(args)
    with _jax.named_scope("loss_head"):
        err = _jnp.square(y.astype(_jnp.float32) - loss_target)
        return 0.5 * _jnp.sum(_jnp.mean(err, axis=-1)) if err.ndim else 0.5 * err


def _adamw(w, g, m, v):
    m = ADAM_B1 * m + (1.0 - ADAM_B1) * g
    v = ADAM_B2 * v + (1.0 - ADAM_B2) * _jnp.square(g)
    m_hat = m / (1.0 - ADAM_B1 ** ADAM_STEP)
    v_hat = v / (1.0 - ADAM_B2 ** ADAM_STEP)
    delta = -ADAM_LR * (m_hat / (_jnp.sqrt(v_hat) + ADAM_EPS) + ADAM_WD * w)
    return delta, m, v


def reference(x, w_in, w_pool, pool_scale, gm_gain, w_spatial, b_spatial, w_br_sb, w_br_pool, w_br_gm, w_out, g_mix_pre, g_mix_post, g_ff_pre, g_ff_post, w_ff_in, w_ff_out, loss_target, m_w_in, m_w_pool, m_pool_scale, m_gm_gain, m_w_spatial, m_b_spatial, m_w_br_sb, m_w_br_pool, m_w_br_gm, m_w_out, m_g_mix_pre, m_g_mix_post, m_g_ff_pre, m_g_ff_post, m_w_ff_in, m_w_ff_out, v_w_in, v_w_pool, v_pool_scale, v_gm_gain, v_w_spatial, v_b_spatial, v_w_br_sb, v_w_br_pool, v_w_br_gm, v_w_out, v_g_mix_pre, v_g_mix_post, v_g_ff_pre, v_g_ff_post, v_w_ff_in, v_w_ff_out):
    given = dict(x=x, w_in=w_in, w_pool=w_pool, pool_scale=pool_scale, gm_gain=gm_gain, w_spatial=w_spatial, b_spatial=b_spatial, w_br_sb=w_br_sb, w_br_pool=w_br_pool, w_br_gm=w_br_gm, w_out=w_out, g_mix_pre=g_mix_pre, g_mix_post=g_mix_post, g_ff_pre=g_ff_pre, g_ff_post=g_ff_post, w_ff_in=w_ff_in, w_ff_out=w_ff_out, loss_target=loss_target, m_w_in=m_w_in, m_w_pool=m_w_pool, m_pool_scale=m_pool_scale, m_gm_gain=m_gm_gain, m_w_spatial=m_w_spatial, m_b_spatial=m_b_spatial, m_w_br_sb=m_w_br_sb, m_w_br_pool=m_w_br_pool, m_w_br_gm=m_w_br_gm, m_w_out=m_w_out, m_g_mix_pre=m_g_mix_pre, m_g_mix_post=m_g_mix_post, m_g_ff_pre=m_g_ff_pre, m_g_ff_post=m_g_ff_post, m_w_ff_in=m_w_ff_in, m_w_ff_out=m_w_ff_out, v_w_in=v_w_in, v_w_pool=v_w_pool, v_pool_scale=v_pool_scale, v_gm_gain=v_gm_gain, v_w_spatial=v_w_spatial, v_b_spatial=v_b_spatial, v_w_br_sb=v_w_br_sb, v_w_br_pool=v_w_br_pool, v_w_br_gm=v_w_br_gm, v_w_out=v_w_out, v_g_mix_pre=v_g_mix_pre, v_g_mix_post=v_g_mix_post, v_g_ff_pre=v_g_ff_pre, v_g_ff_post=v_g_ff_post, v_w_ff_in=v_w_ff_in, v_w_ff_out=v_w_ff_out)
    weights = {n: given[n] for n in TWIN_WEIGHTS}
    shared = {n: given[n] for n in SHARED_INPUTS}
    per_example = {n: given[n] for n in ['x']}
    grad_fn = _jax.value_and_grad(_loss, argnums=(0, 1))

    def one_microbatch(ex, loss_target):
        ex = dict(ex)
        diff = ex.pop(TWIN_DIFF_INPUT)
        return grad_fn(weights, diff, {**shared, **ex}, loss_target)

    if N_MICROBATCH == 1:
        loss, (grad_w, grad_x) = one_microbatch(per_example, given["loss_target"])
    else:
        def body(carry, xs):
            loss_sum, grad_sum = carry
            l_k, (gw_k, gx_k) = one_microbatch(xs[0], xs[1])
            with _jax.named_scope("update"):
                return (loss_sum + l_k, _jax.tree.map(_jnp.add, grad_sum, gw_k)), gx_k

        init = (_jnp.zeros((), _jnp.float32), _jax.tree.map(_jnp.zeros_like, weights))
        (loss, grad_w), grad_x = _jax.lax.scan(body, init, (per_example, given["loss_target"]))
    with _jax.named_scope("update"):
        delta_w, new_m, new_v = {}, {}, {}
        for n in TWIN_WEIGHTS:
            delta_w[n], new_m[n], new_v[n] = _adamw(weights[n], grad_w[n], given["m_" + n], given["v_" + n])
    return (loss, grad_x, *[grad_w[n] for n in TWIN_WEIGHTS], *[delta_w[n] for n in TWIN_WEIGHTS],
            *[new_m[n] for n in TWIN_WEIGHTS], *[new_v[n] for n in TWIN_WEIGHTS])
```

```python
import functools

import numpy as np
import jax
import jax.numpy as jnp
from jax import lax
from jax.experimental import pallas as pl
from jax.experimental.pallas import tpu as pltpu

F32 = jnp.float32
MX = jnp.bfloat16

D = 1024
SB_W = 512
HEAD_PAIR = 128
PW = 256
GROUP = 64
N_GROUPS = 4
CHUNK = 128
POOL_WINDOWS = (2, 4, 8, 16)
D_FF = 4096
D_IN = 5376
QKV_W = 3 * SB_W
REST_W = D_IN - QKV_W
GATE_COL0 = 3 * PW
RMS_EPS = 1e-6
N_DEV = 8
DEPTH = 4

ADAM_LR = 0.001
ADAM_B1 = 0.9
ADAM_B2 = 0.999
ADAM_EPS = 1e-08
ADAM_WD = 0.01
ADAM_STEP = 10

SKIP_LOG = -120.0

VMEM_LIMIT = 48 * 1024 * 1024
W_IN_TILE = 768
PACK_COLS = 1024

BIG = ("w_in", "w_br_sb", "w_br_pool", "w_br_gm", "w_out", "w_ff_in", "w_ff_out")
BIG_SHARD_AXIS = {"w_in": 2, "w_br_sb": 2, "w_br_pool": 2, "w_br_gm": 2, "w_out": 1, "w_ff_in": 2, "w_ff_out": 1}
SMALL = ("w_pool", "pool_scale", "gm_gain", "w_spatial", "b_spatial",
         "g_mix_pre", "g_mix_post", "g_ff_pre", "g_ff_post")
WEIGHTS = ("w_in", "w_pool", "pool_scale", "gm_gain", "w_spatial", "b_spatial", "w_br_sb", "w_br_pool",
           "w_br_gm", "w_out", "g_mix_pre", "g_mix_post", "g_ff_pre", "g_ff_post", "w_ff_in", "w_ff_out")


def _params(sem, vmem=VMEM_LIMIT):
    return pltpu.CompilerParams(dimension_semantics=sem, vmem_limit_bytes=vmem)


def _dot(a, b, dims):
    return lax.dot_general(a, b, (dims, ((), ())), preferred_element_type=F32)


NN = ((1,), (0,))
NT = ((1,), (1,))
TN = ((0,), (0,))


def _split_dot(x, m, left, nsplit):
    acc = None
    r = x
    for s in range(nsplit):
        p = r.astype(jnp.bfloat16)
        d = _dot(m, p, NN) if left else _dot(p, m, NN)
        acc = d if acc is None else acc + d
        if s + 1 < nsplit:
            r = r - p.astype(F32)
    return acc


def _matmul(a, b, *, mode, name, tm, tn, tk, out_dtypes, b_col0=0, n_out=None, epilogue=None, extras=()):
    if mode == "nn":
        m, kdim = a.shape
        n = b.shape[1] if n_out is None else n_out
        dims = NN
    elif mode == "nt":
        m, kdim = a.shape
        n = b.shape[0]
        dims = NT
    else:
        kdim, m = a.shape
        n = b.shape[1]
        dims = TN
    tm, tn, tk = min(tm, m), min(tn, n), min(tk, kdim)
    assert m % tm == 0 and n % tn == 0 and kdim % tk == 0 and b_col0 % tn == 0, (name, m, n, kdim)
    if mode == "nn":
        a_spec = pl.BlockSpec((tm, tk), lambda i, j, k: (i, k))
        b_spec = pl.BlockSpec((tk, tn), lambda i, j, k: (k, j + b_col0 // tn))
    elif mode == "nt":
        a_spec = pl.BlockSpec((tm, tk), lambda i, j, k: (i, k))
        b_spec = pl.BlockSpec((tn, tk), lambda i, j, k: (j, k))
    else:
        a_spec = pl.BlockSpec((tk, tm), lambda i, j, k: (k, i))
        b_spec = pl.BlockSpec((tk, tn), lambda i, j, k: (k, j))
    nk = kdim // tk
    n_extra = len(extras)
    n_outs = len(out_dtypes)
    o_spec = pl.BlockSpec((tm, tn), lambda i, j, k: (i, j))

    def body(a_ref, b_ref, *refs):
        extra_refs = refs[:n_extra]
        out_refs = refs[n_extra:n_extra + n_outs]
        acc_ref = refs[-1]
        k = pl.program_id(2)

        @pl.when(k == 0)
        def _():
            acc_ref[...] = jnp.zeros_like(acc_ref)

        acc_ref[...] += _dot(a_ref[...].astype(MX), b_ref[...].astype(MX), dims)

        @pl.when(k == nk - 1)
        def _():
            acc = acc_ref[...]
            outs = (acc,) if epilogue is None else epilogue(acc, *[e[...] for e in extra_refs])
            for o_ref, val in zip(out_refs, outs):
                o_ref[...] = val.astype(o_ref.dtype)

    outs = pl.pallas_call(
        body, name=name,
        grid=(m // tm, n // tn, nk),
        in_specs=[a_spec, b_spec] + [o_spec] * n_extra,
        out_specs=[o_spec] * n_outs,
        out_shape=[jax.ShapeDtypeStruct((m, n), dt) for dt in out_dtypes],
        scratch_shapes=[pltpu.VMEM((tm, tn), F32)],
        compiler_params=_params(("parallel", "parallel", "arbitrary")),
    )(a, b, *extras)
    return outs[0] if n_outs == 1 else outs


ROW_TILE = 512


def _rms_fwd(x, g, *, res, out_dtype, name):
    s, d = x.shape
    tr = min(ROW_TILE, s)
    has_res = res is not None

    def body(x_ref, g_ref, *refs):
        out_ref = refs[-1]
        xv = x_ref[...]
        y = xv * lax.rsqrt(jnp.mean(xv * xv, axis=-1, keepdims=True) + RMS_EPS) * g_ref[...]
        if has_res:
            y = refs[0][...] + y
        out_ref[...] = y.astype(out_ref.dtype)

    row = pl.BlockSpec((tr, d), lambda i: (i, 0))
    vec = pl.BlockSpec((1, d), lambda i: (0, 0))
    return pl.pallas_call(
        body, name=name, grid=(s // tr,),
        in_specs=[row, vec] + ([row] if has_res else []),
        out_specs=row, out_shape=jax.ShapeDtypeStruct((s, d), out_dtype),
        compiler_params=_params(("parallel",)),
    )(x, g, *([res] if has_res else []))


def _rms_bwd(x, g, dout, *, res, out_dtype, name):
    s, d = x.shape
    tr = min(ROW_TILE, s)
    has_res = res is not None

    def body(x_ref, g_ref, do_ref, *refs):
        dx_ref, dg_ref = refs[-2], refs[-1]
        i = pl.program_id(0)
        xv = x_ref[...]
        do = do_ref[...]
        r = lax.rsqrt(jnp.mean(xv * xv, axis=-1, keepdims=True) + RMS_EPS)
        xhat = xv * r
        dxhat = do * g_ref[...]
        dx = r * (dxhat - xhat * jnp.mean(dxhat * xhat, axis=-1, keepdims=True))
        if has_res:
            dx = refs[0][...] + dx
        dx_ref[...] = dx.astype(dx_ref.dtype)

        @pl.when(i == 0)
        def _():
            dg_ref[...] = jnp.zeros_like(dg_ref)

        dg_ref[...] += jnp.sum(do * xhat, axis=0, keepdims=True)

    row = pl.BlockSpec((tr, d), lambda i: (i, 0))
    vec = pl.BlockSpec((1, d), lambda i: (0, 0))
    return pl.pallas_call(
        body, name=name, grid=(s // tr,),
        in_specs=[row, vec, row] + ([row] if has_res else []),
        out_specs=[row, vec],
        out_shape=[jax.ShapeDtypeStruct((s, d), out_dtype), jax.ShapeDtypeStruct((1, d), F32)],
        compiler_params=_params(("arbitrary",)),
    )(x, g, dout, *([res] if has_res else []))


def _loss_head(y, target):
    s, d = y.shape
    tr = min(ROW_TILE, s)

    def body(y_ref, t_ref, dy_ref, sq_ref):
        i = pl.program_id(0)
        err = y_ref[...] - t_ref[...]
        dy_ref[...] = err * (1.0 / d)

        @pl.when(i == 0)
        def _():
            sq_ref[...] = jnp.zeros_like(sq_ref)

        sq_ref[...] += jnp.sum(err * err, axis=0, keepdims=True)

    row = pl.BlockSpec((tr, d), lambda i: (i, 0))
    vec = pl.BlockSpec((1, d), lambda i: (0, 0))
    return pl.pallas_call(
        body, name="loss_head", grid=(s // tr,),
        in_specs=[row, row], out_specs=[row, vec],
        out_shape=[jax.ShapeDtypeStruct((s, d), F32), jax.ShapeDtypeStruct((1, d), F32)],
        compiler_params=_params(("arbitrary",)),
    )(y, target)


def _sb_masks():
    row = lax.broadcasted_iota(jnp.int32, (CHUNK, CHUNK), 0)
    col = lax.broadcasted_iota(jnp.int32, (CHUNK, CHUNK), 1)
    return row, col


def _sb_scores(qm, kj, strict_or_off):
    z = _dot(qm, kj, NT)
    t = jnp.log1p(jnp.exp(-jnp.abs(z)))
    ln = -(jnp.maximum(z, 0.0) + t)
    lb = ln + z
    ln = jnp.where(strict_or_off, ln, 0.0)
    return ln, lb


def _attn_fwd(qkv):
    s = qkv.shape[0]
    nblk = s // CHUNK
    n_pairs = SB_W // HEAD_PAIR

    def body(q_ref, k_ref, v_ref, o_ref):
        i = pl.program_id(1)
        row, col = _sb_masks()
        strict = col < row
        upper = (row > col).astype(jnp.bfloat16)
        q = q_ref[...]
        lane_hi = col >= GROUP
        outs = []
        for head in range(2):
            in_head = lane_hi if head else jnp.logical_not(lane_hi)
            qm = jnp.where(in_head, q, jnp.zeros_like(q)) * 0.125

            def step(j, c, acc, qm=qm):
                off = pl.multiple_of(j * CHUNK, CHUNK)
                kj = k_ref[pl.ds(off, CHUNK), :]
                vj = v_ref[pl.ds(off, CHUNK), :]
                keep = jnp.logical_or(strict, j < i)
                ln, lb = _sb_scores(qm, kj, keep)
                suffix = _split_dot(ln, upper, False, 2) + c
                a = jnp.where(keep, jnp.exp(lb + suffix), 0.0)
                acc = acc + _dot(a.astype(MX), vj, NN)
                c = c + jnp.sum(ln, axis=1, keepdims=True)
                return c, acc

            def cond(st):
                return jnp.logical_and(st[0] >= 0, st[1])

            def loop(st):
                j, _, c, acc = st
                c, acc = step(j, c, acc)
                return j - 1, jnp.max(c) > SKIP_LOG, c, acc

            init = (i, i >= 0, jnp.zeros((CHUNK, 1), F32), jnp.zeros((CHUNK, HEAD_PAIR), F32))
            outs.append(lax.while_loop(cond, loop, init)[3])
        o_ref[...] = jnp.where(lane_hi, outs[1], outs[0]).astype(o_ref.dtype)

    return pl.pallas_call(
        body, name="attn_fwd", grid=(n_pairs, nblk),
        in_specs=[pl.BlockSpec((CHUNK, HEAD_PAIR), lambda p, i: (i, p)),
                  pl.BlockSpec((s, HEAD_PAIR), lambda p, i: (0, n_pairs + p)),
                  pl.BlockSpec((s, HEAD_PAIR), lambda p, i: (0, 2 * n_pairs + p))],
        out_specs=pl.BlockSpec((CHUNK, HEAD_PAIR), lambda p, i: (i, p)),
        out_shape=jax.ShapeDtypeStruct((s, SB_W), MX),
        compiler_params=_params(("parallel", "parallel")),
    )(qkv, qkv, qkv)


def _attn_bwd(qkv, d_o):
    s = qkv.shape[0]
    nblk = s // CHUNK
    n_pairs = SB_W // HEAD_PAIR

    def body(q_ref, k_ref, v_ref, do_ref, dq_ref, dk_ref, dv_ref, e_s, sig_s, dq_s):
        i = pl.program_id(1)
        head = pl.program_id(2)
        row, col = _sb_masks()
        strict = col < row
        upper = (row > col).astype(jnp.bfloat16)
        lower = (row < col).astype(jnp.bfloat16)
        in_head = jnp.where(col >= GROUP, 1, 0) == head
        q = q_ref[...]
        qm = jnp.where(in_head, q, jnp.zeros_like(q)) * 0.125
        do = do_ref[...]
        dom = jnp.where(in_head, do, jnp.zeros_like(do))

        @pl.when(jnp.logical_and(i == 0, head == 0))
        def _():
            dk_ref[...] = jnp.zeros_like(dk_ref)
            dv_ref[...] = jnp.zeros_like(dv_ref)

        def cond(st):
            return jnp.logical_and(st[0] >= 0, st[1])

        def loop(st):
            j, _, c = st
            off = pl.multiple_of(j * CHUNK, CHUNK)
            kj = k_ref[pl.ds(off, CHUNK), :]
            vj = v_ref[pl.ds(off, CHUNK), :]
            keep = jnp.logical_or(strict, j < i)
            ln, lb = _sb_scores(qm, kj, keep)
            suffix = _split_dot(ln, upper, False, 2) + c
            a = jnp.where(keep, jnp.exp(lb + suffix), 0.0)
            da = _dot(dom, vj, NT)
            e_s[j] = a * da
            sig_s[j] = jnp.exp(lb)
            dv_ref[pl.ds(off, CHUNK), :] += _dot(a.astype(MX), dom, TN)
            c = c + jnp.sum(ln, axis=1, keepdims=True)
            return j - 1, jnp.max(c) > SKIP_LOG, c

        j_end = lax.while_loop(cond, loop, (i, i >= 0, jnp.zeros((CHUNK, 1), F32)))[0]

        def up(j, st):
            cp, dq = st
            off = pl.multiple_of(j * CHUNK, CHUNK)
            kj = k_ref[pl.ds(off, CHUNK), :]
            e = e_s[j]
            sig = sig_s[j]
            prefix = _split_dot(e, lower, False, 2) + cp
            dz = e * (1.0 - sig) - prefix * sig
            dz = jnp.where(jnp.logical_or(strict, j < i), dz, 0.0).astype(MX)
            dq = dq + _dot(dz, kj, NN)
            dk_ref[pl.ds(off, CHUNK), :] += _dot(dz, qm, TN)
            return cp + jnp.sum(e, axis=1, keepdims=True), dq

        dq = lax.fori_loop(j_end + 1, i + 1, up,
                           (jnp.zeros((CHUNK, 1), F32), jnp.zeros((CHUNK, HEAD_PAIR), F32)))[1]
        dq = jnp.where(in_head, dq * 0.125, 0.0)

        @pl.when(head == 0)
        def _():
            dq_s[...] = dq

        @pl.when(head == 1)
        def _():
            dq_ref[...] = (dq_s[...] + dq).astype(dq_ref.dtype)

    blk = pl.BlockSpec((CHUNK, HEAD_PAIR), lambda p, i, h: (i, p))
    full = pl.BlockSpec((s, HEAD_PAIR), lambda p, i, h: (0, p))
    return pl.pallas_call(
        body, name="attn_bwd", grid=(n_pairs, nblk, 2),
        in_specs=[blk,
                  pl.BlockSpec((s, HEAD_PAIR), lambda p, i, h: (0, n_pairs + p)),
                  pl.BlockSpec((s, HEAD_PAIR), lambda p, i, h: (0, 2 * n_pairs + p)),
                  blk],
        out_specs=[blk, full, full],
        out_shape=[jax.ShapeDtypeStruct((s, SB_W), MX), jax.ShapeDtypeStruct((s, SB_W), F32),
                   jax.ShapeDtypeStruct((s, SB_W), F32)],
        scratch_shapes=[pltpu.VMEM((nblk, CHUNK, CHUNK), F32), pltpu.VMEM((nblk, CHUNK, CHUNK), F32),
                        pltpu.VMEM((CHUNK, HEAD_PAIR), F32)],
        compiler_params=_params(("parallel", "arbitrary", "arbitrary"), 56 * 1024 * 1024),
    )(qkv, qkv, qkv, d_o)


POOL_TILE = 256
POOL_HALO = 128


def _pool_bands(tile, transpose):
    cur = np.zeros((N_GROUPS, tile, tile), np.float32)
    halo = np.zeros((N_GROUPS, tile, POOL_HALO), np.float32)
    t = np.arange(tile)[:, None]
    for g, w in enumerate(POOL_WINDOWS):
        if not transpose:
            p = np.arange(tile)[None, :]
            cur[g] = ((t - p >= 0) & (t - p < w))
            ph = np.arange(POOL_HALO)[None, :] - POOL_HALO
            halo[g] = (t - ph < w)
        else:
            p = np.arange(tile)[None, :]
            cur[g] = ((p - t >= 0) & (p - t < w))
            ph = np.arange(POOL_HALO)[None, :] + tile
            halo[g] = (ph - t < w)
    return jnp.asarray(cur, jnp.bfloat16), jnp.asarray(halo, jnp.bfloat16)


def _pool_count(i, tile, rows, row0):
    pos = (i * tile + row0 + lax.broadcasted_iota(jnp.int32, (rows, PW), 0)).astype(F32)
    grp = lax.broadcasted_iota(jnp.int32, (rows, PW), 1) // GROUP
    win = jnp.where(grp == 0, float(POOL_WINDOWS[0]),
                    jnp.where(grp == 1, float(POOL_WINDOWS[1]),
                              jnp.where(grp == 2, float(POOL_WINDOWS[2]), float(POOL_WINDOWS[3]))))
    return jnp.minimum(pos + 1.0, win), grp


def _pool_fwd(rest, wbd, scale):
    s = rest.shape[0]
    tile = min(POOL_TILE, s)
    halo_per_tile = tile // POOL_HALO
    bcur, bhalo = _pool_bands(tile, False)

    def body(cur_ref, prev_ref, bcur_ref, bhalo_ref, wbd_ref, scale_ref, pooled_ref, out_ref):
        i = pl.program_id(0)
        cur = cur_ref[...]
        prev = jnp.where(i > 0, prev_ref[...], 0.0)
        count, grp = _pool_count(i, tile, tile, 0)
        win = jnp.zeros((tile, PW), F32)
        for g in range(N_GROUPS):
            wsum = _split_dot(cur, bcur_ref[g], True, 3) + _split_dot(prev, bhalo_ref[g], True, 3)
            win = jnp.where(grp == g, wsum, win)
        pooled = (win / count - cur).astype(MX)
        pooled_ref[...] = pooled
        out_ref[...] = (_dot(pooled, wbd_ref[...], NN) * scale_ref[...]).astype(out_ref.dtype)

    tile_spec = pl.BlockSpec((tile, PW), lambda i: (i, 0))
    const3 = lambda shape: pl.BlockSpec(shape, lambda i: (0, 0, 0))
    const2 = lambda shape: pl.BlockSpec(shape, lambda i: (0, 0))
    return pl.pallas_call(
        body, name="pool_fwd", grid=(s // tile,),
        in_specs=[tile_spec,
                  pl.BlockSpec((POOL_HALO, PW), lambda i: (jnp.maximum(i * halo_per_tile - 1, 0), 0)),
                  const3(bcur.shape), const3(bhalo.shape), const2((PW, PW)), const2((1, PW))],
        out_specs=[tile_spec, tile_spec],
        out_shape=[jax.ShapeDtypeStruct((s, PW), MX), jax.ShapeDtypeStruct((s, PW), MX)],
        compiler_params=_params(("parallel",)),
    )(rest, rest, bcur, bhalo, wbd, scale)


def _pool_bwd(d_o, pooled, wbd, scale):
    s = d_o.shape[0]
    tile = min(POOL_TILE, s)
    halo_per_tile = tile // POOL_HALO
    n_halo = s // POOL_HALO
    n_tiles = s // tile
    ccur, chalo = _pool_bands(tile, True)

    def body(do_ref, nxt_ref, pooled_ref, ccur_ref, chalo_ref, wbd_ref, scale_ref, dp_ref, dw_ref, ds_ref):
        i = pl.program_id(0)
        do = do_ref[...]
        w = wbd_ref[...]
        sc = scale_ref[...]
        pooled_v = pooled_ref[...]

        @pl.when(i == 0)
        def _():
            dw_ref[...] = jnp.zeros_like(dw_ref)
            ds_ref[...] = jnp.zeros_like(ds_ref)

        ds_ref[...] += jnp.sum(do * _dot(pooled_v, w, NN), axis=0, keepdims=True)
        dmixed = (do * sc).astype(MX)
        dw_ref[...] += _dot(pooled_v, dmixed, TN)
        dpooled = _dot(dmixed, w, NT)
        nxt = jnp.where(i < n_tiles - 1, nxt_ref[...], 0.0)
        dpooled_n = _dot((nxt * sc).astype(MX), w, NT)
        count, grp = _pool_count(i, tile, tile, 0)
        count_n, _ = _pool_count(i, tile, POOL_HALO, tile)
        dq = dpooled / count
        dq_n = dpooled_n / count_n
        acc = jnp.zeros((tile, PW), F32)
        for g in range(N_GROUPS):
            wsum = _split_dot(dq, ccur_ref[g], True, 3) + _split_dot(dq_n, chalo_ref[g], True, 3)
            acc = jnp.where(grp == g, wsum, acc)
        dp_ref[...] = (acc - dpooled).astype(dp_ref.dtype)

    tile_spec = pl.BlockSpec((tile, PW), lambda i: (i, 0))
    const3 = lambda shape: pl.BlockSpec(shape, lambda i: (0, 0, 0))
    const2 = lambda shape: pl.BlockSpec(shape, lambda i: (0, 0))
    return pl.pallas_call(
        body, name="pool_bwd", grid=(n_tiles,),
        in_specs=[tile_spec,
                  pl.BlockSpec((POOL_HALO, PW), lambda i: (jnp.minimum((i + 1) * halo_per_tile, n_halo - 1), 0)),
                  tile_spec, const3(ccur.shape), const3(chalo.shape), const2((PW, PW)), const2((1, PW))],
        out_specs=[tile_spec, const2((PW, PW)), const2((1, PW))],
        out_shape=[jax.ShapeDtypeStruct((s, PW), MX), jax.ShapeDtypeStruct((PW, PW), F32),
                   jax.ShapeDtypeStruct((1, PW), F32)],
        compiler_params=_params(("arbitrary",)),
    )(d_o, d_o, pooled, ccur, chalo, wbd, scale)


GM_TILE = 512
GELU_C = 0.7978845608028654
GELU_A = 0.044715


def _gelu(x):
    return 0.5 * x * (1.0 + jnp.tanh(GELU_C * (x + GELU_A * (x * x * x))))


def _gelu_grad(x):
    th = jnp.tanh(GELU_C * (x + GELU_A * (x * x * x)))
    return 0.5 * (1.0 + th) + 0.5 * x * (1.0 - th * th) * (GELU_C * (1.0 + 3.0 * GELU_A * (x * x)))


def _gm_common(ws_ref):
    row, col = _sb_masks()
    tril = row >= col
    wsm = [jnp.where(tril, ws_ref[g], 0.0).astype(MX) for g in range(N_GROUPS)]
    grp = lax.broadcasted_iota(jnp.int32, (CHUNK, PW), 1) // GROUP
    return tril, wsm, grp


def _gm_mix(wsm, grp, vn_c):
    mixed = jnp.zeros((CHUNK, PW), F32)
    for g in range(N_GROUPS):
        mixed = jnp.where(grp == g, _dot(wsm[g], vn_c, NN), mixed)
    return mixed


def _gm_fwd(rest, gain, ws, bfull):
    s = rest.shape[0]
    tile = min(GM_TILE, s)

    def body(u_ref, v_ref, gain_ref, ws_ref, b_ref, out_ref):
        _, wsm, grp = _gm_common(ws_ref)
        bias = b_ref[...]
        for n in range(tile // CHUNK):
            rows = slice(n * CHUNK, (n + 1) * CHUNK)
            gu = _gelu(u_ref[rows, :])
            gv = _gelu(v_ref[rows, :])
            r = lax.rsqrt(jnp.mean(gv * gv, axis=-1, keepdims=True) + RMS_EPS)
            vn = (gv * r * gain_ref[...]).astype(MX)
            out_ref[rows, :] = (gu * (_gm_mix(wsm, grp, vn) + bias)).astype(out_ref.dtype)

    const2 = lambda shape: pl.BlockSpec(shape, lambda i: (0, 0))
    return pl.pallas_call(
        body, name="gm_fwd", grid=(s // tile,),
        in_specs=[pl.BlockSpec((tile, PW), lambda i: (i, 1)), pl.BlockSpec((tile, PW), lambda i: (i, 2)),
                  const2((1, PW)), pl.BlockSpec((N_GROUPS, CHUNK, CHUNK), lambda i: (0, 0, 0)),
                  const2((CHUNK, PW))],
        out_specs=pl.BlockSpec((tile, PW), lambda i: (i, 0)),
        out_shape=jax.ShapeDtypeStruct((s, PW), MX),
        compiler_params=_params(("parallel",)),
    )(rest, rest, gain, ws, bfull)


def _gm_bwd(d_o, rest, gain, ws, bfull):
    s = rest.shape[0]
    tile = min(GM_TILE, s)

    def body(do_ref, u_ref, v_ref, gain_ref, ws_ref, b_ref, du_ref, dv_ref, dws_ref, db_ref, dgain_ref):
        i = pl.program_id(0)
        tril, wsm, grp = _gm_common(ws_ref)
        bias = b_ref[...]
        gain_v = gain_ref[...]

        @pl.when(i == 0)
        def _():
            dws_ref[...] = jnp.zeros_like(dws_ref)
            db_ref[...] = jnp.zeros_like(db_ref)
            dgain_ref[...] = jnp.zeros_like(dgain_ref)

        for n in range(tile // CHUNK):
            rows = slice(n * CHUNK, (n + 1) * CHUNK)
            u = u_ref[rows, :]
            v = v_ref[rows, :]
            do = do_ref[rows, :]
            gu = _gelu(u)
            gv = _gelu(v)
            r = lax.rsqrt(jnp.mean(gv * gv, axis=-1, keepdims=True) + RMS_EPS)
            vhat = gv * r
            vn = (vhat * gain_v).astype(MX)
            mixed = _gm_mix(wsm, grp, vn)
            du_ref[rows, :] = (do * (mixed + bias) * _gelu_grad(u)).astype(du_ref.dtype)
            dmix = do * gu
            db_ref[...] += dmix
            dmix_mx = dmix.astype(MX)
            dvn = jnp.zeros((CHUNK, PW), F32)
            for g in range(N_GROUPS):
                dmg = jnp.where(grp == g, dmix_mx, jnp.zeros_like(dmix_mx))
                dws_ref[g] += jnp.where(tril, _dot(dmg, vn, NT), 0.0)
                dvn = jnp.where(grp == g, _dot(wsm[g], dmix_mx, TN), dvn)
            dgain_ref[...] += jnp.sum(dvn * vhat, axis=0, keepdims=True)
            dvhat = dvn * gain_v
            dgv = r * (dvhat - vhat * jnp.mean(dvhat * vhat, axis=-1, keepdims=True))
            dv_ref[rows, :] = (dgv * _gelu_grad(v)).astype(dv_ref.dtype)

    const2 = lambda shape: pl.BlockSpec(shape, lambda i: (0, 0))
    ws_spec = pl.BlockSpec((N_GROUPS, CHUNK, CHUNK), lambda i: (0, 0, 0))
    tile_spec = pl.BlockSpec((tile, PW), lambda i: (i, 0))
    return pl.pallas_call(
        body, name="gm_bwd", grid=(s // tile,),
        in_specs=[tile_spec, pl.BlockSpec((tile, PW), lambda i: (i, 1)), pl.BlockSpec((tile, PW), lambda i: (i, 2)),
                  const2((1, PW)), ws_spec, const2((CHUNK, PW))],
        out_specs=[tile_spec, tile_spec, ws_spec, const2((CHUNK, PW)), const2((1, PW))],
        out_shape=[jax.ShapeDtypeStruct((s, PW), MX), jax.ShapeDtypeStruct((s, PW), MX),
                   jax.ShapeDtypeStruct((N_GROUPS, CHUNK, CHUNK), F32),
                   jax.ShapeDtypeStruct((CHUNK, PW), F32), jax.ShapeDtypeStruct((1, PW), F32)],
        compiler_params=_params(("arbitrary",)),
    )(d_o, rest, rest, gain, ws, bfull)


GATE_TM = 512
GATE_TN = 256


def _gate_specs(s):
    tm = min(GATE_TM, s)
    tn = GATE_TN
    gate0 = GATE_COL0 // tn
    per = D // tn
    ins = [pl.BlockSpec((tm, SB_W), lambda i, j: (i, 0)),
           pl.BlockSpec((tm, PW), lambda i, j: (i, 0)),
           pl.BlockSpec((tm, PW), lambda i, j: (i, 0)),
           pl.BlockSpec((SB_W, tn), lambda i, j: (0, j)),
           pl.BlockSpec((PW, tn), lambda i, j: (0, j)),
           pl.BlockSpec((PW, tn), lambda i, j: (0, j)),
           pl.BlockSpec((tm, tn), lambda i, j: (i, gate0 + j)),
           pl.BlockSpec((tm, tn), lambda i, j: (i, gate0 + per + j)),
           pl.BlockSpec((tm, tn), lambda i, j: (i, gate0 + 2 * per + j))]
    return tm, tn, ins, pl.BlockSpec((tm, tn), lambda i, j: (i, j))


def _gate_fwd(o_sb, o_pool, o_gm, w_sb, w_pool, w_gm, rest):
    s = rest.shape[0]
    tm, tn, ins, out = _gate_specs(s)

    def body(o0, o1, o2, w0, w1, w2, g0, g1, g2, merged_ref):
        acc = jax.nn.sigmoid(g0[...]) * _dot(o0[...], w0[...], NN)
        acc += jax.nn.sigmoid(g1[...]) * _dot(o1[...], w1[...], NN)
        acc += jax.nn.sigmoid(g2[...]) * _dot(o2[...], w2[...], NN)
        merged_ref[...] = acc.astype(merged_ref.dtype)

    return pl.pallas_call(
        body, name="gate_fwd", grid=(s // tm, D // tn),
        in_specs=ins, out_specs=out, out_shape=jax.ShapeDtypeStruct((s, D), MX),
        compiler_params=_params(("parallel", "parallel")),
    )(o_sb, o_pool, o_gm, w_sb, w_pool, w_gm, rest, rest, rest)


def _gate_bwd(o_sb, o_pool, o_gm, w_sb, w_pool, w_gm, rest, dmerged):
    s = rest.shape[0]
    tm, tn, ins, out = _gate_specs(s)

    def body(o0, o1, o2, w0, w1, w2, g0, g1, g2, dm_ref, db0, db1, db2, dg0, dg1, dg2):
        dm = dm_ref[...]
        for o, w, g, db, dg in ((o0, w0, g0, db0, dg0), (o1, w1, g1, db1, dg1), (o2, w2, g2, db2, dg2)):
            sg = jax.nn.sigmoid(g[...])
            db[...] = (dm * sg).astype(db.dtype)
            dg[...] = (dm * _dot(o[...], w[...], NN) * (sg * (1.0 - sg))).astype(dg.dtype)

    return pl.pallas_call(
        body, name="gate_bwd", grid=(s // tm, D // tn),
        in_specs=ins + [out], out_specs=[out] * 6,
        out_shape=[jax.ShapeDtypeStruct((s, D), MX)] * 6,
        compiler_params=_params(("parallel", "parallel")),
    )(o_sb, o_pool, o_gm, w_sb, w_pool, w_gm, rest, rest, rest, dmerged)


def _relu2(acc):
    r = jnp.maximum(acc, 0.0)
    return acc, r * r


def _relu2_bwd(acc, a):
    return (acc * (2.0 * jnp.maximum(a, 0.0)),)


def _layer_fwd(x, w):
    h = _rms_fwd(x, w["g_mix_pre"], res=None, out_dtype=MX, name="rms_mix_pre")
    qkv = _matmul(h, w["w_in"], mode="nn", name="proj_qkv", tm=1024, tn=W_IN_TILE, tk=D,
                  out_dtypes=(MX,), n_out=QKV_W)
    rest = _matmul(h, w["w_in"], mode="nn", name="proj_rest", tm=1024, tn=W_IN_TILE, tk=D,
                   out_dtypes=(F32,), b_col0=QKV_W, n_out=REST_W)
    o_sb = _attn_fwd(qkv)
    pooled, o_pool = _pool_fwd(rest, w["wbd"], w["pool_scale"])
    o_gm = _gm_fwd(rest, w["gm_gain"], w["w_spatial"], w["bfull"])
    merged = _gate_fwd(o_sb, o_pool, o_gm, w["w_br_sb"], w["w_br_pool"], w["w_br_gm"], rest)
    y = _matmul(merged, w["w_out"], mode="nn", name="out_proj", tm=1024, tn=512, tk=D, out_dtypes=(F32,))
    x1 = _rms_fwd(y, w["g_mix_post"], res=x, out_dtype=F32, name="rms_mix_post")
    h2 = _rms_fwd(x1, w["g_ff_pre"], res=None, out_dtype=MX, name="rms_ff_pre")
    a, r = _matmul(h2, w["w_ff_in"], mode="nn", name="ff_in", tm=1024, tn=512, tk=D,
                   out_dtypes=(F32, MX), epilogue=_relu2)
    ff = _matmul(r, w["w_ff_out"], mode="nn", name="ff_out", tm=1024, tn=512, tk=1024, out_dtypes=(F32,))
    x2 = _rms_fwd(ff, w["g_ff_post"], res=x1, out_dtype=F32, name="rms_ff_post")
    saved = dict(x=x, h=h, qkv=qkv, rest=rest, o_sb=o_sb, pooled=pooled, o_pool=o_pool, o_gm=o_gm,
                 merged=merged, y=y, x1=x1, h2=h2, a=a, r=r, ff=ff)
    return x2, saved


def _layer_bwd(dx2, w, sv):
    dff, dg_ff_post = _rms_bwd(sv["ff"], w["g_ff_post"], dx2, res=None, out_dtype=MX, name="rms_ff_post_bwd")
    da = _matmul(dff, w["w_ff_out"], mode="nt", name="ff_out_dx", tm=1024, tn=512, tk=D,
                 out_dtypes=(MX,), epilogue=_relu2_bwd, extras=(sv["a"],))
    dw_ff_out = _matmul(sv["r"], dff, mode="tn", name="ff_out_dw", tm=1024, tn=512, tk=1024, out_dtypes=(F32,))
    dh2 = _matmul(da, w["w_ff_in"], mode="nt", name="ff_in_dx", tm=1024, tn=512, tk=1024, out_dtypes=(F32,))
    dw_ff_in = _matmul(sv["h2"], da, mode="tn", name="ff_in_dw", tm=1024, tn=512, tk=1024, out_dtypes=(F32,))
    dx1, dg_ff_pre = _rms_bwd(sv["x1"], w["g_ff_pre"], dh2, res=dx2, out_dtype=F32, name="rms_ff_pre_bwd")

    dy, dg_mix_post = _rms_bwd(sv["y"], w["g_mix_post"], dx1, res=None, out_dtype=MX, name="rms_mix_post_bwd")
    dmerged = _matmul(dy, w["w_out"], mode="nt", name="out_proj_dx", tm=1024, tn=512, tk=D, out_dtypes=(F32,))
    dw_out = _matmul(sv["merged"], dy, mode="tn", name="out_proj_dw", tm=1024, tn=512, tk=1024, out_dtypes=(F32,))
    db_sb, db_pool, db_gm, dg0, dg1, dg2 = _gate_bwd(
        sv["o_sb"], sv["o_pool"], sv["o_gm"], w["w_br_sb"], w["w_br_pool"], w["w_br_gm"], sv["rest"], dmerged)
    do_sb = _matmul(db_sb, w["w_br_sb"], mode="nt", name="br_sb_dx", tm=1024, tn=512, tk=D, out_dtypes=(MX,))
    dw_br_sb = _matmul(sv["o_sb"], db_sb, mode="tn", name="br_sb_dw", tm=512, tn=512, tk=1024, out_dtypes=(F32,))
    do_pool = _matmul(db_pool, w["w_br_pool"], mode="nt", name="br_pool_dx", tm=1024, tn=256, tk=D, out_dtypes=(F32,))
    dw_br_pool = _matmul(sv["o_pool"], db_pool, mode="tn", name="br_pool_dw", tm=256, tn=512, tk=1024,
                         out_dtypes=(F32,))
    do_gm = _matmul(db_gm, w["w_br_gm"], mode="nt", name="br_gm_dx", tm=1024, tn=256, tk=D, out_dtypes=(F32,))
    dw_br_gm = _matmul(sv["o_gm"], db_gm, mode="tn", name="br_gm_dw", tm=256, tn=512, tk=1024, out_dtypes=(F32,))

    du, dv_gm, dws, dbfull, dgain = _gm_bwd(do_gm, sv["rest"], w["gm_gain"], w["w_spatial"], w["bfull"])
    dp, dwbd, dscale = _pool_bwd(do_pool, sv["pooled"], w["wbd"], w["pool_scale"])
    dq, dk, dv = _attn_bwd(sv["qkv"], do_sb)
    dproj = jnp.concatenate([dq, dk.astype(MX), dv.astype(MX), dp, du, dv_gm, dg0, dg1, dg2], axis=1)
    dh = _matmul(dproj, w["w_in"], mode="nt", name="proj_dx", tm=1024, tn=512, tk=W_IN_TILE, out_dtypes=(F32,))
    dw_in = _matmul(sv["h"], dproj, mode="tn", name="proj_dw", tm=1024, tn=W_IN_TILE, tk=1024, out_dtypes=(F32,))
    dx, dg_mix_pre = _rms_bwd(sv["x"], w["g_mix_pre"], dh, res=dx1, out_dtype=F32, name="rms_mix_pre_bwd")

    grads = dict(
        w_in=dw_in, w_br_sb=dw_br_sb, w_br_pool=dw_br_pool, w_br_gm=dw_br_gm, w_out=dw_out,
        w_ff_in=dw_ff_in, w_ff_out=dw_ff_out,
        w_pool=jnp.stack([dwbd[g * GROUP:(g + 1) * GROUP, g * GROUP:(g + 1) * GROUP] for g in range(N_GROUPS)]),
        pool_scale=dscale[0], gm_gain=dgain[0], w_spatial=dws,
        b_spatial=dbfull.reshape(CHUNK, N_GROUPS, GROUP).sum(axis=-1).T,
        g_mix_pre=dg_mix_pre[0], g_mix_post=dg_mix_post[0], g_ff_pre=dg_ff_pre[0], g_ff_post=dg_ff_post[0])
    return dx, grads


def _layer_weights(full, small):
    depth = small["w_pool"].shape[0]
    wbd = jnp.zeros((depth, PW, PW), F32)
    for g in range(N_GROUPS):
        wbd = wbd.at[:, g * GROUP:(g + 1) * GROUP, g * GROUP:(g + 1) * GROUP].set(small["w_pool"][:, g])
    w = dict(full)
    w["wbd"] = wbd.astype(MX)
    w["bfull"] = jnp.repeat(jnp.swapaxes(small["b_spatial"], 1, 2), GROUP, axis=2)
    w["w_spatial"] = small["w_spatial"]
    for n in ("pool_scale", "gm_gain", "g_mix_pre", "g_mix_post", "g_ff_pre", "g_ff_post"):
        w[n] = small[n][:, None, :]
    return w


def _local_step(x, target, full, small):
    w = _layer_weights(full, small)

    def fwd(xc, wl):
        return _layer_fwd(xc, wl)

    x_out, saved = lax.scan(fwd, x, w)
    dy, sq = _loss_head(x_out, target)
    loss = 0.5 * jnp.sum(sq) / x.shape[1]

    def bwd(dxc, args):
        wl, sv = args
        return _layer_bwd(dxc, wl, sv)

    dx, grads = lax.scan(bwd, dy, (w, saved), reverse=True)
    return loss, dx, grads


def _exchange(inp, *, scatter, name):
    block = inp.shape[1:] if scatter else inp.shape

    def body(in_ref, out_ref, send_sems, recv_sems, local_sem):
        x, y, c = lax.axis_index("x"), lax.axis_index("y"), lax.axis_index("c")
        me = 4 * x + 2 * y + c

        def peer(k):
            px = 1 - x if k & 4 else x
            py = 1 - y if k & 2 else y
            pc = 1 - c if k & 1 else c
            return (px, py, pc), 4 * px + 2 * py + pc

        def copy(k):
            dev, idx = peer(k)
            return pltpu.make_async_remote_copy(
                src_ref=in_ref.at[idx] if scatter else in_ref,
                dst_ref=out_ref.at[me],
                send_sem=send_sems.at[k - 1], recv_sem=recv_sems.at[k - 1],
                device_id=dev, device_id_type=pl.DeviceIdType.MESH)

        def arrival(k):
            dev, idx = peer(k)
            return pltpu.make_async_remote_copy(
                src_ref=in_ref.at[idx] if scatter else in_ref,
                dst_ref=out_ref.at[idx],
                send_sem=send_sems.at[k - 1], recv_sem=recv_sems.at[k - 1],
                device_id=dev, device_id_type=pl.DeviceIdType.MESH)

        mine = pltpu.make_async_copy(in_ref.at[me] if scatter else in_ref, out_ref.at[me], local_sem)
        mine.start()
        sends = [copy(k) for k in range(1, N_DEV)]
        for cp in sends:
            cp.start()
        for k in range(1, N_DEV):
            arrival(k).wait_recv()
        for cp in sends:
            cp.wait_send()
        mine.wait()

    return pl.pallas_call(
        body, name=name,
        in_specs=[pl.BlockSpec(memory_space=pl.ANY)],
        out_specs=pl.BlockSpec(memory_space=pl.ANY),
        out_shape=jax.ShapeDtypeStruct((N_DEV,) + tuple(block), inp.dtype),
        scratch_shapes=[pltpu.SemaphoreType.DMA((N_DEV - 1,)), pltpu.SemaphoreType.DMA((N_DEV - 1,)),
                        pltpu.SemaphoreType.DMA],
    )(inp)


def _sum_devices(recv, name):
    _, rows, cols = recv.shape
    tr = rows
    for cand in (512, 256, 128, 64, 32, 16, 8):
        if rows % cand == 0:
            tr = cand
            break

    def body(r_ref, out_ref):
        acc = r_ref[0].astype(F32)
        for d in range(1, N_DEV):
            acc = acc + r_ref[d].astype(F32)
        out_ref[...] = acc

    return pl.pallas_call(
        body, name=name, grid=(rows // tr,),
        in_specs=[pl.BlockSpec((N_DEV, tr, cols), lambda i: (0, i, 0))],
        out_specs=pl.BlockSpec((tr, cols), lambda i: (i, 0)),
        out_shape=jax.ShapeDtypeStruct((rows, cols), F32),
        compiler_params=_params(("parallel",)),
    )(recv)


def _adamw(w, g, m, v, name):
    rows, cols = w.shape
    tr = rows
    for cand in (512, 256, 128, 64, 32, 16, 8):
        if rows % cand == 0:
            tr = cand
            break
    c1 = 1.0 - ADAM_B1 ** ADAM_STEP
    c2 = 1.0 - ADAM_B2 ** ADAM_STEP

    def body(w_ref, g_ref, m_ref, v_ref, d_ref, nm_ref, nv_ref):
        gv = g_ref[...]
        nm = ADAM_B1 * m_ref[...] + (1.0 - ADAM_B1) * gv
        nv = ADAM_B2 * v_ref[...] + (1.0 - ADAM_B2) * (gv * gv)
        nm_ref[...] = nm
        nv_ref[...] = nv
        d_ref[...] = -ADAM_LR * ((nm / c1) / (jnp.sqrt(nv / c2) + ADAM_EPS) + ADAM_WD * w_ref[...])

    spec = pl.BlockSpec((tr, cols), lambda i: (i, 0))
    return pl.pallas_call(
        body, name=name, grid=(rows // tr,),
        in_specs=[spec] * 4, out_specs=[spec] * 3,
        out_shape=[jax.ShapeDtypeStruct((rows, cols), F32)] * 3,
        compiler_params=_params(("parallel",)),
    )(w, g, m, v)


def _pack_rows(flat, dtype):
    n = flat.shape[-1]
    rows = -(-n // PACK_COLS)
    rows = -(-rows // 16) * 16
    pad = rows * PACK_COLS - n
    flat = jnp.pad(flat, [(0, 0)] * (flat.ndim - 1) + [(0, pad)])
    return flat.reshape(flat.shape[:-1] + (rows, PACK_COLS)).astype(dtype)


def _unshard(stacked, axis):
    _, depth, a, b = stacked.shape
    if axis == 2:
        return jnp.transpose(stacked, (1, 2, 0, 3)).reshape(depth, a, N_DEV * b)
    return jnp.transpose(stacked, (1, 0, 2, 3)).reshape(depth, N_DEV * a, b)


def _shard(fullarr, axis):
    depth, a, b = fullarr.shape
    if axis == 2:
        return jnp.transpose(fullarr.reshape(depth, a, N_DEV, b // N_DEV), (2, 0, 1, 3))
    return jnp.transpose(fullarr.reshape(depth, N_DEV, a // N_DEV, b), (1, 0, 2, 3))


def kernel(x, w_in, w_pool, pool_scale, gm_gain, w_spatial, b_spatial, w_br_sb, w_br_pool, w_br_gm, w_out, g_mix_pre, g_mix_post, g_ff_pre, g_ff_post, w_ff_in, w_ff_out, loss_target, m_w_in, m_w_pool, m_pool_scale, m_gm_gain, m_w_spatial, m_b_spatial, m_w_br_sb, m_w_br_pool, m_w_br_gm, m_w_out, m_g_mix_pre, m_g_mix_post, m_g_ff_pre, m_g_ff_post, m_w_ff_in, m_w_ff_out, v_w_in, v_w_pool, v_pool_scale, v_gm_gain, v_w_spatial, v_b_spatial, v_w_br_sb, v_w_br_pool, v_w_br_gm, v_w_out, v_g_mix_pre, v_g_mix_post, v_g_ff_pre, v_g_ff_post, v_w_ff_in, v_w_ff_out):
    args = dict(locals())
    weights = {n: args[n] for n in WEIGHTS}
    mom_m = {n: args["m_" + n] for n in WEIGHTS}
    mom_v = {n: args["v_" + n] for n in WEIGHTS}

    shard_shapes = {n: weights[n].shape for n in BIG}
    sizes = [int(np.prod(shard_shapes[n])) for n in BIG]
    offsets = np.concatenate([[0], np.cumsum(sizes)])
    packed = _pack_rows(jnp.concatenate([weights[n].astype(MX).reshape(-1) for n in BIG]), MX)
    gathered = _exchange(packed, scatter=False, name="gather_weights").reshape(N_DEV, -1)
    full = {}
    for n, lo, hi in zip(BIG, offsets[:-1], offsets[1:]):
        full[n] = _unshard(gathered[:, lo:hi].reshape((N_DEV,) + shard_shapes[n]), BIG_SHARD_AXIS[n])
    small = {n: weights[n] for n in SMALL}

    loss, dx, grads = _local_step(x[0], loss_target[0], full, small)
    loss = lax.psum(loss, ("x", "y", "c"))

    by_dest = jnp.concatenate([_shard(grads[n], BIG_SHARD_AXIS[n]).reshape(N_DEV, -1) for n in BIG], axis=1)
    recv = _exchange(_pack_rows(by_dest, MX), scatter=True, name="scatter_grads")
    g_big = _sum_devices(recv, "sum_grads").reshape(-1)
    small_sizes = [int(np.prod(weights[n].shape)) for n in SMALL]
    small_off = np.concatenate([[0], np.cumsum(small_sizes)])
    small_flat = jnp.concatenate([grads[n].reshape(-1) for n in SMALL])
    g_small = _sum_devices(_exchange(_pack_rows(small_flat, F32), scatter=False, name="gather_small_grads"),
                           "sum_small_grads").reshape(-1)

    grad_w = {}
    for n, lo, hi in zip(BIG, offsets[:-1], offsets[1:]):
        grad_w[n] = g_big[lo:hi].reshape(shard_shapes[n])
    for n, lo, hi in zip(SMALL, small_off[:-1], small_off[1:]):
        grad_w[n] = g_small[lo:hi].reshape(weights[n].shape)

    delta, new_m, new_v = {}, {}, {}
    for n in WEIGHTS:
        shape = weights[n].shape
        view = (-1, shape[-1])
        d, nm, nv = _adamw(weights[n].reshape(view), grad_w[n].reshape(view), mom_m[n].reshape(view),
                           mom_v[n].reshape(view), "adamw_" + n)
        delta[n], new_m[n], new_v[n] = d.reshape(shape), nm.reshape(shape), nv.reshape(shape)

    return (loss, dx[None], *[grad_w[n] for n in WEIGHTS], *[delta[n] for n in WEIGHTS],
            *[new_m[n] for n in WEIGHTS], *[new_v[n] for n in WEIGHTS])
```

```python
import functools

import numpy as np
import jax
import jax.numpy as jnp
from jax import lax
from jax.experimental import pallas as pl
from jax.experimental.pallas import tpu as pltpu

F32 = jnp.float32
MX = jnp.bfloat16

D = 1024
SB_W = 512
HEAD_PAIR = 128
PW = 256
GROUP = 64
N_GROUPS = 4
CHUNK = 128
POOL_WINDOWS = (2, 4, 8, 16)
D_FF = 4096
D_IN = 5376
QKV_W = 3 * SB_W
REST_W = D_IN - QKV_W
GATE_COL0 = 3 * PW
RMS_EPS = 1e-6
N_DEV = 8
DEPTH = 4

ADAM_LR = 0.001
ADAM_B1 = 0.9
ADAM_B2 = 0.999
ADAM_EPS = 1e-08
ADAM_WD = 0.01
ADAM_STEP = 10

SKIP_LOG = -120.0

VMEM_LIMIT = 48 * 1024 * 1024
W_IN_TILE = 768
PACK_COLS = 1024

BIG = ("w_in", "w_br_sb", "w_br_pool", "w_br_gm", "w_out", "w_ff_in", "w_ff_out")
BIG_SHARD_AXIS = {"w_in": 2, "w_br_sb": 2, "w_br_pool": 2, "w_br_gm": 2, "w_out": 1, "w_ff_in": 2, "w_ff_out": 1}
SMALL = ("w_pool", "pool_scale", "gm_gain", "w_spatial", "b_spatial",
         "g_mix_pre", "g_mix_post", "g_ff_pre", "g_ff_post")
WEIGHTS = ("w_in", "w_pool", "pool_scale", "gm_gain", "w_spatial", "b_spatial", "w_br_sb", "w_br_pool",
           "w_br_gm", "w_out", "g_mix_pre", "g_mix_post", "g_ff_pre", "g_ff_post", "w_ff_in", "w_ff_out")


def _params(sem, vmem=VMEM_LIMIT):
    return pltpu.CompilerParams(dimension_semantics=sem, vmem_limit_bytes=vmem)


def _dot(a, b, dims):
    return lax.dot_general(a, b, (dims, ((), ())), preferred_element_type=F32)


NN = ((1,), (0,))
NT = ((1,), (1,))
TN = ((0,), (0,))


def _split_dot(x, m, left, nsplit):
    acc = None
    r = x
    for s in range(nsplit):
        p = r.astype(jnp.bfloat16)
        d = _dot(m, p, NN) if left else _dot(p, m, NN)
        acc = d if acc is None else acc + d
        if s + 1 < nsplit:
            r = r - p.astype(F32)
    return acc


def _matmul(a, b, *, mode, name, tm, tn, tk, out_dtypes, b_col0=0, n_out=None, epilogue=None, extras=()):
    if mode == "nn":
        m, kdim = a.shape
        n = b.shape[1] if n_out is None else n_out
        dims = NN
    elif mode == "nt":
        m, kdim = a.shape
        n = b.shape[0]
        dims = NT
    else:
        kdim, m = a.shape
        n = b.shape[1]
        dims = TN
    tm, tn, tk = min(tm, m), min(tn, n), min(tk, kdim)
    assert m % tm == 0 and n % tn == 0 and kdim % tk == 0 and b_col0 % tn == 0, (name, m, n, kdim)
    if mode == "nn":
        a_spec = pl.BlockSpec((tm, tk), lambda i, j, k: (i, k))
        b_spec = pl.BlockSpec((tk, tn), lambda i, j, k: (k, j + b_col0 // tn))
    elif mode == "nt":
        a_spec = pl.BlockSpec((tm, tk), lambda i, j, k: (i, k))
        b_spec = pl.BlockSpec((tn, tk), lambda i, j, k: (j, k))
    else:
        a_spec = pl.BlockSpec((tk, tm), lambda i, j, k: (k, i))
        b_spec = pl.BlockSpec((tk, tn), lambda i, j, k: (k, j))
    nk = kdim // tk
    n_extra = len(extras)
    n_outs = len(out_dtypes)
    o_spec = pl.BlockSpec((tm, tn), lambda i, j, k: (i, j))

    def body(a_ref, b_ref, *refs):
        extra_refs = refs[:n_extra]
        out_refs = refs[n_extra:n_extra + n_outs]
        acc_ref = refs[-1]
        k = pl.program_id(2)

        @pl.when(k == 0)
        def _():
            acc_ref[...] = jnp.zeros_like(acc_ref)

        acc_ref[...] += _dot(a_ref[...].astype(MX), b_ref[...].astype(MX), dims)

        @pl.when(k == nk - 1)
        def _():
            acc = acc_ref[...]
            outs = (acc,) if epilogue is None else epilogue(acc, *[e[...] for e in extra_refs])
            for o_ref, val in zip(out_refs, outs):
                o_ref[...] = val.astype(o_ref.dtype)

    outs = pl.pallas_call(
        body, name=name,
        grid=(m // tm, n // tn, nk),
        in_specs=[a_spec, b_spec] + [o_spec] * n_extra,
        out_specs=[o_spec] * n_outs,
        out_shape=[jax.ShapeDtypeStruct((m, n), dt) for dt in out_dtypes],
        scratch_shapes=[pltpu.VMEM((tm, tn), F32)],
        compiler_params=_params(("parallel", "parallel", "arbitrary")),
    )(a, b, *extras)
    return outs[0] if n_outs == 1 else outs


ROW_TILE = 512


def _rms_fwd(x, g, *, res, out_dtype, name):
    s, d = x.shape
    tr = min(ROW_TILE, s)
    has_res = res is not None

    def body(x_ref, g_ref, *refs):
        out_ref = refs[-1]
        xv = x_ref[...]
        y = xv * lax.rsqrt(jnp.mean(xv * xv, axis=-1, keepdims=True) + RMS_EPS) * g_ref[...]
        if has_res:
            y = refs[0][...] + y
        out_ref[...] = y.astype(out_ref.dtype)

    row = pl.BlockSpec((tr, d), lambda i: (i, 0))
    vec = pl.BlockSpec((1, d), lambda i: (0, 0))
    return pl.pallas_call(
        body, name=name, grid=(s // tr,),
        in_specs=[row, vec] + ([row] if has_res else []),
        out_specs=row, out_shape=jax.ShapeDtypeStruct((s, d), out_dtype),
        compiler_params=_params(("parallel",)),
    )(x, g, *([res] if has_res else []))


def _rms_bwd(x, g, dout, *, res, out_dtype, name):
    s, d = x.shape
    tr = min(ROW_TILE, s)
    has_res = res is not None

    def body(x_ref, g_ref, do_ref, *refs):
        dx_ref, dg_ref = refs[-2], refs[-1]
        i = pl.program_id(0)
        xv = x_ref[...]
        do = do_ref[...]
        r = lax.rsqrt(jnp.mean(xv * xv, axis=-1, keepdims=True) + RMS_EPS)
        xhat = xv * r
        dxhat = do * g_ref[...]
        dx = r * (dxhat - xhat * jnp.mean(dxhat * xhat, axis=-1, keepdims=True))
        if has_res:
            dx = refs[0][...] + dx
        dx_ref[...] = dx.astype(dx_ref.dtype)

        @pl.when(i == 0)
        def _():
            dg_ref[...] = jnp.zeros_like(dg_ref)

        dg_ref[...] += jnp.sum(do * xhat, axis=0, keepdims=True)

    row = pl.BlockSpec((tr, d), lambda i: (i, 0))
    vec = pl.BlockSpec((1, d), lambda i: (0, 0))
    return pl.pallas_call(
        body, name=name, grid=(s // tr,),
        in_specs=[row, vec, row] + ([row] if has_res else []),
        out_specs=[row, vec],
        out_shape=[jax.ShapeDtypeStruct((s, d), out_dtype), jax.ShapeDtypeStruct((1, d), F32)],
        compiler_params=_params(("arbitrary",)),
    )(x, g, dout, *([res] if has_res else []))


def _loss_head(y, target):
    s, d = y.shape
    tr = min(ROW_TILE, s)

    def body(y_ref, t_ref, dy_ref, sq_ref):
        i = pl.program_id(0)
        err = y_ref[...] - t_ref[...]
        dy_ref[...] = err * (1.0 / d)

        @pl.when(i == 0)
        def _():
            sq_ref[...] = jnp.zeros_like(sq_ref)

        sq_ref[...] += jnp.sum(err * err, axis=0, keepdims=True)

    row = pl.BlockSpec((tr, d), lambda i: (i, 0))
    vec = pl.BlockSpec((1, d), lambda i: (0, 0))
    return pl.pallas_call(
        body, name="loss_head", grid=(s // tr,),
        in_specs=[row, row], out_specs=[row, vec],
        out_shape=[jax.ShapeDtypeStruct((s, d), F32), jax.ShapeDtypeStruct((1, d), F32)],
        compiler_params=_params(("arbitrary",)),
    )(y, target)


def _sb_masks():
    row = lax.broadcasted_iota(jnp.int32, (CHUNK, CHUNK), 0)
    col = lax.broadcasted_iota(jnp.int32, (CHUNK, CHUNK), 1)
    return row, col


SB_WINDOW = 2 * CHUNK

def _sb_consts():
    row = lax.broadcasted_iota(jnp.int32, (CHUNK, SB_WINDOW), 0)
    col = lax.broadcasted_iota(jnp.int32, (CHUNK, SB_WINDOW), 1)
    lane_hi = lax.broadcasted_iota(jnp.int32, (CHUNK, HEAD_PAIR), 1) >= GROUP
    return col - row, col, lane_hi


def _sb_window(i, m, dcol, col):
    start = jnp.maximum((i - 2 * m - 1) * CHUNK, 0)
    hi = (i - 2 * m + 1) * CHUNK
    keep = jnp.logical_and(dcol < i * CHUNK - start, col < hi - start)
    return pl.multiple_of(start, CHUNK), keep


def _sb_scores(qm, kw, keep):
    z = _dot(qm, kw, NT)
    t = jnp.log1p(jnp.exp(-jnp.abs(z)))
    ln = -(jnp.maximum(z, 0.0) + t)
    lb = ln + z
    ln = jnp.where(keep, ln, 0.0)
    return ln, lb


def _sb_tri(upper):
    r = np.arange(SB_WINDOW)
    m = (r[:, None] > r[None, :]) if upper else (r[:, None] < r[None, :])
    return jnp.asarray(m, jnp.bfloat16)


def _attn_fwd(qkv):
    s = qkv.shape[0]
    nblk = s // CHUNK
    n_pairs = SB_W // HEAD_PAIR

    def body(q_ref, k_ref, v_ref, up_ref, o_ref):
        i = pl.program_id(1)
        dcol, col, lane_hi = _sb_consts()
        q = q_ref[...]
        zero = jnp.zeros_like(q)
        qms = (jnp.where(lane_hi, zero, q) * 0.125, jnp.where(lane_hi, q, zero) * 0.125)

        def cond(st):
            return jnp.logical_and(st[0] <= i // 2, st[1])

        def loop(st):
            m = st[0]
            start, keep = _sb_window(i, m, dcol, col)
            kw = k_ref[pl.ds(start, SB_WINDOW), :]
            vw = v_ref[pl.ds(start, SB_WINDOW), :]
            new = []
            for h in range(2):
                c, acc = st[2 + 2 * h], st[3 + 2 * h]
                ln, lb = _sb_scores(qms[h], kw, keep)
                suffix = _split_dot(ln, up_ref[...], False, 2) + c
                a = jnp.where(keep, jnp.exp(lb + suffix), 0.0)
                new += [c + jnp.sum(ln, axis=1, keepdims=True), acc + _dot(a.astype(MX), vw, NN)]
            active = jnp.maximum(jnp.max(new[0]), jnp.max(new[2])) > SKIP_LOG
            return (m + 1, active, *new)

        c0 = jnp.zeros((CHUNK, 1), F32)
        a0 = jnp.zeros((CHUNK, HEAD_PAIR), F32)
        st = lax.while_loop(cond, loop, (i * 0, i >= 0, c0, a0, c0, a0))
        o_ref[...] = jnp.where(lane_hi, st[5], st[3]).astype(o_ref.dtype)

    return pl.pallas_call(
        body, name="attn_fwd", grid=(n_pairs, nblk),
        in_specs=[pl.BlockSpec((CHUNK, HEAD_PAIR), lambda p, i: (i, p)),
                  pl.BlockSpec((s, HEAD_PAIR), lambda p, i: (0, n_pairs + p)),
                  pl.BlockSpec((s, HEAD_PAIR), lambda p, i: (0, 2 * n_pairs + p)),
                  pl.BlockSpec((SB_WINDOW, SB_WINDOW), lambda p, i: (0, 0))],
        out_specs=pl.BlockSpec((CHUNK, HEAD_PAIR), lambda p, i: (i, p)),
        out_shape=jax.ShapeDtypeStruct((s, SB_W), MX),
        compiler_params=_params(("parallel", "parallel")),
    )(qkv, qkv, qkv, _sb_tri(True))


def _attn_bwd(qkv, d_o):
    s = qkv.shape[0]
    nblk = s // CHUNK
    n_pairs = SB_W // HEAD_PAIR
    n_win = nblk // 2 + 1

    def body(q_ref, k_ref, v_ref, do_ref, up_ref, lo_ref, dq_ref, dk_ref, dv_ref, e_s, sig_s):
        i = pl.program_id(1)
        dcol, col, lane_hi = _sb_consts()
        q = q_ref[...]
        do = do_ref[...]
        zero = jnp.zeros_like(q)
        qms = (jnp.where(lane_hi, zero, q) * 0.125, jnp.where(lane_hi, q, zero) * 0.125)
        doms = (jnp.where(lane_hi, zero, do), jnp.where(lane_hi, do, zero))

        @pl.when(i == 0)
        def _():
            dk_ref[...] = jnp.zeros_like(dk_ref)
            dv_ref[...] = jnp.zeros_like(dv_ref)

        def cond(st):
            return jnp.logical_and(st[0] <= i // 2, st[1])

        def loop(st):
            m = st[0]
            start, keep = _sb_window(i, m, dcol, col)
            kw = k_ref[pl.ds(start, SB_WINDOW), :]
            vw = v_ref[pl.ds(start, SB_WINDOW), :]
            cs = []
            dv = None
            for h in range(2):
                c = st[2 + h]
                ln, lb = _sb_scores(qms[h], kw, keep)
                suffix = _split_dot(ln, up_ref[...], False, 2) + c
                a = jnp.where(keep, jnp.exp(lb + suffix), 0.0)
                e_s[h, m] = a * _dot(doms[h], vw, NT)
                sig_s[h, m] = jnp.exp(lb)
                part = _dot(a.astype(MX), doms[h], TN)
                dv = part if dv is None else dv + part
                cs.append(c + jnp.sum(ln, axis=1, keepdims=True))
            dv_ref[pl.ds(start, SB_WINDOW), :] += dv
            active = jnp.maximum(jnp.max(cs[0]), jnp.max(cs[1])) > SKIP_LOG
            return (m + 1, active, *cs)

        c0 = jnp.zeros((CHUNK, 1), F32)
        n_seen = lax.while_loop(cond, loop, (i * 0, i >= 0, c0, c0))[0]

        def up(t, st):
            m = n_seen - 1 - t
            start, keep = _sb_window(i, m, dcol, col)
            kw = k_ref[pl.ds(start, SB_WINDOW), :]
            new = []
            dk = None
            for h in range(2):
                cp, dq = st[2 * h], st[2 * h + 1]
                e = e_s[h, m]
                sig = sig_s[h, m]
                prefix = _split_dot(e, lo_ref[...], False, 2) + cp
                dz = jnp.where(keep, e * (1.0 - sig) - prefix * sig, 0.0).astype(MX)
                part = _dot(dz, qms[h], TN)
                dk = part if dk is None else dk + part
                new += [cp + jnp.sum(e, axis=1, keepdims=True), dq + _dot(dz, kw, NN)]
            dk_ref[pl.ds(start, SB_WINDOW), :] += dk
            return tuple(new)

        a0 = jnp.zeros((CHUNK, HEAD_PAIR), F32)
        st = lax.fori_loop(0, n_seen, up, (c0, a0, c0, a0))
        dq_ref[...] = (jnp.where(lane_hi, st[3], st[1]) * 0.125).astype(dq_ref.dtype)

    blk = pl.BlockSpec((CHUNK, HEAD_PAIR), lambda p, i: (i, p))
    full = pl.BlockSpec((s, HEAD_PAIR), lambda p, i: (0, p))
    tri = pl.BlockSpec((SB_WINDOW, SB_WINDOW), lambda p, i: (0, 0))
    return pl.pallas_call(
        body, name="attn_bwd", grid=(n_pairs, nblk),
        in_specs=[blk,
                  pl.BlockSpec((s, HEAD_PAIR), lambda p, i: (0, n_pairs + p)),
                  pl.BlockSpec((s, HEAD_PAIR), lambda p, i: (0, 2 * n_pairs + p)),
                  blk, tri, tri],
        out_specs=[blk, full, full],
        out_shape=[jax.ShapeDtypeStruct((s, SB_W), MX), jax.ShapeDtypeStruct((s, SB_W), F32),
                   jax.ShapeDtypeStruct((s, SB_W), F32)],
        scratch_shapes=[pltpu.VMEM((2, n_win, CHUNK, SB_WINDOW), F32),
                        pltpu.VMEM((2, n_win, CHUNK, SB_WINDOW), F32)],
        compiler_params=_params(("parallel", "arbitrary"), 56 * 1024 * 1024),
    )(qkv, qkv, qkv, d_o, _sb_tri(True), _sb_tri(False))


POOL_TILE = 256
POOL_HALO = 128


def _pool_bands(tile, transpose):
    cur = np.zeros((N_GROUPS, tile, tile), np.float32)
    halo = np.zeros((N_GROUPS, tile, POOL_HALO), np.float32)
    t = np.arange(tile)[:, None]
    for g, w in enumerate(POOL_WINDOWS):
        if not transpose:
            p = np.arange(tile)[None, :]
            cur[g] = ((t - p >= 0) & (t - p < w))
            ph = np.arange(POOL_HALO)[None, :] - POOL_HALO
            halo[g] = (t - ph < w)
        else:
            p = np.arange(tile)[None, :]
            cur[g] = ((p - t >= 0) & (p - t < w))
            ph = np.arange(POOL_HALO)[None, :] + tile
            halo[g] = (ph - t < w)
    return jnp.asarray(cur, jnp.bfloat16), jnp.asarray(halo, jnp.bfloat16)


def _pool_count(i, tile, rows, row0):
    pos = (i * tile + row0 + lax.broadcasted_iota(jnp.int32, (rows, PW), 0)).astype(F32)
    grp = lax.broadcasted_iota(jnp.int32, (rows, PW), 1) // GROUP
    win = jnp.where(grp == 0, float(POOL_WINDOWS[0]),
                    jnp.where(grp == 1, float(POOL_WINDOWS[1]),
                              jnp.where(grp == 2, float(POOL_WINDOWS[2]), float(POOL_WINDOWS[3]))))
    return jnp.minimum(pos + 1.0, win), grp


def _pool_fwd(rest, wbd, scale):
    s = rest.shape[0]
    tile = min(POOL_TILE, s)
    halo_per_tile = tile // POOL_HALO
    bcur, bhalo = _pool_bands(tile, False)

    def body(cur_ref, prev_ref, bcur_ref, bhalo_ref, wbd_ref, scale_ref, pooled_ref, out_ref):
        i = pl.program_id(0)
        cur = cur_ref[...]
        prev = jnp.where(i > 0, prev_ref[...], 0.0)
        count, grp = _pool_count(i, tile, tile, 0)
        win = jnp.zeros((tile, PW), F32)
        for g in range(N_GROUPS):
            wsum = _split_dot(cur, bcur_ref[g], True, 3) + _split_dot(prev, bhalo_ref[g], True, 3)
            win = jnp.where(grp == g, wsum, win)
        pooled = (win / count - cur).astype(MX)
        pooled_ref[...] = pooled
        out_ref[...] = (_dot(pooled, wbd_ref[...], NN) * scale_ref[...]).astype(out_ref.dtype)

    tile_spec = pl.BlockSpec((tile, PW), lambda i: (i, 0))
    const3 = lambda shape: pl.BlockSpec(shape, lambda i: (0, 0, 0))
    const2 = lambda shape: pl.BlockSpec(shape, lambda i: (0, 0))
    return pl.pallas_call(
        body, name="pool_fwd", grid=(s // tile,),
        in_specs=[tile_spec,
                  pl.BlockSpec((POOL_HALO, PW), lambda i: (jnp.maximum(i * halo_per_tile - 1, 0), 0)),
                  const3(bcur.shape), const3(bhalo.shape), const2((PW, PW)), const2((1, PW))],
        out_specs=[tile_spec, tile_spec],
        out_shape=[jax.ShapeDtypeStruct((s, PW), MX), jax.ShapeDtypeStruct((s, PW), MX)],
        compiler_params=_params(("parallel",)),
    )(rest, rest, bcur, bhalo, wbd, scale)


def _pool_bwd(d_o, pooled, wbd, scale):
    s = d_o.shape[0]
    tile = min(POOL_TILE, s)
    halo_per_tile = tile // POOL_HALO
    n_halo = s // POOL_HALO
    n_tiles = s // tile
    ccur, chalo = _pool_bands(tile, True)

    def body(do_ref, nxt_ref, pooled_ref, ccur_ref, chalo_ref, wbd_ref, scale_ref, dp_ref, dw_ref, ds_ref):
        i = pl.program_id(0)
        do = do_ref[...]
        w = wbd_ref[...]
        sc = scale_ref[...]
        pooled_v = pooled_ref[...]

        @pl.when(i == 0)
        def _():
            dw_ref[...] = jnp.zeros_like(dw_ref)
            ds_ref[...] = jnp.zeros_like(ds_ref)

        ds_ref[...] += jnp.sum(do * _dot(pooled_v, w, NN), axis=0, keepdims=True)
        dmixed = (do * sc).astype(MX)
        dw_ref[...] += _dot(pooled_v, dmixed, TN)
        dpooled = _dot(dmixed, w, NT)
        nxt = jnp.where(i < n_tiles - 1, nxt_ref[...], 0.0)
        dpooled_n = _dot((nxt * sc).astype(MX), w, NT)
        count, grp = _pool_count(i, tile, tile, 0)
        count_n, _ = _pool_count(i, tile, POOL_HALO, tile)
        dq = dpooled / count
        dq_n = dpooled_n / count_n
        acc = jnp.zeros((tile, PW), F32)
        for g in range(N_GROUPS):
            wsum = _split_dot(dq, ccur_ref[g], True, 3) + _split_dot(dq_n, chalo_ref[g], True, 3)
            acc = jnp.where(grp == g, wsum, acc)
        dp_ref[...] = (acc - dpooled).astype(dp_ref.dtype)

    tile_spec = pl.BlockSpec((tile, PW), lambda i: (i, 0))
    const3 = lambda shape: pl.BlockSpec(shape, lambda i: (0, 0, 0))
    const2 = lambda shape: pl.BlockSpec(shape, lambda i: (0, 0))
    return pl.pallas_call(
        body, name="pool_bwd", grid=(n_tiles,),
        in_specs=[tile_spec,
                  pl.BlockSpec((POOL_HALO, PW), lambda i: (jnp.minimum((i + 1) * halo_per_tile, n_halo - 1), 0)),
                  tile_spec, const3(ccur.shape), const3(chalo.shape), const2((PW, PW)), const2((1, PW))],
        out_specs=[tile_spec, const2((PW, PW)), const2((1, PW))],
        out_shape=[jax.ShapeDtypeStruct((s, PW), MX), jax.ShapeDtypeStruct((PW, PW), F32),
                   jax.ShapeDtypeStruct((1, PW), F32)],
        compiler_params=_params(("arbitrary",)),
    )(d_o, d_o, pooled, ccur, chalo, wbd, scale)


GM_TILE = 512
GELU_C = 0.7978845608028654
GELU_A = 0.044715


def _gelu(x):
    return 0.5 * x * (1.0 + jnp.tanh(GELU_C * (x + GELU_A * (x * x * x))))


def _gelu_grad(x):
    th = jnp.tanh(GELU_C * (x + GELU_A * (x * x * x)))
    return 0.5 * (1.0 + th) + 0.5 * x * (1.0 - th * th) * (GELU_C * (1.0 + 3.0 * GELU_A * (x * x)))


def _gm_common(ws_ref):
    row, col = _sb_masks()
    tril = row >= col
    wsm = [jnp.where(tril, ws_ref[g], 0.0).astype(MX) for g in range(N_GROUPS)]
    grp = lax.broadcasted_iota(jnp.int32, (CHUNK, PW), 1) // GROUP
    return tril, wsm, grp


def _gm_mix(wsm, grp, vn_c):
    mixed = jnp.zeros((CHUNK, PW), F32)
    for g in range(N_GROUPS):
        mixed = jnp.where(grp == g, _dot(wsm[g], vn_c, NN), mixed)
    return mixed


def _gm_fwd(rest, gain, ws, bfull):
    s = rest.shape[0]
    tile = min(GM_TILE, s)

    def body(u_ref, v_ref, gain_ref, ws_ref, b_ref, out_ref):
        _, wsm, grp = _gm_common(ws_ref)
        bias = b_ref[...]
        for n in range(tile // CHUNK):
            rows = slice(n * CHUNK, (n + 1) * CHUNK)
            gu = _gelu(u_ref[rows, :])
            gv = _gelu(v_ref[rows, :])
            r = lax.rsqrt(jnp.mean(gv * gv, axis=-1, keepdims=True) + RMS_EPS)
            vn = (gv * r * gain_ref[...]).astype(MX)
            out_ref[rows, :] = (gu * (_gm_mix(wsm, grp, vn) + bias)).astype(out_ref.dtype)

    const2 = lambda shape: pl.BlockSpec(shape, lambda i: (0, 0))
    return pl.pallas_call(
        body, name="gm_fwd", grid=(s // tile,),
        in_specs=[pl.BlockSpec((tile, PW), lambda i: (i, 1)), pl.BlockSpec((tile, PW), lambda i: (i, 2)),
                  const2((1, PW)), pl.BlockSpec((N_GROUPS, CHUNK, CHUNK), lambda i: (0, 0, 0)),
                  const2((CHUNK, PW))],
        out_specs=pl.BlockSpec((tile, PW), lambda i: (i, 0)),
        out_shape=jax.ShapeDtypeStruct((s, PW), MX),
        compiler_params=_params(("parallel",)),
    )(rest, rest, gain, ws, bfull)


def _gm_bwd(d_o, rest, gain, ws, bfull):
    s = rest.shape[0]
    tile = min(GM_TILE, s)

    def body(do_ref, u_ref, v_ref, gain_ref, ws_ref, b_ref, du_ref, dv_ref, dws_ref, db_ref, dgain_ref):
        i = pl.program_id(0)
        tril, wsm, grp = _gm_common(ws_ref)
        bias = b_ref[...]
        gain_v = gain_ref[...]

        @pl.when(i == 0)
        def _():
            dws_ref[...] = jnp.zeros_like(dws_ref)
            db_ref[...] = jnp.zeros_like(db_ref)
            dgain_ref[...] = jnp.zeros_like(dgain_ref)

        for n in range(tile // CHUNK):
            rows = slice(n * CHUNK, (n + 1) * CHUNK)
            u = u_ref[rows, :]
            v = v_ref[rows, :]
            do = do_ref[rows, :]
            gu = _gelu(u)
            gv = _gelu(v)
            r = lax.rsqrt(jnp.mean(gv * gv, axis=-1, keepdims=True) + RMS_EPS)
            vhat = gv * r
            vn = (vhat * gain_v).astype(MX)
            mixed = _gm_mix(wsm, grp, vn)
            du_ref[rows, :] = (do * (mixed + bias) * _gelu_grad(u)).astype(du_ref.dtype)
            dmix = do * gu
            db_ref[...] += dmix
            dmix_mx = dmix.astype(MX)
            dvn = jnp.zeros((CHUNK, PW), F32)
            for g in range(N_GROUPS):
                dmg = jnp.where(grp == g, dmix_mx, jnp.zeros_like(dmix_mx))
                dws_ref[g] += jnp.where(tril, _dot(dmg, vn, NT), 0.0)
                dvn = jnp.where(grp == g, _dot(wsm[g], dmix_mx, TN), dvn)
            dgain_ref[...] += jnp.sum(dvn * vhat, axis=0, keepdims=True)
            dvhat = dvn * gain_v
            dgv = r * (dvhat - vhat * jnp.mean(dvhat * vhat, axis=-1, keepdims=True))
            dv_ref[rows, :] = (dgv * _gelu_grad(v)).astype(dv_ref.dtype)

    const2 = lambda shape: pl.BlockSpec(shape, lambda i: (0, 0))
    ws_spec = pl.BlockSpec((N_GROUPS, CHUNK, CHUNK), lambda i: (0, 0, 0))
    tile_spec = pl.BlockSpec((tile, PW), lambda i: (i, 0))
    return pl.pallas_call(
        body, name="gm_bwd", grid=(s // tile,),
        in_specs=[tile_spec, pl.BlockSpec((tile, PW), lambda i: (i, 1)), pl.BlockSpec((tile, PW), lambda i: (i, 2)),
                  const2((1, PW)), ws_spec, const2((CHUNK, PW))],
        out_specs=[tile_spec, tile_spec, ws_spec, const2((CHUNK, PW)), const2((1, PW))],
        out_shape=[jax.ShapeDtypeStruct((s, PW), MX), jax.ShapeDtypeStruct((s, PW), MX),
                   jax.ShapeDtypeStruct((N_GROUPS, CHUNK, CHUNK), F32),
                   jax.ShapeDtypeStruct((CHUNK, PW), F32), jax.ShapeDtypeStruct((1, PW), F32)],
        compiler_params=_params(("arbitrary",)),
    )(d_o, rest, rest, gain, ws, bfull)


GATE_TM = 512
GATE_TN = 256


def _gate_specs(s):
    tm = min(GATE_TM, s)
    tn = GATE_TN
    gate0 = GATE_COL0 // tn
    per = D // tn
    ins = [pl.BlockSpec((tm, SB_W), lambda i, j: (i, 0)),
           pl.BlockSpec((tm, PW), lambda i, j: (i, 0)),
           pl.BlockSpec((tm, PW), lambda i, j: (i, 0)),
           pl.BlockSpec((SB_W, tn), lambda i, j: (0, j)),
           pl.BlockSpec((PW, tn), lambda i, j: (0, j)),
           pl.BlockSpec((PW, tn), lambda i, j: (0, j)),
           pl.BlockSpec((tm, tn), lambda i, j: (i, gate0 + j)),
           pl.BlockSpec((tm, tn), lambda i, j: (i, gate0 + per + j)),
           pl.BlockSpec((tm, tn), lambda i, j: (i, gate0 + 2 * per + j))]
    return tm, tn, ins, pl.BlockSpec((tm, tn), lambda i, j: (i, j))


def _gate_fwd(o_sb, o_pool, o_gm, w_sb, w_pool, w_gm, rest):
    s = rest.shape[0]
    tm, tn, ins, out = _gate_specs(s)

    def body(o0, o1, o2, w0, w1, w2, g0, g1, g2, merged_ref):
        acc = jax.nn.sigmoid(g0[...]) * _dot(o0[...], w0[...], NN)
        acc += jax.nn.sigmoid(g1[...]) * _dot(o1[...], w1[...], NN)
        acc += jax.nn.sigmoid(g2[...]) * _dot(o2[...], w2[...], NN)
        merged_ref[...] = acc.astype(merged_ref.dtype)

    return pl.pallas_call(
        body, name="gate_fwd", grid=(s // tm, D // tn),
        in_specs=ins, out_specs=out, out_shape=jax.ShapeDtypeStruct((s, D), MX),
        compiler_params=_params(("parallel", "parallel")),
    )(o_sb, o_pool, o_gm, w_sb, w_pool, w_gm, rest, rest, rest)


def _gate_bwd(o_sb, o_pool, o_gm, w_sb, w_pool, w_gm, rest, dmerged):
    s = rest.shape[0]
    tm, tn, ins, out = _gate_specs(s)

    def body(o0, o1, o2, w0, w1, w2, g0, g1, g2, dm_ref, db0, db1, db2, dg0, dg1, dg2):
        dm = dm_ref[...]
        for o, w, g, db, dg in ((o0, w0, g0, db0, dg0), (o1, w1, g1, db1, dg1), (o2, w2, g2, db2, dg2)):
            sg = jax.nn.sigmoid(g[...])
            db[...] = (dm * sg).astype(db.dtype)
            dg[...] = (dm * _dot(o[...], w[...], NN) * (sg * (1.0 - sg))).astype(dg.dtype)

    return pl.pallas_call(
        body, name="gate_bwd", grid=(s // tm, D // tn),
        in_specs=ins + [out], out_specs=[out] * 6,
        out_shape=[jax.ShapeDtypeStruct((s, D), MX)] * 6,
        compiler_params=_params(("parallel", "parallel")),
    )(o_sb, o_pool, o_gm, w_sb, w_pool, w_gm, rest, rest, rest, dmerged)


def _relu2(acc):
    r = jnp.maximum(acc, 0.0)
    return (r * r,)


def _relu2_bwd(acc, r):
    return (acc * (2.0 * jnp.sqrt(r.astype(F32))),)


TILES = {
    "proj_qkv": (1024, W_IN_TILE, 1024), "proj_rest": (1024, W_IN_TILE, 1024),
    "out_proj": (1024, 1024, 1024), "ff_in": (1024, 1024, 1024), "ff_out": (1024, 1024, 512),
    "ff_out_dx": (1024, 1024, 1024), "ff_out_dw": (1024, 1024, 512),
    "ff_in_dx": (1024, 1024, 512), "ff_in_dw": (1024, 1024, 512),
    "out_proj_dx": (1024, 1024, 1024), "out_proj_dw": (1024, 1024, 512),
    "br_sb_dx": (1024, 512, 1024), "br_sb_dw": (512, 1024, 512),
    "br_pool_dx": (1024, 256, 1024), "br_pool_dw": (256, 1024, 512),
    "br_gm_dx": (1024, 256, 1024), "br_gm_dw": (256, 1024, 512),
    "proj_dx": (1024, 1024, W_IN_TILE), "proj_dw": (1024, W_IN_TILE, 512),
}


def _mm(name, a, b, mode, out_dtypes, **kw):
    tm, tn, tk = TILES[name]
    return _matmul(a, b, mode=mode, name=name, tm=tm, tn=tn, tk=tk, out_dtypes=out_dtypes, **kw)


def _layer_fwd(x, w):
    h = _rms_fwd(x, w["g_mix_pre"], res=None, out_dtype=MX, name="rms_mix_pre")
    qkv = _mm("proj_qkv", h, w["w_in"], "nn", (MX,), n_out=QKV_W)
    rest = _mm("proj_rest", h, w["w_in"], "nn", (F32,), b_col0=QKV_W, n_out=REST_W)
    o_sb = _attn_fwd(qkv)
    pooled, o_pool = _pool_fwd(rest, w["wbd"], w["pool_scale"])
    o_gm = _gm_fwd(rest, w["gm_gain"], w["w_spatial"], w["bfull"])
    merged = _gate_fwd(o_sb, o_pool, o_gm, w["w_br_sb"], w["w_br_pool"], w["w_br_gm"], rest)
    y = _mm("out_proj", merged, w["w_out"], "nn", (F32,))
    x1 = _rms_fwd(y, w["g_mix_post"], res=x, out_dtype=F32, name="rms_mix_post")
    h2 = _rms_fwd(x1, w["g_ff_pre"], res=None, out_dtype=MX, name="rms_ff_pre")
    r = _mm("ff_in", h2, w["w_ff_in"], "nn", (MX,), epilogue=_relu2)
    ff = _mm("ff_out", r, w["w_ff_out"], "nn", (F32,))
    x2 = _rms_fwd(ff, w["g_ff_post"], res=x1, out_dtype=F32, name="rms_ff_post")
    saved = dict(x=x, h=h, qkv=qkv, rest=rest, o_sb=o_sb, pooled=pooled, o_pool=o_pool, o_gm=o_gm,
                 merged=merged, y=y, x1=x1, h2=h2, r=r, ff=ff)
    return x2, saved


def _layer_bwd(dx2, w, sv):
    dff, dg_ff_post = _rms_bwd(sv["ff"], w["g_ff_post"], dx2, res=None, out_dtype=MX, name="rms_ff_post_bwd")
    da = _mm("ff_out_dx", dff, w["w_ff_out"], "nt", (MX,), epilogue=_relu2_bwd, extras=(sv["r"],))
    dw_ff_out = _mm("ff_out_dw", sv["r"], dff, "tn", (F32,))
    dh2 = _mm("ff_in_dx", da, w["w_ff_in"], "nt", (F32,))
    dw_ff_in = _mm("ff_in_dw", sv["h2"], da, "tn", (F32,))
    dx1, dg_ff_pre = _rms_bwd(sv["x1"], w["g_ff_pre"], dh2, res=dx2, out_dtype=F32, name="rms_ff_pre_bwd")

    dy, dg_mix_post = _rms_bwd(sv["y"], w["g_mix_post"], dx1, res=None, out_dtype=MX, name="rms_mix_post_bwd")
    dmerged = _mm("out_proj_dx", dy, w["w_out"], "nt", (F32,))
    dw_out = _mm("out_proj_dw", sv["merged"], dy, "tn", (F32,))
    db_sb, db_pool, db_gm, dg0, dg1, dg2 = _gate_bwd(
        sv["o_sb"], sv["o_pool"], sv["o_gm"], w["w_br_sb"], w["w_br_pool"], w["w_br_gm"], sv["rest"], dmerged)
    do_sb = _mm("br_sb_dx", db_sb, w["w_br_sb"], "nt", (MX,))
    dw_br_sb = _mm("br_sb_dw", sv["o_sb"], db_sb, "tn", (F32,))
    do_pool = _mm("br_pool_dx", db_pool, w["w_br_pool"], "nt", (F32,))
    dw_br_pool = _mm("br_pool_dw", sv["o_pool"], db_pool, "tn", (F32,))
    do_gm = _mm("br_gm_dx", db_gm, w["w_br_gm"], "nt", (F32,))
    dw_br_gm = _mm("br_gm_dw", sv["o_gm"], db_gm, "tn", (F32,))

    du, dv_gm, dws, dbfull, dgain = _gm_bwd(do_gm, sv["rest"], w["gm_gain"], w["w_spatial"], w["bfull"])
    dp, dwbd, dscale = _pool_bwd(do_pool, sv["pooled"], w["wbd"], w["pool_scale"])
    dq, dk, dv = _attn_bwd(sv["qkv"], do_sb)
    dproj = jnp.concatenate([dq, dk.astype(MX), dv.astype(MX), dp, du, dv_gm, dg0, dg1, dg2], axis=1)
    dh = _mm("proj_dx", dproj, w["w_in"], "nt", (F32,))
    dw_in = _mm("proj_dw", sv["h"], dproj, "tn", (F32,))
    dx, dg_mix_pre = _rms_bwd(sv["x"], w["g_mix_pre"], dh, res=dx1, out_dtype=F32, name="rms_mix_pre_bwd")

    grads = dict(
        w_in=dw_in, w_br_sb=dw_br_sb, w_br_pool=dw_br_pool, w_br_gm=dw_br_gm, w_out=dw_out,
        w_ff_in=dw_ff_in, w_ff_out=dw_ff_out,
        w_pool=jnp.stack([dwbd[g * GROUP:(g + 1) * GROUP, g * GROUP:(g + 1) * GROUP] for g in range(N_GROUPS)]),
        pool_scale=dscale[0], gm_gain=dgain[0], w_spatial=dws,
        b_spatial=dbfull.reshape(CHUNK, N_GROUPS, GROUP).sum(axis=-1).T,
        g_mix_pre=dg_mix_pre[0], g_mix_post=dg_mix_post[0], g_ff_pre=dg_ff_pre[0], g_ff_post=dg_ff_post[0])
    return dx, grads


def _layer_operands(full_l, small_l):
    wbd = jnp.zeros((PW, PW), F32)
    for g in range(N_GROUPS):
        wbd = wbd.at[g * GROUP:(g + 1) * GROUP, g * GROUP:(g + 1) * GROUP].set(small_l["w_pool"][g])
    w = dict(full_l)
    w["wbd"] = wbd.astype(MX)
    w["bfull"] = jnp.repeat(small_l["b_spatial"].T, GROUP, axis=1)
    w["w_spatial"] = small_l["w_spatial"]
    for n in ("pool_scale", "gm_gain", "g_mix_pre", "g_mix_post", "g_ff_pre", "g_ff_post"):
        w[n] = small_l[n][None, :]
    return w


def _local_step(x, target, layers):
    saved = []
    for w in layers:
        x, sv = _layer_fwd(x, w)
        saved.append(sv)
    dx, sq = _loss_head(x, target)
    loss = 0.5 * jnp.sum(sq) / x.shape[1]
    grads = [None] * len(layers)
    for l in reversed(range(len(layers))):
        dx, grads[l] = _layer_bwd(dx, layers[l], saved[l])
    return loss, dx, grads


def _exchange(inp, *, scatter, name):
    block = inp.shape[1:] if scatter else inp.shape

    def body(in_ref, out_ref, send_sems, recv_sems, local_sem):
        x, y, c = lax.axis_index("x"), lax.axis_index("y"), lax.axis_index("c")
        me = 4 * x + 2 * y + c

        def peer(k):
            px = 1 - x if k & 4 else x
            py = 1 - y if k & 2 else y
            pc = 1 - c if k & 1 else c
            return (px, py, pc), 4 * px + 2 * py + pc

        def copy(k):
            dev, idx = peer(k)
            return pltpu.make_async_remote_copy(
                src_ref=in_ref.at[idx] if scatter else in_ref,
                dst_ref=out_ref.at[me],
                send_sem=send_sems.at[k - 1], recv_sem=recv_sems.at[k - 1],
                device_id=dev, device_id_type=pl.DeviceIdType.MESH)

        def arrival(k):
            dev, idx = peer(k)
            return pltpu.make_async_remote_copy(
                src_ref=in_ref.at[idx] if scatter else in_ref,
                dst_ref=out_ref.at[idx],
                send_sem=send_sems.at[k - 1], recv_sem=recv_sems.at[k - 1],
                device_id=dev, device_id_type=pl.DeviceIdType.MESH)

        mine = pltpu.make_async_copy(in_ref.at[me] if scatter else in_ref, out_ref.at[me], local_sem)
        mine.start()
        sends = [copy(k) for k in range(1, N_DEV)]
        for cp in sends:
            cp.start()
        for k in range(1, N_DEV):
            arrival(k).wait_recv()
        for cp in sends:
            cp.wait_send()
        mine.wait()

    return pl.pallas_call(
        body, name=name,
        in_specs=[pl.BlockSpec(memory_space=pl.ANY)],
        out_specs=pl.BlockSpec(memory_space=pl.ANY),
        out_shape=jax.ShapeDtypeStruct((N_DEV,) + tuple(block), inp.dtype),
        scratch_shapes=[pltpu.SemaphoreType.DMA((N_DEV - 1,)), pltpu.SemaphoreType.DMA((N_DEV - 1,)),
                        pltpu.SemaphoreType.DMA],
    )(inp)


def _sum_devices(recv, name):
    _, rows, cols = recv.shape
    tr = rows
    for cand in (512, 256, 128, 64, 32, 16, 8):
        if rows % cand == 0:
            tr = cand
            break

    def body(r_ref, out_ref):
        acc = r_ref[0].astype(F32)
        for d in range(1, N_DEV):
            acc = acc + r_ref[d].astype(F32)
        out_ref[...] = acc

    return pl.pallas_call(
        body, name=name, grid=(rows // tr,),
        in_specs=[pl.BlockSpec((N_DEV, tr, cols), lambda i: (0, i, 0))],
        out_specs=pl.BlockSpec((tr, cols), lambda i: (i, 0)),
        out_shape=jax.ShapeDtypeStruct((rows, cols), F32),
        compiler_params=_params(("parallel",)),
    )(recv)


def _adamw(w, g, m, v, name):
    rows, cols = w.shape
    tr = rows
    for cand in (512, 256, 128, 64, 32, 16, 8):
        if rows % cand == 0:
            tr = cand
            break
    c1 = 1.0 - ADAM_B1 ** ADAM_STEP
    c2 = 1.0 - ADAM_B2 ** ADAM_STEP

    def body(w_ref, g_ref, m_ref, v_ref, d_ref, nm_ref, nv_ref):
        gv = g_ref[...]
        nm = ADAM_B1 * m_ref[...] + (1.0 - ADAM_B1) * gv
        nv = ADAM_B2 * v_ref[...] + (1.0 - ADAM_B2) * (gv * gv)
        nm_ref[...] = nm
        nv_ref[...] = nv
        d_ref[...] = -ADAM_LR * ((nm / c1) / (jnp.sqrt(nv / c2) + ADAM_EPS) + ADAM_WD * w_ref[...])

    spec = pl.BlockSpec((tr, cols), lambda i: (i, 0))
    return pl.pallas_call(
        body, name=name, grid=(rows // tr,),
        in_specs=[spec] * 4, out_specs=[spec] * 3,
        out_shape=[jax.ShapeDtypeStruct((rows, cols), F32)] * 3,
        compiler_params=_params(("parallel",)),
    )(w, g, m, v)


def _pack_rows(flat, dtype):
    n = flat.shape[-1]
    rows = -(-n // PACK_COLS)
    rows = -(-rows // 16) * 16
    pad = rows * PACK_COLS - n
    flat = jnp.pad(flat, [(0, 0)] * (flat.ndim - 1) + [(0, pad)])
    return flat.reshape(flat.shape[:-1] + (rows, PACK_COLS)).astype(dtype)


def _unshard(shards, axis):
    _, a, b = shards.shape
    if axis == 2:
        return jnp.transpose(shards, (1, 0, 2)).reshape(a, N_DEV * b)
    return shards.reshape(N_DEV * a, b)


def _shard(fullarr, axis):
    depth, a, b = fullarr.shape
    if axis == 2:
        return jnp.transpose(fullarr.reshape(depth, a, N_DEV, b // N_DEV), (2, 0, 1, 3))
    return jnp.transpose(fullarr.reshape(depth, N_DEV, a // N_DEV, b), (1, 0, 2, 3))


def kernel(x, w_in, w_pool, pool_scale, gm_gain, w_spatial, b_spatial, w_br_sb, w_br_pool, w_br_gm, w_out, g_mix_pre, g_mix_post, g_ff_pre, g_ff_post, w_ff_in, w_ff_out, loss_target, m_w_in, m_w_pool, m_pool_scale, m_gm_gain, m_w_spatial, m_b_spatial, m_w_br_sb, m_w_br_pool, m_w_br_gm, m_w_out, m_g_mix_pre, m_g_mix_post, m_g_ff_pre, m_g_ff_post, m_w_ff_in, m_w_ff_out, v_w_in, v_w_pool, v_pool_scale, v_gm_gain, v_w_spatial, v_b_spatial, v_w_br_sb, v_w_br_pool, v_w_br_gm, v_w_out, v_g_mix_pre, v_g_mix_post, v_g_ff_pre, v_g_ff_post, v_w_ff_in, v_w_ff_out):
    args = dict(locals())
    weights = {n: args[n] for n in WEIGHTS}
    mom_m = {n: args["m_" + n] for n in WEIGHTS}
    mom_v = {n: args["v_" + n] for n in WEIGHTS}

    shard_shapes = {n: weights[n].shape for n in BIG}
    sizes = [int(np.prod(shard_shapes[n])) for n in BIG]
    offsets = np.concatenate([[0], np.cumsum(sizes)])
    packed = _pack_rows(jnp.concatenate([weights[n].astype(MX).reshape(-1) for n in BIG]), MX)
    gathered = _exchange(packed, scatter=False, name="gather_weights").reshape(N_DEV, -1)
    depth = w_in.shape[0]
    layers = []
    for l in range(depth):
        full_l = {}
        for n, lo, hi in zip(BIG, offsets[:-1], offsets[1:]):
            shards = gathered[:, lo:hi].reshape((N_DEV,) + shard_shapes[n])[:, l]
            full_l[n] = _unshard(shards, BIG_SHARD_AXIS[n])
        layers.append(_layer_operands(full_l, {n: weights[n][l] for n in SMALL}))

    loss, dx, layer_grads = _local_step(x[0], loss_target[0], layers)
    grads = {n: jnp.stack([g[n] for g in layer_grads]) for n in BIG + SMALL}
    loss = lax.psum(loss, ("x", "y", "c"))

    by_dest = jnp.concatenate([_shard(grads[n], BIG_SHARD_AXIS[n]).reshape(N_DEV, -1) for n in BIG], axis=1)
    recv = _exchange(_pack_rows(by_dest, MX), scatter=True, name="scatter_grads")
    g_big = _sum_devices(recv, "sum_grads").reshape(-1)
    small_sizes = [int(np.prod(weights[n].shape)) for n in SMALL]
    small_off = np.concatenate([[0], np.cumsum(small_sizes)])
    small_flat = jnp.concatenate([grads[n].reshape(-1) for n in SMALL])
    g_small = _sum_devices(_exchange(_pack_rows(small_flat, F32), scatter=False, name="gather_small_grads"),
                           "sum_small_grads").reshape(-1)

    grad_w = {}
    for n, lo, hi in zip(BIG, offsets[:-1], offsets[1:]):
        grad_w[n] = g_big[lo:hi].reshape(shard_shapes[n])
    for n, lo, hi in zip(SMALL, small_off[:-1], small_off[1:]):
        grad_w[n] = g_small[lo:hi].reshape(weights[n].shape)

    delta, new_m, new_v = {}, {}, {}
    for n in WEIGHTS:
        shape = weights[n].shape
        view = (-1, shape[-1])
        d, nm, nv = _adamw(weights[n].reshape(view), grad_w[n].reshape(view), mom_m[n].reshape(view),
                           mom_v[n].reshape(view), "adamw_" + n)
        delta[n], new_m[n], new_v[n] = d.reshape(shape), nm.reshape(shape), nv.reshape(shape)

    return (loss, dx[None], *[grad_w[n] for n in WEIGHTS], *[delta[n] for n in WEIGHTS],
            *[new_m[n] for n in WEIGHTS], *[new_v[n] for n in WEIGHTS])
```

```python
import functools

import numpy as np
import jax
import jax.numpy as jnp
from jax import lax
from jax.experimental import pallas as pl
from jax.experimental.pallas import tpu as pltpu

F32 = jnp.float32
MX = jnp.bfloat16

D = 1024
SB_W = 512
HEAD_PAIR = 128
PW = 256
GROUP = 64
N_GROUPS = 4
CHUNK = 128
POOL_WINDOWS = (2, 4, 8, 16)
D_FF = 4096
D_IN = 5376
QKV_W = 3 * SB_W
REST_W = D_IN - QKV_W
GATE_COL0 = 3 * PW
RMS_EPS = 1e-6
N_DEV = 8
DEPTH = 4

ADAM_LR = 0.001
ADAM_B1 = 0.9
ADAM_B2 = 0.999
ADAM_EPS = 1e-08
ADAM_WD = 0.01
ADAM_STEP = 10

SKIP_LOG = -120.0

VMEM_LIMIT = 48 * 1024 * 1024
W_IN_TILE = 768
PACK_COLS = 1024

BIG = ("w_in", "w_br_sb", "w_br_pool", "w_br_gm", "w_out", "w_ff_in", "w_ff_out")
BIG_SHARD_AXIS = {"w_in": 2, "w_br_sb": 2, "w_br_pool": 2, "w_br_gm": 2, "w_out": 1, "w_ff_in": 2, "w_ff_out": 1}
SMALL = ("w_pool", "pool_scale", "gm_gain", "w_spatial", "b_spatial",
         "g_mix_pre", "g_mix_post", "g_ff_pre", "g_ff_post")
WEIGHTS = ("w_in", "w_pool", "pool_scale", "gm_gain", "w_spatial", "b_spatial", "w_br_sb", "w_br_pool",
           "w_br_gm", "w_out", "g_mix_pre", "g_mix_post", "g_ff_pre", "g_ff_post", "w_ff_in", "w_ff_out")


def _params(sem, vmem=VMEM_LIMIT):
    return pltpu.CompilerParams(dimension_semantics=sem, vmem_limit_bytes=vmem)


def _dot(a, b, dims):
    return lax.dot_general(a, b, (dims, ((), ())), preferred_element_type=F32)


NN = ((1,), (0,))
NT = ((1,), (1,))
TN = ((0,), (0,))


def _split_dot(x, m, left, nsplit):
    acc = None
    r = x
    for s in range(nsplit):
        p = r.astype(jnp.bfloat16)
        d = _dot(m, p, NN) if left else _dot(p, m, NN)
        acc = d if acc is None else acc + d
        if s + 1 < nsplit:
            r = r - p.astype(F32)
    return acc


def _matmul(a, b, *, mode, name, tm, tn, tk, out_dtypes, b_col0=0, n_out=None, epilogue=None, extras=()):
    if mode == "nn":
        m, kdim = a.shape
        n = b.shape[1] if n_out is None else n_out
        dims = NN
    elif mode == "nt":
        m, kdim = a.shape
        n = b.shape[0]
        dims = NT
    else:
        kdim, m = a.shape
        n = b.shape[1]
        dims = TN
    tm, tn, tk = min(tm, m), min(tn, n), min(tk, kdim)
    assert m % tm == 0 and n % tn == 0 and kdim % tk == 0 and b_col0 % tn == 0, (name, m, n, kdim)
    if mode == "nn":
        a_spec = pl.BlockSpec((tm, tk), lambda i, j, k: (i, k))
        b_spec = pl.BlockSpec((tk, tn), lambda i, j, k: (k, j + b_col0 // tn))
    elif mode == "nt":
        a_spec = pl.BlockSpec((tm, tk), lambda i, j, k: (i, k))
        b_spec = pl.BlockSpec((tn, tk), lambda i, j, k: (j, k))
    else:
        a_spec = pl.BlockSpec((tk, tm), lambda i, j, k: (k, i))
        b_spec = pl.BlockSpec((tk, tn), lambda i, j, k: (k, j))
    nk = kdim // tk
    n_extra = len(extras)
    n_outs = len(out_dtypes)
    o_spec = pl.BlockSpec((tm, tn), lambda i, j, k: (i, j))

    def body(a_ref, b_ref, *refs):
        extra_refs = refs[:n_extra]
        out_refs = refs[n_extra:n_extra + n_outs]
        acc_ref = refs[-1]
        k = pl.program_id(2)

        @pl.when(k == 0)
        def _():
            acc_ref[...] = jnp.zeros_like(acc_ref)

        acc_ref[...] += _dot(a_ref[...].astype(MX), b_ref[...].astype(MX), dims)

        @pl.when(k == nk - 1)
        def _():
            acc = acc_ref[...]
            outs = (acc,) if epilogue is None else epilogue(acc, *[e[...] for e in extra_refs])
            for o_ref, val in zip(out_refs, outs):
                o_ref[...] = val.astype(o_ref.dtype)

    outs = pl.pallas_call(
        body, name=name,
        grid=(m // tm, n // tn, nk),
        in_specs=[a_spec, b_spec] + [o_spec] * n_extra,
        out_specs=[o_spec] * n_outs,
        out_shape=[jax.ShapeDtypeStruct((m, n), dt) for dt in out_dtypes],
        scratch_shapes=[pltpu.VMEM((tm, tn), F32)],
        compiler_params=_params(("parallel", "parallel", "arbitrary")),
    )(a, b, *extras)
    return outs[0] if n_outs == 1 else outs


ROW_TILE = 512


def _rms_fwd(x, g, *, res, out_dtype, name):
    s, d = x.shape
    tr = min(ROW_TILE, s)
    has_res = res is not None

    def body(x_ref, g_ref, *refs):
        out_ref = refs[-1]
        xv = x_ref[...]
        y = xv * lax.rsqrt(jnp.mean(xv * xv, axis=-1, keepdims=True) + RMS_EPS) * g_ref[...]
        if has_res:
            y = refs[0][...] + y
        out_ref[...] = y.astype(out_ref.dtype)

    row = pl.BlockSpec((tr, d), lambda i: (i, 0))
    vec = pl.BlockSpec((1, d), lambda i: (0, 0))
    return pl.pallas_call(
        body, name=name, grid=(s // tr,),
        in_specs=[row, vec] + ([row] if has_res else []),
        out_specs=row, out_shape=jax.ShapeDtypeStruct((s, d), out_dtype),
        compiler_params=_params(("parallel",)),
    )(x, g, *([res] if has_res else []))


def _rms_bwd(x, g, dout, *, res, out_dtype, name):
    s, d = x.shape
    tr = min(ROW_TILE, s)
    has_res = res is not None

    def body(x_ref, g_ref, do_ref, *refs):
        dx_ref, dg_ref = refs[-2], refs[-1]
        i = pl.program_id(0)
        xv = x_ref[...]
        do = do_ref[...]
        r = lax.rsqrt(jnp.mean(xv * xv, axis=-1, keepdims=True) + RMS_EPS)
        xhat = xv * r
        dxhat = do * g_ref[...]
        dx = r * (dxhat - xhat * jnp.mean(dxhat * xhat, axis=-1, keepdims=True))
        if has_res:
            dx = refs[0][...] + dx
        dx_ref[...] = dx.astype(dx_ref.dtype)

        @pl.when(i == 0)
        def _():
            dg_ref[...] = jnp.zeros_like(dg_ref)

        dg_ref[...] += jnp.sum(do * xhat, axis=0, keepdims=True)

    row = pl.BlockSpec((tr, d), lambda i: (i, 0))
    vec = pl.BlockSpec((1, d), lambda i: (0, 0))
    return pl.pallas_call(
        body, name=name, grid=(s // tr,),
        in_specs=[row, vec, row] + ([row] if has_res else []),
        out_specs=[row, vec],
        out_shape=[jax.ShapeDtypeStruct((s, d), out_dtype), jax.ShapeDtypeStruct((1, d), F32)],
        compiler_params=_params(("arbitrary",)),
    )(x, g, dout, *([res] if has_res else []))


def _loss_head(y, target):
    s, d = y.shape
    tr = min(ROW_TILE, s)

    def body(y_ref, t_ref, dy_ref, sq_ref):
        i = pl.program_id(0)
        err = y_ref[...] - t_ref[...]
        dy_ref[...] = err * (1.0 / d)

        @pl.when(i == 0)
        def _():
            sq_ref[...] = jnp.zeros_like(sq_ref)

        sq_ref[...] += jnp.sum(err * err, axis=0, keepdims=True)

    row = pl.BlockSpec((tr, d), lambda i: (i, 0))
    vec = pl.BlockSpec((1, d), lambda i: (0, 0))
    return pl.pallas_call(
        body, name="loss_head", grid=(s // tr,),
        in_specs=[row, row], out_specs=[row, vec],
        out_shape=[jax.ShapeDtypeStruct((s, d), F32), jax.ShapeDtypeStruct((1, d), F32)],
        compiler_params=_params(("arbitrary",)),
    )(y, target)


def _sb_masks():
    row = lax.broadcasted_iota(jnp.int32, (CHUNK, CHUNK), 0)
    col = lax.broadcasted_iota(jnp.int32, (CHUNK, CHUNK), 1)
    return row, col


SB_WINDOW = 2 * CHUNK

def _sb_consts():
    row = lax.broadcasted_iota(jnp.int32, (CHUNK, SB_WINDOW), 0)
    col = lax.broadcasted_iota(jnp.int32, (CHUNK, SB_WINDOW), 1)
    lane_hi = lax.broadcasted_iota(jnp.int32, (CHUNK, HEAD_PAIR), 1) >= GROUP
    return col - row, col, lane_hi


def _sb_window(i, m, dcol, col):
    start = jnp.maximum((i - 2 * m - 1) * CHUNK, 0)
    hi = (i - 2 * m + 1) * CHUNK
    keep = jnp.logical_and(dcol < i * CHUNK - start, col < hi - start)
    return pl.multiple_of(start, CHUNK), keep


def _sb_scores(qm, kw, keep):
    z = _dot(qm, kw, NT)
    t = jnp.log1p(jnp.exp(-jnp.abs(z)))
    ln = -(jnp.maximum(z, 0.0) + t)
    lb = ln + z
    ln = jnp.where(keep, ln, 0.0)
    return ln, lb


def _sb_tri(upper):
    r = np.arange(SB_WINDOW)
    m = (r[:, None] > r[None, :]) if upper else (r[:, None] < r[None, :])
    return jnp.asarray(m, jnp.bfloat16)


def _attn_fwd(qkv):
    s = qkv.shape[0]
    nblk = s // CHUNK
    n_pairs = SB_W // HEAD_PAIR

    def body(q_ref, k_ref, v_ref, up_ref, o_ref):
        i = pl.program_id(1)
        dcol, col, lane_hi = _sb_consts()
        q = q_ref[...]
        zero = jnp.zeros_like(q)
        qms = (jnp.where(lane_hi, zero, q) * 0.125, jnp.where(lane_hi, q, zero) * 0.125)

        def cond(st):
            return jnp.logical_and(st[0] <= i // 2, st[1])

        def loop(st):
            m = st[0]
            start, keep = _sb_window(i, m, dcol, col)
            kw = k_ref[pl.ds(start, SB_WINDOW), :]
            vw = v_ref[pl.ds(start, SB_WINDOW), :]
            new = []
            for h in range(2):
                c, acc = st[2 + 2 * h], st[3 + 2 * h]
                ln, lb = _sb_scores(qms[h], kw, keep)
                suffix = _split_dot(ln, up_ref[...], False, 2) + c
                a = jnp.where(keep, jnp.exp(lb + suffix), 0.0)
                new += [c + jnp.sum(ln, axis=1, keepdims=True), acc + _dot(a.astype(MX), vw, NN)]
            active = jnp.maximum(jnp.max(new[0]), jnp.max(new[2])) > SKIP_LOG
            return (m + 1, active, *new)

        c0 = jnp.zeros((CHUNK, 1), F32)
        a0 = jnp.zeros((CHUNK, HEAD_PAIR), F32)
        st = lax.while_loop(cond, loop, (i * 0, i >= 0, c0, a0, c0, a0))
        o_ref[...] = jnp.where(lane_hi, st[5], st[3]).astype(o_ref.dtype)

    return pl.pallas_call(
        body, name="attn_fwd", grid=(n_pairs, nblk),
        in_specs=[pl.BlockSpec((CHUNK, HEAD_PAIR), lambda p, i: (i, p)),
                  pl.BlockSpec((s, HEAD_PAIR), lambda p, i: (0, n_pairs + p)),
                  pl.BlockSpec((s, HEAD_PAIR), lambda p, i: (0, 2 * n_pairs + p)),
                  pl.BlockSpec((SB_WINDOW, SB_WINDOW), lambda p, i: (0, 0))],
        out_specs=pl.BlockSpec((CHUNK, HEAD_PAIR), lambda p, i: (i, p)),
        out_shape=jax.ShapeDtypeStruct((s, SB_W), MX),
        compiler_params=_params(("parallel", "parallel")),
    )(qkv, qkv, qkv, _sb_tri(True))


def _attn_bwd(qkv, d_o):
    s = qkv.shape[0]
    nblk = s // CHUNK
    n_pairs = SB_W // HEAD_PAIR
    n_win = nblk // 2 + 1

    def body(q_ref, k_ref, v_ref, do_ref, up_ref, lo_ref, dq_ref, dk_ref, dv_ref, e_s, sig_s):
        i = pl.program_id(1)
        dcol, col, lane_hi = _sb_consts()
        q = q_ref[...]
        do = do_ref[...]
        zero = jnp.zeros_like(q)
        qms = (jnp.where(lane_hi, zero, q) * 0.125, jnp.where(lane_hi, q, zero) * 0.125)
        doms = (jnp.where(lane_hi, zero, do), jnp.where(lane_hi, do, zero))

        @pl.when(i == 0)
        def _():
            dk_ref[...] = jnp.zeros_like(dk_ref)
            dv_ref[...] = jnp.zeros_like(dv_ref)

        def cond(st):
            return jnp.logical_and(st[0] <= i // 2, st[1])

        def loop(st):
            m = st[0]
            start, keep = _sb_window(i, m, dcol, col)
            kw = k_ref[pl.ds(start, SB_WINDOW), :]
            vw = v_ref[pl.ds(start, SB_WINDOW), :]
            cs = []
            dv = None
            for h in range(2):
                c = st[2 + h]
                ln, lb = _sb_scores(qms[h], kw, keep)
                suffix = _split_dot(ln, up_ref[...], False, 2) + c
                a = jnp.where(keep, jnp.exp(lb + suffix), 0.0)
                e_s[h, m] = a * _dot(doms[h], vw, NT)
                sig_s[h, m] = jnp.exp(lb)
                part = _dot(a.astype(MX), doms[h], TN)
                dv = part if dv is None else dv + part
                cs.append(c + jnp.sum(ln, axis=1, keepdims=True))
            dv_ref[pl.ds(start, SB_WINDOW), :] += dv
            active = jnp.maximum(jnp.max(cs[0]), jnp.max(cs[1])) > SKIP_LOG
            return (m + 1, active, *cs)

        c0 = jnp.zeros((CHUNK, 1), F32)
        n_seen = lax.while_loop(cond, loop, (i * 0, i >= 0, c0, c0))[0]

        def up(t, st):
            m = n_seen - 1 - t
            start, keep = _sb_window(i, m, dcol, col)
            kw = k_ref[pl.ds(start, SB_WINDOW), :]
            new = []
            dk = None
            for h in range(2):
                cp, dq = st[2 * h], st[2 * h + 1]
                e = e_s[h, m]
                sig = sig_s[h, m]
                prefix = _split_dot(e, lo_ref[...], False, 2) + cp
                dz = jnp.where(keep, e * (1.0 - sig) - prefix * sig, 0.0).astype(MX)
                part = _dot(dz, qms[h], TN)
                dk = part if dk is None else dk + part
                new += [cp + jnp.sum(e, axis=1, keepdims=True), dq + _dot(dz, kw, NN)]
            dk_ref[pl.ds(start, SB_WINDOW), :] += dk
            return tuple(new)

        a0 = jnp.zeros((CHUNK, HEAD_PAIR), F32)
        st = lax.fori_loop(0, n_seen, up, (c0, a0, c0, a0))
        dq_ref[...] = (jnp.where(lane_hi, st[3], st[1]) * 0.125).astype(dq_ref.dtype)

    blk = pl.BlockSpec((CHUNK, HEAD_PAIR), lambda p, i: (i, p))
    full = pl.BlockSpec((s, HEAD_PAIR), lambda p, i: (0, p))
    tri = pl.BlockSpec((SB_WINDOW, SB_WINDOW), lambda p, i: (0, 0))
    return pl.pallas_call(
        body, name="attn_bwd", grid=(n_pairs, nblk),
        in_specs=[blk,
                  pl.BlockSpec((s, HEAD_PAIR), lambda p, i: (0, n_pairs + p)),
                  pl.BlockSpec((s, HEAD_PAIR), lambda p, i: (0, 2 * n_pairs + p)),
                  blk, tri, tri],
        out_specs=[blk, full, full],
        out_shape=[jax.ShapeDtypeStruct((s, SB_W), MX), jax.ShapeDtypeStruct((s, SB_W), F32),
                   jax.ShapeDtypeStruct((s, SB_W), F32)],
        scratch_shapes=[pltpu.VMEM((2, n_win, CHUNK, SB_WINDOW), F32),
                        pltpu.VMEM((2, n_win, CHUNK, SB_WINDOW), F32)],
        compiler_params=_params(("parallel", "arbitrary"), 56 * 1024 * 1024),
    )(qkv, qkv, qkv, d_o, _sb_tri(True), _sb_tri(False))


POOL_TILE = 256
POOL_HALO = 128


def _pool_bands(tile, transpose):
    cur = np.zeros((N_GROUPS, tile, tile), np.float32)
    halo = np.zeros((N_GROUPS, tile, POOL_HALO), np.float32)
    t = np.arange(tile)[:, None]
    for g, w in enumerate(POOL_WINDOWS):
        if not transpose:
            p = np.arange(tile)[None, :]
            cur[g] = ((t - p >= 0) & (t - p < w))
            ph = np.arange(POOL_HALO)[None, :] - POOL_HALO
            halo[g] = (t - ph < w)
        else:
            p = np.arange(tile)[None, :]
            cur[g] = ((p - t >= 0) & (p - t < w))
            ph = np.arange(POOL_HALO)[None, :] + tile
            halo[g] = (ph - t < w)
    return jnp.asarray(cur, jnp.bfloat16), jnp.asarray(halo, jnp.bfloat16)


def _pool_count(i, tile, rows, row0):
    pos = (i * tile + row0 + lax.broadcasted_iota(jnp.int32, (rows, PW), 0)).astype(F32)
    grp = lax.broadcasted_iota(jnp.int32, (rows, PW), 1) // GROUP
    win = jnp.where(grp == 0, float(POOL_WINDOWS[0]),
                    jnp.where(grp == 1, float(POOL_WINDOWS[1]),
                              jnp.where(grp == 2, float(POOL_WINDOWS[2]), float(POOL_WINDOWS[3]))))
    return jnp.minimum(pos + 1.0, win), grp


def _pool_fwd(rest, wbd, scale):
    s = rest.shape[0]
    tile = min(POOL_TILE, s)
    halo_per_tile = tile // POOL_HALO
    bcur, bhalo = _pool_bands(tile, False)

    def body(cur_ref, prev_ref, bcur_ref, bhalo_ref, wbd_ref, scale_ref, pooled_ref, out_ref):
        i = pl.program_id(0)
        cur = cur_ref[...]
        prev = jnp.where(i > 0, prev_ref[...], 0.0)
        count, grp = _pool_count(i, tile, tile, 0)
        win = jnp.zeros((tile, PW), F32)
        for g in range(N_GROUPS):
            wsum = _split_dot(cur, bcur_ref[g], True, 3) + _split_dot(prev, bhalo_ref[g], True, 3)
            win = jnp.where(grp == g, wsum, win)
        pooled = (win / count - cur).astype(MX)
        pooled_ref[...] = pooled
        out_ref[...] = (_dot(pooled, wbd_ref[...], NN) * scale_ref[...]).astype(out_ref.dtype)

    tile_spec = pl.BlockSpec((tile, PW), lambda i: (i, 0))
    const3 = lambda shape: pl.BlockSpec(shape, lambda i: (0, 0, 0))
    const2 = lambda shape: pl.BlockSpec(shape, lambda i: (0, 0))
    return pl.pallas_call(
        body, name="pool_fwd", grid=(s // tile,),
        in_specs=[tile_spec,
                  pl.BlockSpec((POOL_HALO, PW), lambda i: (jnp.maximum(i * halo_per_tile - 1, 0), 0)),
                  const3(bcur.shape), const3(bhalo.shape), const2((PW, PW)), const2((1, PW))],
        out_specs=[tile_spec, tile_spec],
        out_shape=[jax.ShapeDtypeStruct((s, PW), MX), jax.ShapeDtypeStruct((s, PW), MX)],
        compiler_params=_params(("parallel",)),
    )(rest, rest, bcur, bhalo, wbd, scale)


def _pool_bwd(d_o, pooled, wbd, scale):
    s = d_o.shape[0]
    tile = min(POOL_TILE, s)
    halo_per_tile = tile // POOL_HALO
    n_halo = s // POOL_HALO
    n_tiles = s // tile
    ccur, chalo = _pool_bands(tile, True)

    def body(do_ref, nxt_ref, pooled_ref, ccur_ref, chalo_ref, wbd_ref, scale_ref, dp_ref, dw_ref, ds_ref):
        i = pl.program_id(0)
        do = do_ref[...]
        w = wbd_ref[...]
        sc = scale_ref[...]
        pooled_v = pooled_ref[...]

        @pl.when(i == 0)
        def _():
            dw_ref[...] = jnp.zeros_like(dw_ref)
            ds_ref[...] = jnp.zeros_like(ds_ref)

        ds_ref[...] += jnp.sum(do * _dot(pooled_v, w, NN), axis=0, keepdims=True)
        dmixed = (do * sc).astype(MX)
        dw_ref[...] += _dot(pooled_v, dmixed, TN)
        dpooled = _dot(dmixed, w, NT)
        nxt = jnp.where(i < n_tiles - 1, nxt_ref[...], 0.0)
        dpooled_n = _dot((nxt * sc).astype(MX), w, NT)
        count, grp = _pool_count(i, tile, tile, 0)
        count_n, _ = _pool_count(i, tile, POOL_HALO, tile)
        dq = dpooled / count
        dq_n = dpooled_n / count_n
        acc = jnp.zeros((tile, PW), F32)
        for g in range(N_GROUPS):
            wsum = _split_dot(dq, ccur_ref[g], True, 3) + _split_dot(dq_n, chalo_ref[g], True, 3)
            acc = jnp.where(grp == g, wsum, acc)
        dp_ref[...] = (acc - dpooled).astype(dp_ref.dtype)

    tile_spec = pl.BlockSpec((tile, PW), lambda i: (i, 0))
    const3 = lambda shape: pl.BlockSpec(shape, lambda i: (0, 0, 0))
    const2 = lambda shape: pl.BlockSpec(shape, lambda i: (0, 0))
    return pl.pallas_call(
        body, name="pool_bwd", grid=(n_tiles,),
        in_specs=[tile_spec,
                  pl.BlockSpec((POOL_HALO, PW), lambda i: (jnp.minimum((i + 1) * halo_per_tile, n_halo - 1), 0)),
                  tile_spec, const3(ccur.shape), const3(chalo.shape), const2((PW, PW)), const2((1, PW))],
        out_specs=[tile_spec, const2((PW, PW)), const2((1, PW))],
        out_shape=[jax.ShapeDtypeStruct((s, PW), MX), jax.ShapeDtypeStruct((PW, PW), F32),
                   jax.ShapeDtypeStruct((1, PW), F32)],
        compiler_params=_params(("arbitrary",)),
    )(d_o, d_o, pooled, ccur, chalo, wbd, scale)


GM_TILE = 512
GELU_C = 0.7978845608028654
GELU_A = 0.044715


def _gelu(x):
    return 0.5 * x * (1.0 + jnp.tanh(GELU_C * (x + GELU_A * (x * x * x))))


def _gelu_grad(x):
    th = jnp.tanh(GELU_C * (x + GELU_A * (x * x * x)))
    return 0.5 * (1.0 + th) + 0.5 * x * (1.0 - th * th) * (GELU_C * (1.0 + 3.0 * GELU_A * (x * x)))


def _gm_common(ws_ref):
    row, col = _sb_masks()
    tril = row >= col
    wsm = [jnp.where(tril, ws_ref[g], 0.0).astype(MX) for g in range(N_GROUPS)]
    grp = lax.broadcasted_iota(jnp.int32, (CHUNK, PW), 1) // GROUP
    return tril, wsm, grp


def _gm_mix(wsm, grp, vn_c):
    mixed = jnp.zeros((CHUNK, PW), F32)
    for g in range(N_GROUPS):
        mixed = jnp.where(grp == g, _dot(wsm[g], vn_c, NN), mixed)
    return mixed


def _gm_fwd(rest, gain, ws, bfull):
    s = rest.shape[0]
    tile = min(GM_TILE, s)

    def body(u_ref, v_ref, gain_ref, ws_ref, b_ref, out_ref):
        _, wsm, grp = _gm_common(ws_ref)
        bias = b_ref[...]
        for n in range(tile // CHUNK):
            rows = slice(n * CHUNK, (n + 1) * CHUNK)
            gu = _gelu(u_ref[rows, :])
            gv = _gelu(v_ref[rows, :])
            r = lax.rsqrt(jnp.mean(gv * gv, axis=-1, keepdims=True) + RMS_EPS)
            vn = (gv * r * gain_ref[...]).astype(MX)
            out_ref[rows, :] = (gu * (_gm_mix(wsm, grp, vn) + bias)).astype(out_ref.dtype)

    const2 = lambda shape: pl.BlockSpec(shape, lambda i: (0, 0))
    return pl.pallas_call(
        body, name="gm_fwd", grid=(s // tile,),
        in_specs=[pl.BlockSpec((tile, PW), lambda i: (i, 1)), pl.BlockSpec((tile, PW), lambda i: (i, 2)),
                  const2((1, PW)), pl.BlockSpec((N_GROUPS, CHUNK, CHUNK), lambda i: (0, 0, 0)),
                  const2((CHUNK, PW))],
        out_specs=pl.BlockSpec((tile, PW), lambda i: (i, 0)),
        out_shape=jax.ShapeDtypeStruct((s, PW), MX),
        compiler_params=_params(("parallel",)),
    )(rest, rest, gain, ws, bfull)


def _gm_bwd(d_o, rest, gain, ws, bfull):
    s = rest.shape[0]
    tile = min(GM_TILE, s)

    def body(do_ref, u_ref, v_ref, gain_ref, ws_ref, b_ref, du_ref, dv_ref, dws_ref, db_ref, dgain_ref):
        i = pl.program_id(0)
        tril, wsm, grp = _gm_common(ws_ref)
        bias = b_ref[...]
        gain_v = gain_ref[...]

        @pl.when(i == 0)
        def _():
            dws_ref[...] = jnp.zeros_like(dws_ref)
            db_ref[...] = jnp.zeros_like(db_ref)
            dgain_ref[...] = jnp.zeros_like(dgain_ref)

        for n in range(tile // CHUNK):
            rows = slice(n * CHUNK, (n + 1) * CHUNK)
            u = u_ref[rows, :]
            v = v_ref[rows, :]
            do = do_ref[rows, :]
            gu = _gelu(u)
            gv = _gelu(v)
            r = lax.rsqrt(jnp.mean(gv * gv, axis=-1, keepdims=True) + RMS_EPS)
            vhat = gv * r
            vn = (vhat * gain_v).astype(MX)
            mixed = _gm_mix(wsm, grp, vn)
            du_ref[rows, :] = (do * (mixed + bias) * _gelu_grad(u)).astype(du_ref.dtype)
            dmix = do * gu
            db_ref[...] += dmix
            dmix_mx = dmix.astype(MX)
            dvn = jnp.zeros((CHUNK, PW), F32)
            for g in range(N_GROUPS):
                dmg = jnp.where(grp == g, dmix_mx, jnp.zeros_like(dmix_mx))
                dws_ref[g] += jnp.where(tril, _dot(dmg, vn, NT), 0.0)
                dvn = jnp.where(grp == g, _dot(wsm[g], dmix_mx, TN), dvn)
            dgain_ref[...] += jnp.sum(dvn * vhat, axis=0, keepdims=True)
            dvhat = dvn * gain_v
            dgv = r * (dvhat - vhat * jnp.mean(dvhat * vhat, axis=-1, keepdims=True))
            dv_ref[rows, :] = (dgv * _gelu_grad(v)).astype(dv_ref.dtype)

    const2 = lambda shape: pl.BlockSpec(shape, lambda i: (0, 0))
    ws_spec = pl.BlockSpec((N_GROUPS, CHUNK, CHUNK), lambda i: (0, 0, 0))
    tile_spec = pl.BlockSpec((tile, PW), lambda i: (i, 0))
    return pl.pallas_call(
        body, name="gm_bwd", grid=(s // tile,),
        in_specs=[tile_spec, pl.BlockSpec((tile, PW), lambda i: (i, 1)), pl.BlockSpec((tile, PW), lambda i: (i, 2)),
                  const2((1, PW)), ws_spec, const2((CHUNK, PW))],
        out_specs=[tile_spec, tile_spec, ws_spec, const2((CHUNK, PW)), const2((1, PW))],
        out_shape=[jax.ShapeDtypeStruct((s, PW), MX), jax.ShapeDtypeStruct((s, PW), MX),
                   jax.ShapeDtypeStruct((N_GROUPS, CHUNK, CHUNK), F32),
                   jax.ShapeDtypeStruct((CHUNK, PW), F32), jax.ShapeDtypeStruct((1, PW), F32)],
        compiler_params=_params(("arbitrary",)),
    )(d_o, rest, rest, gain, ws, bfull)


GATE_TM = 512
GATE_TN = 256


def _gate_specs(s):
    tm = min(GATE_TM, s)
    tn = GATE_TN
    gate0 = GATE_COL0 // tn
    per = D // tn
    ins = [pl.BlockSpec((tm, SB_W), lambda i, j: (i, 0)),
           pl.BlockSpec((tm, PW), lambda i, j: (i, 0)),
           pl.BlockSpec((tm, PW), lambda i, j: (i, 0)),
           pl.BlockSpec((SB_W, tn), lambda i, j: (0, j)),
           pl.BlockSpec((PW, tn), lambda i, j: (0, j)),
           pl.BlockSpec((PW, tn), lambda i, j: (0, j)),
           pl.BlockSpec((tm, tn), lambda i, j: (i, gate0 + j)),
           pl.BlockSpec((tm, tn), lambda i, j: (i, gate0 + per + j)),
           pl.BlockSpec((tm, tn), lambda i, j: (i, gate0 + 2 * per + j))]
    return tm, tn, ins, pl.BlockSpec((tm, tn), lambda i, j: (i, j))


def _gate_fwd(o_sb, o_pool, o_gm, w_sb, w_pool, w_gm, rest):
    s = rest.shape[0]
    tm, tn, ins, out = _gate_specs(s)

    def body(o0, o1, o2, w0, w1, w2, g0, g1, g2, merged_ref):
        acc = jax.nn.sigmoid(g0[...]) * _dot(o0[...], w0[...], NN)
        acc += jax.nn.sigmoid(g1[...]) * _dot(o1[...], w1[...], NN)
        acc += jax.nn.sigmoid(g2[...]) * _dot(o2[...], w2[...], NN)
        merged_ref[...] = acc.astype(merged_ref.dtype)

    return pl.pallas_call(
        body, name="gate_fwd", grid=(s // tm, D // tn),
        in_specs=ins, out_specs=out, out_shape=jax.ShapeDtypeStruct((s, D), MX),
        compiler_params=_params(("parallel", "parallel")),
    )(o_sb, o_pool, o_gm, w_sb, w_pool, w_gm, rest, rest, rest)


def _gate_bwd(o_sb, o_pool, o_gm, w_sb, w_pool, w_gm, rest, dmerged):
    s = rest.shape[0]
    tm, tn, ins, out = _gate_specs(s)

    def body(o0, o1, o2, w0, w1, w2, g0, g1, g2, dm_ref, db0, db1, db2, dg0, dg1, dg2):
        dm = dm_ref[...]
        for o, w, g, db, dg in ((o0, w0, g0, db0, dg0), (o1, w1, g1, db1, dg1), (o2, w2, g2, db2, dg2)):
            sg = jax.nn.sigmoid(g[...])
            db[...] = (dm * sg).astype(db.dtype)
            dg[...] = (dm * _dot(o[...], w[...], NN) * (sg * (1.0 - sg))).astype(dg.dtype)

    return pl.pallas_call(
        body, name="gate_bwd", grid=(s // tm, D // tn),
        in_specs=ins + [out], out_specs=[out] * 6,
        out_shape=[jax.ShapeDtypeStruct((s, D), MX)] * 6,
        compiler_params=_params(("parallel", "parallel")),
    )(o_sb, o_pool, o_gm, w_sb, w_pool, w_gm, rest, rest, rest, dmerged)


def _relu2(acc):
    r = jnp.maximum(acc, 0.0)
    return (r * r,)


def _relu2_bwd(acc, r):
    return (acc * (2.0 * jnp.sqrt(r.astype(F32))),)


TILES = {
    "proj_qkv": (1024, W_IN_TILE, 1024), "proj_rest": (1024, W_IN_TILE, 1024),
    "out_proj": (1024, 1024, 1024), "ff_in": (1024, 1024, 1024), "ff_out": (1024, 1024, 512),
    "ff_out_dx": (1024, 1024, 1024), "ff_out_dw": (1024, 1024, 512),
    "ff_in_dx": (1024, 1024, 512), "ff_in_dw": (1024, 1024, 512),
    "out_proj_dx": (1024, 1024, 1024), "out_proj_dw": (1024, 1024, 512),
    "br_sb_dx": (1024, 512, 1024), "br_sb_dw": (512, 1024, 512),
    "br_pool_dx": (1024, 256, 1024), "br_pool_dw": (256, 1024, 512),
    "br_gm_dx": (1024, 256, 1024), "br_gm_dw": (256, 1024, 512),
    "proj_dx": (1024, 1024, W_IN_TILE), "proj_dw": (1024, W_IN_TILE, 512),
}


def _mm(name, a, b, mode, out_dtypes, **kw):
    tm, tn, tk = TILES[name]
    return _matmul(a, b, mode=mode, name=name, tm=tm, tn=tn, tk=tk, out_dtypes=out_dtypes, **kw)


def _layer_fwd(x, w):
    h = _rms_fwd(x, w["g_mix_pre"], res=None, out_dtype=MX, name="rms_mix_pre")
    qkv = _mm("proj_qkv", h, w["w_in"], "nn", (MX,), n_out=QKV_W)
    rest = _mm("proj_rest", h, w["w_in"], "nn", (F32,), b_col0=QKV_W, n_out=REST_W)
    o_sb = _attn_fwd(qkv)
    pooled, o_pool = _pool_fwd(rest, w["wbd"], w["pool_scale"])
    o_gm = _gm_fwd(rest, w["gm_gain"], w["w_spatial"], w["bfull"])
    merged = _gate_fwd(o_sb, o_pool, o_gm, w["w_br_sb"], w["w_br_pool"], w["w_br_gm"], rest)
    y = _mm("out_proj", merged, w["w_out"], "nn", (F32,))
    x1 = _rms_fwd(y, w["g_mix_post"], res=x, out_dtype=F32, name="rms_mix_post")
    h2 = _rms_fwd(x1, w["g_ff_pre"], res=None, out_dtype=MX, name="rms_ff_pre")
    r = _mm("ff_in", h2, w["w_ff_in"], "nn", (MX,), epilogue=_relu2)
    ff = _mm("ff_out", r, w["w_ff_out"], "nn", (F32,))
    x2 = _rms_fwd(ff, w["g_ff_post"], res=x1, out_dtype=F32, name="rms_ff_post")
    saved = dict(x=x, h=h, qkv=qkv, rest=rest, o_sb=o_sb, pooled=pooled, o_pool=o_pool, o_gm=o_gm,
                 merged=merged, y=y, x1=x1, h2=h2, r=r, ff=ff)
    return x2, saved


def _layer_bwd(dx2, w, sv):
    dff, dg_ff_post = _rms_bwd(sv["ff"], w["g_ff_post"], dx2, res=None, out_dtype=MX, name="rms_ff_post_bwd")
    da = _mm("ff_out_dx", dff, w["w_ff_out"], "nt", (MX,), epilogue=_relu2_bwd, extras=(sv["r"],))
    dw_ff_out = _mm("ff_out_dw", sv["r"], dff, "tn", (MX,))
    dh2 = _mm("ff_in_dx", da, w["w_ff_in"], "nt", (F32,))
    dw_ff_in = _mm("ff_in_dw", sv["h2"], da, "tn", (MX,))
    dx1, dg_ff_pre = _rms_bwd(sv["x1"], w["g_ff_pre"], dh2, res=dx2, out_dtype=F32, name="rms_ff_pre_bwd")

    dy, dg_mix_post = _rms_bwd(sv["y"], w["g_mix_post"], dx1, res=None, out_dtype=MX, name="rms_mix_post_bwd")
    dmerged = _mm("out_proj_dx", dy, w["w_out"], "nt", (F32,))
    dw_out = _mm("out_proj_dw", sv["merged"], dy, "tn", (MX,))
    db_sb, db_pool, db_gm, dg0, dg1, dg2 = _gate_bwd(
        sv["o_sb"], sv["o_pool"], sv["o_gm"], w["w_br_sb"], w["w_br_pool"], w["w_br_gm"], sv["rest"], dmerged)
    do_sb = _mm("br_sb_dx", db_sb, w["w_br_sb"], "nt", (MX,))
    dw_br_sb = _mm("br_sb_dw", sv["o_sb"], db_sb, "tn", (MX,))
    do_pool = _mm("br_pool_dx", db_pool, w["w_br_pool"], "nt", (F32,))
    dw_br_pool = _mm("br_pool_dw", sv["o_pool"], db_pool, "tn", (MX,))
    do_gm = _mm("br_gm_dx", db_gm, w["w_br_gm"], "nt", (F32,))
    dw_br_gm = _mm("br_gm_dw", sv["o_gm"], db_gm, "tn", (MX,))

    du, dv_gm, dws, dbfull, dgain = _gm_bwd(do_gm, sv["rest"], w["gm_gain"], w["w_spatial"], w["bfull"])
    dp, dwbd, dscale = _pool_bwd(do_pool, sv["pooled"], w["wbd"], w["pool_scale"])
    dq, dk, dv = _attn_bwd(sv["qkv"], do_sb)
    dproj = jnp.concatenate([dq, dk.astype(MX), dv.astype(MX), dp, du, dv_gm, dg0, dg1, dg2], axis=1)
    dh = _mm("proj_dx", dproj, w["w_in"], "nt", (F32,))
    dw_in = _mm("proj_dw", sv["h"], dproj, "tn", (MX,))
    dx, dg_mix_pre = _rms_bwd(sv["x"], w["g_mix_pre"], dh, res=dx1, out_dtype=F32, name="rms_mix_pre_bwd")

    grads = dict(
        w_in=dw_in, w_br_sb=dw_br_sb, w_br_pool=dw_br_pool, w_br_gm=dw_br_gm, w_out=dw_out,
        w_ff_in=dw_ff_in, w_ff_out=dw_ff_out,
        w_pool=jnp.stack([dwbd[g * GROUP:(g + 1) * GROUP, g * GROUP:(g + 1) * GROUP] for g in range(N_GROUPS)]),
        pool_scale=dscale[0], gm_gain=dgain[0], w_spatial=dws,
        b_spatial=dbfull.reshape(CHUNK, N_GROUPS, GROUP).sum(axis=-1).T,
        g_mix_pre=dg_mix_pre[0], g_mix_post=dg_mix_post[0], g_ff_pre=dg_ff_pre[0], g_ff_post=dg_ff_post[0])
    return dx, grads


def _layer_operands(full_l, small_l):
    wbd = jnp.zeros((PW, PW), F32)
    for g in range(N_GROUPS):
        wbd = wbd.at[g * GROUP:(g + 1) * GROUP, g * GROUP:(g + 1) * GROUP].set(small_l["w_pool"][g])
    w = dict(full_l)
    w["wbd"] = wbd.astype(MX)
    w["bfull"] = jnp.repeat(small_l["b_spatial"].T, GROUP, axis=1)
    w["w_spatial"] = small_l["w_spatial"]
    for n in ("pool_scale", "gm_gain", "g_mix_pre", "g_mix_post", "g_ff_pre", "g_ff_post"):
        w[n] = small_l[n][None, :]
    return w


def _local_step(x, target, layers):
    saved = []
    for w in layers:
        x, sv = _layer_fwd(x, w)
        saved.append(sv)
    dx, sq = _loss_head(x, target)
    loss = 0.5 * jnp.sum(sq) / x.shape[1]
    grads = [None] * len(layers)
    for l in reversed(range(len(layers))):
        dx, grads[l] = _layer_bwd(dx, layers[l], saved[l])
    return loss, dx, grads


def _mesh_place():
    x, y, c = lax.axis_index("x"), lax.axis_index("y"), lax.axis_index("c")

    def peer(k):
        px = 1 - x if k & 4 else x
        py = 1 - y if k & 2 else y
        pc = 1 - c if k & 1 else c
        return (px, py, pc), 4 * px + 2 * py + pc

    return 4 * x + 2 * y + c, peer


def _shard_window(ref, name, shard_shape, idx):
    if name == "w_in":
        return ref.at[idx]
    _, a, b = shard_shape
    if BIG_SHARD_AXIS[name] == 2:
        return ref.at[:, :, pl.ds(pl.multiple_of(idx * b, b), b)]
    return ref.at[:, pl.ds(pl.multiple_of(idx * a, a), a), :]


def _full_shape(name, shard_shape):
    depth, a, b = shard_shape
    if name == "w_in":
        return (N_DEV, depth, a, b)
    return (depth, a, N_DEV * b) if BIG_SHARD_AXIS[name] == 2 else (depth, N_DEV * a, b)


def _exchange_big(arrays, shard_shapes, *, scatter, name):
    n = len(BIG)

    def body(*refs):
        ins, outs = refs[:n], refs[n:2 * n]
        send_sems, recv_sems, local_sems = refs[2 * n:]
        me, peer = _mesh_place()

        def window(ref, p, idx):
            return _shard_window(ref, BIG[p], shard_shapes[BIG[p]], idx)

        def copy(p, k, landing):
            dev, idx = peer(k)
            if scatter:
                src, dst = window(ins[p], p, idx), outs[p].at[idx if landing else me]
            else:
                src, dst = ins[p], window(outs[p], p, idx if landing else me)
            return pltpu.make_async_remote_copy(
                src_ref=src, dst_ref=dst, send_sem=send_sems.at[p, k - 1], recv_sem=recv_sems.at[p, k - 1],
                device_id=dev, device_id_type=pl.DeviceIdType.MESH)

        started = []
        for p in range(n):
            if scatter:
                mine = pltpu.make_async_copy(window(ins[p], p, me), outs[p].at[me], local_sems.at[p])
            else:
                mine = pltpu.make_async_copy(ins[p], window(outs[p], p, me), local_sems.at[p])
            mine.start()
            started.append(mine)
        sends = [copy(p, k, False) for p in range(n) for k in range(1, N_DEV)]
        for cp in sends:
            cp.start()
        for p in range(n):
            for k in range(1, N_DEV):
                copy(p, k, True).wait_recv()
        for cp in sends:
            cp.wait_send()
        for mine in started:
            mine.wait()

    if scatter:
        out_shapes = [(N_DEV,) + tuple(shard_shapes[nm]) for nm in BIG]
    else:
        out_shapes = [_full_shape(nm, shard_shapes[nm]) for nm in BIG]
    outs = pl.pallas_call(
        body, name=name,
        in_specs=[pl.BlockSpec(memory_space=pl.ANY)] * n,
        out_specs=[pl.BlockSpec(memory_space=pl.ANY)] * n,
        out_shape=[jax.ShapeDtypeStruct(s, a.dtype) for s, a in zip(out_shapes, arrays)],
        scratch_shapes=[pltpu.SemaphoreType.DMA((n, N_DEV - 1)), pltpu.SemaphoreType.DMA((n, N_DEV - 1)),
                        pltpu.SemaphoreType.DMA((n,))],
    )(*arrays)
    return dict(zip(BIG, outs))


def _gather_rows(inp, *, name):
    def body(in_ref, out_ref, send_sems, recv_sems, local_sem):
        me, peer = _mesh_place()

        def copy(k, landing):
            dev, idx = peer(k)
            return pltpu.make_async_remote_copy(
                src_ref=in_ref, dst_ref=out_ref.at[idx if landing else me],
                send_sem=send_sems.at[k - 1], recv_sem=recv_sems.at[k - 1],
                device_id=dev, device_id_type=pl.DeviceIdType.MESH)

        mine = pltpu.make_async_copy(in_ref, out_ref.at[me], local_sem)
        mine.start()
        sends = [copy(k, False) for k in range(1, N_DEV)]
        for cp in sends:
            cp.start()
        for k in range(1, N_DEV):
            copy(k, True).wait_recv()
        for cp in sends:
            cp.wait_send()
        mine.wait()

    return pl.pallas_call(
        body, name=name,
        in_specs=[pl.BlockSpec(memory_space=pl.ANY)],
        out_specs=pl.BlockSpec(memory_space=pl.ANY),
        out_shape=jax.ShapeDtypeStruct((N_DEV,) + inp.shape, inp.dtype),
        scratch_shapes=[pltpu.SemaphoreType.DMA((N_DEV - 1,)), pltpu.SemaphoreType.DMA((N_DEV - 1,)),
                        pltpu.SemaphoreType.DMA],
    )(inp)


def _sum_devices(recv, name):
    _, rows, cols = recv.shape
    tr = rows
    for cand in (512, 256, 128, 64, 32, 16, 8):
        if rows % cand == 0:
            tr = cand
            break

    def body(r_ref, out_ref):
        acc = r_ref[0].astype(F32)
        for d in range(1, N_DEV):
            acc = acc + r_ref[d].astype(F32)
        out_ref[...] = acc

    return pl.pallas_call(
        body, name=name, grid=(rows // tr,),
        in_specs=[pl.BlockSpec((N_DEV, tr, cols), lambda i: (0, i, 0))],
        out_specs=pl.BlockSpec((tr, cols), lambda i: (i, 0)),
        out_shape=jax.ShapeDtypeStruct((rows, cols), F32),
        compiler_params=_params(("parallel",)),
    )(recv)


def _adamw(w, g, m, v, name):
    rows, cols = w.shape
    tr = rows
    for cand in (512, 256, 128, 64, 32, 16, 8):
        if rows % cand == 0:
            tr = cand
            break
    c1 = 1.0 - ADAM_B1 ** ADAM_STEP
    c2 = 1.0 - ADAM_B2 ** ADAM_STEP

    def body(w_ref, g_ref, m_ref, v_ref, d_ref, nm_ref, nv_ref):
        gv = g_ref[...]
        nm = ADAM_B1 * m_ref[...] + (1.0 - ADAM_B1) * gv
        nv = ADAM_B2 * v_ref[...] + (1.0 - ADAM_B2) * (gv * gv)
        nm_ref[...] = nm
        nv_ref[...] = nv
        d_ref[...] = -ADAM_LR * ((nm / c1) / (jnp.sqrt(nv / c2) + ADAM_EPS) + ADAM_WD * w_ref[...])

    spec = pl.BlockSpec((tr, cols), lambda i: (i, 0))
    return pl.pallas_call(
        body, name=name, grid=(rows // tr,),
        in_specs=[spec] * 4, out_specs=[spec] * 3,
        out_shape=[jax.ShapeDtypeStruct((rows, cols), F32)] * 3,
        compiler_params=_params(("parallel",)),
    )(w, g, m, v)


def _pack_rows(flat, dtype):
    n = flat.shape[-1]
    rows = -(-n // PACK_COLS)
    rows = -(-rows // 16) * 16
    pad = rows * PACK_COLS - n
    flat = jnp.pad(flat, [(0, 0)] * (flat.ndim - 1) + [(0, pad)])
    return flat.reshape(flat.shape[:-1] + (rows, PACK_COLS)).astype(dtype)


def kernel(x, w_in, w_pool, pool_scale, gm_gain, w_spatial, b_spatial, w_br_sb, w_br_pool, w_br_gm, w_out, g_mix_pre, g_mix_post, g_ff_pre, g_ff_post, w_ff_in, w_ff_out, loss_target, m_w_in, m_w_pool, m_pool_scale, m_gm_gain, m_w_spatial, m_b_spatial, m_w_br_sb, m_w_br_pool, m_w_br_gm, m_w_out, m_g_mix_pre, m_g_mix_post, m_g_ff_pre, m_g_ff_post, m_w_ff_in, m_w_ff_out, v_w_in, v_w_pool, v_pool_scale, v_gm_gain, v_w_spatial, v_b_spatial, v_w_br_sb, v_w_br_pool, v_w_br_gm, v_w_out, v_g_mix_pre, v_g_mix_post, v_g_ff_pre, v_g_ff_post, v_w_ff_in, v_w_ff_out):
    args = dict(locals())
    weights = {n: args[n] for n in WEIGHTS}
    mom_m = {n: args["m_" + n] for n in WEIGHTS}
    mom_v = {n: args["v_" + n] for n in WEIGHTS}

    shard_shapes = {n: weights[n].shape for n in BIG}
    depth, d_model, w_in_cols = shard_shapes["w_in"]
    full = _exchange_big([weights[n].astype(MX) for n in BIG], shard_shapes, scatter=False, name="gather_weights")
    full["w_in"] = jnp.transpose(full["w_in"], (1, 2, 0, 3)).reshape(depth, d_model, N_DEV * w_in_cols)
    layers = [_layer_operands({n: full[n][l] for n in BIG}, {n: weights[n][l] for n in SMALL})
              for l in range(depth)]

    loss, dx, layer_grads = _local_step(x[0], loss_target[0], layers)
    grads = {n: jnp.stack([g[n] for g in layer_grads]) for n in BIG + SMALL}
    loss = lax.psum(loss, ("x", "y", "c"))

    grads["w_in"] = jnp.transpose(grads["w_in"].reshape(depth, d_model, N_DEV, w_in_cols), (2, 0, 1, 3))
    recv = _exchange_big([grads[n] for n in BIG], shard_shapes, scatter=True, name="scatter_grads")
    grad_w = {}
    for n in BIG:
        _, a, b = shard_shapes[n]
        grad_w[n] = _sum_devices(recv[n].reshape(N_DEV, depth * a, b), "sum_grads_" + n).reshape(shard_shapes[n])
    small_sizes = [int(np.prod(weights[n].shape)) for n in SMALL]
    small_off = np.concatenate([[0], np.cumsum(small_sizes)])
    small_flat = jnp.concatenate([grads[n].reshape(-1) for n in SMALL])
    g_small = _sum_devices(_gather_rows(_pack_rows(small_flat, F32), name="gather_small_grads"),
                           "sum_small_grads").reshape(-1)
    for n, lo, hi in zip(SMALL, small_off[:-1], small_off[1:]):
        grad_w[n] = g_small[lo:hi].reshape(weights[n].shape)

    delta, new_m, new_v = {}, {}, {}
    for n in WEIGHTS:
        shape = weights[n].shape
        view = (-1, shape[-1])
        d, nm, nv = _adamw(weights[n].reshape(view), grad_w[n].reshape(view), mom_m[n].reshape(view),
                           mom_v[n].reshape(view), "adamw_" + n)
        delta[n], new_m[n], new_v[n] = d.reshape(shape), nm.reshape(shape), nv.reshape(shape)

    return (loss, dx[None], *[grad_w[n] for n in WEIGHTS], *[delta[n] for n in WEIGHTS],
            *[new_m[n] for n in WEIGHTS], *[new_v[n] for n in WEIGHTS])
```

```python
import functools

import numpy as np
import jax
import jax.numpy as jnp
from jax import lax
from jax.experimental import pallas as pl
from jax.experimental.pallas import tpu as pltpu

F32 = jnp.float32
MX = jnp.bfloat16

D = 1024
SB_W = 512
HEAD_PAIR = 128
PW = 256
GROUP = 64
N_GROUPS = 4
CHUNK = 128
POOL_WINDOWS = (2, 4, 8, 16)
D_FF = 4096
D_IN = 5376
QKV_W = 3 * SB_W
REST_W = D_IN - QKV_W
GATE_COL0 = 3 * PW
RMS_EPS = 1e-6
N_DEV = 8
DEPTH = 4

ADAM_LR = 0.001
ADAM_B1 = 0.9
ADAM_B2 = 0.999
ADAM_EPS = 1e-08
ADAM_WD = 0.01
ADAM_STEP = 10

SKIP_LOG = -120.0

VMEM_LIMIT = 48 * 1024 * 1024
W_IN_TILE = 768
PACK_COLS = 1024

BIG = ("w_in", "w_br_sb", "w_br_pool", "w_br_gm", "w_out", "w_ff_in", "w_ff_out")
BIG_SHARD_AXIS = {"w_in": 2, "w_br_sb": 2, "w_br_pool": 2, "w_br_gm": 2, "w_out": 1, "w_ff_in": 2, "w_ff_out": 1}
SMALL = ("w_pool", "pool_scale", "gm_gain", "w_spatial", "b_spatial",
         "g_mix_pre", "g_mix_post", "g_ff_pre", "g_ff_post")
WEIGHTS = ("w_in", "w_pool", "pool_scale", "gm_gain", "w_spatial", "b_spatial", "w_br_sb", "w_br_pool",
           "w_br_gm", "w_out", "g_mix_pre", "g_mix_post", "g_ff_pre", "g_ff_post", "w_ff_in", "w_ff_out")


def _params(sem, vmem=VMEM_LIMIT):
    return pltpu.CompilerParams(dimension_semantics=sem, vmem_limit_bytes=vmem)


def _dot(a, b, dims):
    return lax.dot_general(a, b, (dims, ((), ())), preferred_element_type=F32)


NN = ((1,), (0,))
NT = ((1,), (1,))
TN = ((0,), (0,))


def _split_dot(x, m, left, nsplit):
    acc = None
    r = x
    for s in range(nsplit):
        p = r.astype(jnp.bfloat16)
        d = _dot(m, p, NN) if left else _dot(p, m, NN)
        acc = d if acc is None else acc + d
        if s + 1 < nsplit:
            r = r - p.astype(F32)
    return acc


def _matmul(a, b, *, mode, name, tm, tn, tk, out_dtypes, b_col0=0, n_out=None, epilogue=None, extras=()):
    if mode == "nn":
        m, kdim = a.shape
        n = b.shape[1] if n_out is None else n_out
        dims = NN
    elif mode == "nt":
        m, kdim = a.shape
        n = b.shape[0]
        dims = NT
    else:
        kdim, m = a.shape
        n = b.shape[1]
        dims = TN
    tm, tn, tk = min(tm, m), min(tn, n), min(tk, kdim)
    assert m % tm == 0 and n % tn == 0 and kdim % tk == 0 and b_col0 % tn == 0, (name, m, n, kdim)
    if mode == "nn":
        a_spec = pl.BlockSpec((tm, tk), lambda i, j, k: (i, k))
        b_spec = pl.BlockSpec((tk, tn), lambda i, j, k: (k, j + b_col0 // tn))
    elif mode == "nt":
        a_spec = pl.BlockSpec((tm, tk), lambda i, j, k: (i, k))
        b_spec = pl.BlockSpec((tn, tk), lambda i, j, k: (j, k))
    else:
        a_spec = pl.BlockSpec((tk, tm), lambda i, j, k: (k, i))
        b_spec = pl.BlockSpec((tk, tn), lambda i, j, k: (k, j))
    nk = kdim // tk
    n_extra = len(extras)
    n_outs = len(out_dtypes)
    o_spec = pl.BlockSpec((tm, tn), lambda i, j, k: (i, j))

    def body(a_ref, b_ref, *refs):
        extra_refs = refs[:n_extra]
        out_refs = refs[n_extra:n_extra + n_outs]
        k = pl.program_id(2)
        part = _dot(a_ref[...].astype(MX), b_ref[...].astype(MX), dims)

        def finish(acc):
            outs = (acc,) if epilogue is None else epilogue(acc, *[e[...] for e in extra_refs])
            for o_ref, val in zip(out_refs, outs):
                o_ref[...] = val.astype(o_ref.dtype)

        if nk == 1:
            finish(part)
            return
        acc_ref = refs[-1]

        @pl.when(k == 0)
        def _():
            acc_ref[...] = part

        @pl.when(jnp.logical_and(k > 0, k < nk - 1))
        def _():
            acc_ref[...] += part

        @pl.when(k == nk - 1)
        def _():
            finish(acc_ref[...] + part)

    outs = pl.pallas_call(
        body, name=name,
        grid=(m // tm, n // tn, nk),
        in_specs=[a_spec, b_spec] + [o_spec] * n_extra,
        out_specs=[o_spec] * n_outs,
        out_shape=[jax.ShapeDtypeStruct((m, n), dt) for dt in out_dtypes],
        scratch_shapes=[pltpu.VMEM((tm, tn), F32)] if nk > 1 else [],
        compiler_params=_params(("parallel", "parallel", "arbitrary")),
    )(a, b, *extras)
    return outs[0] if n_outs == 1 else outs


ROW_TILE = 512


def _rms_fwd(x, g, *, res, out_dtype, name):
    s, d = x.shape
    tr = min(ROW_TILE, s)
    has_res = res is not None

    def body(x_ref, g_ref, *refs):
        out_ref = refs[-1]
        xv = x_ref[...]
        y = xv * lax.rsqrt(jnp.mean(xv * xv, axis=-1, keepdims=True) + RMS_EPS) * g_ref[...]
        if has_res:
            y = refs[0][...] + y
        out_ref[...] = y.astype(out_ref.dtype)

    row = pl.BlockSpec((tr, d), lambda i: (i, 0))
    vec = pl.BlockSpec((1, d), lambda i: (0, 0))
    return pl.pallas_call(
        body, name=name, grid=(s // tr,),
        in_specs=[row, vec] + ([row] if has_res else []),
        out_specs=row, out_shape=jax.ShapeDtypeStruct((s, d), out_dtype),
        compiler_params=_params(("parallel",)),
    )(x, g, *([res] if has_res else []))


def _rms_bwd(x, g, dout, *, res, out_dtype, name):
    s, d = x.shape
    tr = min(ROW_TILE, s)
    has_res = res is not None

    def body(x_ref, g_ref, do_ref, *refs):
        dx_ref, dg_ref = refs[-2], refs[-1]
        i = pl.program_id(0)
        xv = x_ref[...]
        do = do_ref[...]
        r = lax.rsqrt(jnp.mean(xv * xv, axis=-1, keepdims=True) + RMS_EPS)
        xhat = xv * r
        dxhat = do * g_ref[...]
        dx = r * (dxhat - xhat * jnp.mean(dxhat * xhat, axis=-1, keepdims=True))
        if has_res:
            dx = refs[0][...] + dx
        dx_ref[...] = dx.astype(dx_ref.dtype)

        @pl.when(i == 0)
        def _():
            dg_ref[...] = jnp.zeros_like(dg_ref)

        dg_ref[...] += jnp.sum(do * xhat, axis=0, keepdims=True)

    row = pl.BlockSpec((tr, d), lambda i: (i, 0))
    vec = pl.BlockSpec((1, d), lambda i: (0, 0))
    return pl.pallas_call(
        body, name=name, grid=(s // tr,),
        in_specs=[row, vec, row] + ([row] if has_res else []),
        out_specs=[row, vec],
        out_shape=[jax.ShapeDtypeStruct((s, d), out_dtype), jax.ShapeDtypeStruct((1, d), F32)],
        compiler_params=_params(("arbitrary",)),
    )(x, g, dout, *([res] if has_res else []))


def _loss_head(y, target):
    s, d = y.shape
    tr = min(ROW_TILE, s)

    def body(y_ref, t_ref, dy_ref, sq_ref):
        i = pl.program_id(0)
        err = y_ref[...] - t_ref[...]
        dy_ref[...] = err * (1.0 / d)

        @pl.when(i == 0)
        def _():
            sq_ref[...] = jnp.zeros_like(sq_ref)

        sq_ref[...] += jnp.sum(err * err, axis=0, keepdims=True)

    row = pl.BlockSpec((tr, d), lambda i: (i, 0))
    vec = pl.BlockSpec((1, d), lambda i: (0, 0))
    return pl.pallas_call(
        body, name="loss_head", grid=(s // tr,),
        in_specs=[row, row], out_specs=[row, vec],
        out_shape=[jax.ShapeDtypeStruct((s, d), F32), jax.ShapeDtypeStruct((1, d), F32)],
        compiler_params=_params(("arbitrary",)),
    )(y, target)


def _sb_masks():
    row = lax.broadcasted_iota(jnp.int32, (CHUNK, CHUNK), 0)
    col = lax.broadcasted_iota(jnp.int32, (CHUNK, CHUNK), 1)
    return row, col


SB_WINDOW = 2 * CHUNK

def _sb_consts():
    row = lax.broadcasted_iota(jnp.int32, (CHUNK, SB_WINDOW), 0)
    col = lax.broadcasted_iota(jnp.int32, (CHUNK, SB_WINDOW), 1)
    lane_hi = lax.broadcasted_iota(jnp.int32, (CHUNK, HEAD_PAIR), 1) >= GROUP
    return col - row, col, lane_hi


def _sb_window(i, m, dcol, col):
    start = jnp.maximum((i - 2 * m - 1) * CHUNK, 0)
    hi = (i - 2 * m + 1) * CHUNK
    keep = jnp.logical_and(dcol < i * CHUNK - start, col < hi - start)
    return pl.multiple_of(start, CHUNK), keep


def _sb_scores(qm, kw, keep):
    z = _dot(qm, kw, NT)
    t = jnp.log(1.0 + jnp.exp(-jnp.abs(z)))
    ln = -(jnp.maximum(z, 0.0) + t)
    lb = ln + z
    ln = jnp.where(keep, ln, 0.0)
    return ln, lb


def _sb_tri(upper):
    r = np.arange(SB_WINDOW)
    m = (r[:, None] > r[None, :]) if upper else (r[:, None] < r[None, :])
    return jnp.asarray(m, jnp.bfloat16)


def _riding_exchange(exchange):
    if exchange is None:
        return [], [], [], []
    arrays, shard_shapes, scatter = exchange
    shapes = [jax.ShapeDtypeStruct(s, a.dtype)
              for s, a in zip(_exchange_out_shapes(shard_shapes, scatter), arrays)]
    return list(arrays), shapes, [pl.BlockSpec(memory_space=pl.ANY)] * len(arrays), _exchange_sems()


def _ride(exchange, ins, outs, sems, first, last):
    if exchange is None:
        return lambda: None
    start, finish = _exchange_plan(ins, outs, sems, exchange[1], exchange[2])
    pl.when(first)(start)
    return lambda: pl.when(last)(finish)


def _attn_fwd(qkv, exchange=None):
    s = qkv.shape[0]
    nblk = s // CHUNK
    n_pairs = SB_W // HEAD_PAIR
    x_arrays, x_shapes, x_specs, x_sems = _riding_exchange(exchange)
    nx = len(x_arrays)

    def body(q_ref, k_ref, v_ref, up_ref, *refs):
        o_ref = refs[nx]
        i = pl.program_id(1)
        pair = pl.program_id(0)
        end_exchange = _ride(exchange, refs[:nx], refs[nx + 1:2 * nx + 1], refs[2 * nx + 1:],
                             jnp.logical_and(pair == 0, i == 0),
                             jnp.logical_and(pair == n_pairs - 1, i == nblk - 1))
        dcol, col, lane_hi = _sb_consts()
        q = q_ref[...]
        zero = jnp.zeros_like(q)
        qms = (jnp.where(lane_hi, zero, q) * 0.125, jnp.where(lane_hi, q, zero) * 0.125)

        def cond(st):
            return jnp.logical_and(st[0] <= i // 2, st[1])

        def loop(st):
            m = st[0]
            start, keep = _sb_window(i, m, dcol, col)
            kw = k_ref[pl.ds(start, SB_WINDOW), :]
            vw = v_ref[pl.ds(start, SB_WINDOW), :]
            new = []
            for h in range(2):
                c, acc = st[2 + 2 * h], st[3 + 2 * h]
                ln, lb = _sb_scores(qms[h], kw, keep)
                suffix = _split_dot(ln, up_ref[...], False, 2) + c
                a = jnp.where(keep, jnp.exp(lb + suffix), 0.0)
                new += [c + jnp.sum(ln, axis=1, keepdims=True), acc + _dot(a.astype(MX), vw, NN)]
            active = jnp.maximum(jnp.max(new[0]), jnp.max(new[2])) > SKIP_LOG
            return (m + 1, active, *new)

        c0 = jnp.zeros((CHUNK, 1), F32)
        a0 = jnp.zeros((CHUNK, HEAD_PAIR), F32)
        st = lax.while_loop(cond, loop, (i * 0, i >= 0, c0, a0, c0, a0))
        o_ref[...] = jnp.where(lane_hi, st[5], st[3]).astype(o_ref.dtype)
        end_exchange()

    outs = pl.pallas_call(
        body, name="attn_fwd" if exchange is None else "attn_fwd_gather", grid=(n_pairs, nblk),
        in_specs=[pl.BlockSpec((CHUNK, HEAD_PAIR), lambda p, i: (i, p)),
                  pl.BlockSpec((s, HEAD_PAIR), lambda p, i: (0, n_pairs + p)),
                  pl.BlockSpec((s, HEAD_PAIR), lambda p, i: (0, 2 * n_pairs + p)),
                  pl.BlockSpec((SB_WINDOW, SB_WINDOW), lambda p, i: (0, 0))] + x_specs,
        out_specs=[pl.BlockSpec((CHUNK, HEAD_PAIR), lambda p, i: (i, p))] + x_specs,
        out_shape=[jax.ShapeDtypeStruct((s, SB_W), MX)] + x_shapes,
        scratch_shapes=x_sems,
        compiler_params=_params(("arbitrary", "arbitrary")),
    )(qkv, qkv, qkv, _sb_tri(True), *x_arrays)
    return outs[0], dict(zip(BIG, outs[1:]))


def _attn_bwd(qkv, d_o, exchange=None):
    s = qkv.shape[0]
    nblk = s // CHUNK
    n_pairs = SB_W // HEAD_PAIR
    n_win = nblk // 2 + 1
    x_arrays, x_shapes, x_specs, x_sems = _riding_exchange(exchange)
    nx = len(x_arrays)

    def body(q_ref, k_ref, v_ref, do_ref, up_ref, lo_ref, *refs):
        dq_ref, dk_ref, dv_ref = refs[nx:nx + 3]
        e_s, sig_s = refs[2 * nx + 3:2 * nx + 5]
        i = pl.program_id(1)
        pair = pl.program_id(0)
        end_exchange = _ride(exchange, refs[:nx], refs[nx + 3:2 * nx + 3], refs[2 * nx + 5:],
                             jnp.logical_and(pair == 0, i == 0),
                             jnp.logical_and(pair == n_pairs - 1, i == nblk - 1))
        dcol, col, lane_hi = _sb_consts()
        q = q_ref[...]
        do = do_ref[...]
        zero = jnp.zeros_like(q)
        qms = (jnp.where(lane_hi, zero, q) * 0.125, jnp.where(lane_hi, q, zero) * 0.125)
        doms = (jnp.where(lane_hi, zero, do), jnp.where(lane_hi, do, zero))

        @pl.when(i == 0)
        def _():
            dk_ref[...] = jnp.zeros_like(dk_ref)
            dv_ref[...] = jnp.zeros_like(dv_ref)

        def cond(st):
            return jnp.logical_and(st[0] <= i // 2, st[1])

        def loop(st):
            m = st[0]
            start, keep = _sb_window(i, m, dcol, col)
            kw = k_ref[pl.ds(start, SB_WINDOW), :]
            vw = v_ref[pl.ds(start, SB_WINDOW), :]
            cs = []
            dv = None
            for h in range(2):
                c = st[2 + h]
                ln, lb = _sb_scores(qms[h], kw, keep)
                suffix = _split_dot(ln, up_ref[...], False, 2) + c
                a = jnp.where(keep, jnp.exp(lb + suffix), 0.0)
                e_s[h, m] = a * _dot(doms[h], vw, NT)
                sig_s[h, m] = jnp.exp(lb)
                part = _dot(a.astype(MX), doms[h], TN)
                dv = part if dv is None else dv + part
                cs.append(c + jnp.sum(ln, axis=1, keepdims=True))
            dv_ref[pl.ds(start, SB_WINDOW), :] += dv
            active = jnp.maximum(jnp.max(cs[0]), jnp.max(cs[1])) > SKIP_LOG
            return (m + 1, active, *cs)

        c0 = jnp.zeros((CHUNK, 1), F32)
        n_seen = lax.while_loop(cond, loop, (i * 0, i >= 0, c0, c0))[0]

        def up(t, st):
            m = n_seen - 1 - t
            start, keep = _sb_window(i, m, dcol, col)
            kw = k_ref[pl.ds(start, SB_WINDOW), :]
            new = []
            dk = None
            for h in range(2):
                cp, dq = st[2 * h], st[2 * h + 1]
                e = e_s[h, m]
                sig = sig_s[h, m]
                prefix = _split_dot(e, lo_ref[...], False, 2) + cp
                dz = jnp.where(keep, e * (1.0 - sig) - prefix * sig, 0.0).astype(MX)
                part = _dot(dz, qms[h], TN)
                dk = part if dk is None else dk + part
                new += [cp + jnp.sum(e, axis=1, keepdims=True), dq + _dot(dz, kw, NN)]
            dk_ref[pl.ds(start, SB_WINDOW), :] += dk
            return tuple(new)

        a0 = jnp.zeros((CHUNK, HEAD_PAIR), F32)
        st = lax.fori_loop(0, n_seen, up, (c0, a0, c0, a0))
        dq_ref[...] = (jnp.where(lane_hi, st[3], st[1]) * 0.125).astype(dq_ref.dtype)
        end_exchange()

    blk = pl.BlockSpec((CHUNK, HEAD_PAIR), lambda p, i: (i, p))
    full = pl.BlockSpec((s, HEAD_PAIR), lambda p, i: (0, p))
    tri = pl.BlockSpec((SB_WINDOW, SB_WINDOW), lambda p, i: (0, 0))
    outs = pl.pallas_call(
        body, name="attn_bwd" if exchange is None else "attn_bwd_scatter", grid=(n_pairs, nblk),
        in_specs=[blk,
                  pl.BlockSpec((s, HEAD_PAIR), lambda p, i: (0, n_pairs + p)),
                  pl.BlockSpec((s, HEAD_PAIR), lambda p, i: (0, 2 * n_pairs + p)),
                  blk, tri, tri] + x_specs,
        out_specs=[blk, full, full] + x_specs,
        out_shape=[jax.ShapeDtypeStruct((s, SB_W), MX), jax.ShapeDtypeStruct((s, SB_W), F32),
                   jax.ShapeDtypeStruct((s, SB_W), F32)] + x_shapes,
        scratch_shapes=[pltpu.VMEM((2, n_win, CHUNK, SB_WINDOW), F32),
                        pltpu.VMEM((2, n_win, CHUNK, SB_WINDOW), F32)] + x_sems,
        compiler_params=_params(("arbitrary", "arbitrary"), 56 * 1024 * 1024),
    )(qkv, qkv, qkv, d_o, _sb_tri(True), _sb_tri(False), *x_arrays)
    return outs[0], outs[1], outs[2], dict(zip(BIG, outs[3:]))


POOL_TILE = 256
POOL_HALO = 128


def _pool_bands(tile, transpose):
    cur = np.zeros((N_GROUPS, tile, tile), np.float32)
    halo = np.zeros((N_GROUPS, tile, POOL_HALO), np.float32)
    t = np.arange(tile)[:, None]
    for g, w in enumerate(POOL_WINDOWS):
        if not transpose:
            p = np.arange(tile)[None, :]
            cur[g] = ((t - p >= 0) & (t - p < w))
            ph = np.arange(POOL_HALO)[None, :] - POOL_HALO
            halo[g] = (t - ph < w)
        else:
            p = np.arange(tile)[None, :]
            cur[g] = ((p - t >= 0) & (p - t < w))
            ph = np.arange(POOL_HALO)[None, :] + tile
            halo[g] = (ph - t < w)
    return jnp.asarray(cur, jnp.bfloat16), jnp.asarray(halo, jnp.bfloat16)


def _pool_count(i, tile, rows, row0):
    pos = (i * tile + row0 + lax.broadcasted_iota(jnp.int32, (rows, PW), 0)).astype(F32)
    grp = lax.broadcasted_iota(jnp.int32, (rows, PW), 1) // GROUP
    win = jnp.where(grp == 0, float(POOL_WINDOWS[0]),
                    jnp.where(grp == 1, float(POOL_WINDOWS[1]),
                              jnp.where(grp == 2, float(POOL_WINDOWS[2]), float(POOL_WINDOWS[3]))))
    return jnp.minimum(pos + 1.0, win), grp


def _pool_fwd(rest, wbd, scale):
    s = rest.shape[0]
    tile = min(POOL_TILE, s)
    halo_per_tile = tile // POOL_HALO
    bcur, bhalo = _pool_bands(tile, False)

    def body(cur_ref, prev_ref, bcur_ref, bhalo_ref, wbd_ref, scale_ref, pooled_ref, out_ref):
        i = pl.program_id(0)
        cur = cur_ref[...]
        prev = jnp.where(i > 0, prev_ref[...], 0.0)
        count, grp = _pool_count(i, tile, tile, 0)
        win = jnp.zeros((tile, PW), F32)
        for g in range(N_GROUPS):
            wsum = _split_dot(cur, bcur_ref[g], True, 3) + _split_dot(prev, bhalo_ref[g], True, 3)
            win = jnp.where(grp == g, wsum, win)
        pooled = (win / count - cur).astype(MX)
        pooled_ref[...] = pooled
        out_ref[...] = (_dot(pooled, wbd_ref[...], NN) * scale_ref[...]).astype(out_ref.dtype)

    tile_spec = pl.BlockSpec((tile, PW), lambda i: (i, 0))
    const3 = lambda shape: pl.BlockSpec(shape, lambda i: (0, 0, 0))
    const2 = lambda shape: pl.BlockSpec(shape, lambda i: (0, 0))
    return pl.pallas_call(
        body, name="pool_fwd", grid=(s // tile,),
        in_specs=[tile_spec,
                  pl.BlockSpec((POOL_HALO, PW), lambda i: (jnp.maximum(i * halo_per_tile - 1, 0), 0)),
                  const3(bcur.shape), const3(bhalo.shape), const2((PW, PW)), const2((1, PW))],
        out_specs=[tile_spec, tile_spec],
        out_shape=[jax.ShapeDtypeStruct((s, PW), MX), jax.ShapeDtypeStruct((s, PW), MX)],
        compiler_params=_params(("parallel",)),
    )(rest, rest, bcur, bhalo, wbd, scale)


def _pool_bwd(d_o, pooled, wbd, scale):
    s = d_o.shape[0]
    tile = min(POOL_TILE, s)
    halo_per_tile = tile // POOL_HALO
    n_halo = s // POOL_HALO
    n_tiles = s // tile
    ccur, chalo = _pool_bands(tile, True)

    def body(do_ref, nxt_ref, pooled_ref, ccur_ref, chalo_ref, wbd_ref, scale_ref, dp_ref, dw_ref, ds_ref):
        i = pl.program_id(0)
        do = do_ref[...]
        w = wbd_ref[...]
        sc = scale_ref[...]
        pooled_v = pooled_ref[...]

        @pl.when(i == 0)
        def _():
            dw_ref[...] = jnp.zeros_like(dw_ref)
            ds_ref[...] = jnp.zeros_like(ds_ref)

        ds_ref[...] += jnp.sum(do * _dot(pooled_v, w, NN), axis=0, keepdims=True)
        dmixed = (do * sc).astype(MX)
        dw_ref[...] += _dot(pooled_v, dmixed, TN)
        dpooled = _dot(dmixed, w, NT)
        nxt = jnp.where(i < n_tiles - 1, nxt_ref[...], 0.0)
        dpooled_n = _dot((nxt * sc).astype(MX), w, NT)
        count, grp = _pool_count(i, tile, tile, 0)
        count_n, _ = _pool_count(i, tile, POOL_HALO, tile)
        dq = dpooled / count
        dq_n = dpooled_n / count_n
        acc = jnp.zeros((tile, PW), F32)
        for g in range(N_GROUPS):
            wsum = _split_dot(dq, ccur_ref[g], True, 3) + _split_dot(dq_n, chalo_ref[g], True, 3)
            acc = jnp.where(grp == g, wsum, acc)
        dp_ref[...] = (acc - dpooled).astype(dp_ref.dtype)

    tile_spec = pl.BlockSpec((tile, PW), lambda i: (i, 0))
    const3 = lambda shape: pl.BlockSpec(shape, lambda i: (0, 0, 0))
    const2 = lambda shape: pl.BlockSpec(shape, lambda i: (0, 0))
    return pl.pallas_call(
        body, name="pool_bwd", grid=(n_tiles,),
        in_specs=[tile_spec,
                  pl.BlockSpec((POOL_HALO, PW), lambda i: (jnp.minimum((i + 1) * halo_per_tile, n_halo - 1), 0)),
                  tile_spec, const3(ccur.shape), const3(chalo.shape), const2((PW, PW)), const2((1, PW))],
        out_specs=[tile_spec, const2((PW, PW)), const2((1, PW))],
        out_shape=[jax.ShapeDtypeStruct((s, PW), MX), jax.ShapeDtypeStruct((PW, PW), F32),
                   jax.ShapeDtypeStruct((1, PW), F32)],
        compiler_params=_params(("arbitrary",)),
    )(d_o, d_o, pooled, ccur, chalo, wbd, scale)


GM_TILE = 512
GELU_C = 0.7978845608028654
GELU_A = 0.044715


def _gelu(x):
    return 0.5 * x * (1.0 + jnp.tanh(GELU_C * (x + GELU_A * (x * x * x))))


def _gelu_grad(x):
    th = jnp.tanh(GELU_C * (x + GELU_A * (x * x * x)))
    return 0.5 * (1.0 + th) + 0.5 * x * (1.0 - th * th) * (GELU_C * (1.0 + 3.0 * GELU_A * (x * x)))


def _gm_common(ws_ref):
    row, col = _sb_masks()
    tril = row >= col
    wsm = [jnp.where(tril, ws_ref[g], 0.0).astype(MX) for g in range(N_GROUPS)]
    grp = lax.broadcasted_iota(jnp.int32, (CHUNK, PW), 1) // GROUP
    return tril, wsm, grp


def _gm_mix(wsm, grp, vn_c):
    mixed = jnp.zeros((CHUNK, PW), F32)
    for g in range(N_GROUPS):
        mixed = jnp.where(grp == g, _dot(wsm[g], vn_c, NN), mixed)
    return mixed


def _gm_fwd(rest, gain, ws, bfull):
    s = rest.shape[0]
    tile = min(GM_TILE, s)

    def body(u_ref, v_ref, gain_ref, ws_ref, b_ref, out_ref):
        _, wsm, grp = _gm_common(ws_ref)
        bias = b_ref[...]
        for n in range(tile // CHUNK):
            rows = slice(n * CHUNK, (n + 1) * CHUNK)
            gu = _gelu(u_ref[rows, :])
            gv = _gelu(v_ref[rows, :])
            r = lax.rsqrt(jnp.mean(gv * gv, axis=-1, keepdims=True) + RMS_EPS)
            vn = (gv * r * gain_ref[...]).astype(MX)
            out_ref[rows, :] = (gu * (_gm_mix(wsm, grp, vn) + bias)).astype(out_ref.dtype)

    const2 = lambda shape: pl.BlockSpec(shape, lambda i: (0, 0))
    return pl.pallas_call(
        body, name="gm_fwd", grid=(s // tile,),
        in_specs=[pl.BlockSpec((tile, PW), lambda i: (i, 1)), pl.BlockSpec((tile, PW), lambda i: (i, 2)),
                  const2((1, PW)), pl.BlockSpec((N_GROUPS, CHUNK, CHUNK), lambda i: (0, 0, 0)),
                  const2((CHUNK, PW))],
        out_specs=pl.BlockSpec((tile, PW), lambda i: (i, 0)),
        out_shape=jax.ShapeDtypeStruct((s, PW), MX),
        compiler_params=_params(("parallel",)),
    )(rest, rest, gain, ws, bfull)


def _gm_bwd(d_o, rest, gain, ws, bfull):
    s = rest.shape[0]
    tile = min(GM_TILE, s)

    def body(do_ref, u_ref, v_ref, gain_ref, ws_ref, b_ref, du_ref, dv_ref, dws_ref, db_ref, dgain_ref):
        i = pl.program_id(0)
        tril, wsm, grp = _gm_common(ws_ref)
        bias = b_ref[...]
        gain_v = gain_ref[...]

        @pl.when(i == 0)
        def _():
            dws_ref[...] = jnp.zeros_like(dws_ref)
            db_ref[...] = jnp.zeros_like(db_ref)
            dgain_ref[...] = jnp.zeros_like(dgain_ref)

        for n in range(tile // CHUNK):
            rows = slice(n * CHUNK, (n + 1) * CHUNK)
            u = u_ref[rows, :]
            v = v_ref[rows, :]
            do = do_ref[rows, :]
            gu = _gelu(u)
            gv = _gelu(v)
            r = lax.rsqrt(jnp.mean(gv * gv, axis=-1, keepdims=True) + RMS_EPS)
            vhat = gv * r
            vn = (vhat * gain_v).astype(MX)
            mixed = _gm_mix(wsm, grp, vn)
            du_ref[rows, :] = (do * (mixed + bias) * _gelu_grad(u)).astype(du_ref.dtype)
            dmix = do * gu
            db_ref[...] += dmix
            dmix_mx = dmix.astype(MX)
            dvn = jnp.zeros((CHUNK, PW), F32)
            for g in range(N_GROUPS):
                dmg = jnp.where(grp == g, dmix_mx, jnp.zeros_like(dmix_mx))
                dws_ref[g] += jnp.where(tril, _dot(dmg, vn, NT), 0.0)
                dvn = jnp.where(grp == g, _dot(wsm[g], dmix_mx, TN), dvn)
            dgain_ref[...] += jnp.sum(dvn * vhat, axis=0, keepdims=True)
            dvhat = dvn * gain_v
            dgv = r * (dvhat - vhat * jnp.mean(dvhat * vhat, axis=-1, keepdims=True))
            dv_ref[rows, :] = (dgv * _gelu_grad(v)).astype(dv_ref.dtype)

    const2 = lambda shape: pl.BlockSpec(shape, lambda i: (0, 0))
    ws_spec = pl.BlockSpec((N_GROUPS, CHUNK, CHUNK), lambda i: (0, 0, 0))
    tile_spec = pl.BlockSpec((tile, PW), lambda i: (i, 0))
    return pl.pallas_call(
        body, name="gm_bwd", grid=(s // tile,),
        in_specs=[tile_spec, pl.BlockSpec((tile, PW), lambda i: (i, 1)), pl.BlockSpec((tile, PW), lambda i: (i, 2)),
                  const2((1, PW)), ws_spec, const2((CHUNK, PW))],
        out_specs=[tile_spec, tile_spec, ws_spec, const2((CHUNK, PW)), const2((1, PW))],
        out_shape=[jax.ShapeDtypeStruct((s, PW), MX), jax.ShapeDtypeStruct((s, PW), MX),
                   jax.ShapeDtypeStruct((N_GROUPS, CHUNK, CHUNK), F32),
                   jax.ShapeDtypeStruct((CHUNK, PW), F32), jax.ShapeDtypeStruct((1, PW), F32)],
        compiler_params=_params(("arbitrary",)),
    )(d_o, rest, rest, gain, ws, bfull)


GATE_TM = 512
GATE_TN = 256


def _gate_specs(s):
    tm = min(GATE_TM, s)
    tn = GATE_TN
    gate0 = GATE_COL0 // tn
    per = D // tn
    ins = [pl.BlockSpec((tm, SB_W), lambda i, j: (i, 0)),
           pl.BlockSpec((tm, PW), lambda i, j: (i, 0)),
           pl.BlockSpec((tm, PW), lambda i, j: (i, 0)),
           pl.BlockSpec((SB_W, tn), lambda i, j: (0, j)),
           pl.BlockSpec((PW, tn), lambda i, j: (0, j)),
           pl.BlockSpec((PW, tn), lambda i, j: (0, j)),
           pl.BlockSpec((tm, tn), lambda i, j: (i, gate0 + j)),
           pl.BlockSpec((tm, tn), lambda i, j: (i, gate0 + per + j)),
           pl.BlockSpec((tm, tn), lambda i, j: (i, gate0 + 2 * per + j))]
    return tm, tn, ins, pl.BlockSpec((tm, tn), lambda i, j: (i, j))


def _gate_fwd(o_sb, o_pool, o_gm, w_sb, w_pool, w_gm, rest):
    s = rest.shape[0]
    tm, tn, ins, out = _gate_specs(s)

    def body(o0, o1, o2, w0, w1, w2, g0, g1, g2, merged_ref):
        acc = jax.nn.sigmoid(g0[...]) * _dot(o0[...], w0[...], NN)
        acc += jax.nn.sigmoid(g1[...]) * _dot(o1[...], w1[...], NN)
        acc += jax.nn.sigmoid(g2[...]) * _dot(o2[...], w2[...], NN)
        merged_ref[...] = acc.astype(merged_ref.dtype)

    return pl.pallas_call(
        body, name="gate_fwd", grid=(s // tm, D // tn),
        in_specs=ins, out_specs=out, out_shape=jax.ShapeDtypeStruct((s, D), MX),
        compiler_params=_params(("parallel", "parallel")),
    )(o_sb, o_pool, o_gm, w_sb, w_pool, w_gm, rest, rest, rest)


def _gate_bwd(o_sb, o_pool, o_gm, w_sb, w_pool, w_gm, rest, dmerged):
    s = rest.shape[0]
    tm, tn, ins, out = _gate_specs(s)

    def body(o0, o1, o2, w0, w1, w2, g0, g1, g2, dm_ref, db0, db1, db2, dg0, dg1, dg2):
        dm = dm_ref[...]
        for o, w, g, db, dg in ((o0, w0, g0, db0, dg0), (o1, w1, g1, db1, dg1), (o2, w2, g2, db2, dg2)):
            sg = jax.nn.sigmoid(g[...])
            db[...] = (dm * sg).astype(db.dtype)
            dg[...] = (dm * _dot(o[...], w[...], NN) * (sg * (1.0 - sg))).astype(dg.dtype)

    return pl.pallas_call(
        body, name="gate_bwd", grid=(s // tm, D // tn),
        in_specs=ins + [out], out_specs=[out] * 6,
        out_shape=[jax.ShapeDtypeStruct((s, D), MX)] * 6,
        compiler_params=_params(("parallel", "parallel")),
    )(o_sb, o_pool, o_gm, w_sb, w_pool, w_gm, rest, rest, rest, dmerged)


def _relu2(acc):
    r = jnp.maximum(acc, 0.0)
    return (r * r,)


def _relu2_bwd(acc, r):
    return (acc * (2.0 * jnp.sqrt(r.astype(F32))),)


TILES = {
    "proj_qkv": (1024, W_IN_TILE, 1024), "proj_rest": (1024, W_IN_TILE, 1024),
    "out_proj": (1024, 1024, 1024), "ff_in": (1024, 1024, 1024), "ff_out": (1024, 1024, 2048),
    "ff_out_dx": (1024, 1024, 1024), "ff_out_dw": (1024, 1024, 1024),
    "ff_in_dx": (1024, 1024, 2048), "ff_in_dw": (1024, 1024, 1024),
    "out_proj_dx": (1024, 1024, 1024), "out_proj_dw": (1024, 1024, 1024),
    "br_sb_dx": (1024, 512, 1024), "br_sb_dw": (512, 1024, 1024),
    "br_pool_dx": (1024, 256, 1024), "br_pool_dw": (256, 1024, 1024),
    "br_gm_dx": (1024, 256, 1024), "br_gm_dw": (256, 1024, 1024),
    "proj_dx": (1024, 1024, 1792), "proj_dw": (1024, W_IN_TILE, 1024),
}


def _mm(name, a, b, mode, out_dtypes, **kw):
    tm, tn, tk = TILES[name]
    return _matmul(a, b, mode=mode, name=name, tm=tm, tn=tn, tk=tk, out_dtypes=out_dtypes, **kw)


def _layer_fwd(x, w, gather=None):
    h = _rms_fwd(x, w["g_mix_pre"], res=None, out_dtype=MX, name="rms_mix_pre")
    qkv = _mm("proj_qkv", h, w["w_in"], "nn", (MX,), n_out=QKV_W)
    rest = _mm("proj_rest", h, w["w_in"], "nn", (F32,), b_col0=QKV_W, n_out=REST_W)
    o_sb, gathered = _attn_fwd(qkv, gather)
    pooled, o_pool = _pool_fwd(rest, w["wbd"], w["pool_scale"])
    o_gm = _gm_fwd(rest, w["gm_gain"], w["w_spatial"], w["bfull"])
    merged = _gate_fwd(o_sb, o_pool, o_gm, w["w_br_sb"], w["w_br_pool"], w["w_br_gm"], rest)
    y = _mm("out_proj", merged, w["w_out"], "nn", (F32,))
    x1 = _rms_fwd(y, w["g_mix_post"], res=x, out_dtype=F32, name="rms_mix_post")
    h2 = _rms_fwd(x1, w["g_ff_pre"], res=None, out_dtype=MX, name="rms_ff_pre")
    r = _mm("ff_in", h2, w["w_ff_in"], "nn", (MX,), epilogue=_relu2)
    ff = _mm("ff_out", r, w["w_ff_out"], "nn", (F32,))
    x2 = _rms_fwd(ff, w["g_ff_post"], res=x1, out_dtype=F32, name="rms_ff_post")
    saved = dict(x=x, h=h, qkv=qkv, rest=rest, o_sb=o_sb, pooled=pooled, o_pool=o_pool, o_gm=o_gm,
                 merged=merged, y=y, x1=x1, h2=h2, r=r, ff=ff)
    return x2, saved, gathered


def _layer_bwd(dx2, w, sv, scatter=None):
    dff, dg_ff_post = _rms_bwd(sv["ff"], w["g_ff_post"], dx2, res=None, out_dtype=MX, name="rms_ff_post_bwd")
    da = _mm("ff_out_dx", dff, w["w_ff_out"], "nt", (MX,), epilogue=_relu2_bwd, extras=(sv["r"],))
    dw_ff_out = _mm("ff_out_dw", sv["r"], dff, "tn", (MX,))
    dh2 = _mm("ff_in_dx", da, w["w_ff_in"], "nt", (F32,))
    dw_ff_in = _mm("ff_in_dw", sv["h2"], da, "tn", (MX,))
    dx1, dg_ff_pre = _rms_bwd(sv["x1"], w["g_ff_pre"], dh2, res=dx2, out_dtype=F32, name="rms_ff_pre_bwd")

    dy, dg_mix_post = _rms_bwd(sv["y"], w["g_mix_post"], dx1, res=None, out_dtype=MX, name="rms_mix_post_bwd")
    dmerged = _mm("out_proj_dx", dy, w["w_out"], "nt", (F32,))
    dw_out = _mm("out_proj_dw", sv["merged"], dy, "tn", (MX,))
    db_sb, db_pool, db_gm, dg0, dg1, dg2 = _gate_bwd(
        sv["o_sb"], sv["o_pool"], sv["o_gm"], w["w_br_sb"], w["w_br_pool"], w["w_br_gm"], sv["rest"], dmerged)
    do_sb = _mm("br_sb_dx", db_sb, w["w_br_sb"], "nt", (MX,))
    dw_br_sb = _mm("br_sb_dw", sv["o_sb"], db_sb, "tn", (MX,))
    do_pool = _mm("br_pool_dx", db_pool, w["w_br_pool"], "nt", (F32,))
    dw_br_pool = _mm("br_pool_dw", sv["o_pool"], db_pool, "tn", (MX,))
    do_gm = _mm("br_gm_dx", db_gm, w["w_br_gm"], "nt", (F32,))
    dw_br_gm = _mm("br_gm_dw", sv["o_gm"], db_gm, "tn", (MX,))

    du, dv_gm, dws, dbfull, dgain = _gm_bwd(do_gm, sv["rest"], w["gm_gain"], w["w_spatial"], w["bfull"])
    dp, dwbd, dscale = _pool_bwd(do_pool, sv["pooled"], w["wbd"], w["pool_scale"])
    dq, dk, dv, scattered = _attn_bwd(sv["qkv"], do_sb, scatter)
    dproj = jnp.concatenate([dq, dk.astype(MX), dv.astype(MX), dp, du, dv_gm, dg0, dg1, dg2], axis=1)
    dh = _mm("proj_dx", dproj, w["w_in"], "nt", (F32,))
    dw_in = _mm("proj_dw", sv["h"], dproj, "tn", (MX,))
    dx, dg_mix_pre = _rms_bwd(sv["x"], w["g_mix_pre"], dh, res=dx1, out_dtype=F32, name="rms_mix_pre_bwd")

    grads = dict(
        w_in=dw_in, w_br_sb=dw_br_sb, w_br_pool=dw_br_pool, w_br_gm=dw_br_gm, w_out=dw_out,
        w_ff_in=dw_ff_in, w_ff_out=dw_ff_out,
        w_pool=jnp.stack([dwbd[g * GROUP:(g + 1) * GROUP, g * GROUP:(g + 1) * GROUP] for g in range(N_GROUPS)]),
        pool_scale=dscale[0], gm_gain=dgain[0], w_spatial=dws,
        b_spatial=dbfull.reshape(CHUNK, N_GROUPS, GROUP).sum(axis=-1).T,
        g_mix_pre=dg_mix_pre[0], g_mix_post=dg_mix_post[0], g_ff_pre=dg_ff_pre[0], g_ff_post=dg_ff_post[0])
    return dx, grads, scattered


def _layer_operands(full_l, small_l):
    wbd = jnp.zeros((PW, PW), F32)
    for g in range(N_GROUPS):
        wbd = wbd.at[g * GROUP:(g + 1) * GROUP, g * GROUP:(g + 1) * GROUP].set(small_l["w_pool"][g])
    w = dict(full_l)
    w["wbd"] = wbd.astype(MX)
    w["bfull"] = jnp.repeat(small_l["b_spatial"].T, GROUP, axis=1)
    w["w_spatial"] = small_l["w_spatial"]
    for n in ("pool_scale", "gm_gain", "g_mix_pre", "g_mix_post", "g_ff_pre", "g_ff_post"):
        w[n] = small_l[n][None, :]
    return w


def _local_step(x, target, layers):
    saved = []
    for w in layers:
        x, sv, _ = _layer_fwd(x, w)
        saved.append(sv)
    dx, sq = _loss_head(x, target)
    loss = 0.5 * jnp.sum(sq) / x.shape[1]
    grads = [None] * len(layers)
    for l in reversed(range(len(layers))):
        dx, grads[l], _ = _layer_bwd(dx, layers[l], saved[l])
    return loss, dx, grads


def _mesh_place():
    x, y, c = lax.axis_index("x"), lax.axis_index("y"), lax.axis_index("c")

    def peer(k):
        px = 1 - x if k & 4 else x
        py = 1 - y if k & 2 else y
        pc = 1 - c if k & 1 else c
        return (px, py, pc), 4 * px + 2 * py + pc

    return 4 * x + 2 * y + c, peer


def _shard_window(ref, name, shard_shape, idx):
    if name == "w_in":
        return ref.at[idx]
    a, b = shard_shape
    if BIG_SHARD_AXIS[name] == 2:
        return ref.at[:, pl.ds(pl.multiple_of(idx * b, b), b)]
    return ref.at[pl.ds(pl.multiple_of(idx * a, a), a), :]


def _full_shape(name, shard_shape):
    a, b = shard_shape
    if name == "w_in":
        return (N_DEV, a, b)
    return (a, N_DEV * b) if BIG_SHARD_AXIS[name] == 2 else (N_DEV * a, b)


def _exchange_out_shapes(shard_shapes, scatter):
    if scatter:
        return [(N_DEV,) + tuple(shard_shapes[nm]) for nm in BIG]
    return [_full_shape(nm, shard_shapes[nm]) for nm in BIG]


def _exchange_sems():
    n = len(BIG)
    return [pltpu.SemaphoreType.DMA((n, N_DEV - 1)), pltpu.SemaphoreType.DMA((n, N_DEV - 1)),
            pltpu.SemaphoreType.DMA((n,))]


def _exchange_plan(ins, outs, sems, shard_shapes, scatter):
    send_sems, recv_sems, local_sems = sems
    me, peer = _mesh_place()
    n = len(BIG)

    def window(ref, p, idx):
        return _shard_window(ref, BIG[p], shard_shapes[BIG[p]], idx)

    def copy(p, k, landing):
        dev, idx = peer(k)
        if scatter:
            src, dst = window(ins[p], p, idx), outs[p].at[idx if landing else me]
        else:
            src, dst = ins[p], window(outs[p], p, idx if landing else me)
        return pltpu.make_async_remote_copy(
            src_ref=src, dst_ref=dst, send_sem=send_sems.at[p, k - 1], recv_sem=recv_sems.at[p, k - 1],
            device_id=dev, device_id_type=pl.DeviceIdType.MESH)

    def local(p):
        if scatter:
            return pltpu.make_async_copy(window(ins[p], p, me), outs[p].at[me], local_sems.at[p])
        return pltpu.make_async_copy(ins[p], window(outs[p], p, me), local_sems.at[p])

    pairs = [(p, k) for p in range(n) for k in range(1, N_DEV)]

    def start():
        for p in range(n):
            local(p).start()
        for p, k in pairs:
            copy(p, k, False).start()

    def finish():
        for p, k in pairs:
            copy(p, k, True).wait_recv()
        for p, k in pairs:
            copy(p, k, False).wait_send()
        for p in range(n):
            local(p).wait()

    return start, finish


def _exchange_big(arrays, shard_shapes, *, scatter, name):
    n = len(BIG)

    def body(*refs):
        start, finish = _exchange_plan(refs[:n], refs[n:2 * n], refs[2 * n:], shard_shapes, scatter)
        start()
        finish()

    outs = pl.pallas_call(
        body, name=name,
        in_specs=[pl.BlockSpec(memory_space=pl.ANY)] * n,
        out_specs=[pl.BlockSpec(memory_space=pl.ANY)] * n,
        out_shape=[jax.ShapeDtypeStruct(s, a.dtype)
                   for s, a in zip(_exchange_out_shapes(shard_shapes, scatter), arrays)],
        scratch_shapes=_exchange_sems(),
    )(*arrays)
    return dict(zip(BIG, outs))


def _gather_rows(inp, *, name):
    def body(in_ref, out_ref, send_sems, recv_sems, local_sem):
        me, peer = _mesh_place()

        def copy(k, landing):
            dev, idx = peer(k)
            return pltpu.make_async_remote_copy(
                src_ref=in_ref, dst_ref=out_ref.at[idx if landing else me],
                send_sem=send_sems.at[k - 1], recv_sem=recv_sems.at[k - 1],
                device_id=dev, device_id_type=pl.DeviceIdType.MESH)

        mine = pltpu.make_async_copy(in_ref, out_ref.at[me], local_sem)
        mine.start()
        sends = [copy(k, False) for k in range(1, N_DEV)]
        for cp in sends:
            cp.start()
        for k in range(1, N_DEV):
            copy(k, True).wait_recv()
        for cp in sends:
            cp.wait_send()
        mine.wait()

    return pl.pallas_call(
        body, name=name,
        in_specs=[pl.BlockSpec(memory_space=pl.ANY)],
        out_specs=pl.BlockSpec(memory_space=pl.ANY),
        out_shape=jax.ShapeDtypeStruct((N_DEV,) + inp.shape, inp.dtype),
        scratch_shapes=[pltpu.SemaphoreType.DMA((N_DEV - 1,)), pltpu.SemaphoreType.DMA((N_DEV - 1,)),
                        pltpu.SemaphoreType.DMA],
    )(inp)


def _sum_devices(recv, name):
    _, rows, cols = recv.shape
    tr = rows
    for cand in (512, 256, 128, 64, 32, 16, 8):
        if rows % cand == 0:
            tr = cand
            break

    def body(r_ref, out_ref):
        acc = r_ref[0].astype(F32)
        for d in range(1, N_DEV):
            acc = acc + r_ref[d].astype(F32)
        out_ref[...] = acc

    return pl.pallas_call(
        body, name=name, grid=(rows // tr,),
        in_specs=[pl.BlockSpec((N_DEV, tr, cols), lambda i: (0, i, 0))],
        out_specs=pl.BlockSpec((tr, cols), lambda i: (i, 0)),
        out_shape=jax.ShapeDtypeStruct((rows, cols), F32),
        compiler_params=_params(("parallel",)),
    )(recv)


def _adamw(w, g, m, v, name):
    rows, cols = w.shape
    tr = rows
    for cand in (512, 256, 128, 64, 32, 16, 8):
        if rows % cand == 0:
            tr = cand
            break
    c1 = 1.0 - ADAM_B1 ** ADAM_STEP
    c2 = 1.0 - ADAM_B2 ** ADAM_STEP

    def body(w_ref, g_ref, m_ref, v_ref, d_ref, nm_ref, nv_ref):
        gv = g_ref[...]
        nm = ADAM_B1 * m_ref[...] + (1.0 - ADAM_B1) * gv
        nv = ADAM_B2 * v_ref[...] + (1.0 - ADAM_B2) * (gv * gv)
        nm_ref[...] = nm
        nv_ref[...] = nv
        d_ref[...] = -ADAM_LR * ((nm / c1) / (jnp.sqrt(nv / c2) + ADAM_EPS) + ADAM_WD * w_ref[...])

    spec = pl.BlockSpec((tr, cols), lambda i: (i, 0))
    return pl.pallas_call(
        body, name=name, grid=(rows // tr,),
        in_specs=[spec] * 4, out_specs=[spec] * 3,
        out_shape=[jax.ShapeDtypeStruct((rows, cols), F32)] * 3,
        compiler_params=_params(("parallel",)),
    )(w, g, m, v)


def _pack_rows(flat, dtype):
    n = flat.shape[-1]
    rows = -(-n // PACK_COLS)
    rows = -(-rows // 16) * 16
    pad = rows * PACK_COLS - n
    flat = jnp.pad(flat, [(0, 0)] * (flat.ndim - 1) + [(0, pad)])
    return flat.reshape(flat.shape[:-1] + (rows, PACK_COLS)).astype(dtype)


def kernel(x, w_in, w_pool, pool_scale, gm_gain, w_spatial, b_spatial, w_br_sb, w_br_pool, w_br_gm, w_out, g_mix_pre, g_mix_post, g_ff_pre, g_ff_post, w_ff_in, w_ff_out, loss_target, m_w_in, m_w_pool, m_pool_scale, m_gm_gain, m_w_spatial, m_b_spatial, m_w_br_sb, m_w_br_pool, m_w_br_gm, m_w_out, m_g_mix_pre, m_g_mix_post, m_g_ff_pre, m_g_ff_post, m_w_ff_in, m_w_ff_out, v_w_in, v_w_pool, v_pool_scale, v_gm_gain, v_w_spatial, v_b_spatial, v_w_br_sb, v_w_br_pool, v_w_br_gm, v_w_out, v_g_mix_pre, v_g_mix_post, v_g_ff_pre, v_g_ff_post, v_w_ff_in, v_w_ff_out):
    args = dict(locals())
    weights = {n: args[n] for n in WEIGHTS}
    mom_m = {n: args["m_" + n] for n in WEIGHTS}
    mom_v = {n: args["v_" + n] for n in WEIGHTS}

    shard_shapes = {n: weights[n].shape[1:] for n in BIG}
    depth, d_model, w_in_cols = weights["w_in"].shape

    def gather_of(l):
        return [weights[n][l].astype(MX) for n in BIG], shard_shapes, False

    def as_operands(full, l):
        full = dict(full)
        full["w_in"] = jnp.transpose(full["w_in"], (1, 0, 2)).reshape(d_model, N_DEV * w_in_cols)
        return _layer_operands(full, {n: weights[n][l] for n in SMALL})

    full = _exchange_big(gather_of(0)[0], shard_shapes, scatter=False, name="gather_weights")
    xc = x[0]
    layers, saved = [], []
    for l in range(depth):
        layers.append(as_operands(full, l))
        xc, sv, full = _layer_fwd(xc, layers[l], gather_of(l + 1) if l + 1 < depth else None)
        saved.append(sv)
    dx, sq = _loss_head(xc, loss_target[0])
    loss = lax.psum(0.5 * jnp.sum(sq) / d_model, ("x", "y", "c"))

    def scatter_of(g):
        g = dict(g)
        g["w_in"] = jnp.transpose(g["w_in"].reshape(d_model, N_DEV, w_in_cols), (1, 0, 2))
        return [g[n] for n in BIG], shard_shapes, True

    layer_grads, recv = [None] * depth, [None] * depth
    for l in reversed(range(depth)):
        riding = scatter_of(layer_grads[l + 1]) if l + 1 < depth else None
        dx, layer_grads[l], arrived = _layer_bwd(dx, layers[l], saved[l], riding)
        if riding is not None:
            recv[l + 1] = arrived
    recv[0] = _exchange_big(scatter_of(layer_grads[0])[0], shard_shapes, scatter=True, name="scatter_grads")
    grad_w = {n: jnp.stack([_sum_devices(recv[l][n], "sum_grads_" + n) for l in range(depth)]) for n in BIG}
    small_sizes = [int(np.prod(weights[n].shape)) for n in SMALL]
    small_off = np.concatenate([[0], np.cumsum(small_sizes)])
    small_flat = jnp.concatenate([jnp.stack([g[n] for g in layer_grads]).reshape(-1) for n in SMALL])
    g_small = _sum_devices(_gather_rows(_pack_rows(small_flat, F32), name="gather_small_grads"),
                           "sum_small_grads").reshape(-1)
    for n, lo, hi in zip(SMALL, small_off[:-1], small_off[1:]):
        grad_w[n] = g_small[lo:hi].reshape(weights[n].shape)

    delta, new_m, new_v = {}, {}, {}
    for n in WEIGHTS:
        shape = weights[n].shape
        view = (-1, shape[-1])
        d, nm, nv = _adamw(weights[n].reshape(view), grad_w[n].reshape(view), mom_m[n].reshape(view),
                           mom_v[n].reshape(view), "adamw_" + n)
        delta[n], new_m[n], new_v[n] = d.reshape(shape), nm.reshape(shape), nv.reshape(shape)

    return (loss, dx[None], *[grad_w[n] for n in WEIGHTS], *[delta[n] for n in WEIGHTS],
            *[new_m[n] for n in WEIGHTS], *[new_v[n] for n in WEIGHTS])
```

```python
import functools

import numpy as np
import jax
import jax.numpy as jnp
from jax import lax
from jax.experimental import pallas as pl
from jax.experimental.pallas import tpu as pltpu

F32 = jnp.float32
MX = jnp.bfloat16

D = 1024
SB_W = 512
HEAD_PAIR = 128
PW = 256
GROUP = 64
N_GROUPS = 4
CHUNK = 128
POOL_WINDOWS = (2, 4, 8, 16)
D_FF = 4096
D_IN = 5376
QKV_W = 3 * SB_W
REST_W = D_IN - QKV_W
GATE_COL0 = 3 * PW
RMS_EPS = 1e-6
N_DEV = 8
DEPTH = 4

ADAM_LR = 0.001
ADAM_B1 = 0.9
ADAM_B2 = 0.999
ADAM_EPS = 1e-08
ADAM_WD = 0.01
ADAM_STEP = 10

SKIP_LOG = -120.0

VMEM_LIMIT = 48 * 1024 * 1024
W_IN_TILE = 768
PACK_COLS = 1024

BIG = ("w_in", "w_br_sb", "w_br_pool", "w_br_gm", "w_out", "w_ff_in", "w_ff_out")
BIG_SHARD_AXIS = {"w_in": 2, "w_br_sb": 2, "w_br_pool": 2, "w_br_gm": 2, "w_out": 1, "w_ff_in": 2, "w_ff_out": 1}
SMALL = ("w_pool", "pool_scale", "gm_gain", "w_spatial", "b_spatial",
         "g_mix_pre", "g_mix_post", "g_ff_pre", "g_ff_post")
WEIGHTS = ("w_in", "w_pool", "pool_scale", "gm_gain", "w_spatial", "b_spatial", "w_br_sb", "w_br_pool",
           "w_br_gm", "w_out", "g_mix_pre", "g_mix_post", "g_ff_pre", "g_ff_post", "w_ff_in", "w_ff_out")


def _params(sem, vmem=VMEM_LIMIT):
    return pltpu.CompilerParams(dimension_semantics=sem, vmem_limit_bytes=vmem)


def _dot(a, b, dims):
    return lax.dot_general(a, b, (dims, ((), ())), preferred_element_type=F32)


NN = ((1,), (0,))
NT = ((1,), (1,))
TN = ((0,), (0,))


def _split_dot(x, m, left, nsplit):
    acc = None
    r = x
    for s in range(nsplit):
        p = r.astype(jnp.bfloat16)
        d = _dot(m, p, NN) if left else _dot(p, m, NN)
        acc = d if acc is None else acc + d
        if s + 1 < nsplit:
            r = r - p.astype(F32)
    return acc


def _matmul(a, b, *, mode, name, tm, tn, tk, out_dtypes, b_col0=0, n_out=None, epilogue=None, extras=()):
    if mode == "nn":
        m, kdim = a.shape
        n = b.shape[1] if n_out is None else n_out
        dims = NN
    elif mode == "nt":
        m, kdim = a.shape
        n = b.shape[0]
        dims = NT
    else:
        kdim, m = a.shape
        n = b.shape[1]
        dims = TN
    tm, tn, tk = min(tm, m), min(tn, n), min(tk, kdim)
    assert m % tm == 0 and n % tn == 0 and kdim % tk == 0 and b_col0 % tn == 0, (name, m, n, kdim)
    if mode == "nn":
        a_spec = pl.BlockSpec((tm, tk), lambda i, j, k: (i, k))
        b_spec = pl.BlockSpec((tk, tn), lambda i, j, k: (k, j + b_col0 // tn))
    elif mode == "nt":
        a_spec = pl.BlockSpec((tm, tk), lambda i, j, k: (i, k))
        b_spec = pl.BlockSpec((tn, tk), lambda i, j, k: (j, k))
    else:
        a_spec = pl.BlockSpec((tk, tm), lambda i, j, k: (k, i))
        b_spec = pl.BlockSpec((tk, tn), lambda i, j, k: (k, j))
    nk = kdim // tk
    n_extra = len(extras)
    n_outs = len(out_dtypes)
    o_spec = pl.BlockSpec((tm, tn), lambda i, j, k: (i, j))

    def body(a_ref, b_ref, *refs):
        extra_refs = refs[:n_extra]
        out_refs = refs[n_extra:n_extra + n_outs]
        k = pl.program_id(2)
        part = _dot(a_ref[...].astype(MX), b_ref[...].astype(MX), dims)

        def finish(acc):
            outs = (acc,) if epilogue is None else epilogue(acc, *[e[...] for e in extra_refs])
            for o_ref, val in zip(out_refs, outs):
                o_ref[...] = val.astype(o_ref.dtype)

        if nk == 1:
            finish(part)
            return
        acc_ref = refs[-1]

        @pl.when(k == 0)
        def _():
            acc_ref[...] = part

        @pl.when(jnp.logical_and(k > 0, k < nk - 1))
        def _():
            acc_ref[...] += part

        @pl.when(k == nk - 1)
        def _():
            finish(acc_ref[...] + part)

    outs = pl.pallas_call(
        body, name=name,
        grid=(m // tm, n // tn, nk),
        in_specs=[a_spec, b_spec] + [o_spec] * n_extra,
        out_specs=[o_spec] * n_outs,
        out_shape=[jax.ShapeDtypeStruct((m, n), dt) for dt in out_dtypes],
        scratch_shapes=[pltpu.VMEM((tm, tn), F32)] if nk > 1 else [],
        compiler_params=_params(("parallel", "parallel", "arbitrary")),
    )(a, b, *extras)
    return outs[0] if n_outs == 1 else outs


ROW_TILE = 512


def _rms_fwd(x, g, *, res, out_dtype, name):
    s, d = x.shape
    tr = min(ROW_TILE, s)
    has_res = res is not None

    def body(x_ref, g_ref, *refs):
        out_ref = refs[-1]
        xv = x_ref[...]
        y = xv * lax.rsqrt(jnp.mean(xv * xv, axis=-1, keepdims=True) + RMS_EPS) * g_ref[...]
        if has_res:
            y = refs[0][...] + y
        out_ref[...] = y.astype(out_ref.dtype)

    row = pl.BlockSpec((tr, d), lambda i: (i, 0))
    vec = pl.BlockSpec((1, d), lambda i: (0, 0))
    return pl.pallas_call(
        body, name=name, grid=(s // tr,),
        in_specs=[row, vec] + ([row] if has_res else []),
        out_specs=row, out_shape=jax.ShapeDtypeStruct((s, d), out_dtype),
        compiler_params=_params(("parallel",)),
    )(x, g, *([res] if has_res else []))


def _rms_bwd(x, g, dout, *, res, out_dtype, name):
    s, d = x.shape
    tr = min(ROW_TILE, s)
    has_res = res is not None

    def body(x_ref, g_ref, do_ref, *refs):
        dx_ref, dg_ref = refs[-2], refs[-1]
        i = pl.program_id(0)
        xv = x_ref[...]
        do = do_ref[...]
        r = lax.rsqrt(jnp.mean(xv * xv, axis=-1, keepdims=True) + RMS_EPS)
        xhat = xv * r
        dxhat = do * g_ref[...]
        dx = r * (dxhat - xhat * jnp.mean(dxhat * xhat, axis=-1, keepdims=True))
        if has_res:
            dx = refs[0][...] + dx
        dx_ref[...] = dx.astype(dx_ref.dtype)

        @pl.when(i == 0)
        def _():
            dg_ref[...] = jnp.zeros_like(dg_ref)

        dg_ref[...] += jnp.sum(do * xhat, axis=0, keepdims=True)

    row = pl.BlockSpec((tr, d), lambda i: (i, 0))
    vec = pl.BlockSpec((1, d), lambda i: (0, 0))
    return pl.pallas_call(
        body, name=name, grid=(s // tr,),
        in_specs=[row, vec, row] + ([row] if has_res else []),
        out_specs=[row, vec],
        out_shape=[jax.ShapeDtypeStruct((s, d), out_dtype), jax.ShapeDtypeStruct((1, d), F32)],
        compiler_params=_params(("arbitrary",)),
    )(x, g, dout, *([res] if has_res else []))


def _loss_head(y, target):
    s, d = y.shape
    tr = min(ROW_TILE, s)

    def body(y_ref, t_ref, dy_ref, sq_ref):
        i = pl.program_id(0)
        err = y_ref[...] - t_ref[...]
        dy_ref[...] = err * (1.0 / d)

        @pl.when(i == 0)
        def _():
            sq_ref[...] = jnp.zeros_like(sq_ref)

        sq_ref[...] += jnp.sum(err * err, axis=0, keepdims=True)

    row = pl.BlockSpec((tr, d), lambda i: (i, 0))
    vec = pl.BlockSpec((1, d), lambda i: (0, 0))
    return pl.pallas_call(
        body, name="loss_head", grid=(s // tr,),
        in_specs=[row, row], out_specs=[row, vec],
        out_shape=[jax.ShapeDtypeStruct((s, d), F32), jax.ShapeDtypeStruct((1, d), F32)],
        compiler_params=_params(("arbitrary",)),
    )(y, target)


def _sb_masks():
    row = lax.broadcasted_iota(jnp.int32, (CHUNK, CHUNK), 0)
    col = lax.broadcasted_iota(jnp.int32, (CHUNK, CHUNK), 1)
    return row, col


SB_WINDOW = 2 * CHUNK
SB_ROWS = CHUNK // 2


def _sb_consts():
    row = lax.broadcasted_iota(jnp.int32, (CHUNK, SB_WINDOW), 0)
    col = lax.broadcasted_iota(jnp.int32, (CHUNK, SB_WINDOW), 1)
    lane_hi = lax.broadcasted_iota(jnp.int32, (SB_ROWS, HEAD_PAIR), 1) >= GROUP
    return col - jnp.bitwise_and(row, SB_ROWS - 1), col, lane_hi


def _sb_window(t0, m, dcol, col):
    hi = t0 + SB_ROWS - SB_WINDOW * m
    start = jnp.maximum(hi - SB_WINDOW, 0)
    keep = jnp.logical_and(dcol < t0 - start, col < hi - start)
    return pl.multiple_of(start, SB_ROWS), keep


def _sb_stack(x, lane_hi):
    zero = jnp.zeros_like(x)
    return jnp.concatenate([jnp.where(lane_hi, zero, x), jnp.where(lane_hi, x, zero)], axis=0)


def _sb_unstack(y, lane_hi):
    return jnp.where(lane_hi, y[SB_ROWS:], y[:SB_ROWS])


def _sb_logs(z, keep):
    t = jnp.log(1.0 + jnp.exp(-jnp.abs(z)))
    ln = -(jnp.maximum(z, 0.0) + t)
    lb = ln + z
    ln = jnp.where(keep, ln, 0.0)
    return ln, lb


def _sb_scores(qm, kw, keep):
    return _sb_logs(_dot(qm, kw, NT), keep)


def _sb_tri(upper):
    r = np.arange(SB_WINDOW)
    m = (r[:, None] > r[None, :]) if upper else (r[:, None] < r[None, :])
    return jnp.asarray(m, jnp.bfloat16)


def _riding_exchange(exchange):
    if exchange is None:
        return [], [], [], []
    arrays, shard_shapes, scatter = exchange
    shapes = [jax.ShapeDtypeStruct(s, a.dtype)
              for s, a in zip(_exchange_out_shapes(shard_shapes, scatter), arrays)]
    return list(arrays), shapes, [pl.BlockSpec(memory_space=pl.ANY)] * len(arrays), _exchange_sems()


def _ride(exchange, ins, outs, sems, first, last):
    if exchange is None:
        return lambda: None
    start, finish = _exchange_plan(ins, outs, sems, exchange[1], exchange[2])
    pl.when(first)(start)
    return lambda: pl.when(last)(finish)


def _attn_fwd(qkv, exchange=None):
    s = qkv.shape[0]
    nblk = s // CHUNK
    n_pairs = SB_W // HEAD_PAIR
    x_arrays, x_shapes, x_specs, x_sems = _riding_exchange(exchange)
    nx = len(x_arrays)

    def body(q_ref, k_ref, v_ref, up_ref, *refs):
        o_ref = refs[nx]
        i = pl.program_id(1)
        pair = pl.program_id(0)
        end_exchange = _ride(exchange, refs[:nx], refs[nx + 1:2 * nx + 1], refs[2 * nx + 1:],
                             jnp.logical_and(pair == 0, i == 0),
                             jnp.logical_and(pair == n_pairs - 1, i == nblk - 1))
        dcol, col, lane_hi = _sb_consts()
        halves = range(CHUNK // SB_ROWS)
        t0s = [i * CHUNK + r * SB_ROWS for r in halves]
        qs = [_sb_stack(q_ref[r * SB_ROWS:(r + 1) * SB_ROWS, :], lane_hi) * 0.125 for r in halves]

        def cond(st):
            return jnp.logical_and(st[0] * SB_WINDOW < (i + 1) * CHUNK, st[1])

        def loop(st):
            m = st[0]
            wins = [_sb_window(t0s[r], m, dcol, col) for r in halves]
            kws = [k_ref[pl.ds(wins[r][0], SB_WINDOW), :] for r in halves]
            vws = [v_ref[pl.ds(wins[r][0], SB_WINDOW), :] for r in halves]
            keeps = [wins[r][1] for r in halves]
            zs = [_dot(qs[r], kws[r], NT) for r in halves]
            lnlb = [_sb_logs(zs[r], keeps[r]) for r in halves]
            his = [lnlb[r][0].astype(jnp.bfloat16) for r in halves]
            los = [(lnlb[r][0] - his[r].astype(F32)).astype(jnp.bfloat16) for r in halves]
            s_hi = [_dot(his[r], up_ref[...], NN) for r in halves]
            s_lo = [_dot(los[r], up_ref[...], NN) for r in halves]
            a = [jnp.where(keeps[r], jnp.exp(lnlb[r][1] + (s_hi[r] + s_lo[r] + st[2 + 2 * r])), 0.0).astype(MX)
                 for r in halves]
            new = []
            for r in halves:
                new += [st[2 + 2 * r] + jnp.sum(lnlb[r][0], axis=1, keepdims=True),
                        st[3 + 2 * r] + _dot(a[r], vws[r], NN)]
            active = jnp.maximum(jnp.max(new[0]), jnp.max(new[2])) > SKIP_LOG
            return (m + 1, active, *new)

        c0 = jnp.zeros((CHUNK, 1), F32)
        a0 = jnp.zeros((CHUNK, HEAD_PAIR), F32)
        st = lax.while_loop(cond, loop, (i * 0, i >= 0, c0, a0, c0, a0))
        for r in halves:
            o_ref[r * SB_ROWS:(r + 1) * SB_ROWS, :] = _sb_unstack(st[3 + 2 * r], lane_hi).astype(o_ref.dtype)
        end_exchange()

    outs = pl.pallas_call(
        body, name="attn_fwd" if exchange is None else "attn_fwd_gather", grid=(n_pairs, nblk),
        in_specs=[pl.BlockSpec((CHUNK, HEAD_PAIR), lambda p, i: (i, p)),
                  pl.BlockSpec((s, HEAD_PAIR), lambda p, i: (0, n_pairs + p)),
                  pl.BlockSpec((s, HEAD_PAIR), lambda p, i: (0, 2 * n_pairs + p)),
                  pl.BlockSpec((SB_WINDOW, SB_WINDOW), lambda p, i: (0, 0))] + x_specs,
        out_specs=[pl.BlockSpec((CHUNK, HEAD_PAIR), lambda p, i: (i, p))] + x_specs,
        out_shape=[jax.ShapeDtypeStruct((s, SB_W), MX)] + x_shapes,
        scratch_shapes=x_sems,
        compiler_params=_params(("arbitrary", "arbitrary")),
    )(qkv, qkv, qkv, _sb_tri(True), *x_arrays)
    return outs[0], dict(zip(BIG, outs[1:]))


def _attn_bwd(qkv, d_o, exchange=None):
    s = qkv.shape[0]
    nblk = s // CHUNK
    n_pairs = SB_W // HEAD_PAIR
    n_win = nblk // 2 + 1
    x_arrays, x_shapes, x_specs, x_sems = _riding_exchange(exchange)
    nx = len(x_arrays)

    def body(q_ref, k_ref, v_ref, do_ref, up_ref, lo_ref, *refs):
        dq_ref, dk_ref, dv_ref = refs[nx:nx + 3]
        e_s, sig_s = refs[2 * nx + 3:2 * nx + 5]
        i = pl.program_id(1)
        pair = pl.program_id(0)
        end_exchange = _ride(exchange, refs[:nx], refs[nx + 3:2 * nx + 3], refs[2 * nx + 5:],
                             jnp.logical_and(pair == 0, i == 0),
                             jnp.logical_and(pair == n_pairs - 1, i == nblk - 1))
        dcol, col, lane_hi = _sb_consts()
        halves = range(CHUNK // SB_ROWS)
        t0s = [i * CHUNK + r * SB_ROWS for r in halves]
        qs = [_sb_stack(q_ref[r * SB_ROWS:(r + 1) * SB_ROWS, :], lane_hi) * 0.125 for r in halves]
        dos = [_sb_stack(do_ref[r * SB_ROWS:(r + 1) * SB_ROWS, :], lane_hi) for r in halves]

        @pl.when(i == 0)
        def _():
            dk_ref[...] = jnp.zeros_like(dk_ref)
            dv_ref[...] = jnp.zeros_like(dv_ref)

        def cond(st):
            return jnp.logical_and(st[0] * SB_WINDOW < (i + 1) * CHUNK, st[1])

        def loop(st):
            m = st[0]
            wins = [_sb_window(t0s[r], m, dcol, col) for r in halves]
            kws = [k_ref[pl.ds(wins[r][0], SB_WINDOW), :] for r in halves]
            vws = [v_ref[pl.ds(wins[r][0], SB_WINDOW), :] for r in halves]
            keeps = [wins[r][1] for r in halves]
            zs = [_dot(qs[r], kws[r], NT) for r in halves]
            das = [_dot(dos[r], vws[r], NT) for r in halves]
            lnlb = [_sb_logs(zs[r], keeps[r]) for r in halves]
            his = [lnlb[r][0].astype(jnp.bfloat16) for r in halves]
            los = [(lnlb[r][0] - his[r].astype(F32)).astype(jnp.bfloat16) for r in halves]
            s_hi = [_dot(his[r], up_ref[...], NN) for r in halves]
            s_lo = [_dot(los[r], up_ref[...], NN) for r in halves]
            a = [jnp.where(keeps[r], jnp.exp(lnlb[r][1] + (s_hi[r] + s_lo[r] + st[2 + r])), 0.0) for r in halves]
            for r in halves:
                e_s[r, m] = a[r] * das[r]
                sig_s[r, m] = jnp.exp(lnlb[r][1])
            parts = [_dot(a[r].astype(MX), dos[r], TN) for r in halves]
            for r in halves:
                dv_ref[pl.ds(wins[r][0], SB_WINDOW), :] += parts[r]
            cs = [st[2 + r] + jnp.sum(lnlb[r][0], axis=1, keepdims=True) for r in halves]
            active = jnp.maximum(jnp.max(cs[0]), jnp.max(cs[1])) > SKIP_LOG
            return (m + 1, active, *cs)

        c0 = jnp.zeros((CHUNK, 1), F32)
        n_seen = lax.while_loop(cond, loop, (i * 0, i >= 0, c0, c0))[0]

        def up(t, st):
            m = n_seen - 1 - t
            wins = [_sb_window(t0s[r], m, dcol, col) for r in halves]
            kws = [k_ref[pl.ds(wins[r][0], SB_WINDOW), :] for r in halves]
            es = [e_s[r, m] for r in halves]
            his = [es[r].astype(jnp.bfloat16) for r in halves]
            los = [(es[r] - his[r].astype(F32)).astype(jnp.bfloat16) for r in halves]
            p_hi = [_dot(his[r], lo_ref[...], NN) for r in halves]
            p_lo = [_dot(los[r], lo_ref[...], NN) for r in halves]
            dzs = []
            for r in halves:
                sig = sig_s[r, m]
                prefix = p_hi[r] + p_lo[r] + st[2 * r]
                dzs.append(jnp.where(wins[r][1], es[r] * (1.0 - sig) - prefix * sig, 0.0).astype(MX))
            parts = [_dot(dzs[r], qs[r], TN) for r in halves]
            for r in halves:
                dk_ref[pl.ds(wins[r][0], SB_WINDOW), :] += parts[r]
            new = []
            for r in halves:
                new += [st[2 * r] + jnp.sum(es[r], axis=1, keepdims=True),
                        st[2 * r + 1] + _dot(dzs[r], kws[r], NN)]
            return tuple(new)

        a0 = jnp.zeros((CHUNK, HEAD_PAIR), F32)
        st = lax.fori_loop(0, n_seen, up, (c0, a0, c0, a0))
        for r in halves:
            dq_ref[r * SB_ROWS:(r + 1) * SB_ROWS, :] = (_sb_unstack(st[2 * r + 1], lane_hi) * 0.125).astype(dq_ref.dtype)
        end_exchange()

    blk = pl.BlockSpec((CHUNK, HEAD_PAIR), lambda p, i: (i, p))
    full = pl.BlockSpec((s, HEAD_PAIR), lambda p, i: (0, p))
    tri = pl.BlockSpec((SB_WINDOW, SB_WINDOW), lambda p, i: (0, 0))
    outs = pl.pallas_call(
        body, name="attn_bwd" if exchange is None else "attn_bwd_scatter", grid=(n_pairs, nblk),
        in_specs=[blk,
                  pl.BlockSpec((s, HEAD_PAIR), lambda p, i: (0, n_pairs + p)),
                  pl.BlockSpec((s, HEAD_PAIR), lambda p, i: (0, 2 * n_pairs + p)),
                  blk, tri, tri] + x_specs,
        out_specs=[blk, full, full] + x_specs,
        out_shape=[jax.ShapeDtypeStruct((s, SB_W), MX), jax.ShapeDtypeStruct((s, SB_W), F32),
                   jax.ShapeDtypeStruct((s, SB_W), F32)] + x_shapes,
        scratch_shapes=[pltpu.VMEM((2, n_win, CHUNK, SB_WINDOW), F32),
                        pltpu.VMEM((2, n_win, CHUNK, SB_WINDOW), F32)] + x_sems,
        compiler_params=_params(("arbitrary", "arbitrary"), 56 * 1024 * 1024),
    )(qkv, qkv, qkv, d_o, _sb_tri(True), _sb_tri(False), *x_arrays)
    return outs[0], outs[1], outs[2], dict(zip(BIG, outs[3:]))


POOL_TILE = 256
POOL_HALO = 128


def _pool_bands(tile, transpose):
    cur = np.zeros((N_GROUPS, tile, tile), np.float32)
    halo = np.zeros((N_GROUPS, tile, POOL_HALO), np.float32)
    t = np.arange(tile)[:, None]
    for g, w in enumerate(POOL_WINDOWS):
        if not transpose:
            p = np.arange(tile)[None, :]
            cur[g] = ((t - p >= 0) & (t - p < w))
            ph = np.arange(POOL_HALO)[None, :] - POOL_HALO
            halo[g] = (t - ph < w)
        else:
            p = np.arange(tile)[None, :]
            cur[g] = ((p - t >= 0) & (p - t < w))
            ph = np.arange(POOL_HALO)[None, :] + tile
            halo[g] = (ph - t < w)
    return jnp.asarray(cur, jnp.bfloat16), jnp.asarray(halo, jnp.bfloat16)


def _pool_count(i, tile, rows, row0):
    pos = (i * tile + row0 + lax.broadcasted_iota(jnp.int32, (rows, PW), 0)).astype(F32)
    grp = lax.broadcasted_iota(jnp.int32, (rows, PW), 1) // GROUP
    win = jnp.where(grp == 0, float(POOL_WINDOWS[0]),
                    jnp.where(grp == 1, float(POOL_WINDOWS[1]),
                              jnp.where(grp == 2, float(POOL_WINDOWS[2]), float(POOL_WINDOWS[3]))))
    return jnp.minimum(pos + 1.0, win), grp


def _pool_fwd(rest, wbd, scale):
    s = rest.shape[0]
    tile = min(POOL_TILE, s)
    halo_per_tile = tile // POOL_HALO
    bcur, bhalo = _pool_bands(tile, False)

    def body(cur_ref, prev_ref, bcur_ref, bhalo_ref, wbd_ref, scale_ref, pooled_ref, out_ref):
        i = pl.program_id(0)
        cur = cur_ref[...]
        prev = jnp.where(i > 0, prev_ref[...], 0.0)
        count, grp = _pool_count(i, tile, tile, 0)
        win = jnp.zeros((tile, PW), F32)
        for g in range(N_GROUPS):
            wsum = _split_dot(cur, bcur_ref[g], True, 3) + _split_dot(prev, bhalo_ref[g], True, 3)
            win = jnp.where(grp == g, wsum, win)
        pooled = (win / count - cur).astype(MX)
        pooled_ref[...] = pooled
        out_ref[...] = (_dot(pooled, wbd_ref[...], NN) * scale_ref[...]).astype(out_ref.dtype)

    tile_spec = pl.BlockSpec((tile, PW), lambda i: (i, 0))
    const3 = lambda shape: pl.BlockSpec(shape, lambda i: (0, 0, 0))
    const2 = lambda shape: pl.BlockSpec(shape, lambda i: (0, 0))
    return pl.pallas_call(
        body, name="pool_fwd", grid=(s // tile,),
        in_specs=[tile_spec,
                  pl.BlockSpec((POOL_HALO, PW), lambda i: (jnp.maximum(i * halo_per_tile - 1, 0), 0)),
                  const3(bcur.shape), const3(bhalo.shape), const2((PW, PW)), const2((1, PW))],
        out_specs=[tile_spec, tile_spec],
        out_shape=[jax.ShapeDtypeStruct((s, PW), MX), jax.ShapeDtypeStruct((s, PW), MX)],
        compiler_params=_params(("parallel",)),
    )(rest, rest, bcur, bhalo, wbd, scale)


def _pool_bwd(d_o, pooled, wbd, scale):
    s = d_o.shape[0]
    tile = min(POOL_TILE, s)
    halo_per_tile = tile // POOL_HALO
    n_halo = s // POOL_HALO
    n_tiles = s // tile
    ccur, chalo = _pool_bands(tile, True)

    def body(do_ref, nxt_ref, pooled_ref, ccur_ref, chalo_ref, wbd_ref, scale_ref, dp_ref, dw_ref, ds_ref):
        i = pl.program_id(0)
        do = do_ref[...]
        w = wbd_ref[...]
        sc = scale_ref[...]
        pooled_v = pooled_ref[...]

        @pl.when(i == 0)
        def _():
            dw_ref[...] = jnp.zeros_like(dw_ref)
            ds_ref[...] = jnp.zeros_like(ds_ref)

        ds_ref[...] += jnp.sum(do * _dot(pooled_v, w, NN), axis=0, keepdims=True)
        dmixed = (do * sc).astype(MX)
        dw_ref[...] += _dot(pooled_v, dmixed, TN)
        dpooled = _dot(dmixed, w, NT)
        nxt = jnp.where(i < n_tiles - 1, nxt_ref[...], 0.0)
        dpooled_n = _dot((nxt * sc).astype(MX), w, NT)
        count, grp = _pool_count(i, tile, tile, 0)
        count_n, _ = _pool_count(i, tile, POOL_HALO, tile)
        dq = dpooled / count
        dq_n = dpooled_n / count_n
        acc = jnp.zeros((tile, PW), F32)
        for g in range(N_GROUPS):
            wsum = _split_dot(dq, ccur_ref[g], True, 3) + _split_dot(dq_n, chalo_ref[g], True, 3)
            acc = jnp.where(grp == g, wsum, acc)
        dp_ref[...] = (acc - dpooled).astype(dp_ref.dtype)

    tile_spec = pl.BlockSpec((tile, PW), lambda i: (i, 0))
    const3 = lambda shape: pl.BlockSpec(shape, lambda i: (0, 0, 0))
    const2 = lambda shape: pl.BlockSpec(shape, lambda i: (0, 0))
    return pl.pallas_call(
        body, name="pool_bwd", grid=(n_tiles,),
        in_specs=[tile_spec,
                  pl.BlockSpec((POOL_HALO, PW), lambda i: (jnp.minimum((i + 1) * halo_per_tile, n_halo - 1), 0)),
                  tile_spec, const3(ccur.shape), const3(chalo.shape), const2((PW, PW)), const2((1, PW))],
        out_specs=[tile_spec, const2((PW, PW)), const2((1, PW))],
        out_shape=[jax.ShapeDtypeStruct((s, PW), MX), jax.ShapeDtypeStruct((PW, PW), F32),
                   jax.ShapeDtypeStruct((1, PW), F32)],
        compiler_params=_params(("arbitrary",)),
    )(d_o, d_o, pooled, ccur, chalo, wbd, scale)


GM_TILE = 512
GELU_C = 0.7978845608028654
GELU_A = 0.044715


def _gelu(x):
    return 0.5 * x * (1.0 + jnp.tanh(GELU_C * (x + GELU_A * (x * x * x))))


def _gelu_grad(x):
    th = jnp.tanh(GELU_C * (x + GELU_A * (x * x * x)))
    return 0.5 * (1.0 + th) + 0.5 * x * (1.0 - th * th) * (GELU_C * (1.0 + 3.0 * GELU_A * (x * x)))


def _gm_common(ws_ref):
    row, col = _sb_masks()
    tril = row >= col
    wsm = [jnp.where(tril, ws_ref[g], 0.0).astype(MX) for g in range(N_GROUPS)]
    grp = lax.broadcasted_iota(jnp.int32, (CHUNK, PW), 1) // GROUP
    return tril, wsm, grp


def _gm_mix(wsm, grp, vn_c):
    mixed = jnp.zeros((CHUNK, PW), F32)
    for g in range(N_GROUPS):
        mixed = jnp.where(grp == g, _dot(wsm[g], vn_c, NN), mixed)
    return mixed


def _gm_fwd(rest, gain, ws, bfull):
    s = rest.shape[0]
    tile = min(GM_TILE, s)

    def body(u_ref, v_ref, gain_ref, ws_ref, b_ref, out_ref):
        _, wsm, grp = _gm_common(ws_ref)
        bias = b_ref[...]
        for n in range(tile // CHUNK):
            rows = slice(n * CHUNK, (n + 1) * CHUNK)
            gu = _gelu(u_ref[rows, :])
            gv = _gelu(v_ref[rows, :])
            r = lax.rsqrt(jnp.mean(gv * gv, axis=-1, keepdims=True) + RMS_EPS)
            vn = (gv * r * gain_ref[...]).astype(MX)
            out_ref[rows, :] = (gu * (_gm_mix(wsm, grp, vn) + bias)).astype(out_ref.dtype)

    const2 = lambda shape: pl.BlockSpec(shape, lambda i: (0, 0))
    return pl.pallas_call(
        body, name="gm_fwd", grid=(s // tile,),
        in_specs=[pl.BlockSpec((tile, PW), lambda i: (i, 1)), pl.BlockSpec((tile, PW), lambda i: (i, 2)),
                  const2((1, PW)), pl.BlockSpec((N_GROUPS, CHUNK, CHUNK), lambda i: (0, 0, 0)),
                  const2((CHUNK, PW))],
        out_specs=pl.BlockSpec((tile, PW), lambda i: (i, 0)),
        out_shape=jax.ShapeDtypeStruct((s, PW), MX),
        compiler_params=_params(("parallel",)),
    )(rest, rest, gain, ws, bfull)


def _gm_bwd(d_o, rest, gain, ws, bfull):
    s = rest.shape[0]
    tile = min(GM_TILE, s)

    def body(do_ref, u_ref, v_ref, gain_ref, ws_ref, b_ref, du_ref, dv_ref, dws_ref, db_ref, dgain_ref):
        i = pl.program_id(0)
        tril, wsm, grp = _gm_common(ws_ref)
        bias = b_ref[...]
        gain_v = gain_ref[...]

        @pl.when(i == 0)
        def _():
            dws_ref[...] = jnp.zeros_like(dws_ref)
            db_ref[...] = jnp.zeros_like(db_ref)
            dgain_ref[...] = jnp.zeros_like(dgain_ref)

        for n in range(tile // CHUNK):
            rows = slice(n * CHUNK, (n + 1) * CHUNK)
            u = u_ref[rows, :]
            v = v_ref[rows, :]
            do = do_ref[rows, :]
            gu = _gelu(u)
            gv = _gelu(v)
            r = lax.rsqrt(jnp.mean(gv * gv, axis=-1, keepdims=True) + RMS_EPS)
            vhat = gv * r
            vn = (vhat * gain_v).astype(MX)
            mixed = _gm_mix(wsm, grp, vn)
            du_ref[rows, :] = (do * (mixed + bias) * _gelu_grad(u)).astype(du_ref.dtype)
            dmix = do * gu
            db_ref[...] += dmix
            dmix_mx = dmix.astype(MX)
            dvn = jnp.zeros((CHUNK, PW), F32)
            for g in range(N_GROUPS):
                dmg = jnp.where(grp == g, dmix_mx, jnp.zeros_like(dmix_mx))
                dws_ref[g] += jnp.where(tril, _dot(dmg, vn, NT), 0.0)
                dvn = jnp.where(grp == g, _dot(wsm[g], dmix_mx, TN), dvn)
            dgain_ref[...] += jnp.sum(dvn * vhat, axis=0, keepdims=True)
            dvhat = dvn * gain_v
            dgv = r * (dvhat - vhat * jnp.mean(dvhat * vhat, axis=-1, keepdims=True))
            dv_ref[rows, :] = (dgv * _gelu_grad(v)).astype(dv_ref.dtype)

    const2 = lambda shape: pl.BlockSpec(shape, lambda i: (0, 0))
    ws_spec = pl.BlockSpec((N_GROUPS, CHUNK, CHUNK), lambda i: (0, 0, 0))
    tile_spec = pl.BlockSpec((tile, PW), lambda i: (i, 0))
    return pl.pallas_call(
        body, name="gm_bwd", grid=(s // tile,),
        in_specs=[tile_spec, pl.BlockSpec((tile, PW), lambda i: (i, 1)), pl.BlockSpec((tile, PW), lambda i: (i, 2)),
                  const2((1, PW)), ws_spec, const2((CHUNK, PW))],
        out_specs=[tile_spec, tile_spec, ws_spec, const2((CHUNK, PW)), const2((1, PW))],
        out_shape=[jax.ShapeDtypeStruct((s, PW), MX), jax.ShapeDtypeStruct((s, PW), MX),
                   jax.ShapeDtypeStruct((N_GROUPS, CHUNK, CHUNK), F32),
                   jax.ShapeDtypeStruct((CHUNK, PW), F32), jax.ShapeDtypeStruct((1, PW), F32)],
        compiler_params=_params(("arbitrary",)),
    )(d_o, rest, rest, gain, ws, bfull)


GATE_TM = 512
GATE_TN = 256


def _gate_specs(s):
    tm = min(GATE_TM, s)
    tn = GATE_TN
    gate0 = GATE_COL0 // tn
    per = D // tn
    ins = [pl.BlockSpec((tm, SB_W), lambda i, j: (i, 0)),
           pl.BlockSpec((tm, PW), lambda i, j: (i, 0)),
           pl.BlockSpec((tm, PW), lambda i, j: (i, 0)),
           pl.BlockSpec((SB_W, tn), lambda i, j: (0, j)),
           pl.BlockSpec((PW, tn), lambda i, j: (0, j)),
           pl.BlockSpec((PW, tn), lambda i, j: (0, j)),
           pl.BlockSpec((tm, tn), lambda i, j: (i, gate0 + j)),
           pl.BlockSpec((tm, tn), lambda i, j: (i, gate0 + per + j)),
           pl.BlockSpec((tm, tn), lambda i, j: (i, gate0 + 2 * per + j))]
    return tm, tn, ins, pl.BlockSpec((tm, tn), lambda i, j: (i, j))


def _gate_fwd(o_sb, o_pool, o_gm, w_sb, w_pool, w_gm, rest):
    s = rest.shape[0]
    tm, tn, ins, out = _gate_specs(s)

    def body(o0, o1, o2, w0, w1, w2, g0, g1, g2, merged_ref):
        acc = jax.nn.sigmoid(g0[...]) * _dot(o0[...], w0[...], NN)
        acc += jax.nn.sigmoid(g1[...]) * _dot(o1[...], w1[...], NN)
        acc += jax.nn.sigmoid(g2[...]) * _dot(o2[...], w2[...], NN)
        merged_ref[...] = acc.astype(merged_ref.dtype)

    return pl.pallas_call(
        body, name="gate_fwd", grid=(s // tm, D // tn),
        in_specs=ins, out_specs=out, out_shape=jax.ShapeDtypeStruct((s, D), MX),
        compiler_params=_params(("parallel", "parallel")),
    )(o_sb, o_pool, o_gm, w_sb, w_pool, w_gm, rest, rest, rest)


def _gate_bwd(o_sb, o_pool, o_gm, w_sb, w_pool, w_gm, rest, dmerged):
    s = rest.shape[0]
    tm, tn, ins, out = _gate_specs(s)

    def body(o0, o1, o2, w0, w1, w2, g0, g1, g2, dm_ref, db0, db1, db2, dg0, dg1, dg2):
        dm = dm_ref[...]
        for o, w, g, db, dg in ((o0, w0, g0, db0, dg0), (o1, w1, g1, db1, dg1), (o2, w2, g2, db2, dg2)):
            sg = jax.nn.sigmoid(g[...])
            db[...] = (dm * sg).astype(db.dtype)
            dg[...] = (dm * _dot(o[...], w[...], NN) * (sg * (1.0 - sg))).astype(dg.dtype)

    return pl.pallas_call(
        body, name="gate_bwd", grid=(s // tm, D // tn),
        in_specs=ins + [out], out_specs=[out] * 6,
        out_shape=[jax.ShapeDtypeStruct((s, D), MX)] * 6,
        compiler_params=_params(("parallel", "parallel")),
    )(o_sb, o_pool, o_gm, w_sb, w_pool, w_gm, rest, rest, rest, dmerged)


def _relu2(acc):
    r = jnp.maximum(acc, 0.0)
    return (r * r,)


def _relu2_bwd(acc, r):
    return (acc * (2.0 * jnp.sqrt(r.astype(F32))),)


TILES = {
    "proj_qkv": (1024, W_IN_TILE, 1024), "proj_rest": (1024, W_IN_TILE, 1024),
    "out_proj": (1024, 1024, 1024), "ff_in": (1024, 1024, 1024), "ff_out": (1024, 1024, 2048),
    "ff_out_dx": (1024, 1024, 1024), "ff_out_dw": (1024, 1024, 1024),
    "ff_in_dx": (1024, 1024, 2048), "ff_in_dw": (1024, 1024, 1024),
    "out_proj_dx": (1024, 1024, 1024), "out_proj_dw": (1024, 1024, 1024),
    "br_sb_dx": (1024, 512, 1024), "br_sb_dw": (512, 1024, 1024),
    "br_pool_dx": (1024, 256, 1024), "br_pool_dw": (256, 1024, 1024),
    "br_gm_dx": (1024, 256, 1024), "br_gm_dw": (256, 1024, 1024),
    "proj_dx": (1024, 1024, 1792), "proj_dw": (1024, W_IN_TILE, 1024),
}


def _mm(name, a, b, mode, out_dtypes, **kw):
    tm, tn, tk = TILES[name]
    return _matmul(a, b, mode=mode, name=name, tm=tm, tn=tn, tk=tk, out_dtypes=out_dtypes, **kw)


def _layer_fwd(x, w, gather=None):
    h = _rms_fwd(x, w["g_mix_pre"], res=None, out_dtype=MX, name="rms_mix_pre")
    qkv = _mm("proj_qkv", h, w["w_in"], "nn", (MX,), n_out=QKV_W)
    rest = _mm("proj_rest", h, w["w_in"], "nn", (F32,), b_col0=QKV_W, n_out=REST_W)
    o_sb, gathered = _attn_fwd(qkv, gather)
    pooled, o_pool = _pool_fwd(rest, w["wbd"], w["pool_scale"])
    o_gm = _gm_fwd(rest, w["gm_gain"], w["w_spatial"], w["bfull"])
    merged = _gate_fwd(o_sb, o_pool, o_gm, w["w_br_sb"], w["w_br_pool"], w["w_br_gm"], rest)
    y = _mm("out_proj", merged, w["w_out"], "nn", (F32,))
    x1 = _rms_fwd(y, w["g_mix_post"], res=x, out_dtype=F32, name="rms_mix_post")
    h2 = _rms_fwd(x1, w["g_ff_pre"], res=None, out_dtype=MX, name="rms_ff_pre")
    r = _mm("ff_in", h2, w["w_ff_in"], "nn", (MX,), epilogue=_relu2)
    ff = _mm("ff_out", r, w["w_ff_out"], "nn", (F32,))
    x2 = _rms_fwd(ff, w["g_ff_post"], res=x1, out_dtype=F32, name="rms_ff_post")
    saved = dict(x=x, h=h, qkv=qkv, rest=rest, o_sb=o_sb, pooled=pooled, o_pool=o_pool, o_gm=o_gm,
                 merged=merged, y=y, x1=x1, h2=h2, r=r, ff=ff)
    return x2, saved, gathered


def _layer_bwd(dx2, w, sv, scatter=None):
    dff, dg_ff_post = _rms_bwd(sv["ff"], w["g_ff_post"], dx2, res=None, out_dtype=MX, name="rms_ff_post_bwd")
    da = _mm("ff_out_dx", dff, w["w_ff_out"], "nt", (MX,), epilogue=_relu2_bwd, extras=(sv["r"],))
    dw_ff_out = _mm("ff_out_dw", sv["r"], dff, "tn", (MX,))
    dh2 = _mm("ff_in_dx", da, w["w_ff_in"], "nt", (F32,))
    dw_ff_in = _mm("ff_in_dw", sv["h2"], da, "tn", (MX,))
    dx1, dg_ff_pre = _rms_bwd(sv["x1"], w["g_ff_pre"], dh2, res=dx2, out_dtype=F32, name="rms_ff_pre_bwd")

    dy, dg_mix_post = _rms_bwd(sv["y"], w["g_mix_post"], dx1, res=None, out_dtype=MX, name="rms_mix_post_bwd")
    dmerged = _mm("out_proj_dx", dy, w["w_out"], "nt", (F32,))
    dw_out = _mm("out_proj_dw", sv["merged"], dy, "tn", (MX,))
    db_sb, db_pool, db_gm, dg0, dg1, dg2 = _gate_bwd(
        sv["o_sb"], sv["o_pool"], sv["o_gm"], w["w_br_sb"], w["w_br_pool"], w["w_br_gm"], sv["rest"], dmerged)
    do_sb = _mm("br_sb_dx", db_sb, w["w_br_sb"], "nt", (MX,))
    dw_br_sb = _mm("br_sb_dw", sv["o_sb"], db_sb, "tn", (MX,))
    do_pool = _mm("br_pool_dx", db_pool, w["w_br_pool"], "nt", (F32,))
    dw_br_pool = _mm("br_pool_dw", sv["o_pool"], db_pool, "tn", (MX,))
    do_gm = _mm("br_gm_dx", db_gm, w["w_br_gm"], "nt", (F32,))
    dw_br_gm = _mm("br_gm_dw", sv["o_gm"], db_gm, "tn", (MX,))

    du, dv_gm, dws, dbfull, dgain = _gm_bwd(do_gm, sv["rest"], w["gm_gain"], w["w_spatial"], w["bfull"])
    dp, dwbd, dscale = _pool_bwd(do_pool, sv["pooled"], w["wbd"], w["pool_scale"])
    dq, dk, dv, scattered = _attn_bwd(sv["qkv"], do_sb, scatter)
    dproj = jnp.concatenate([dq, dk.astype(MX), dv.astype(MX), dp, du, dv_gm, dg0, dg1, dg2], axis=1)
    dh = _mm("proj_dx", dproj, w["w_in"], "nt", (F32,))
    dw_in = _mm("proj_dw", sv["h"], dproj, "tn", (MX,))
    dx, dg_mix_pre = _rms_bwd(sv["x"], w["g_mix_pre"], dh, res=dx1, out_dtype=F32, name="rms_mix_pre_bwd")

    grads = dict(
        w_in=dw_in, w_br_sb=dw_br_sb, w_br_pool=dw_br_pool, w_br_gm=dw_br_gm, w_out=dw_out,
        w_ff_in=dw_ff_in, w_ff_out=dw_ff_out,
        w_pool=jnp.stack([dwbd[g * GROUP:(g + 1) * GROUP, g * GROUP:(g + 1) * GROUP] for g in range(N_GROUPS)]),
        pool_scale=dscale[0], gm_gain=dgain[0], w_spatial=dws,
        b_spatial=dbfull.reshape(CHUNK, N_GROUPS, GROUP).sum(axis=-1).T,
        g_mix_pre=dg_mix_pre[0], g_mix_post=dg_mix_post[0], g_ff_pre=dg_ff_pre[0], g_ff_post=dg_ff_post[0])
    return dx, grads, scattered


def _layer_operands(full_l, small_l):
    wbd = jnp.zeros((PW, PW), F32)
    for g in range(N_GROUPS):
        wbd = wbd.at[g * GROUP:(g + 1) * GROUP, g * GROUP:(g + 1) * GROUP].set(small_l["w_pool"][g])
    w = dict(full_l)
    w["wbd"] = wbd.astype(MX)
    w["bfull"] = jnp.repeat(small_l["b_spatial"].T, GROUP, axis=1)
    w["w_spatial"] = small_l["w_spatial"]
    for n in ("pool_scale", "gm_gain", "g_mix_pre", "g_mix_post", "g_ff_pre", "g_ff_post"):
        w[n] = small_l[n][None, :]
    return w


def _local_step(x, target, layers):
    saved = []
    for w in layers:
        x, sv, _ = _layer_fwd(x, w)
        saved.append(sv)
    dx, sq = _loss_head(x, target)
    loss = 0.5 * jnp.sum(sq) / x.shape[1]
    grads = [None] * len(layers)
    for l in reversed(range(len(layers))):
        dx, grads[l], _ = _layer_bwd(dx, layers[l], saved[l])
    return loss, dx, grads


def _mesh_place():
    x, y, c = lax.axis_index("x"), lax.axis_index("y"), lax.axis_index("c")

    def peer(k):
        px = 1 - x if k & 4 else x
        py = 1 - y if k & 2 else y
        pc = 1 - c if k & 1 else c
        return (px, py, pc), 4 * px + 2 * py + pc

    return 4 * x + 2 * y + c, peer


def _shard_window(ref, name, shard_shape, idx):
    if name == "w_in":
        return ref.at[idx]
    a, b = shard_shape
    if BIG_SHARD_AXIS[name] == 2:
        return ref.at[:, pl.ds(pl.multiple_of(idx * b, b), b)]
    return ref.at[pl.ds(pl.multiple_of(idx * a, a), a), :]


def _full_shape(name, shard_shape):
    a, b = shard_shape
    if name == "w_in":
        return (N_DEV, a, b)
    return (a, N_DEV * b) if BIG_SHARD_AXIS[name] == 2 else (N_DEV * a, b)


def _exchange_out_shapes(shard_shapes, scatter):
    if scatter:
        return [(N_DEV,) + tuple(shard_shapes[nm]) for nm in BIG]
    return [_full_shape(nm, shard_shapes[nm]) for nm in BIG]


def _exchange_sems():
    n = len(BIG)
    return [pltpu.SemaphoreType.DMA((n, N_DEV - 1)), pltpu.SemaphoreType.DMA((n, N_DEV - 1)),
            pltpu.SemaphoreType.DMA((n,))]


def _exchange_plan(ins, outs, sems, shard_shapes, scatter):
    send_sems, recv_sems, local_sems = sems
    me, peer = _mesh_place()
    n = len(BIG)

    def window(ref, p, idx):
        return _shard_window(ref, BIG[p], shard_shapes[BIG[p]], idx)

    def copy(p, k, landing):
        dev, idx = peer(k)
        if scatter:
            src, dst = window(ins[p], p, idx), outs[p].at[idx if landing else me]
        else:
            src, dst = ins[p], window(outs[p], p, idx if landing else me)
        return pltpu.make_async_remote_copy(
            src_ref=src, dst_ref=dst, send_sem=send_sems.at[p, k - 1], recv_sem=recv_sems.at[p, k - 1],
            device_id=dev, device_id_type=pl.DeviceIdType.MESH)

    def local(p):
        if scatter:
            return pltpu.make_async_copy(window(ins[p], p, me), outs[p].at[me], local_sems.at[p])
        return pltpu.make_async_copy(ins[p], window(outs[p], p, me), local_sems.at[p])

    pairs = [(p, k) for p in range(n) for k in range(1, N_DEV)]

    def start():
        for p in range(n):
            local(p).start()
        for p, k in pairs:
            copy(p, k, False).start()

    def finish():
        for p, k in pairs:
            copy(p, k, True).wait_recv()
        for p, k in pairs:
            copy(p, k, False).wait_send()
        for p in range(n):
            local(p).wait()

    return start, finish


def _exchange_big(arrays, shard_shapes, *, scatter, name):
    n = len(BIG)

    def body(*refs):
        start, finish = _exchange_plan(refs[:n], refs[n:2 * n], refs[2 * n:], shard_shapes, scatter)
        start()
        finish()

    outs = pl.pallas_call(
        body, name=name,
        in_specs=[pl.BlockSpec(memory_space=pl.ANY)] * n,
        out_specs=[pl.BlockSpec(memory_space=pl.ANY)] * n,
        out_shape=[jax.ShapeDtypeStruct(s, a.dtype)
                   for s, a in zip(_exchange_out_shapes(shard_shapes, scatter), arrays)],
        scratch_shapes=_exchange_sems(),
    )(*arrays)
    return dict(zip(BIG, outs))


def _gather_rows(inp, *, name):
    def body(in_ref, out_ref, send_sems, recv_sems, local_sem):
        me, peer = _mesh_place()

        def copy(k, landing):
            dev, idx = peer(k)
            return pltpu.make_async_remote_copy(
                src_ref=in_ref, dst_ref=out_ref.at[idx if landing else me],
                send_sem=send_sems.at[k - 1], recv_sem=recv_sems.at[k - 1],
                device_id=dev, device_id_type=pl.DeviceIdType.MESH)

        mine = pltpu.make_async_copy(in_ref, out_ref.at[me], local_sem)
        mine.start()
        sends = [copy(k, False) for k in range(1, N_DEV)]
        for cp in sends:
            cp.start()
        for k in range(1, N_DEV):
            copy(k, True).wait_recv()
        for cp in sends:
            cp.wait_send()
        mine.wait()

    return pl.pallas_call(
        body, name=name,
        in_specs=[pl.BlockSpec(memory_space=pl.ANY)],
        out_specs=pl.BlockSpec(memory_space=pl.ANY),
        out_shape=jax.ShapeDtypeStruct((N_DEV,) + inp.shape, inp.dtype),
        scratch_shapes=[pltpu.SemaphoreType.DMA((N_DEV - 1,)), pltpu.SemaphoreType.DMA((N_DEV - 1,)),
                        pltpu.SemaphoreType.DMA],
    )(inp)


def _sum_devices(recv, name):
    _, rows, cols = recv.shape
    tr = rows
    for cand in (512, 256, 128, 64, 32, 16, 8):
        if rows % cand == 0:
            tr = cand
            break

    def body(r_ref, out_ref):
        acc = r_ref[0].astype(F32)
        for d in range(1, N_DEV):
            acc = acc + r_ref[d].astype(F32)
        out_ref[...] = acc

    return pl.pallas_call(
        body, name=name, grid=(rows // tr,),
        in_specs=[pl.BlockSpec((N_DEV, tr, cols), lambda i: (0, i, 0))],
        out_specs=pl.BlockSpec((tr, cols), lambda i: (i, 0)),
        out_shape=jax.ShapeDtypeStruct((rows, cols), F32),
        compiler_params=_params(("parallel",)),
    )(recv)


def _adamw(w, g, m, v, name):
    rows, cols = w.shape
    tr = rows
    for cand in (512, 256, 128, 64, 32, 16, 8):
        if rows % cand == 0:
            tr = cand
            break
    c1 = 1.0 - ADAM_B1 ** ADAM_STEP
    c2 = 1.0 - ADAM_B2 ** ADAM_STEP

    def body(w_ref, g_ref, m_ref, v_ref, d_ref, nm_ref, nv_ref):
        gv = g_ref[...]
        nm = ADAM_B1 * m_ref[...] + (1.0 - ADAM_B1) * gv
        nv = ADAM_B2 * v_ref[...] + (1.0 - ADAM_B2) * (gv * gv)
        nm_ref[...] = nm
        nv_ref[...] = nv
        d_ref[...] = -ADAM_LR * ((nm / c1) / (jnp.sqrt(nv / c2) + ADAM_EPS) + ADAM_WD * w_ref[...])

    spec = pl.BlockSpec((tr, cols), lambda i: (i, 0))
    return pl.pallas_call(
        body, name=name, grid=(rows // tr,),
        in_specs=[spec] * 4, out_specs=[spec] * 3,
        out_shape=[jax.ShapeDtypeStruct((rows, cols), F32)] * 3,
        compiler_params=_params(("parallel",)),
    )(w, g, m, v)


def _pack_rows(flat, dtype):
    n = flat.shape[-1]
    rows = -(-n // PACK_COLS)
    rows = -(-rows // 16) * 16
    pad = rows * PACK_COLS - n
    flat = jnp.pad(flat, [(0, 0)] * (flat.ndim - 1) + [(0, pad)])
    return flat.reshape(flat.shape[:-1] + (rows, PACK_COLS)).astype(dtype)


def kernel(x, w_in, w_pool, pool_scale, gm_gain, w_spatial, b_spatial, w_br_sb, w_br_pool, w_br_gm, w_out, g_mix_pre, g_mix_post, g_ff_pre, g_ff_post, w_ff_in, w_ff_out, loss_target, m_w_in, m_w_pool, m_pool_scale, m_gm_gain, m_w_spatial, m_b_spatial, m_w_br_sb, m_w_br_pool, m_w_br_gm, m_w_out, m_g_mix_pre, m_g_mix_post, m_g_ff_pre, m_g_ff_post, m_w_ff_in, m_w_ff_out, v_w_in, v_w_pool, v_pool_scale, v_gm_gain, v_w_spatial, v_b_spatial, v_w_br_sb, v_w_br_pool, v_w_br_gm, v_w_out, v_g_mix_pre, v_g_mix_post, v_g_ff_pre, v_g_ff_post, v_w_ff_in, v_w_ff_out):
    args = dict(locals())
    weights = {n: args[n] for n in WEIGHTS}
    mom_m = {n: args["m_" + n] for n in WEIGHTS}
    mom_v = {n: args["v_" + n] for n in WEIGHTS}

    shard_shapes = {n: weights[n].shape[1:] for n in BIG}
    depth, d_model, w_in_cols = weights["w_in"].shape

    def gather_of(l):
        return [weights[n][l].astype(MX) for n in BIG], shard_shapes, False

    def as_operands(full, l):
        full = dict(full)
        full["w_in"] = jnp.transpose(full["w_in"], (1, 0, 2)).reshape(d_model, N_DEV * w_in_cols)
        return _layer_operands(full, {n: weights[n][l] for n in SMALL})

    full = _exchange_big(gather_of(0)[0], shard_shapes, scatter=False, name="gather_weights")
    xc = x[0]
    layers, saved = [], []
    for l in range(depth):
        layers.append(as_operands(full, l))
        xc, sv, full = _layer_fwd(xc, layers[l], gather_of(l + 1) if l + 1 < depth else None)
        saved.append(sv)
    dx, sq = _loss_head(xc, loss_target[0])
    loss = lax.psum(0.5 * jnp.sum(sq) / d_model, ("x", "y", "c"))

    def scatter_of(g):
        g = dict(g)
        g["w_in"] = jnp.transpose(g["w_in"].reshape(d_model, N_DEV, w_in_cols), (1, 0, 2))
        return [g[n] for n in BIG], shard_shapes, True

    layer_grads, recv = [None] * depth, [None] * depth
    for l in reversed(range(depth)):
        riding = scatter_of(layer_grads[l + 1]) if l + 1 < depth else None
        dx, layer_grads[l], arrived = _layer_bwd(dx, layers[l], saved[l], riding)
        if riding is not None:
            recv[l + 1] = arrived
    recv[0] = _exchange_big(scatter_of(layer_grads[0])[0], shard_shapes, scatter=True, name="scatter_grads")
    grad_w = {n: jnp.stack([_sum_devices(recv[l][n], "sum_grads_" + n) for l in range(depth)]) for n in BIG}
    small_sizes = [int(np.prod(weights[n].shape)) for n in SMALL]
    small_off = np.concatenate([[0], np.cumsum(small_sizes)])
    small_flat = jnp.concatenate([jnp.stack([g[n] for g in layer_grads]).reshape(-1) for n in SMALL])
    g_small = _sum_devices(_gather_rows(_pack_rows(small_flat, F32), name="gather_small_grads"),
                           "sum_small_grads").reshape(-1)
    for n, lo, hi in zip(SMALL, small_off[:-1], small_off[1:]):
        grad_w[n] = g_small[lo:hi].reshape(weights[n].shape)

    delta, new_m, new_v = {}, {}, {}
    for n in WEIGHTS:
        shape = weights[n].shape
        view = (-1, shape[-1])
        d, nm, nv = _adamw(weights[n].reshape(view), grad_w[n].reshape(view), mom_m[n].reshape(view),
                           mom_v[n].reshape(view), "adamw_" + n)
        delta[n], new_m[n], new_v[n] = d.reshape(shape), nm.reshape(shape), nv.reshape(shape)

    return (loss, dx[None], *[grad_w[n] for n in WEIGHTS], *[delta[n] for n in WEIGHTS],
            *[new_m[n] for n in WEIGHTS], *[new_v[n] for n in WEIGHTS])
```

```python
import functools

import numpy as np
import jax
import jax.numpy as jnp
from jax import lax
from jax.experimental import pallas as pl
from jax.experimental.pallas import tpu as pltpu

F32 = jnp.float32
MX = jnp.bfloat16

D = 1024
SB_W = 512
HEAD_PAIR = 128
PW = 256
GROUP = 64
N_GROUPS = 4
CHUNK = 128
POOL_WINDOWS = (2, 4, 8, 16)
D_FF = 4096
D_IN = 5376
QKV_W = 3 * SB_W
REST_W = D_IN - QKV_W
GATE_COL0 = 3 * PW
RMS_EPS = 1e-6
N_DEV = 8
DEPTH = 4

ADAM_LR = 0.001
ADAM_B1 = 0.9
ADAM_B2 = 0.999
ADAM_EPS = 1e-08
ADAM_WD = 0.01
ADAM_STEP = 10

SKIP_LOG = -120.0

VMEM_LIMIT = 48 * 1024 * 1024
W_IN_TILE = 768
PACK_COLS = 1024

BIG = ("w_in", "w_br_sb", "w_br_pool", "w_br_gm", "w_out", "w_ff_in", "w_ff_out")
BIG_SHARD_AXIS = {"w_in": 2, "w_br_sb": 2, "w_br_pool": 2, "w_br_gm": 2, "w_out": 1, "w_ff_in": 2, "w_ff_out": 1}
SMALL = ("w_pool", "pool_scale", "gm_gain", "w_spatial", "b_spatial",
         "g_mix_pre", "g_mix_post", "g_ff_pre", "g_ff_post")
WEIGHTS = ("w_in", "w_pool", "pool_scale", "gm_gain", "w_spatial", "b_spatial", "w_br_sb", "w_br_pool",
           "w_br_gm", "w_out", "g_mix_pre", "g_mix_post", "g_ff_pre", "g_ff_post", "w_ff_in", "w_ff_out")


def _params(sem, vmem=VMEM_LIMIT):
    return pltpu.CompilerParams(dimension_semantics=sem, vmem_limit_bytes=vmem)


def _dot(a, b, dims):
    return lax.dot_general(a, b, (dims, ((), ())), preferred_element_type=F32)


NN = ((1,), (0,))
NT = ((1,), (1,))
TN = ((0,), (0,))


def _split_dot(x, m, left, nsplit):
    acc = None
    r = x
    for s in range(nsplit):
        p = r.astype(jnp.bfloat16)
        d = _dot(m, p, NN) if left else _dot(p, m, NN)
        acc = d if acc is None else acc + d
        if s + 1 < nsplit:
            r = r - p.astype(F32)
    return acc


def _matmul(a, b, *, mode, name, tm, tn, tk, out_dtypes, b_col0=0, n_out=None, epilogue=None, extras=()):
    if mode == "nn":
        m, kdim = a.shape
        n = b.shape[1] if n_out is None else n_out
        dims = NN
    elif mode == "nt":
        m, kdim = a.shape
        n = b.shape[0]
        dims = NT
    else:
        kdim, m = a.shape
        n = b.shape[1]
        dims = TN
    tm, tn, tk = min(tm, m), min(tn, n), min(tk, kdim)
    assert m % tm == 0 and n % tn == 0 and kdim % tk == 0 and b_col0 % tn == 0, (name, m, n, kdim)
    if mode == "nn":
        a_spec = pl.BlockSpec((tm, tk), lambda i, j, k: (i, k))
        b_spec = pl.BlockSpec((tk, tn), lambda i, j, k: (k, j + b_col0 // tn))
    elif mode == "nt":
        a_spec = pl.BlockSpec((tm, tk), lambda i, j, k: (i, k))
        b_spec = pl.BlockSpec((tn, tk), lambda i, j, k: (j, k))
    else:
        a_spec = pl.BlockSpec((tk, tm), lambda i, j, k: (k, i))
        b_spec = pl.BlockSpec((tk, tn), lambda i, j, k: (k, j))
    nk = kdim // tk
    n_extra = len(extras)
    n_outs = len(out_dtypes)
    o_spec = pl.BlockSpec((tm, tn), lambda i, j, k: (i, j))

    def body(a_ref, b_ref, *refs):
        extra_refs = refs[:n_extra]
        out_refs = refs[n_extra:n_extra + n_outs]
        k = pl.program_id(2)
        part = _dot(a_ref[...].astype(MX), b_ref[...].astype(MX), dims)

        def finish(acc):
            outs = (acc,) if epilogue is None else epilogue(acc, *[e[...] for e in extra_refs])
            for o_ref, val in zip(out_refs, outs):
                o_ref[...] = val.astype(o_ref.dtype)

        if nk == 1:
            finish(part)
            return
        acc_ref = refs[-1]

        @pl.when(k == 0)
        def _():
            acc_ref[...] = part

        @pl.when(jnp.logical_and(k > 0, k < nk - 1))
        def _():
            acc_ref[...] += part

        @pl.when(k == nk - 1)
        def _():
            finish(acc_ref[...] + part)

    outs = pl.pallas_call(
        body, name=name,
        grid=(m // tm, n // tn, nk),
        in_specs=[a_spec, b_spec] + [o_spec] * n_extra,
        out_specs=[o_spec] * n_outs,
        out_shape=[jax.ShapeDtypeStruct((m, n), dt) for dt in out_dtypes],
        scratch_shapes=[pltpu.VMEM((tm, tn), F32)] if nk > 1 else [],
        compiler_params=_params(("parallel", "parallel", "arbitrary")),
    )(a, b, *extras)
    return outs[0] if n_outs == 1 else outs


ROW_TILE = 1024


def _rms_fwd(x, g, *, res, out_dtype, name):
    s, d = x.shape
    tr = min(ROW_TILE, s)
    has_res = res is not None

    def body(x_ref, g_ref, *refs):
        out_ref = refs[-1]
        xv = x_ref[...]
        y = xv * lax.rsqrt(jnp.mean(xv * xv, axis=-1, keepdims=True) + RMS_EPS) * g_ref[...]
        if has_res:
            y = refs[0][...] + y
        out_ref[...] = y.astype(out_ref.dtype)

    row = pl.BlockSpec((tr, d), lambda i: (i, 0))
    vec = pl.BlockSpec((1, d), lambda i: (0, 0))
    return pl.pallas_call(
        body, name=name, grid=(s // tr,),
        in_specs=[row, vec] + ([row] if has_res else []),
        out_specs=row, out_shape=jax.ShapeDtypeStruct((s, d), out_dtype),
        compiler_params=_params(("parallel",)),
    )(x, g, *([res] if has_res else []))


def _rms_bwd(x, g, dout, *, res, out_dtype, name):
    s, d = x.shape
    tr = min(ROW_TILE, s)
    has_res = res is not None

    def body(x_ref, g_ref, do_ref, *refs):
        dx_ref, dg_ref = refs[-2], refs[-1]
        i = pl.program_id(0)
        xv = x_ref[...]
        do = do_ref[...]
        r = lax.rsqrt(jnp.mean(xv * xv, axis=-1, keepdims=True) + RMS_EPS)
        xhat = xv * r
        dxhat = do * g_ref[...]
        dx = r * (dxhat - xhat * jnp.mean(dxhat * xhat, axis=-1, keepdims=True))
        if has_res:
            dx = refs[0][...] + dx
        dx_ref[...] = dx.astype(dx_ref.dtype)

        @pl.when(i == 0)
        def _():
            dg_ref[...] = jnp.zeros_like(dg_ref)

        dg_ref[...] += jnp.sum(do * xhat, axis=0, keepdims=True)

    row = pl.BlockSpec((tr, d), lambda i: (i, 0))
    vec = pl.BlockSpec((1, d), lambda i: (0, 0))
    return pl.pallas_call(
        body, name=name, grid=(s // tr,),
        in_specs=[row, vec, row] + ([row] if has_res else []),
        out_specs=[row, vec],
        out_shape=[jax.ShapeDtypeStruct((s, d), out_dtype), jax.ShapeDtypeStruct((1, d), F32)],
        compiler_params=_params(("arbitrary",)),
    )(x, g, dout, *([res] if has_res else []))


def _loss_head(y, target):
    s, d = y.shape
    tr = min(ROW_TILE, s)

    def body(y_ref, t_ref, dy_ref, sq_ref):
        i = pl.program_id(0)
        err = y_ref[...] - t_ref[...]
        dy_ref[...] = err * (1.0 / d)

        @pl.when(i == 0)
        def _():
            sq_ref[...] = jnp.zeros_like(sq_ref)

        sq_ref[...] += jnp.sum(err * err, axis=0, keepdims=True)

    row = pl.BlockSpec((tr, d), lambda i: (i, 0))
    vec = pl.BlockSpec((1, d), lambda i: (0, 0))
    return pl.pallas_call(
        body, name="loss_head", grid=(s // tr,),
        in_specs=[row, row], out_specs=[row, vec],
        out_shape=[jax.ShapeDtypeStruct((s, d), F32), jax.ShapeDtypeStruct((1, d), F32)],
        compiler_params=_params(("arbitrary",)),
    )(y, target)


def _sb_masks():
    row = lax.broadcasted_iota(jnp.int32, (CHUNK, CHUNK), 0)
    col = lax.broadcasted_iota(jnp.int32, (CHUNK, CHUNK), 1)
    return row, col


SB_WINDOW = 2 * CHUNK
SB_ROWS = CHUNK // 2


def _sb_consts():
    row = lax.broadcasted_iota(jnp.int32, (CHUNK, SB_WINDOW), 0)
    col = lax.broadcasted_iota(jnp.int32, (CHUNK, SB_WINDOW), 1)
    lane_hi = lax.broadcasted_iota(jnp.int32, (SB_ROWS, HEAD_PAIR), 1) >= GROUP
    return col - jnp.bitwise_and(row, SB_ROWS - 1), col, lane_hi


def _sb_window(t0, m, dcol, col):
    hi = t0 + SB_ROWS - SB_WINDOW * m
    start = jnp.maximum(hi - SB_WINDOW, 0)
    keep = jnp.logical_and(dcol < t0 - start, col < hi - start)
    return pl.multiple_of(start, SB_ROWS), keep


def _sb_stack(x, lane_hi):
    zero = jnp.zeros_like(x)
    return jnp.concatenate([jnp.where(lane_hi, zero, x), jnp.where(lane_hi, x, zero)], axis=0)


def _sb_unstack(y, lane_hi):
    return jnp.where(lane_hi, y[SB_ROWS:], y[:SB_ROWS])


def _sb_logs(z, keep):
    t = jnp.log(1.0 + jnp.exp(-jnp.abs(z)))
    ln = -(jnp.maximum(z, 0.0) + t)
    lb = ln + z
    ln = jnp.where(keep, ln, 0.0)
    return ln, lb


def _sb_scores(qm, kw, keep):
    return _sb_logs(_dot(qm, kw, NT), keep)


def _sb_tri(upper):
    r = np.arange(SB_WINDOW)
    m = (r[:, None] > r[None, :]) if upper else (r[:, None] < r[None, :])
    return jnp.asarray(m, jnp.bfloat16)


def _riding_exchange(exchange):
    if exchange is None:
        return [], [], [], []
    arrays, shard_shapes, scatter = exchange
    shapes = [jax.ShapeDtypeStruct(s, a.dtype)
              for s, a in zip(_exchange_out_shapes(shard_shapes, scatter), arrays)]
    return list(arrays), shapes, [pl.BlockSpec(memory_space=pl.ANY)] * len(arrays), _exchange_sems()


def _ride(exchange, ins, outs, sems, first, last):
    if exchange is None:
        return lambda: None
    start, finish = _exchange_plan(ins, outs, sems, exchange[1], exchange[2])
    pl.when(first)(start)
    return lambda: pl.when(last)(finish)


def _attn_fwd(qkv, exchange=None):
    s = qkv.shape[0]
    nblk = s // CHUNK
    n_pairs = SB_W // HEAD_PAIR
    x_arrays, x_shapes, x_specs, x_sems = _riding_exchange(exchange)
    nx = len(x_arrays)

    def body(q_ref, k_ref, v_ref, up_ref, *refs):
        o_ref = refs[nx]
        i = pl.program_id(1)
        pair = pl.program_id(0)
        end_exchange = _ride(exchange, refs[:nx], refs[nx + 1:2 * nx + 1], refs[2 * nx + 1:],
                             jnp.logical_and(pair == 0, i == 0),
                             jnp.logical_and(pair == n_pairs - 1, i == nblk - 1))
        dcol, col, lane_hi = _sb_consts()
        halves = range(CHUNK // SB_ROWS)
        t0s = [i * CHUNK + r * SB_ROWS for r in halves]
        qs = [_sb_stack(q_ref[r * SB_ROWS:(r + 1) * SB_ROWS, :], lane_hi) * 0.125 for r in halves]

        def cond(st):
            return jnp.logical_and(st[0] * SB_WINDOW < (i + 1) * CHUNK, st[1])

        def loop(st):
            m = st[0]
            wins = [_sb_window(t0s[r], m, dcol, col) for r in halves]
            kws = [k_ref[pl.ds(wins[r][0], SB_WINDOW), :] for r in halves]
            vws = [v_ref[pl.ds(wins[r][0], SB_WINDOW), :] for r in halves]
            keeps = [wins[r][1] for r in halves]
            zs = [_dot(qs[r], kws[r], NT) for r in halves]
            lnlb = [_sb_logs(zs[r], keeps[r]) for r in halves]
            his = [lnlb[r][0].astype(jnp.bfloat16) for r in halves]
            los = [(lnlb[r][0] - his[r].astype(F32)).astype(jnp.bfloat16) for r in halves]
            s_hi = [_dot(his[r], up_ref[...], NN) for r in halves]
            s_lo = [_dot(los[r], up_ref[...], NN) for r in halves]
            a = [jnp.where(keeps[r], jnp.exp(lnlb[r][1] + (s_hi[r] + s_lo[r] + st[2 + 2 * r])), 0.0).astype(MX)
                 for r in halves]
            new = []
            for r in halves:
                new += [st[2 + 2 * r] + jnp.sum(lnlb[r][0], axis=1, keepdims=True),
                        st[3 + 2 * r] + _dot(a[r], vws[r], NN)]
            active = jnp.maximum(jnp.max(new[0]), jnp.max(new[2])) > SKIP_LOG
            return (m + 1, active, *new)

        c0 = jnp.zeros((CHUNK, 1), F32)
        a0 = jnp.zeros((CHUNK, HEAD_PAIR), F32)
        st = lax.while_loop(cond, loop, (i * 0, i >= 0, c0, a0, c0, a0))
        for r in halves:
            o_ref[r * SB_ROWS:(r + 1) * SB_ROWS, :] = _sb_unstack(st[3 + 2 * r], lane_hi).astype(o_ref.dtype)
        end_exchange()

    outs = pl.pallas_call(
        body, name="attn_fwd" if exchange is None else "attn_fwd_gather", grid=(n_pairs, nblk),
        in_specs=[pl.BlockSpec((CHUNK, HEAD_PAIR), lambda p, i: (i, p)),
                  pl.BlockSpec((s, HEAD_PAIR), lambda p, i: (0, n_pairs + p)),
                  pl.BlockSpec((s, HEAD_PAIR), lambda p, i: (0, 2 * n_pairs + p)),
                  pl.BlockSpec((SB_WINDOW, SB_WINDOW), lambda p, i: (0, 0))] + x_specs,
        out_specs=[pl.BlockSpec((CHUNK, HEAD_PAIR), lambda p, i: (i, p))] + x_specs,
        out_shape=[jax.ShapeDtypeStruct((s, SB_W), MX)] + x_shapes,
        scratch_shapes=x_sems,
        compiler_params=_params(("arbitrary", "arbitrary")),
    )(qkv, qkv, qkv, _sb_tri(True), *x_arrays)
    return outs[0], dict(zip(BIG, outs[1:]))


def _attn_bwd(qkv, d_o, exchange=None):
    s = qkv.shape[0]
    nblk = s // CHUNK
    n_pairs = SB_W // HEAD_PAIR
    n_win = nblk // 2 + 1
    x_arrays, x_shapes, x_specs, x_sems = _riding_exchange(exchange)
    nx = len(x_arrays)

    def body(q_ref, k_ref, v_ref, do_ref, up_ref, lo_ref, *refs):
        dq_ref, dk_ref, dv_ref = refs[nx:nx + 3]
        e_s, sig_s = refs[2 * nx + 3:2 * nx + 5]
        i = pl.program_id(1)
        pair = pl.program_id(0)
        end_exchange = _ride(exchange, refs[:nx], refs[nx + 3:2 * nx + 3], refs[2 * nx + 5:],
                             jnp.logical_and(pair == 0, i == 0),
                             jnp.logical_and(pair == n_pairs - 1, i == nblk - 1))
        dcol, col, lane_hi = _sb_consts()
        halves = range(CHUNK // SB_ROWS)
        t0s = [i * CHUNK + r * SB_ROWS for r in halves]
        qs = [_sb_stack(q_ref[r * SB_ROWS:(r + 1) * SB_ROWS, :], lane_hi) * 0.125 for r in halves]
        dos = [_sb_stack(do_ref[r * SB_ROWS:(r + 1) * SB_ROWS, :], lane_hi) for r in halves]

        @pl.when(i == 0)
        def _():
            dk_ref[...] = jnp.zeros_like(dk_ref)
            dv_ref[...] = jnp.zeros_like(dv_ref)

        def cond(st):
            return jnp.logical_and(st[0] * SB_WINDOW < (i + 1) * CHUNK, st[1])

        def loop(st):
            m = st[0]
            wins = [_sb_window(t0s[r], m, dcol, col) for r in halves]
            kws = [k_ref[pl.ds(wins[r][0], SB_WINDOW), :] for r in halves]
            vws = [v_ref[pl.ds(wins[r][0], SB_WINDOW), :] for r in halves]
            keeps = [wins[r][1] for r in halves]
            zs = [_dot(qs[r], kws[r], NT) for r in halves]
            das = [_dot(dos[r], vws[r], NT) for r in halves]
            lnlb = [_sb_logs(zs[r], keeps[r]) for r in halves]
            his = [lnlb[r][0].astype(jnp.bfloat16) for r in halves]
            los = [(lnlb[r][0] - his[r].astype(F32)).astype(jnp.bfloat16) for r in halves]
            s_hi = [_dot(his[r], up_ref[...], NN) for r in halves]
            s_lo = [_dot(los[r], up_ref[...], NN) for r in halves]
            a = [jnp.where(keeps[r], jnp.exp(lnlb[r][1] + (s_hi[r] + s_lo[r] + st[2 + r])), 0.0) for r in halves]
            for r in halves:
                e_s[r, m] = a[r] * das[r]
                sig_s[r, m] = jnp.exp(lnlb[r][1])
            parts = [_dot(a[r].astype(MX), dos[r], TN) for r in halves]
            for r in halves:
                dv_ref[pl.ds(wins[r][0], SB_WINDOW), :] += parts[r]
            cs = [st[2 + r] + jnp.sum(lnlb[r][0], axis=1, keepdims=True) for r in halves]
            active = jnp.maximum(jnp.max(cs[0]), jnp.max(cs[1])) > SKIP_LOG
            return (m + 1, active, *cs)

        c0 = jnp.zeros((CHUNK, 1), F32)
        n_seen = lax.while_loop(cond, loop, (i * 0, i >= 0, c0, c0))[0]

        def up(t, st):
            m = n_seen - 1 - t
            wins = [_sb_window(t0s[r], m, dcol, col) for r in halves]
            kws = [k_ref[pl.ds(wins[r][0], SB_WINDOW), :] for r in halves]
            es = [e_s[r, m] for r in halves]
            his = [es[r].astype(jnp.bfloat16) for r in halves]
            los = [(es[r] - his[r].astype(F32)).astype(jnp.bfloat16) for r in halves]
            p_hi = [_dot(his[r], lo_ref[...], NN) for r in halves]
            p_lo = [_dot(los[r], lo_ref[...], NN) for r in halves]
            dzs = []
            for r in halves:
                sig = sig_s[r, m]
                prefix = p_hi[r] + p_lo[r] + st[2 * r]
                dzs.append(jnp.where(wins[r][1], es[r] * (1.0 - sig) - prefix * sig, 0.0).astype(MX))
            parts = [_dot(dzs[r], qs[r], TN) for r in halves]
            for r in halves:
                dk_ref[pl.ds(wins[r][0], SB_WINDOW), :] += parts[r]
            new = []
            for r in halves:
                new += [st[2 * r] + jnp.sum(es[r], axis=1, keepdims=True),
                        st[2 * r + 1] + _dot(dzs[r], kws[r], NN)]
            return tuple(new)

        a0 = jnp.zeros((CHUNK, HEAD_PAIR), F32)
        st = lax.fori_loop(0, n_seen, up, (c0, a0, c0, a0))
        for r in halves:
            dq_ref[r * SB_ROWS:(r + 1) * SB_ROWS, :] = (_sb_unstack(st[2 * r + 1], lane_hi) * 0.125).astype(dq_ref.dtype)
        end_exchange()

    blk = pl.BlockSpec((CHUNK, HEAD_PAIR), lambda p, i: (i, p))
    full = pl.BlockSpec((s, HEAD_PAIR), lambda p, i: (0, p))
    tri = pl.BlockSpec((SB_WINDOW, SB_WINDOW), lambda p, i: (0, 0))
    outs = pl.pallas_call(
        body, name="attn_bwd" if exchange is None else "attn_bwd_scatter", grid=(n_pairs, nblk),
        in_specs=[blk,
                  pl.BlockSpec((s, HEAD_PAIR), lambda p, i: (0, n_pairs + p)),
                  pl.BlockSpec((s, HEAD_PAIR), lambda p, i: (0, 2 * n_pairs + p)),
                  blk, tri, tri] + x_specs,
        out_specs=[blk, full, full] + x_specs,
        out_shape=[jax.ShapeDtypeStruct((s, SB_W), MX), jax.ShapeDtypeStruct((s, SB_W), F32),
                   jax.ShapeDtypeStruct((s, SB_W), F32)] + x_shapes,
        scratch_shapes=[pltpu.VMEM((2, n_win, CHUNK, SB_WINDOW), F32),
                        pltpu.VMEM((2, n_win, CHUNK, SB_WINDOW), F32)] + x_sems,
        compiler_params=_params(("arbitrary", "arbitrary"), 56 * 1024 * 1024),
    )(qkv, qkv, qkv, d_o, _sb_tri(True), _sb_tri(False), *x_arrays)
    return outs[0], outs[1], outs[2], dict(zip(BIG, outs[3:]))


POOL_TILE = 256
POOL_HALO = 128


def _pool_bands(tile, transpose):
    cur = np.zeros((N_GROUPS, tile, tile), np.float32)
    halo = np.zeros((N_GROUPS, tile, POOL_HALO), np.float32)
    t = np.arange(tile)[:, None]
    for g, w in enumerate(POOL_WINDOWS):
        if not transpose:
            p = np.arange(tile)[None, :]
            cur[g] = ((t - p >= 0) & (t - p < w))
            ph = np.arange(POOL_HALO)[None, :] - POOL_HALO
            halo[g] = (t - ph < w)
        else:
            p = np.arange(tile)[None, :]
            cur[g] = ((p - t >= 0) & (p - t < w))
            ph = np.arange(POOL_HALO)[None, :] + tile
            halo[g] = (ph - t < w)
    return jnp.asarray(cur, jnp.bfloat16), jnp.asarray(halo, jnp.bfloat16)


def _pool_count(i, tile, rows, row0):
    pos = (i * tile + row0 + lax.broadcasted_iota(jnp.int32, (rows, PW), 0)).astype(F32)
    grp = lax.broadcasted_iota(jnp.int32, (rows, PW), 1) // GROUP
    win = jnp.where(grp == 0, float(POOL_WINDOWS[0]),
                    jnp.where(grp == 1, float(POOL_WINDOWS[1]),
                              jnp.where(grp == 2, float(POOL_WINDOWS[2]), float(POOL_WINDOWS[3]))))
    return jnp.minimum(pos + 1.0, win), grp


def _pool_fwd(rest, wbd, scale):
    s = rest.shape[0]
    tile = min(POOL_TILE, s)
    halo_per_tile = tile // POOL_HALO
    bcur, bhalo = _pool_bands(tile, False)

    def body(cur_ref, prev_ref, bcur_ref, bhalo_ref, wbd_ref, scale_ref, pooled_ref, out_ref):
        i = pl.program_id(0)
        cur = cur_ref[...]
        prev = jnp.where(i > 0, prev_ref[...], 0.0)
        count, grp = _pool_count(i, tile, tile, 0)
        win = jnp.zeros((tile, PW), F32)
        for g in range(N_GROUPS):
            wsum = _split_dot(cur, bcur_ref[g], True, 3) + _split_dot(prev, bhalo_ref[g], True, 3)
            win = jnp.where(grp == g, wsum, win)
        pooled = (win / count - cur).astype(MX)
        pooled_ref[...] = pooled
        out_ref[...] = (_dot(pooled, wbd_ref[...], NN) * scale_ref[...]).astype(out_ref.dtype)

    tile_spec = pl.BlockSpec((tile, PW), lambda i: (i, 0))
    const3 = lambda shape: pl.BlockSpec(shape, lambda i: (0, 0, 0))
    const2 = lambda shape: pl.BlockSpec(shape, lambda i: (0, 0))
    return pl.pallas_call(
        body, name="pool_fwd", grid=(s // tile,),
        in_specs=[tile_spec,
                  pl.BlockSpec((POOL_HALO, PW), lambda i: (jnp.maximum(i * halo_per_tile - 1, 0), 0)),
                  const3(bcur.shape), const3(bhalo.shape), const2((PW, PW)), const2((1, PW))],
        out_specs=[tile_spec, tile_spec],
        out_shape=[jax.ShapeDtypeStruct((s, PW), MX), jax.ShapeDtypeStruct((s, PW), MX)],
        compiler_params=_params(("parallel",)),
    )(rest, rest, bcur, bhalo, wbd, scale)


def _pool_bwd(d_o, pooled, wbd, scale):
    s = d_o.shape[0]
    tile = min(POOL_TILE, s)
    halo_per_tile = tile // POOL_HALO
    n_halo = s // POOL_HALO
    n_tiles = s // tile
    ccur, chalo = _pool_bands(tile, True)

    def body(do_ref, nxt_ref, pooled_ref, ccur_ref, chalo_ref, wbd_ref, scale_ref, dp_ref, dw_ref, ds_ref):
        i = pl.program_id(0)
        do = do_ref[...]
        w = wbd_ref[...]
        sc = scale_ref[...]
        pooled_v = pooled_ref[...]

        @pl.when(i == 0)
        def _():
            dw_ref[...] = jnp.zeros_like(dw_ref)
            ds_ref[...] = jnp.zeros_like(ds_ref)

        ds_ref[...] += jnp.sum(do * _dot(pooled_v, w, NN), axis=0, keepdims=True)
        dmixed = (do * sc).astype(MX)
        dw_ref[...] += _dot(pooled_v, dmixed, TN)
        dpooled = _dot(dmixed, w, NT)
        nxt = jnp.where(i < n_tiles - 1, nxt_ref[...], 0.0)
        dpooled_n = _dot((nxt * sc).astype(MX), w, NT)
        count, grp = _pool_count(i, tile, tile, 0)
        count_n, _ = _pool_count(i, tile, POOL_HALO, tile)
        dq = dpooled / count
        dq_n = dpooled_n / count_n
        acc = jnp.zeros((tile, PW), F32)
        for g in range(N_GROUPS):
            wsum = _split_dot(dq, ccur_ref[g], True, 3) + _split_dot(dq_n, chalo_ref[g], True, 3)
            acc = jnp.where(grp == g, wsum, acc)
        dp_ref[...] = (acc - dpooled).astype(dp_ref.dtype)

    tile_spec = pl.BlockSpec((tile, PW), lambda i: (i, 0))
    const3 = lambda shape: pl.BlockSpec(shape, lambda i: (0, 0, 0))
    const2 = lambda shape: pl.BlockSpec(shape, lambda i: (0, 0))
    return pl.pallas_call(
        body, name="pool_bwd", grid=(n_tiles,),
        in_specs=[tile_spec,
                  pl.BlockSpec((POOL_HALO, PW), lambda i: (jnp.minimum((i + 1) * halo_per_tile, n_halo - 1), 0)),
                  tile_spec, const3(ccur.shape), const3(chalo.shape), const2((PW, PW)), const2((1, PW))],
        out_specs=[tile_spec, const2((PW, PW)), const2((1, PW))],
        out_shape=[jax.ShapeDtypeStruct((s, PW), MX), jax.ShapeDtypeStruct((PW, PW), F32),
                   jax.ShapeDtypeStruct((1, PW), F32)],
        compiler_params=_params(("arbitrary",)),
    )(d_o, d_o, pooled, ccur, chalo, wbd, scale)


GM_TILE = 512
GELU_C = 0.7978845608028654
GELU_A = 0.044715


def _gelu(x):
    return 0.5 * x * (1.0 + jnp.tanh(GELU_C * (x + GELU_A * (x * x * x))))


def _gelu_grad(x):
    th = jnp.tanh(GELU_C * (x + GELU_A * (x * x * x)))
    return 0.5 * (1.0 + th) + 0.5 * x * (1.0 - th * th) * (GELU_C * (1.0 + 3.0 * GELU_A * (x * x)))


def _gm_common(ws_ref):
    row, col = _sb_masks()
    tril = row >= col
    wsm = [jnp.where(tril, ws_ref[g], 0.0).astype(MX) for g in range(N_GROUPS)]
    grp = lax.broadcasted_iota(jnp.int32, (CHUNK, PW), 1) // GROUP
    return tril, wsm, grp


def _gm_mix(wsm, grp, vn_c):
    mixed = jnp.zeros((CHUNK, PW), F32)
    for g in range(N_GROUPS):
        mixed = jnp.where(grp == g, _dot(wsm[g], vn_c, NN), mixed)
    return mixed


def _gm_fwd(rest, gain, ws, bfull):
    s = rest.shape[0]
    tile = min(GM_TILE, s)

    def body(u_ref, v_ref, gain_ref, ws_ref, b_ref, out_ref):
        _, wsm, grp = _gm_common(ws_ref)
        bias = b_ref[...]
        for n in range(tile // CHUNK):
            rows = slice(n * CHUNK, (n + 1) * CHUNK)
            gu = _gelu(u_ref[rows, :])
            gv = _gelu(v_ref[rows, :])
            r = lax.rsqrt(jnp.mean(gv * gv, axis=-1, keepdims=True) + RMS_EPS)
            vn = (gv * r * gain_ref[...]).astype(MX)
            out_ref[rows, :] = (gu * (_gm_mix(wsm, grp, vn) + bias)).astype(out_ref.dtype)

    const2 = lambda shape: pl.BlockSpec(shape, lambda i: (0, 0))
    return pl.pallas_call(
        body, name="gm_fwd", grid=(s // tile,),
        in_specs=[pl.BlockSpec((tile, PW), lambda i: (i, 1)), pl.BlockSpec((tile, PW), lambda i: (i, 2)),
                  const2((1, PW)), pl.BlockSpec((N_GROUPS, CHUNK, CHUNK), lambda i: (0, 0, 0)),
                  const2((CHUNK, PW))],
        out_specs=pl.BlockSpec((tile, PW), lambda i: (i, 0)),
        out_shape=jax.ShapeDtypeStruct((s, PW), MX),
        compiler_params=_params(("parallel",)),
    )(rest, rest, gain, ws, bfull)


def _gm_bwd(d_o, rest, gain, ws, bfull):
    s = rest.shape[0]
    tile = min(GM_TILE, s)

    def body(do_ref, u_ref, v_ref, gain_ref, ws_ref, b_ref, du_ref, dv_ref, dws_ref, db_ref, dgain_ref):
        i = pl.program_id(0)
        tril, wsm, grp = _gm_common(ws_ref)
        bias = b_ref[...]
        gain_v = gain_ref[...]

        @pl.when(i == 0)
        def _():
            dws_ref[...] = jnp.zeros_like(dws_ref)
            db_ref[...] = jnp.zeros_like(db_ref)
            dgain_ref[...] = jnp.zeros_like(dgain_ref)

        for n in range(tile // CHUNK):
            rows = slice(n * CHUNK, (n + 1) * CHUNK)
            u = u_ref[rows, :]
            v = v_ref[rows, :]
            do = do_ref[rows, :]
            gu = _gelu(u)
            gv = _gelu(v)
            r = lax.rsqrt(jnp.mean(gv * gv, axis=-1, keepdims=True) + RMS_EPS)
            vhat = gv * r
            vn = (vhat * gain_v).astype(MX)
            mixed = _gm_mix(wsm, grp, vn)
            du_ref[rows, :] = (do * (mixed + bias) * _gelu_grad(u)).astype(du_ref.dtype)
            dmix = do * gu
            db_ref[...] += dmix
            dmix_mx = dmix.astype(MX)
            dvn = jnp.zeros((CHUNK, PW), F32)
            for g in range(N_GROUPS):
                dmg = jnp.where(grp == g, dmix_mx, jnp.zeros_like(dmix_mx))
                dws_ref[g] += jnp.where(tril, _dot(dmg, vn, NT), 0.0)
                dvn = jnp.where(grp == g, _dot(wsm[g], dmix_mx, TN), dvn)
            dgain_ref[...] += jnp.sum(dvn * vhat, axis=0, keepdims=True)
            dvhat = dvn * gain_v
            dgv = r * (dvhat - vhat * jnp.mean(dvhat * vhat, axis=-1, keepdims=True))
            dv_ref[rows, :] = (dgv * _gelu_grad(v)).astype(dv_ref.dtype)

    const2 = lambda shape: pl.BlockSpec(shape, lambda i: (0, 0))
    ws_spec = pl.BlockSpec((N_GROUPS, CHUNK, CHUNK), lambda i: (0, 0, 0))
    tile_spec = pl.BlockSpec((tile, PW), lambda i: (i, 0))
    return pl.pallas_call(
        body, name="gm_bwd", grid=(s // tile,),
        in_specs=[tile_spec, pl.BlockSpec((tile, PW), lambda i: (i, 1)), pl.BlockSpec((tile, PW), lambda i: (i, 2)),
                  const2((1, PW)), ws_spec, const2((CHUNK, PW))],
        out_specs=[tile_spec, tile_spec, ws_spec, const2((CHUNK, PW)), const2((1, PW))],
        out_shape=[jax.ShapeDtypeStruct((s, PW), MX), jax.ShapeDtypeStruct((s, PW), MX),
                   jax.ShapeDtypeStruct((N_GROUPS, CHUNK, CHUNK), F32),
                   jax.ShapeDtypeStruct((CHUNK, PW), F32), jax.ShapeDtypeStruct((1, PW), F32)],
        compiler_params=_params(("arbitrary",)),
    )(d_o, rest, rest, gain, ws, bfull)


GATE_TM = 2048
GATE_TN = 256


def _gate_specs(s):
    tm = min(GATE_TM, s)
    tn = GATE_TN
    gate0 = GATE_COL0 // tn
    per = D // tn
    ins = [pl.BlockSpec((tm, SB_W), lambda i, j: (i, 0)),
           pl.BlockSpec((tm, PW), lambda i, j: (i, 0)),
           pl.BlockSpec((tm, PW), lambda i, j: (i, 0)),
           pl.BlockSpec((SB_W, tn), lambda i, j: (0, j)),
           pl.BlockSpec((PW, tn), lambda i, j: (0, j)),
           pl.BlockSpec((PW, tn), lambda i, j: (0, j)),
           pl.BlockSpec((tm, tn), lambda i, j: (i, gate0 + j)),
           pl.BlockSpec((tm, tn), lambda i, j: (i, gate0 + per + j)),
           pl.BlockSpec((tm, tn), lambda i, j: (i, gate0 + 2 * per + j))]
    return tm, tn, ins, pl.BlockSpec((tm, tn), lambda i, j: (i, j))


def _gate_fwd(o_sb, o_pool, o_gm, w_sb, w_pool, w_gm, rest):
    s = rest.shape[0]
    tm, tn, ins, out = _gate_specs(s)

    def body(o0, o1, o2, w0, w1, w2, g0, g1, g2, merged_ref):
        acc = jax.nn.sigmoid(g0[...]) * _dot(o0[...], w0[...], NN)
        acc += jax.nn.sigmoid(g1[...]) * _dot(o1[...], w1[...], NN)
        acc += jax.nn.sigmoid(g2[...]) * _dot(o2[...], w2[...], NN)
        merged_ref[...] = acc.astype(merged_ref.dtype)

    return pl.pallas_call(
        body, name="gate_fwd", grid=(s // tm, D // tn),
        in_specs=ins, out_specs=out, out_shape=jax.ShapeDtypeStruct((s, D), MX),
        compiler_params=_params(("parallel", "parallel")),
    )(o_sb, o_pool, o_gm, w_sb, w_pool, w_gm, rest, rest, rest)


def _gate_bwd(o_sb, o_pool, o_gm, w_sb, w_pool, w_gm, rest, dmerged):
    s = rest.shape[0]
    tm, tn, ins, out = _gate_specs(s)

    def body(o0, o1, o2, w0, w1, w2, g0, g1, g2, dm_ref, db0, db1, db2, dg0, dg1, dg2):
        dm = dm_ref[...]
        for o, w, g, db, dg in ((o0, w0, g0, db0, dg0), (o1, w1, g1, db1, dg1), (o2, w2, g2, db2, dg2)):
            sg = jax.nn.sigmoid(g[...])
            db[...] = (dm * sg).astype(db.dtype)
            dg[...] = (dm * _dot(o[...], w[...], NN) * (sg * (1.0 - sg))).astype(dg.dtype)

    return pl.pallas_call(
        body, name="gate_bwd", grid=(s // tm, D // tn),
        in_specs=ins + [out], out_specs=[out] * 6,
        out_shape=[jax.ShapeDtypeStruct((s, D), MX)] * 6,
        compiler_params=_params(("parallel", "parallel")),
    )(o_sb, o_pool, o_gm, w_sb, w_pool, w_gm, rest, rest, rest, dmerged)


def _relu2(acc):
    r = jnp.maximum(acc, 0.0)
    return r * r, 2.0 * r


def _relu2_bwd(acc, slope):
    return (acc * slope.astype(F32),)


TILES = {
    "proj_qkv": (2048, W_IN_TILE, 1024), "proj_rest": (2048, W_IN_TILE, 1024),
    "out_proj": (2048, 1024, 1024), "ff_in": (2048, 1024, 1024), "ff_out": (1024, 1024, 4096),
    "ff_out_dx": (2048, 1024, 1024), "ff_out_dw": (1024, 1024, 2048),
    "ff_in_dx": (1024, 1024, 4096), "ff_in_dw": (1024, 2048, 1024),
    "out_proj_dx": (2048, 1024, 1024), "out_proj_dw": (1024, 1024, 2048),
    "br_sb_dx": (2048, 512, 1024), "br_sb_dw": (512, 1024, 2048),
    "br_pool_dx": (2048, 256, 1024), "br_pool_dw": (256, 1024, 2048),
    "br_gm_dx": (2048, 256, 1024), "br_gm_dw": (256, 1024, 2048),
    "proj_dx": (512, 1024, 5376), "proj_dw": (1024, W_IN_TILE, 2048),
}


def _mm(name, a, b, mode, out_dtypes, **kw):
    tm, tn, tk = TILES[name]
    return _matmul(a, b, mode=mode, name=name, tm=tm, tn=tn, tk=tk, out_dtypes=out_dtypes, **kw)


def _layer_fwd(x, w, gather=None):
    h = _rms_fwd(x, w["g_mix_pre"], res=None, out_dtype=MX, name="rms_mix_pre")
    qkv = _mm("proj_qkv", h, w["w_in"], "nn", (MX,), n_out=QKV_W)
    rest = _mm("proj_rest", h, w["w_in"], "nn", (F32,), b_col0=QKV_W, n_out=REST_W)
    o_sb, gathered = _attn_fwd(qkv, gather)
    pooled, o_pool = _pool_fwd(rest, w["wbd"], w["pool_scale"])
    o_gm = _gm_fwd(rest, w["gm_gain"], w["w_spatial"], w["bfull"])
    merged = _gate_fwd(o_sb, o_pool, o_gm, w["w_br_sb"], w["w_br_pool"], w["w_br_gm"], rest)
    y = _mm("out_proj", merged, w["w_out"], "nn", (F32,))
    x1 = _rms_fwd(y, w["g_mix_post"], res=x, out_dtype=F32, name="rms_mix_post")
    h2 = _rms_fwd(x1, w["g_ff_pre"], res=None, out_dtype=MX, name="rms_ff_pre")
    r, r_slope = _mm("ff_in", h2, w["w_ff_in"], "nn", (MX, MX), epilogue=_relu2)
    ff = _mm("ff_out", r, w["w_ff_out"], "nn", (F32,))
    x2 = _rms_fwd(ff, w["g_ff_post"], res=x1, out_dtype=F32, name="rms_ff_post")
    saved = dict(x=x, h=h, qkv=qkv, rest=rest, o_sb=o_sb, pooled=pooled, o_pool=o_pool, o_gm=o_gm,
                 merged=merged, y=y, x1=x1, h2=h2, r=r, r_slope=r_slope, ff=ff)
    return x2, saved, gathered


def _layer_bwd(dx2, w, sv, scatter=None):
    dff, dg_ff_post = _rms_bwd(sv["ff"], w["g_ff_post"], dx2, res=None, out_dtype=MX, name="rms_ff_post_bwd")
    da = _mm("ff_out_dx", dff, w["w_ff_out"], "nt", (MX,), epilogue=_relu2_bwd, extras=(sv["r_slope"],))
    dw_ff_out = _mm("ff_out_dw", sv["r"], dff, "tn", (MX,))
    dh2 = _mm("ff_in_dx", da, w["w_ff_in"], "nt", (F32,))
    dw_ff_in = _mm("ff_in_dw", sv["h2"], da, "tn", (MX,))
    dx1, dg_ff_pre = _rms_bwd(sv["x1"], w["g_ff_pre"], dh2, res=dx2, out_dtype=F32, name="rms_ff_pre_bwd")

    dy, dg_mix_post = _rms_bwd(sv["y"], w["g_mix_post"], dx1, res=None, out_dtype=MX, name="rms_mix_post_bwd")
    dmerged = _mm("out_proj_dx", dy, w["w_out"], "nt", (F32,))
    dw_out = _mm("out_proj_dw", sv["merged"], dy, "tn", (MX,))
    db_sb, db_pool, db_gm, dg0, dg1, dg2 = _gate_bwd(
        sv["o_sb"], sv["o_pool"], sv["o_gm"], w["w_br_sb"], w["w_br_pool"], w["w_br_gm"], sv["rest"], dmerged)
    do_sb = _mm("br_sb_dx", db_sb, w["w_br_sb"], "nt", (MX,))
    dw_br_sb = _mm("br_sb_dw", sv["o_sb"], db_sb, "tn", (MX,))
    do_pool = _mm("br_pool_dx", db_pool, w["w_br_pool"], "nt", (F32,))
    dw_br_pool = _mm("br_pool_dw", sv["o_pool"], db_pool, "tn", (MX,))
    do_gm = _mm("br_gm_dx", db_gm, w["w_br_gm"], "nt", (F32,))
    dw_br_gm = _mm("br_gm_dw", sv["o_gm"], db_gm, "tn", (MX,))

    du, dv_gm, dws, dbfull, dgain = _gm_bwd(do_gm, sv["rest"], w["gm_gain"], w["w_spatial"], w["bfull"])
    dp, dwbd, dscale = _pool_bwd(do_pool, sv["pooled"], w["wbd"], w["pool_scale"])
    dq, dk, dv, scattered = _attn_bwd(sv["qkv"], do_sb, scatter)
    dproj = jnp.concatenate([dq, dk.astype(MX), dv.astype(MX), dp, du, dv_gm, dg0, dg1, dg2], axis=1)
    dh = _mm("proj_dx", dproj, w["w_in"], "nt", (F32,))
    dw_in = _mm("proj_dw", sv["h"], dproj, "tn", (MX,))
    dx, dg_mix_pre = _rms_bwd(sv["x"], w["g_mix_pre"], dh, res=dx1, out_dtype=F32, name="rms_mix_pre_bwd")

    grads = dict(
        w_in=dw_in, w_br_sb=dw_br_sb, w_br_pool=dw_br_pool, w_br_gm=dw_br_gm, w_out=dw_out,
        w_ff_in=dw_ff_in, w_ff_out=dw_ff_out,
        w_pool=jnp.stack([dwbd[g * GROUP:(g + 1) * GROUP, g * GROUP:(g + 1) * GROUP] for g in range(N_GROUPS)]),
        pool_scale=dscale[0], gm_gain=dgain[0], w_spatial=dws,
        b_spatial=dbfull.reshape(CHUNK, N_GROUPS, GROUP).sum(axis=-1).T,
        g_mix_pre=dg_mix_pre[0], g_mix_post=dg_mix_post[0], g_ff_pre=dg_ff_pre[0], g_ff_post=dg_ff_post[0])
    return dx, grads, scattered


def _layer_operands(full_l, small_l):
    wbd = jnp.zeros((PW, PW), F32)
    for g in range(N_GROUPS):
        wbd = wbd.at[g * GROUP:(g + 1) * GROUP, g * GROUP:(g + 1) * GROUP].set(small_l["w_pool"][g])
    w = dict(full_l)
    w["wbd"] = wbd.astype(MX)
    w["bfull"] = jnp.repeat(small_l["b_spatial"].T, GROUP, axis=1)
    w["w_spatial"] = small_l["w_spatial"]
    for n in ("pool_scale", "gm_gain", "g_mix_pre", "g_mix_post", "g_ff_pre", "g_ff_post"):
        w[n] = small_l[n][None, :]
    return w


def _local_step(x, target, layers):
    saved = []
    for w in layers:
        x, sv, _ = _layer_fwd(x, w)
        saved.append(sv)
    dx, sq = _loss_head(x, target)
    loss = 0.5 * jnp.sum(sq) / x.shape[1]
    grads = [None] * len(layers)
    for l in reversed(range(len(layers))):
        dx, grads[l], _ = _layer_bwd(dx, layers[l], saved[l])
    return loss, dx, grads


def _mesh_place():
    x, y, c = lax.axis_index("x"), lax.axis_index("y"), lax.axis_index("c")

    def peer(k):
        px = 1 - x if k & 4 else x
        py = 1 - y if k & 2 else y
        pc = 1 - c if k & 1 else c
        return (px, py, pc), 4 * px + 2 * py + pc

    return 4 * x + 2 * y + c, peer


def _shard_window(ref, name, shard_shape, idx):
    if name == "w_in":
        return ref.at[idx]
    a, b = shard_shape
    if BIG_SHARD_AXIS[name] == 2:
        return ref.at[:, pl.ds(pl.multiple_of(idx * b, b), b)]
    return ref.at[pl.ds(pl.multiple_of(idx * a, a), a), :]


def _full_shape(name, shard_shape):
    a, b = shard_shape
    if name == "w_in":
        return (N_DEV, a, b)
    return (a, N_DEV * b) if BIG_SHARD_AXIS[name] == 2 else (N_DEV * a, b)


def _exchange_out_shapes(shard_shapes, scatter):
    if scatter:
        return [(N_DEV,) + tuple(shard_shapes[nm]) for nm in BIG]
    return [_full_shape(nm, shard_shapes[nm]) for nm in BIG]


def _exchange_sems():
    n = len(BIG)
    return [pltpu.SemaphoreType.DMA((n, N_DEV - 1)), pltpu.SemaphoreType.DMA((n, N_DEV - 1)),
            pltpu.SemaphoreType.DMA((n,))]


def _exchange_plan(ins, outs, sems, shard_shapes, scatter):
    send_sems, recv_sems, local_sems = sems
    me, peer = _mesh_place()
    n = len(BIG)

    def window(ref, p, idx):
        return _shard_window(ref, BIG[p], shard_shapes[BIG[p]], idx)

    def copy(p, k, landing):
        dev, idx = peer(k)
        if scatter:
            src, dst = window(ins[p], p, idx), outs[p].at[idx if landing else me]
        else:
            src, dst = ins[p], window(outs[p], p, idx if landing else me)
        return pltpu.make_async_remote_copy(
            src_ref=src, dst_ref=dst, send_sem=send_sems.at[p, k - 1], recv_sem=recv_sems.at[p, k - 1],
            device_id=dev, device_id_type=pl.DeviceIdType.MESH)

    def local(p):
        if scatter:
            return pltpu.make_async_copy(window(ins[p], p, me), outs[p].at[me], local_sems.at[p])
        return pltpu.make_async_copy(ins[p], window(outs[p], p, me), local_sems.at[p])

    pairs = [(p, k) for p in range(n) for k in range(1, N_DEV)]

    def start():
        for p in range(n):
            local(p).start()
        for p, k in pairs:
            copy(p, k, False).start()

    def finish():
        for p, k in pairs:
            copy(p, k, True).wait_recv()
        for p, k in pairs:
            copy(p, k, False).wait_send()
        for p in range(n):
            local(p).wait()

    return start, finish


def _exchange_big(arrays, shard_shapes, *, scatter, name):
    n = len(BIG)

    def body(*refs):
        start, finish = _exchange_plan(refs[:n], refs[n:2 * n], refs[2 * n:], shard_shapes, scatter)
        start()
        finish()

    outs = pl.pallas_call(
        body, name=name,
        in_specs=[pl.BlockSpec(memory_space=pl.ANY)] * n,
        out_specs=[pl.BlockSpec(memory_space=pl.ANY)] * n,
        out_shape=[jax.ShapeDtypeStruct(s, a.dtype)
                   for s, a in zip(_exchange_out_shapes(shard_shapes, scatter), arrays)],
        scratch_shapes=_exchange_sems(),
    )(*arrays)
    return dict(zip(BIG, outs))


def _gather_rows(inp, *, name):
    def body(in_ref, out_ref, send_sems, recv_sems, local_sem):
        me, peer = _mesh_place()

        def copy(k, landing):
            dev, idx = peer(k)
            return pltpu.make_async_remote_copy(
                src_ref=in_ref, dst_ref=out_ref.at[idx if landing else me],
                send_sem=send_sems.at[k - 1], recv_sem=recv_sems.at[k - 1],
                device_id=dev, device_id_type=pl.DeviceIdType.MESH)

        mine = pltpu.make_async_copy(in_ref, out_ref.at[me], local_sem)
        mine.start()
        sends = [copy(k, False) for k in range(1, N_DEV)]
        for cp in sends:
            cp.start()
        for k in range(1, N_DEV):
            copy(k, True).wait_recv()
        for cp in sends:
            cp.wait_send()
        mine.wait()

    return pl.pallas_call(
        body, name=name,
        in_specs=[pl.BlockSpec(memory_space=pl.ANY)],
        out_specs=pl.BlockSpec(memory_space=pl.ANY),
        out_shape=jax.ShapeDtypeStruct((N_DEV,) + inp.shape, inp.dtype),
        scratch_shapes=[pltpu.SemaphoreType.DMA((N_DEV - 1,)), pltpu.SemaphoreType.DMA((N_DEV - 1,)),
                        pltpu.SemaphoreType.DMA],
    )(inp)


def _sum_devices(recv, name):
    _, rows, cols = recv.shape
    tr = rows
    for cand in (512, 256, 128, 64, 32, 16, 8):
        if rows % cand == 0:
            tr = cand
            break

    def body(r_ref, out_ref):
        acc = r_ref[0].astype(F32)
        for d in range(1, N_DEV):
            acc = acc + r_ref[d].astype(F32)
        out_ref[...] = acc

    return pl.pallas_call(
        body, name=name, grid=(rows // tr,),
        in_specs=[pl.BlockSpec((N_DEV, tr, cols), lambda i: (0, i, 0))],
        out_specs=pl.BlockSpec((tr, cols), lambda i: (i, 0)),
        out_shape=jax.ShapeDtypeStruct((rows, cols), F32),
        compiler_params=_params(("parallel",)),
    )(recv)


def _adamw(w, g, m, v, name):
    rows, cols = w.shape
    tr = rows
    for cand in (512, 256, 128, 64, 32, 16, 8):
        if rows % cand == 0:
            tr = cand
            break
    c1 = 1.0 - ADAM_B1 ** ADAM_STEP
    c2 = 1.0 - ADAM_B2 ** ADAM_STEP

    def body(w_ref, g_ref, m_ref, v_ref, d_ref, nm_ref, nv_ref):
        gv = g_ref[...]
        nm = ADAM_B1 * m_ref[...] + (1.0 - ADAM_B1) * gv
        nv = ADAM_B2 * v_ref[...] + (1.0 - ADAM_B2) * (gv * gv)
        nm_ref[...] = nm
        nv_ref[...] = nv
        d_ref[...] = -ADAM_LR * ((nm / c1) / (jnp.sqrt(nv / c2) + ADAM_EPS) + ADAM_WD * w_ref[...])

    spec = pl.BlockSpec((tr, cols), lambda i: (i, 0))
    return pl.pallas_call(
        body, name=name, grid=(rows // tr,),
        in_specs=[spec] * 4, out_specs=[spec] * 3,
        out_shape=[jax.ShapeDtypeStruct((rows, cols), F32)] * 3,
        compiler_params=_params(("parallel",)),
    )(w, g, m, v)


def _pack_rows(flat, dtype):
    n = flat.shape[-1]
    rows = -(-n // PACK_COLS)
    rows = -(-rows // 16) * 16
    pad = rows * PACK_COLS - n
    flat = jnp.pad(flat, [(0, 0)] * (flat.ndim - 1) + [(0, pad)])
    return flat.reshape(flat.shape[:-1] + (rows, PACK_COLS)).astype(dtype)


def kernel(x, w_in, w_pool, pool_scale, gm_gain, w_spatial, b_spatial, w_br_sb, w_br_pool, w_br_gm, w_out, g_mix_pre, g_mix_post, g_ff_pre, g_ff_post, w_ff_in, w_ff_out, loss_target, m_w_in, m_w_pool, m_pool_scale, m_gm_gain, m_w_spatial, m_b_spatial, m_w_br_sb, m_w_br_pool, m_w_br_gm, m_w_out, m_g_mix_pre, m_g_mix_post, m_g_ff_pre, m_g_ff_post, m_w_ff_in, m_w_ff_out, v_w_in, v_w_pool, v_pool_scale, v_gm_gain, v_w_spatial, v_b_spatial, v_w_br_sb, v_w_br_pool, v_w_br_gm, v_w_out, v_g_mix_pre, v_g_mix_post, v_g_ff_pre, v_g_ff_post, v_w_ff_in, v_w_ff_out):
    args = dict(locals())
    weights = {n: args[n] for n in WEIGHTS}
    mom_m = {n: args["m_" + n] for n in WEIGHTS}
    mom_v = {n: args["v_" + n] for n in WEIGHTS}

    shard_shapes = {n: weights[n].shape[1:] for n in BIG}
    depth, d_model, w_in_cols = weights["w_in"].shape

    def gather_of(l):
        return [weights[n][l].astype(MX) for n in BIG], shard_shapes, False

    def as_operands(full, l):
        full = dict(full)
        full["w_in"] = jnp.transpose(full["w_in"], (1, 0, 2)).reshape(d_model, N_DEV * w_in_cols)
        return _layer_operands(full, {n: weights[n][l] for n in SMALL})

    full = _exchange_big(gather_of(0)[0], shard_shapes, scatter=False, name="gather_weights")
    xc = x[0]
    layers, saved = [], []
    for l in range(depth):
        layers.append(as_operands(full, l))
        xc, sv, full = _layer_fwd(xc, layers[l], gather_of(l + 1) if l + 1 < depth else None)
        saved.append(sv)
    dx, sq = _loss_head(xc, loss_target[0])
    loss = lax.psum(0.5 * jnp.sum(sq) / d_model, ("x", "y", "c"))

    def scatter_of(g):
        g = dict(g)
        g["w_in"] = jnp.transpose(g["w_in"].reshape(d_model, N_DEV, w_in_cols), (1, 0, 2))
        return [g[n] for n in BIG], shard_shapes, True

    layer_grads, recv = [None] * depth, [None] * depth
    for l in reversed(range(depth)):
        riding = scatter_of(layer_grads[l + 1]) if l + 1 < depth else None
        dx, layer_grads[l], arrived = _layer_bwd(dx, layers[l], saved[l], riding)
        if riding is not None:
            recv[l + 1] = arrived
    recv[0] = _exchange_big(scatter_of(layer_grads[0])[0], shard_shapes, scatter=True, name="scatter_grads")
    grad_w = {n: jnp.stack([_sum_devices(recv[l][n], "sum_grads_" + n) for l in range(depth)]) for n in BIG}
    small_sizes = [int(np.prod(weights[n].shape)) for n in SMALL]
    small_off = np.concatenate([[0], np.cumsum(small_sizes)])
    small_flat = jnp.concatenate([jnp.stack([g[n] for g in layer_grads]).reshape(-1) for n in SMALL])
    g_small = _sum_devices(_gather_rows(_pack_rows(small_flat, F32), name="gather_small_grads"),
                           "sum_small_grads").reshape(-1)
    for n, lo, hi in zip(SMALL, small_off[:-1], small_off[1:]):
        grad_w[n] = g_small[lo:hi].reshape(weights[n].shape)

    delta, new_m, new_v = {}, {}, {}
    for n in WEIGHTS:
        shape = weights[n].shape
        view = (-1, shape[-1])
        d, nm, nv = _adamw(weights[n].reshape(view), grad_w[n].reshape(view), mom_m[n].reshape(view),
                           mom_v[n].reshape(view), "adamw_" + n)
        delta[n], new_m[n], new_v[n] = d.reshape(shape), nm.reshape(shape), nv.reshape(shape)

    return (loss, dx[None], *[grad_w[n] for n in WEIGHTS], *[delta[n] for n in WEIGHTS],
            *[new_m[n] for n in WEIGHTS], *[new_v[n] for n in WEIGHTS])
```

```python
import functools

import numpy as np
import jax
import jax.numpy as jnp
from jax import lax
from jax.experimental import pallas as pl
from jax.experimental.pallas import tpu as pltpu

F32 = jnp.float32
MX = jnp.bfloat16

D = 1024
SB_W = 512
HEAD_PAIR = 128
PW = 256
GROUP = 64
N_GROUPS = 4
CHUNK = 128
POOL_WINDOWS = (2, 4, 8, 16)
D_FF = 4096
D_IN = 5376
QKV_W = 3 * SB_W
REST_W = D_IN - QKV_W
GATE_COL0 = 3 * PW
RMS_EPS = 1e-6
N_DEV = 8
DEPTH = 4

ADAM_LR = 0.001
ADAM_B1 = 0.9
ADAM_B2 = 0.999
ADAM_EPS = 1e-08
ADAM_WD = 0.01
ADAM_STEP = 10

SKIP_LOG = -120.0

VMEM_LIMIT = 48 * 1024 * 1024
W_IN_TILE = 768
PACK_COLS = 1024

BIG = ("w_in", "w_br_sb", "w_br_pool", "w_br_gm", "w_out", "w_ff_in", "w_ff_out")
BIG_SHARD_AXIS = {"w_in": 2, "w_br_sb": 2, "w_br_pool": 2, "w_br_gm": 2, "w_out": 1, "w_ff_in": 2, "w_ff_out": 1}
SMALL = ("w_pool", "pool_scale", "gm_gain", "w_spatial", "b_spatial",
         "g_mix_pre", "g_mix_post", "g_ff_pre", "g_ff_post")
WEIGHTS = ("w_in", "w_pool", "pool_scale", "gm_gain", "w_spatial", "b_spatial", "w_br_sb", "w_br_pool",
           "w_br_gm", "w_out", "g_mix_pre", "g_mix_post", "g_ff_pre", "g_ff_post", "w_ff_in", "w_ff_out")


def _params(sem, vmem=VMEM_LIMIT):
    return pltpu.CompilerParams(dimension_semantics=sem, vmem_limit_bytes=vmem)


def _dot(a, b, dims):
    return lax.dot_general(a, b, (dims, ((), ())), preferred_element_type=F32)


NN = ((1,), (0,))
NT = ((1,), (1,))
TN = ((0,), (0,))


def _split_dot(x, m, left, nsplit):
    acc = None
    r = x
    for s in range(nsplit):
        p = r.astype(jnp.bfloat16)
        d = _dot(m, p, NN) if left else _dot(p, m, NN)
        acc = d if acc is None else acc + d
        if s + 1 < nsplit:
            r = r - p.astype(F32)
    return acc


def _matmul(a, b, *, mode, name, tm, tn, tk, out_dtypes, b_col0=0, n_out=None, epilogue=None, extras=()):
    if mode == "nn":
        m, kdim = a.shape
        n = b.shape[1] if n_out is None else n_out
        dims = NN
    elif mode == "nt":
        m, kdim = a.shape
        n = b.shape[0]
        dims = NT
    else:
        kdim, m = a.shape
        n = b.shape[1]
        dims = TN
    tm, tn, tk = min(tm, m), min(tn, n), min(tk, kdim)
    assert m % tm == 0 and n % tn == 0 and kdim % tk == 0 and b_col0 % tn == 0, (name, m, n, kdim)
    if mode == "nn":
        a_spec = pl.BlockSpec((tm, tk), lambda i, j, k: (i, k))
        b_spec = pl.BlockSpec((tk, tn), lambda i, j, k: (k, j + b_col0 // tn))
    elif mode == "nt":
        a_spec = pl.BlockSpec((tm, tk), lambda i, j, k: (i, k))
        b_spec = pl.BlockSpec((tn, tk), lambda i, j, k: (j, k))
    else:
        a_spec = pl.BlockSpec((tk, tm), lambda i, j, k: (k, i))
        b_spec = pl.BlockSpec((tk, tn), lambda i, j, k: (k, j))
    nk = kdim // tk
    n_extra = len(extras)
    n_outs = len(out_dtypes)
    o_spec = pl.BlockSpec((tm, tn), lambda i, j, k: (i, j))

    def body(a_ref, b_ref, *refs):
        extra_refs = refs[:n_extra]
        out_refs = refs[n_extra:n_extra + n_outs]
        k = pl.program_id(2)
        part = _dot(a_ref[...].astype(MX), b_ref[...].astype(MX), dims)

        def finish(acc):
            outs = (acc,) if epilogue is None else epilogue(acc, *[e[...] for e in extra_refs])
            for o_ref, val in zip(out_refs, outs):
                o_ref[...] = val.astype(o_ref.dtype)

        if nk == 1:
            finish(part)
            return
        acc_ref = refs[-1]

        @pl.when(k == 0)
        def _():
            acc_ref[...] = part

        @pl.when(jnp.logical_and(k > 0, k < nk - 1))
        def _():
            acc_ref[...] += part

        @pl.when(k == nk - 1)
        def _():
            finish(acc_ref[...] + part)

    outs = pl.pallas_call(
        body, name=name,
        grid=(m // tm, n // tn, nk),
        in_specs=[a_spec, b_spec] + [o_spec] * n_extra,
        out_specs=[o_spec] * n_outs,
        out_shape=[jax.ShapeDtypeStruct((m, n), dt) for dt in out_dtypes],
        scratch_shapes=[pltpu.VMEM((tm, tn), F32)] if nk > 1 else [],
        compiler_params=_params(("parallel", "parallel", "arbitrary")),
    )(a, b, *extras)
    return outs[0] if n_outs == 1 else outs


ROW_TILE = 1024


def _rms_fwd(x, g, *, res, out_dtype, name):
    s, d = x.shape
    tr = min(ROW_TILE, s)
    has_res = res is not None

    def body(x_ref, g_ref, *refs):
        out_ref = refs[-1]
        xv = x_ref[...]
        y = xv * lax.rsqrt(jnp.mean(xv * xv, axis=-1, keepdims=True) + RMS_EPS) * g_ref[...]
        if has_res:
            y = refs[0][...] + y
        out_ref[...] = y.astype(out_ref.dtype)

    row = pl.BlockSpec((tr, d), lambda i: (i, 0))
    vec = pl.BlockSpec((1, d), lambda i: (0, 0))
    return pl.pallas_call(
        body, name=name, grid=(s // tr,),
        in_specs=[row, vec] + ([row] if has_res else []),
        out_specs=row, out_shape=jax.ShapeDtypeStruct((s, d), out_dtype),
        compiler_params=_params(("parallel",)),
    )(x, g, *([res] if has_res else []))


def _rms_bwd(x, g, dout, *, res, out_dtype, name):
    s, d = x.shape
    tr = min(ROW_TILE, s)
    has_res = res is not None

    def body(x_ref, g_ref, do_ref, *refs):
        dx_ref, dg_ref = refs[-2], refs[-1]
        i = pl.program_id(0)
        xv = x_ref[...]
        do = do_ref[...]
        r = lax.rsqrt(jnp.mean(xv * xv, axis=-1, keepdims=True) + RMS_EPS)
        xhat = xv * r
        dxhat = do * g_ref[...]
        dx = r * (dxhat - xhat * jnp.mean(dxhat * xhat, axis=-1, keepdims=True))
        if has_res:
            dx = refs[0][...] + dx
        dx_ref[...] = dx.astype(dx_ref.dtype)

        @pl.when(i == 0)
        def _():
            dg_ref[...] = jnp.zeros_like(dg_ref)

        dg_ref[...] += jnp.sum(do * xhat, axis=0, keepdims=True)

    row = pl.BlockSpec((tr, d), lambda i: (i, 0))
    vec = pl.BlockSpec((1, d), lambda i: (0, 0))
    return pl.pallas_call(
        body, name=name, grid=(s // tr,),
        in_specs=[row, vec, row] + ([row] if has_res else []),
        out_specs=[row, vec],
        out_shape=[jax.ShapeDtypeStruct((s, d), out_dtype), jax.ShapeDtypeStruct((1, d), F32)],
        compiler_params=_params(("arbitrary",)),
    )(x, g, dout, *([res] if has_res else []))


def _loss_head(y, target):
    s, d = y.shape
    tr = min(ROW_TILE, s)

    def body(y_ref, t_ref, dy_ref, sq_ref):
        i = pl.program_id(0)
        err = y_ref[...] - t_ref[...]
        dy_ref[...] = err * (1.0 / d)

        @pl.when(i == 0)
        def _():
            sq_ref[...] = jnp.zeros_like(sq_ref)

        sq_ref[...] += jnp.sum(err * err, axis=0, keepdims=True)

    row = pl.BlockSpec((tr, d), lambda i: (i, 0))
    vec = pl.BlockSpec((1, d), lambda i: (0, 0))
    return pl.pallas_call(
        body, name="loss_head", grid=(s // tr,),
        in_specs=[row, row], out_specs=[row, vec],
        out_shape=[jax.ShapeDtypeStruct((s, d), F32), jax.ShapeDtypeStruct((1, d), F32)],
        compiler_params=_params(("arbitrary",)),
    )(y, target)


def _sb_masks():
    row = lax.broadcasted_iota(jnp.int32, (CHUNK, CHUNK), 0)
    col = lax.broadcasted_iota(jnp.int32, (CHUNK, CHUNK), 1)
    return row, col


SB_WINDOW = 2 * CHUNK
SB_ROWS = CHUNK // 2
SB_FWD_QUERIES = 256
SB_BWD_QUERIES = 256

def _sb_consts():
    row = lax.broadcasted_iota(jnp.int32, (CHUNK, SB_WINDOW), 0)
    col = lax.broadcasted_iota(jnp.int32, (CHUNK, SB_WINDOW), 1)
    lane_hi = lax.broadcasted_iota(jnp.int32, (SB_ROWS, HEAD_PAIR), 1) >= GROUP
    return col - jnp.bitwise_and(row, SB_ROWS - 1), col, lane_hi


def _sb_window(t0, m, dcol, col):
    hi = t0 + SB_ROWS - SB_WINDOW * m
    start = jnp.maximum(hi - SB_WINDOW, 0)
    keep = jnp.logical_and(dcol < t0 - start, col < hi - start)
    return pl.multiple_of(start, SB_ROWS), keep


def _sb_stack(x, lane_hi):
    zero = jnp.zeros_like(x)
    return jnp.concatenate([jnp.where(lane_hi, zero, x), jnp.where(lane_hi, x, zero)], axis=0)


def _sb_unstack(y, lane_hi):
    return jnp.where(lane_hi, y[SB_ROWS:], y[:SB_ROWS])


def _sb_logs(z, keep):
    t = jnp.log(1.0 + jnp.exp(-jnp.abs(z)))
    ln = -(jnp.maximum(z, 0.0) + t)
    lb = ln + z
    ln = jnp.where(keep, ln, 0.0)
    return ln, lb


def _sb_scores(qm, kw, keep):
    return _sb_logs(_dot(qm, kw, NT), keep)


def _sb_tri(upper):
    r = np.arange(SB_WINDOW)
    m = (r[:, None] > r[None, :]) if upper else (r[:, None] < r[None, :])
    return jnp.asarray(m, jnp.bfloat16)


def _riding_exchange(exchange):
    if exchange is None:
        return [], [], [], []
    arrays, shard_shapes, scatter = exchange
    shapes = [jax.ShapeDtypeStruct(s, a.dtype)
              for s, a in zip(_exchange_out_shapes(shard_shapes, scatter), arrays)]
    return list(arrays), shapes, [pl.BlockSpec(memory_space=pl.ANY)] * len(arrays), _exchange_sems()


def _ride(exchange, ins, outs, sems, first, last):
    if exchange is None:
        return lambda: None
    start, finish = _exchange_plan(ins, outs, sems, exchange[1], exchange[2])
    pl.when(first)(start)
    return lambda: pl.when(last)(finish)


def _attn_fwd(qkv, exchange=None):
    s = qkv.shape[0]
    qb = min(SB_FWD_QUERIES, s)
    nblk = s // qb
    n_pairs = SB_W // HEAD_PAIR
    x_arrays, x_shapes, x_specs, x_sems = _riding_exchange(exchange)
    nx = len(x_arrays)

    def body(q_ref, k_ref, v_ref, up_ref, *refs):
        o_ref = refs[nx]
        i = pl.program_id(1)
        pair = pl.program_id(0)
        end_exchange = _ride(exchange, refs[:nx], refs[nx + 1:2 * nx + 1], refs[2 * nx + 1:],
                             jnp.logical_and(pair == 0, i == 0),
                             jnp.logical_and(pair == n_pairs - 1, i == nblk - 1))
        dcol, col, lane_hi = _sb_consts()
        halves = range(qb // SB_ROWS)
        t0s = [i * qb + r * SB_ROWS for r in halves]
        qs = [_sb_stack(q_ref[r * SB_ROWS:(r + 1) * SB_ROWS, :], lane_hi) * 0.125 for r in halves]

        def cond(st):
            return jnp.logical_and(st[0] * SB_WINDOW < (i + 1) * qb, st[1])

        def loop(st):
            m = st[0]
            wins = [_sb_window(t0s[r], m, dcol, col) for r in halves]
            kws = [k_ref[pl.ds(wins[r][0], SB_WINDOW), :] for r in halves]
            vws = [v_ref[pl.ds(wins[r][0], SB_WINDOW), :] for r in halves]
            keeps = [wins[r][1] for r in halves]
            zs = [_dot(qs[r], kws[r], NT) for r in halves]
            lnlb = [_sb_logs(zs[r], keeps[r]) for r in halves]
            his = [lnlb[r][0].astype(jnp.bfloat16) for r in halves]
            los = [(lnlb[r][0] - his[r].astype(F32)).astype(jnp.bfloat16) for r in halves]
            s_hi = [_dot(his[r], up_ref[...], NN) for r in halves]
            s_lo = [_dot(los[r], up_ref[...], NN) for r in halves]
            a = [jnp.where(keeps[r], jnp.exp(lnlb[r][1] + (s_hi[r] + s_lo[r] + st[2 + 2 * r])), 0.0).astype(MX)
                 for r in halves]
            new = []
            for r in halves:
                new += [st[2 + 2 * r] + jnp.sum(lnlb[r][0], axis=1, keepdims=True),
                        st[3 + 2 * r] + _dot(a[r], vws[r], NN)]
            active = functools.reduce(jnp.maximum, [jnp.max(c) for c in new[0::2]]) > SKIP_LOG
            return (m + 1, active, *new)

        c0 = jnp.zeros((CHUNK, 1), F32)
        a0 = jnp.zeros((CHUNK, HEAD_PAIR), F32)
        st = lax.while_loop(cond, loop, (i * 0, i >= 0) + (c0, a0) * len(halves))
        for r in halves:
            o_ref[r * SB_ROWS:(r + 1) * SB_ROWS, :] = _sb_unstack(st[3 + 2 * r], lane_hi).astype(o_ref.dtype)
        end_exchange()

    outs = pl.pallas_call(
        body, name="attn_fwd" if exchange is None else "attn_fwd_gather", grid=(n_pairs, nblk),
        in_specs=[pl.BlockSpec((qb, HEAD_PAIR), lambda p, i: (i, p)),
                  pl.BlockSpec((s, HEAD_PAIR), lambda p, i: (0, n_pairs + p)),
                  pl.BlockSpec((s, HEAD_PAIR), lambda p, i: (0, 2 * n_pairs + p)),
                  pl.BlockSpec((SB_WINDOW, SB_WINDOW), lambda p, i: (0, 0))] + x_specs,
        out_specs=[pl.BlockSpec((qb, HEAD_PAIR), lambda p, i: (i, p))] + x_specs,
        out_shape=[jax.ShapeDtypeStruct((s, SB_W), MX)] + x_shapes,
        scratch_shapes=x_sems,
        compiler_params=_params(("arbitrary", "arbitrary")),
    )(qkv, qkv, qkv, _sb_tri(True), *x_arrays)
    return outs[0], dict(zip(BIG, outs[1:]))


def _attn_bwd(qkv, d_o, exchange=None):
    s = qkv.shape[0]
    qb = min(SB_BWD_QUERIES, s)
    nblk = s // qb
    n_pairs = SB_W // HEAD_PAIR
    n_win = s // SB_WINDOW
    x_arrays, x_shapes, x_specs, x_sems = _riding_exchange(exchange)
    nx = len(x_arrays)

    def body(q_ref, k_ref, v_ref, do_ref, up_ref, lo_ref, *refs):
        dq_ref, dk_ref, dv_ref = refs[nx:nx + 3]
        e_s = refs[2 * nx + 3]
        i = pl.program_id(1)
        pair = pl.program_id(0)
        end_exchange = _ride(exchange, refs[:nx], refs[nx + 3:2 * nx + 3], refs[2 * nx + 4:],
                             jnp.logical_and(pair == 0, i == 0),
                             jnp.logical_and(pair == n_pairs - 1, i == nblk - 1))
        dcol, col, lane_hi = _sb_consts()
        halves = range(qb // SB_ROWS)
        t0s = [i * qb + r * SB_ROWS for r in halves]
        qs = [_sb_stack(q_ref[r * SB_ROWS:(r + 1) * SB_ROWS, :], lane_hi) * 0.125 for r in halves]
        dos =[_sb_stack(do_ref[r * SB_ROWS:(r + 1) * SB_ROWS, :], lane_hi) for r in halves]

        @pl.when(i == 0)
        def _():
            dk_ref[...] = jnp.zeros_like(dk_ref)
            dv_ref[...] = jnp.zeros_like(dv_ref)

        def cond(st):
            return jnp.logical_and(st[0] * SB_WINDOW < (i + 1) * qb, st[1])

        def loop(st):
            m = st[0]
            wins = [_sb_window(t0s[r], m, dcol, col) for r in halves]
            kws = [k_ref[pl.ds(wins[r][0], SB_WINDOW), :] for r in halves]
            vws = [v_ref[pl.ds(wins[r][0], SB_WINDOW), :] for r in halves]
            keeps = [wins[r][1] for r in halves]
            zs = [_dot(qs[r], kws[r], NT) for r in halves]
            das = [_dot(dos[r], vws[r], NT) for r in halves]
            lnlb = [_sb_logs(zs[r], keeps[r]) for r in halves]
            his = [lnlb[r][0].astype(jnp.bfloat16) for r in halves]
            los = [(lnlb[r][0] - his[r].astype(F32)).astype(jnp.bfloat16) for r in halves]
            s_hi = [_dot(his[r], up_ref[...], NN) for r in halves]
            s_lo = [_dot(los[r], up_ref[...], NN) for r in halves]
            a = [jnp.where(keeps[r], jnp.exp(lnlb[r][1] + (s_hi[r] + s_lo[r] + st[2 + r])), 0.0) for r in halves]
            for r in halves:
                e_s[r, m] = a[r] * das[r]
            parts = [_dot(a[r].astype(MX), dos[r], TN) for r in halves]
            for r in halves:
                dv_ref[pl.ds(wins[r][0], SB_WINDOW), :] += parts[r]
            cs = [st[2 + r] + jnp.sum(lnlb[r][0], axis=1, keepdims=True) for r in halves]
            active = functools.reduce(jnp.maximum, [jnp.max(c) for c in cs]) > SKIP_LOG
            return (m + 1, active, *cs)

        c0 = jnp.zeros((CHUNK, 1), F32)
        n_seen = lax.while_loop(cond, loop, (i * 0, i >= 0) + (c0,) * len(halves))[0]

        def up(t, st):
            m = n_seen - 1 - t
            wins = [_sb_window(t0s[r], m, dcol, col) for r in halves]
            kws = [k_ref[pl.ds(wins[r][0], SB_WINDOW), :] for r in halves]
            es = [e_s[r, m] for r in halves]
            zs = [_dot(qs[r], kws[r], NT) for r in halves]
            his = [es[r].astype(jnp.bfloat16) for r in halves]
            los = [(es[r] - his[r].astype(F32)).astype(jnp.bfloat16) for r in halves]
            p_hi = [_dot(his[r], lo_ref[...], NN) for r in halves]
            p_lo = [_dot(los[r], lo_ref[...], NN) for r in halves]
            dzs = []
            for r in halves:
                u = jnp.exp(-jnp.abs(zs[r]))
                big = 1.0 / (1.0 + u)
                small = u * big
                pos = zs[r] >= 0.0
                prefix = p_hi[r] + p_lo[r] + st[2 * r]
                dz = es[r] * jnp.where(pos, small, big) - prefix * jnp.where(pos, big, small)
                dzs.append(jnp.where(wins[r][1], dz, 0.0).astype(MX))
            parts = [_dot(dzs[r], qs[r], TN) for r in halves]
            for r in halves:
                dk_ref[pl.ds(wins[r][0], SB_WINDOW), :] += parts[r]
            new = []
            for r in halves:
                new += [st[2 * r] + jnp.sum(es[r], axis=1, keepdims=True),
                        st[2 * r + 1] + _dot(dzs[r], kws[r], NN)]
            return tuple(new)

        a0 = jnp.zeros((CHUNK, HEAD_PAIR), F32)
        st = lax.fori_loop(0, n_seen, up, (c0, a0) * len(halves))
        for r in halves:
            dq_ref[r * SB_ROWS:(r + 1) * SB_ROWS, :] = (_sb_unstack(st[2 * r + 1], lane_hi) * 0.125).astype(dq_ref.dtype)
        end_exchange()

    blk = pl.BlockSpec((qb, HEAD_PAIR), lambda p, i: (i, p))
    full = pl.BlockSpec((s, HEAD_PAIR), lambda p, i: (0, p))
    tri = pl.BlockSpec((SB_WINDOW, SB_WINDOW), lambda p, i: (0, 0))
    outs = pl.pallas_call(
        body, name="attn_bwd" if exchange is None else "attn_bwd_scatter", grid=(n_pairs, nblk),
        in_specs=[blk,
                  pl.BlockSpec((s, HEAD_PAIR), lambda p, i: (0, n_pairs + p)),
                  pl.BlockSpec((s, HEAD_PAIR), lambda p, i: (0, 2 * n_pairs + p)),
                  blk, tri, tri] + x_specs,
        out_specs=[blk, full, full] + x_specs,
        out_shape=[jax.ShapeDtypeStruct((s, SB_W), MX), jax.ShapeDtypeStruct((s, SB_W), F32),
                   jax.ShapeDtypeStruct((s, SB_W), F32)] + x_shapes,
        scratch_shapes=[pltpu.VMEM((qb // SB_ROWS, n_win, CHUNK, SB_WINDOW), F32)] + x_sems,
        compiler_params=_params(("arbitrary", "arbitrary"), 56 * 1024 * 1024),
    )(qkv, qkv, qkv, d_o, _sb_tri(True), _sb_tri(False), *x_arrays)
    return outs[0], outs[1], outs[2], dict(zip(BIG, outs[3:]))


POOL_TILE = 256
POOL_HALO = 128


def _pool_bands(tile, transpose):
    cur = np.zeros((N_GROUPS, tile, tile), np.float32)
    halo = np.zeros((N_GROUPS, tile, POOL_HALO), np.float32)
    t = np.arange(tile)[:, None]
    for g, w in enumerate(POOL_WINDOWS):
        if not transpose:
            p = np.arange(tile)[None, :]
            cur[g] = ((t - p >= 0) & (t - p < w))
            ph = np.arange(POOL_HALO)[None, :] - POOL_HALO
            halo[g] = (t - ph < w)
        else:
            p = np.arange(tile)[None, :]
            cur[g] = ((p - t >= 0) & (p - t < w))
            ph = np.arange(POOL_HALO)[None, :] + tile
            halo[g] = (ph - t < w)
    return jnp.asarray(cur, jnp.bfloat16), jnp.asarray(halo, jnp.bfloat16)


def _pool_count(i, tile, rows, row0):
    pos = (i * tile + row0 + lax.broadcasted_iota(jnp.int32, (rows, PW), 0)).astype(F32)
    grp = lax.broadcasted_iota(jnp.int32, (rows, PW), 1) // GROUP
    win = jnp.where(grp == 0, float(POOL_WINDOWS[0]),
                    jnp.where(grp == 1, float(POOL_WINDOWS[1]),
                              jnp.where(grp == 2, float(POOL_WINDOWS[2]), float(POOL_WINDOWS[3]))))
    return jnp.minimum(pos + 1.0, win), grp


def _pool_fwd(rest, wbd, scale):
    s = rest.shape[0]
    tile = min(POOL_TILE, s)
    halo_per_tile = tile // POOL_HALO
    bcur, bhalo = _pool_bands(tile, False)

    def body(cur_ref, prev_ref, bcur_ref, bhalo_ref, wbd_ref, scale_ref, pooled_ref, out_ref):
        i = pl.program_id(0)
        cur = cur_ref[...]
        prev = jnp.where(i > 0, prev_ref[...], 0.0)
        count, grp = _pool_count(i, tile, tile, 0)
        win = jnp.zeros((tile, PW), F32)
        for g in range(N_GROUPS):
            wsum = _split_dot(cur, bcur_ref[g], True, 3) + _split_dot(prev, bhalo_ref[g], True, 3)
            win = jnp.where(grp == g, wsum, win)
        pooled = (win / count - cur).astype(MX)
        pooled_ref[...] = pooled
        out_ref[...] = (_dot(pooled, wbd_ref[...], NN) * scale_ref[...]).astype(out_ref.dtype)

    tile_spec = pl.BlockSpec((tile, PW), lambda i: (i, 0))
    const3 = lambda shape: pl.BlockSpec(shape, lambda i: (0, 0, 0))
    const2 = lambda shape: pl.BlockSpec(shape, lambda i: (0, 0))
    return pl.pallas_call(
        body, name="pool_fwd", grid=(s // tile,),
        in_specs=[tile_spec,
                  pl.BlockSpec((POOL_HALO, PW), lambda i: (jnp.maximum(i * halo_per_tile - 1, 0), 0)),
                  const3(bcur.shape), const3(bhalo.shape), const2((PW, PW)), const2((1, PW))],
        out_specs=[tile_spec, tile_spec],
        out_shape=[jax.ShapeDtypeStruct((s, PW), MX), jax.ShapeDtypeStruct((s, PW), MX)],
        compiler_params=_params(("parallel",)),
    )(rest, rest, bcur, bhalo, wbd, scale)


def _pool_bwd(d_o, pooled, wbd, scale):
    s = d_o.shape[0]
    tile = min(POOL_TILE, s)
    halo_per_tile = tile // POOL_HALO
    n_halo = s // POOL_HALO
    n_tiles = s // tile
    ccur, chalo = _pool_bands(tile, True)

    def body(do_ref, nxt_ref, pooled_ref, ccur_ref, chalo_ref, wbd_ref, scale_ref, dp_ref, dw_ref, ds_ref):
        i = pl.program_id(0)
        do = do_ref[...]
        w = wbd_ref[...]
        sc = scale_ref[...]
        pooled_v = pooled_ref[...]

        @pl.when(i == 0)
        def _():
            dw_ref[...] = jnp.zeros_like(dw_ref)
            ds_ref[...] = jnp.zeros_like(ds_ref)

        ds_ref[...] += jnp.sum(do * _dot(pooled_v, w, NN), axis=0, keepdims=True)
        dmixed = (do * sc).astype(MX)
        dw_ref[...] += _dot(pooled_v, dmixed, TN)
        dpooled = _dot(dmixed, w, NT)
        nxt = jnp.where(i < n_tiles - 1, nxt_ref[...], 0.0)
        dpooled_n = _dot((nxt * sc).astype(MX), w, NT)
        count, grp = _pool_count(i, tile, tile, 0)
        count_n, _ = _pool_count(i, tile, POOL_HALO, tile)
        dq = dpooled / count
        dq_n = dpooled_n / count_n
        acc = jnp.zeros((tile, PW), F32)
        for g in range(N_GROUPS):
            wsum = _split_dot(dq, ccur_ref[g], True, 3) + _split_dot(dq_n, chalo_ref[g], True, 3)
            acc = jnp.where(grp == g, wsum, acc)
        dp_ref[...] = (acc - dpooled).astype(dp_ref.dtype)

    tile_spec = pl.BlockSpec((tile, PW), lambda i: (i, 0))
    const3 = lambda shape: pl.BlockSpec(shape, lambda i: (0, 0, 0))
    const2 = lambda shape: pl.BlockSpec(shape, lambda i: (0, 0))
    return pl.pallas_call(
        body, name="pool_bwd", grid=(n_tiles,),
        in_specs=[tile_spec,
                  pl.BlockSpec((POOL_HALO, PW), lambda i: (jnp.minimum((i + 1) * halo_per_tile, n_halo - 1), 0)),
                  tile_spec, const3(ccur.shape), const3(chalo.shape), const2((PW, PW)), const2((1, PW))],
        out_specs=[tile_spec, const2((PW, PW)), const2((1, PW))],
        out_shape=[jax.ShapeDtypeStruct((s, PW), MX), jax.ShapeDtypeStruct((PW, PW), F32),
                   jax.ShapeDtypeStruct((1, PW), F32)],
        compiler_params=_params(("arbitrary",)),
    )(d_o, d_o, pooled, ccur, chalo, wbd, scale)


GM_TILE = 512
GELU_C = 0.7978845608028654
GELU_A = 0.044715


def _gelu(x):
    return 0.5 * x * (1.0 + jnp.tanh(GELU_C * (x + GELU_A * (x * x * x))))


def _gelu_grad(x):
    th = jnp.tanh(GELU_C * (x + GELU_A * (x * x * x)))
    return 0.5 * (1.0 + th) + 0.5 * x * (1.0 - th * th) * (GELU_C * (1.0 + 3.0 * GELU_A * (x * x)))


def _gm_common(ws_ref):
    row, col = _sb_masks()
    tril = row >= col
    wsm = [jnp.where(tril, ws_ref[g], 0.0).astype(MX) for g in range(N_GROUPS)]
    grp = lax.broadcasted_iota(jnp.int32, (CHUNK, PW), 1) // GROUP
    return tril, wsm, grp


def _gm_mix(wsm, grp, vn_c):
    mixed = jnp.zeros((CHUNK, PW), F32)
    for g in range(N_GROUPS):
        mixed = jnp.where(grp == g, _dot(wsm[g], vn_c, NN), mixed)
    return mixed


def _gm_fwd(rest, gain, ws, bfull):
    s = rest.shape[0]
    tile = min(GM_TILE, s)

    def body(u_ref, v_ref, gain_ref, ws_ref, b_ref, out_ref):
        _, wsm, grp = _gm_common(ws_ref)
        bias = b_ref[...]
        for n in range(tile // CHUNK):
            rows = slice(n * CHUNK, (n + 1) * CHUNK)
            gu = _gelu(u_ref[rows, :])
            gv = _gelu(v_ref[rows, :])
            r = lax.rsqrt(jnp.mean(gv * gv, axis=-1, keepdims=True) + RMS_EPS)
            vn = (gv * r * gain_ref[...]).astype(MX)
            out_ref[rows, :] = (gu * (_gm_mix(wsm, grp, vn) + bias)).astype(out_ref.dtype)

    const2 = lambda shape: pl.BlockSpec(shape, lambda i: (0, 0))
    return pl.pallas_call(
        body, name="gm_fwd", grid=(s // tile,),
        in_specs=[pl.BlockSpec((tile, PW), lambda i: (i, 1)), pl.BlockSpec((tile, PW), lambda i: (i, 2)),
                  const2((1, PW)), pl.BlockSpec((N_GROUPS, CHUNK, CHUNK), lambda i: (0, 0, 0)),
                  const2((CHUNK, PW))],
        out_specs=pl.BlockSpec((tile, PW), lambda i: (i, 0)),
        out_shape=jax.ShapeDtypeStruct((s, PW), MX),
        compiler_params=_params(("parallel",)),
    )(rest, rest, gain, ws, bfull)


def _gm_bwd(d_o, rest, gain, ws, bfull):
    s = rest.shape[0]
    tile = min(GM_TILE, s)

    def body(do_ref, u_ref, v_ref, gain_ref, ws_ref, b_ref, du_ref, dv_ref, dws_ref, db_ref, dgain_ref):
        i = pl.program_id(0)
        tril, wsm, grp = _gm_common(ws_ref)
        bias = b_ref[...]
        gain_v = gain_ref[...]

        @pl.when(i == 0)
        def _():
            dws_ref[...] = jnp.zeros_like(dws_ref)
            db_ref[...] = jnp.zeros_like(db_ref)
            dgain_ref[...] = jnp.zeros_like(dgain_ref)

        for n in range(tile // CHUNK):
            rows = slice(n * CHUNK, (n + 1) * CHUNK)
            u = u_ref[rows, :]
            v = v_ref[rows, :]
            do = do_ref[rows, :]
            gu = _gelu(u)
            gv = _gelu(v)
            r = lax.rsqrt(jnp.mean(gv * gv, axis=-1, keepdims=True) + RMS_EPS)
            vhat = gv * r
            vn = (vhat * gain_v).astype(MX)
            mixed = _gm_mix(wsm, grp, vn)
            du_ref[rows, :] = (do * (mixed + bias) * _gelu_grad(u)).astype(du_ref.dtype)
            dmix = do * gu
            db_ref[...] += dmix
            dmix_mx = dmix.astype(MX)
            dvn = jnp.zeros((CHUNK, PW), F32)
            for g in range(N_GROUPS):
                dmg = jnp.where(grp == g, dmix_mx, jnp.zeros_like(dmix_mx))
                dws_ref[g] += jnp.where(tril, _dot(dmg, vn, NT), 0.0)
                dvn = jnp.where(grp == g, _dot(wsm[g], dmix_mx, TN), dvn)
            dgain_ref[...] += jnp.sum(dvn * vhat, axis=0, keepdims=True)
            dvhat = dvn * gain_v
            dgv = r * (dvhat - vhat * jnp.mean(dvhat * vhat, axis=-1, keepdims=True))
            dv_ref[rows, :] = (dgv * _gelu_grad(v)).astype(dv_ref.dtype)

    const2 = lambda shape: pl.BlockSpec(shape, lambda i: (0, 0))
    ws_spec = pl.BlockSpec((N_GROUPS, CHUNK, CHUNK), lambda i: (0, 0, 0))
    tile_spec = pl.BlockSpec((tile, PW), lambda i: (i, 0))
    return pl.pallas_call(
        body, name="gm_bwd", grid=(s // tile,),
        in_specs=[tile_spec, pl.BlockSpec((tile, PW), lambda i: (i, 1)), pl.BlockSpec((tile, PW), lambda i: (i, 2)),
                  const2((1, PW)), ws_spec, const2((CHUNK, PW))],
        out_specs=[tile_spec, tile_spec, ws_spec, const2((CHUNK, PW)), const2((1, PW))],
        out_shape=[jax.ShapeDtypeStruct((s, PW), MX), jax.ShapeDtypeStruct((s, PW), MX),
                   jax.ShapeDtypeStruct((N_GROUPS, CHUNK, CHUNK), F32),
                   jax.ShapeDtypeStruct((CHUNK, PW), F32), jax.ShapeDtypeStruct((1, PW), F32)],
        compiler_params=_params(("arbitrary",)),
    )(d_o, rest, rest, gain, ws, bfull)


GATE_TM = 2048
GATE_TN = 256


def _gate_specs(s):
    tm = min(GATE_TM, s)
    tn = GATE_TN
    gate0 = GATE_COL0 // tn
    per = D // tn
    ins = [pl.BlockSpec((tm, SB_W), lambda i, j: (i, 0)),
           pl.BlockSpec((tm, PW), lambda i, j: (i, 0)),
           pl.BlockSpec((tm, PW), lambda i, j: (i, 0)),
           pl.BlockSpec((SB_W, tn), lambda i, j: (0, j)),
           pl.BlockSpec((PW, tn), lambda i, j: (0, j)),
           pl.BlockSpec((PW, tn), lambda i, j: (0, j)),
           pl.BlockSpec((tm, tn), lambda i, j: (i, gate0 + j)),
           pl.BlockSpec((tm, tn), lambda i, j: (i, gate0 + per + j)),
           pl.BlockSpec((tm, tn), lambda i, j: (i, gate0 + 2 * per + j))]
    return tm, tn, ins, pl.BlockSpec((tm, tn), lambda i, j: (i, j))


def _gate_fwd(o_sb, o_pool, o_gm, w_sb, w_pool, w_gm, rest):
    s = rest.shape[0]
    tm, tn, ins, out = _gate_specs(s)

    def body(o0, o1, o2, w0, w1, w2, g0, g1, g2, merged_ref):
        acc = jax.nn.sigmoid(g0[...]) * _dot(o0[...], w0[...], NN)
        acc += jax.nn.sigmoid(g1[...]) * _dot(o1[...], w1[...], NN)
        acc += jax.nn.sigmoid(g2[...]) * _dot(o2[...], w2[...], NN)
        merged_ref[...] = acc.astype(merged_ref.dtype)

    return pl.pallas_call(
        body, name="gate_fwd", grid=(s // tm, D // tn),
        in_specs=ins, out_specs=out, out_shape=jax.ShapeDtypeStruct((s, D), MX),
        compiler_params=_params(("parallel", "parallel")),
    )(o_sb, o_pool, o_gm, w_sb, w_pool, w_gm, rest, rest, rest)


def _gate_bwd(o_sb, o_pool, o_gm, w_sb, w_pool, w_gm, rest, dmerged):
    s = rest.shape[0]
    tm, tn, ins, out = _gate_specs(s)

    def body(o0, o1, o2, w0, w1, w2, g0, g1, g2, dm_ref, db0, db1, db2, dg0, dg1, dg2):
        dm = dm_ref[...]
        for o, w, g, db, dg in ((o0, w0, g0, db0, dg0), (o1, w1, g1, db1, dg1), (o2, w2, g2, db2, dg2)):
            sg = jax.nn.sigmoid(g[...])
            db[...] = (dm * sg).astype(db.dtype)
            dg[...] = (dm * _dot(o[...], w[...], NN) * (sg * (1.0 - sg))).astype(dg.dtype)

    return pl.pallas_call(
        body, name="gate_bwd", grid=(s // tm, D // tn),
        in_specs=ins + [out], out_specs=[out] * 6,
        out_shape=[jax.ShapeDtypeStruct((s, D), MX)] * 6,
        compiler_params=_params(("parallel", "parallel")),
    )(o_sb, o_pool, o_gm, w_sb, w_pool, w_gm, rest, rest, rest, dmerged)


def _relu2(acc):
    r = jnp.maximum(acc, 0.0)
    return r * r, 2.0 * r


def _relu2_bwd(acc, slope):
    return (acc * slope.astype(F32),)


TILES = {
    "proj_qkv": (2048, W_IN_TILE, 1024), "proj_rest": (2048, W_IN_TILE, 1024),
    "out_proj": (2048, 1024, 1024), "ff_in": (2048, 1024, 1024), "ff_out": (1024, 1024, 4096),
    "ff_out_dx": (2048, 1024, 1024), "ff_out_dw": (1024, 1024, 2048),
    "ff_in_dx": (1024, 1024, 4096), "ff_in_dw": (1024, 2048, 1024),
    "out_proj_dx": (2048, 1024, 1024), "out_proj_dw": (1024, 1024, 2048),
    "br_sb_dx": (2048, 512, 1024), "br_sb_dw": (512, 1024, 2048),
    "br_pool_dx": (2048, 256, 1024), "br_pool_dw": (256, 1024, 2048),
    "br_gm_dx": (2048, 256, 1024), "br_gm_dw": (256, 1024, 2048),
    "proj_dx": (512, 1024, 5376), "proj_dw": (1024, W_IN_TILE, 2048),
}


def _mm(name, a, b, mode, out_dtypes, **kw):
    tm, tn, tk = TILES[name]
    return _matmul(a, b, mode=mode, name=name, tm=tm, tn=tn, tk=tk, out_dtypes=out_dtypes, **kw)


def _layer_fwd(x, w, gather=None):
    h = _rms_fwd(x, w["g_mix_pre"], res=None, out_dtype=MX, name="rms_mix_pre")
    qkv = _mm("proj_qkv", h, w["w_in"], "nn", (MX,), n_out=QKV_W)
    rest = _mm("proj_rest", h, w["w_in"], "nn", (F32,), b_col0=QKV_W, n_out=REST_W)
    o_sb, gathered = _attn_fwd(qkv, gather)
    pooled, o_pool = _pool_fwd(rest, w["wbd"], w["pool_scale"])
    o_gm = _gm_fwd(rest, w["gm_gain"], w["w_spatial"], w["bfull"])
    merged = _gate_fwd(o_sb, o_pool, o_gm, w["w_br_sb"], w["w_br_pool"], w["w_br_gm"], rest)
    y = _mm("out_proj", merged, w["w_out"], "nn", (F32,))
    x1 = _rms_fwd(y, w["g_mix_post"], res=x, out_dtype=F32, name="rms_mix_post")
    h2 = _rms_fwd(x1, w["g_ff_pre"], res=None, out_dtype=MX, name="rms_ff_pre")
    r, r_slope = _mm("ff_in", h2, w["w_ff_in"], "nn", (MX, MX), epilogue=_relu2)
    ff = _mm("ff_out", r, w["w_ff_out"], "nn", (F32,))
    x2 = _rms_fwd(ff, w["g_ff_post"], res=x1, out_dtype=F32, name="rms_ff_post")
    saved = dict(x=x, h=h, qkv=qkv, rest=rest, o_sb=o_sb, pooled=pooled, o_pool=o_pool, o_gm=o_gm,
                 merged=merged, y=y, x1=x1, h2=h2, r=r, r_slope=r_slope, ff=ff)
    return x2, saved, gathered


def _layer_bwd(dx2, w, sv, scatter=None):
    dff, dg_ff_post = _rms_bwd(sv["ff"], w["g_ff_post"], dx2, res=None, out_dtype=MX, name="rms_ff_post_bwd")
    da = _mm("ff_out_dx", dff, w["w_ff_out"], "nt", (MX,), epilogue=_relu2_bwd, extras=(sv["r_slope"],))
    dw_ff_out = _mm("ff_out_dw", sv["r"], dff, "tn", (MX,))
    dh2 = _mm("ff_in_dx", da, w["w_ff_in"], "nt", (F32,))
    dw_ff_in = _mm("ff_in_dw", sv["h2"], da, "tn", (MX,))
    dx1, dg_ff_pre = _rms_bwd(sv["x1"], w["g_ff_pre"], dh2, res=dx2, out_dtype=F32, name="rms_ff_pre_bwd")

    dy, dg_mix_post = _rms_bwd(sv["y"], w["g_mix_post"], dx1, res=None, out_dtype=MX, name="rms_mix_post_bwd")
    dmerged = _mm("out_proj_dx", dy, w["w_out"], "nt", (F32,))
    dw_out = _mm("out_proj_dw", sv["merged"], dy, "tn", (MX,))
    db_sb, db_pool, db_gm, dg0, dg1, dg2 = _gate_bwd(
        sv["o_sb"], sv["o_pool"], sv["o_gm"], w["w_br_sb"], w["w_br_pool"], w["w_br_gm"], sv["rest"], dmerged)
    do_sb = _mm("br_sb_dx", db_sb, w["w_br_sb"], "nt", (MX,))
    dw_br_sb = _mm("br_sb_dw", sv["o_sb"], db_sb, "tn", (MX,))
    do_pool = _mm("br_pool_dx", db_pool, w["w_br_pool"], "nt", (F32,))
    dw_br_pool = _mm("br_pool_dw", sv["o_pool"], db_pool, "tn", (MX,))
    do_gm = _mm("br_gm_dx", db_gm, w["w_br_gm"], "nt", (F32,))
    dw_br_gm = _mm("br_gm_dw", sv["o_gm"], db_gm, "tn", (MX,))

    du, dv_gm, dws, dbfull, dgain = _gm_bwd(do_gm, sv["rest"], w["gm_gain"], w["w_spatial"], w["bfull"])
    dp, dwbd, dscale = _pool_bwd(do_pool, sv["pooled"], w["wbd"], w["pool_scale"])
    dq, dk, dv, scattered = _attn_bwd(sv["qkv"], do_sb, scatter)
    dproj = jnp.concatenate([dq, dk.astype(MX), dv.astype(MX), dp, du, dv_gm, dg0, dg1, dg2], axis=1)
    dh = _mm("proj_dx", dproj, w["w_in"], "nt", (F32,))
    dw_in = _mm("proj_dw", sv["h"], dproj, "tn", (MX,))
    dx, dg_mix_pre = _rms_bwd(sv["x"], w["g_mix_pre"], dh, res=dx1, out_dtype=F32, name="rms_mix_pre_bwd")

    grads = dict(
        w_in=dw_in, w_br_sb=dw_br_sb, w_br_pool=dw_br_pool, w_br_gm=dw_br_gm, w_out=dw_out,
        w_ff_in=dw_ff_in, w_ff_out=dw_ff_out,
        w_pool=jnp.stack([dwbd[g * GROUP:(g + 1) * GROUP, g * GROUP:(g + 1) * GROUP] for g in range(N_GROUPS)]),
        pool_scale=dscale[0], gm_gain=dgain[0], w_spatial=dws,
        b_spatial=dbfull.reshape(CHUNK, N_GROUPS, GROUP).sum(axis=-1).T,
        g_mix_pre=dg_mix_pre[0], g_mix_post=dg_mix_post[0], g_ff_pre=dg_ff_pre[0], g_ff_post=dg_ff_post[0])
    return dx, grads, scattered


def _layer_operands(full_l, small_l):
    wbd = jnp.zeros((PW, PW), F32)
    for g in range(N_GROUPS):
        wbd = wbd.at[g * GROUP:(g + 1) * GROUP, g * GROUP:(g + 1) * GROUP].set(small_l["w_pool"][g])
    w = dict(full_l)
    w["wbd"] = wbd.astype(MX)
    w["bfull"] = jnp.repeat(small_l["b_spatial"].T, GROUP, axis=1)
    w["w_spatial"] = small_l["w_spatial"]
    for n in ("pool_scale", "gm_gain", "g_mix_pre", "g_mix_post", "g_ff_pre", "g_ff_post"):
        w[n] = small_l[n][None, :]
    return w


def _local_step(x, target, layers):
    saved = []
    for w in layers:
        x, sv, _ = _layer_fwd(x, w)
        saved.append(sv)
    dx, sq = _loss_head(x, target)
    loss = 0.5 * jnp.sum(sq) / x.shape[1]
    grads = [None] * len(layers)
    for l in reversed(range(len(layers))):
        dx, grads[l], _ = _layer_bwd(dx, layers[l], saved[l])
    return loss, dx, grads


def _mesh_place():
    x, y, c = lax.axis_index("x"), lax.axis_index("y"), lax.axis_index("c")

    def peer(k):
        px = 1 - x if k & 4 else x
        py = 1 - y if k & 2 else y
        pc = 1 - c if k & 1 else c
        return (px, py, pc), 4 * px + 2 * py + pc

    return 4 * x + 2 * y + c, peer


def _shard_window(ref, name, shard_shape, idx):
    if name == "w_in":
        return ref.at[idx]
    a, b = shard_shape
    if BIG_SHARD_AXIS[name] == 2:
        return ref.at[:, pl.ds(pl.multiple_of(idx * b, b), b)]
    return ref.at[pl.ds(pl.multiple_of(idx * a, a), a), :]


def _full_shape(name, shard_shape):
    a, b = shard_shape
    if name == "w_in":
        return (N_DEV, a, b)
    return (a, N_DEV * b) if BIG_SHARD_AXIS[name] == 2 else (N_DEV * a, b)


def _exchange_out_shapes(shard_shapes, scatter):
    if scatter:
        return [(N_DEV,) + tuple(shard_shapes[nm]) for nm in BIG]
    return [_full_shape(nm, shard_shapes[nm]) for nm in BIG]


def _exchange_sems():
    n = len(BIG)
    return [pltpu.SemaphoreType.DMA((n, N_DEV - 1)), pltpu.SemaphoreType.DMA((n, N_DEV - 1)),
            pltpu.SemaphoreType.DMA((n,))]


def _exchange_plan(ins, outs, sems, shard_shapes, scatter):
    send_sems, recv_sems, local_sems = sems
    me, peer = _mesh_place()
    n = len(BIG)

    def window(ref, p, idx):
        return _shard_window(ref, BIG[p], shard_shapes[BIG[p]], idx)

    def copy(p, k, landing):
        dev, idx = peer(k)
        if scatter:
            src, dst = window(ins[p], p, idx), outs[p].at[idx if landing else me]
        else:
            src, dst = ins[p], window(outs[p], p, idx if landing else me)
        return pltpu.make_async_remote_copy(
            src_ref=src, dst_ref=dst, send_sem=send_sems.at[p, k - 1], recv_sem=recv_sems.at[p, k - 1],
            device_id=dev, device_id_type=pl.DeviceIdType.MESH)

    def local(p):
        if scatter:
            return pltpu.make_async_copy(window(ins[p], p, me), outs[p].at[me], local_sems.at[p])
        return pltpu.make_async_copy(ins[p], window(outs[p], p, me), local_sems.at[p])

    pairs = [(p, k) for p in range(n) for k in range(1, N_DEV)]

    def start():
        for p in range(n):
            local(p).start()
        for p, k in pairs:
            copy(p, k, False).start()

    def finish():
        for p, k in pairs:
            copy(p, k, True).wait_recv()
        for p, k in pairs:
            copy(p, k, False).wait_send()
        for p in range(n):
            local(p).wait()

    return start, finish


def _exchange_big(arrays, shard_shapes, *, scatter, name):
    n = len(BIG)

    def body(*refs):
        start, finish = _exchange_plan(refs[:n], refs[n:2 * n], refs[2 * n:], shard_shapes, scatter)
        start()
        finish()

    outs = pl.pallas_call(
        body, name=name,
        in_specs=[pl.BlockSpec(memory_space=pl.ANY)] * n,
        out_specs=[pl.BlockSpec(memory_space=pl.ANY)] * n,
        out_shape=[jax.ShapeDtypeStruct(s, a.dtype)
                   for s, a in zip(_exchange_out_shapes(shard_shapes, scatter), arrays)],
        scratch_shapes=_exchange_sems(),
    )(*arrays)
    return dict(zip(BIG, outs))


def _gather_rows(inp, *, name):
    def body(in_ref, out_ref, send_sems, recv_sems, local_sem):
        me, peer = _mesh_place()

        def copy(k, landing):
            dev, idx = peer(k)
            return pltpu.make_async_remote_copy(
                src_ref=in_ref, dst_ref=out_ref.at[idx if landing else me],
                send_sem=send_sems.at[k - 1], recv_sem=recv_sems.at[k - 1],
                device_id=dev, device_id_type=pl.DeviceIdType.MESH)

        mine = pltpu.make_async_copy(in_ref, out_ref.at[me], local_sem)
        mine.start()
        sends = [copy(k, False) for k in range(1, N_DEV)]
        for cp in sends:
            cp.start()
        for k in range(1, N_DEV):
            copy(k, True).wait_recv()
        for cp in sends:
            cp.wait_send()
        mine.wait()

    return pl.pallas_call(
        body, name=name,
        in_specs=[pl.BlockSpec(memory_space=pl.ANY)],
        out_specs=pl.BlockSpec(memory_space=pl.ANY),
        out_shape=jax.ShapeDtypeStruct((N_DEV,) + inp.shape, inp.dtype),
        scratch_shapes=[pltpu.SemaphoreType.DMA((N_DEV - 1,)), pltpu.SemaphoreType.DMA((N_DEV - 1,)),
                        pltpu.SemaphoreType.DMA],
    )(inp)


def _sum_devices(recv, name):
    _, rows, cols = recv.shape
    tr = rows
    for cand in (512, 256, 128, 64, 32, 16, 8):
        if rows % cand == 0:
            tr = cand
            break

    def body(r_ref, out_ref):
        acc = r_ref[0].astype(F32)
        for d in range(1, N_DEV):
            acc = acc + r_ref[d].astype(F32)
        out_ref[...] = acc

    return pl.pallas_call(
        body, name=name, grid=(rows // tr,),
        in_specs=[pl.BlockSpec((N_DEV, tr, cols), lambda i: (0, i, 0))],
        out_specs=pl.BlockSpec((tr, cols), lambda i: (i, 0)),
        out_shape=jax.ShapeDtypeStruct((rows, cols), F32),
        compiler_params=_params(("parallel",)),
    )(recv)


def _adamw(w, g, m, v, name):
    rows, cols = w.shape
    tr = rows
    for cand in (512, 256, 128, 64, 32, 16, 8):
        if rows % cand == 0:
            tr = cand
            break
    c1 = 1.0 - ADAM_B1 ** ADAM_STEP
    c2 = 1.0 - ADAM_B2 ** ADAM_STEP

    def body(w_ref, g_ref, m_ref, v_ref, d_ref, nm_ref, nv_ref):
        gv = g_ref[...]
        nm = ADAM_B1 * m_ref[...] + (1.0 - ADAM_B1) * gv
        nv = ADAM_B2 * v_ref[...] + (1.0 - ADAM_B2) * (gv * gv)
        nm_ref[...] = nm
        nv_ref[...] = nv
        d_ref[...] = -ADAM_LR * ((nm / c1) / (jnp.sqrt(nv / c2) + ADAM_EPS) + ADAM_WD * w_ref[...])

    spec = pl.BlockSpec((tr, cols), lambda i: (i, 0))
    return pl.pallas_call(
        body, name=name, grid=(rows // tr,),
        in_specs=[spec] * 4, out_specs=[spec] * 3,
        out_shape=[jax.ShapeDtypeStruct((rows, cols), F32)] * 3,
        compiler_params=_params(("parallel",)),
    )(w, g, m, v)


def _pack_rows(flat, dtype):
    n = flat.shape[-1]
    rows = -(-n // PACK_COLS)
    rows = -(-rows // 16) * 16
    pad = rows * PACK_COLS - n
    flat = jnp.pad(flat, [(0, 0)] * (flat.ndim - 1) + [(0, pad)])
    return flat.reshape(flat.shape[:-1] + (rows, PACK_COLS)).astype(dtype)


def kernel(x, w_in, w_pool, pool_scale, gm_gain, w_spatial, b_spatial, w_br_sb, w_br_pool, w_br_gm, w_out, g_mix_pre, g_mix_post, g_ff_pre, g_ff_post, w_ff_in, w_ff_out, loss_target, m_w_in, m_w_pool, m_pool_scale, m_gm_gain, m_w_spatial, m_b_spatial, m_w_br_sb, m_w_br_pool, m_w_br_gm, m_w_out, m_g_mix_pre, m_g_mix_post, m_g_ff_pre, m_g_ff_post, m_w_ff_in, m_w_ff_out, v_w_in, v_w_pool, v_pool_scale, v_gm_gain, v_w_spatial, v_b_spatial, v_w_br_sb, v_w_br_pool, v_w_br_gm, v_w_out, v_g_mix_pre, v_g_mix_post, v_g_ff_pre, v_g_ff_post, v_w_ff_in, v_w_ff_out):
    args = dict(locals())
    weights = {n: args[n] for n in WEIGHTS}
    mom_m = {n: args["m_" + n] for n in WEIGHTS}
    mom_v = {n: args["v_" + n] for n in WEIGHTS}

    shard_shapes = {n: weights[n].shape[1:] for n in BIG}
    depth, d_model, w_in_cols = weights["w_in"].shape

    def gather_of(l):
        return [weights[n][l].astype(MX) for n in BIG], shard_shapes, False

    def as_operands(full, l):
        full = dict(full)
        full["w_in"] = jnp.transpose(full["w_in"], (1, 0, 2)).reshape(d_model, N_DEV * w_in_cols)
        return _layer_operands(full, {n: weights[n][l] for n in SMALL})

    full = _exchange_big(gather_of(0)[0], shard_shapes, scatter=False, name="gather_weights")
    xc = x[0]
    layers, saved = [], []
    for l in range(depth):
        layers.append(as_operands(full, l))
        xc, sv, full = _layer_fwd(xc, layers[l], gather_of(l + 1) if l + 1 < depth else None)
        saved.append(sv)
    dx, sq = _loss_head(xc, loss_target[0])
    loss = lax.psum(0.5 * jnp.sum(sq) / d_model, ("x", "y", "c"))

    def scatter_of(g):
        g = dict(g)
        g["w_in"] = jnp.transpose(g["w_in"].reshape(d_model, N_DEV, w_in_cols), (1, 0, 2))
        return [g[n] for n in BIG], shard_shapes, True

    layer_grads, recv = [None] * depth, [None] * depth
    for l in reversed(range(depth)):
        riding = scatter_of(layer_grads[l + 1]) if l + 1 < depth else None
        dx, layer_grads[l], arrived = _layer_bwd(dx, layers[l], saved[l], riding)
        if riding is not None:
            recv[l + 1] = arrived
    recv[0] = _exchange_big(scatter_of(layer_grads[0])[0], shard_shapes, scatter=True, name="scatter_grads")
    grad_w = {n: jnp.stack([_sum_devices(recv[l][n], "sum_grads_" + n) for l in range(depth)]) for n in BIG}
    small_sizes = [int(np.prod(weights[n].shape)) for n in SMALL]
    small_off = np.concatenate([[0], np.cumsum(small_sizes)])
    small_flat = jnp.concatenate([jnp.stack([g[n] for g in layer_grads]).reshape(-1) for n in SMALL])
    g_small = _sum_devices(_gather_rows(_pack_rows(small_flat, F32), name="gather_small_grads"),
                           "sum_small_grads").reshape(-1)
    for n, lo, hi in zip(SMALL, small_off[:-1], small_off[1:]):
        grad_w[n] = g_small[lo:hi].reshape(weights[n].shape)

    delta, new_m, new_v = {}, {}, {}
    for n in WEIGHTS:
        shape = weights[n].shape
        view = (-1, shape[-1])
        d, nm, nv = _adamw(weights[n].reshape(view), grad_w[n].reshape(view), mom_m[n].reshape(view),
                           mom_v[n].reshape(view), "adamw_" + n)
        delta[n], new_m[n], new_v[n] = d.reshape(shape), nm.reshape(shape), nv.reshape(shape)

    return (loss, dx[None], *[grad_w[n] for n in WEIGHTS], *[delta[n] for n in WEIGHTS],
            *[new_m[n] for n in WEIGHTS], *[new_v[n] for n in WEIGHTS])
```

```python
import functools

import numpy as np
import jax
import jax.numpy as jnp
from jax import lax
from jax.experimental import pallas as pl
from jax.experimental.pallas import tpu as pltpu

F32 = jnp.float32
MX = jnp.bfloat16

D = 1024
SB_W = 512
HEAD_PAIR = 128
PW = 256
GROUP = 64
N_GROUPS = 4
CHUNK = 128
POOL_WINDOWS = (2, 4, 8, 16)
D_FF = 4096
D_IN = 5376
QKV_W = 3 * SB_W
REST_W = D_IN - QKV_W
GATE_COL0 = 3 * PW
RMS_EPS = 1e-6
N_DEV = 8
DEPTH = 4

ADAM_LR = 0.001
ADAM_B1 = 0.9
ADAM_B2 = 0.999
ADAM_EPS = 1e-08
ADAM_WD = 0.01
ADAM_STEP = 10

SKIP_LOG = -120.0

VMEM_LIMIT = 48 * 1024 * 1024
W_IN_TILE = 768
PACK_COLS = 1024

BIG = ("w_in", "w_br_sb", "w_br_pool", "w_br_gm", "w_out", "w_ff_in", "w_ff_out")
BIG_SHARD_AXIS = {"w_in": 2, "w_br_sb": 2, "w_br_pool": 2, "w_br_gm": 2, "w_out": 1, "w_ff_in": 2, "w_ff_out": 1}
SMALL = ("w_pool", "pool_scale", "gm_gain", "w_spatial", "b_spatial",
         "g_mix_pre", "g_mix_post", "g_ff_pre", "g_ff_post")
WEIGHTS = ("w_in", "w_pool", "pool_scale", "gm_gain", "w_spatial", "b_spatial", "w_br_sb", "w_br_pool",
           "w_br_gm", "w_out", "g_mix_pre", "g_mix_post", "g_ff_pre", "g_ff_post", "w_ff_in", "w_ff_out")


def _params(sem, vmem=VMEM_LIMIT):
    return pltpu.CompilerParams(dimension_semantics=sem, vmem_limit_bytes=vmem)


def _dot(a, b, dims):
    return lax.dot_general(a, b, (dims, ((), ())), preferred_element_type=F32)


NN = ((1,), (0,))
NT = ((1,), (1,))
TN = ((0,), (0,))


def _split_dot(x, m, left, nsplit):
    acc = None
    r = x
    for s in range(nsplit):
        p = r.astype(jnp.bfloat16)
        d = _dot(m, p, NN) if left else _dot(p, m, NN)
        acc = d if acc is None else acc + d
        if s + 1 < nsplit:
            r = r - p.astype(F32)
    return acc


def _matmul(a, b, *, mode, name, tm, tn, tk, out_dtypes, b_col0=0, n_out=None, epilogue=None, extras=(),
            exchange=None):
    if mode == "nn":
        m, kdim = a.shape
        n = b.shape[1] if n_out is None else n_out
        dims = NN
    elif mode == "nt":
        m, kdim = a.shape
        n = b.shape[0]
        dims = NT
    else:
        kdim, m = a.shape
        n = b.shape[1]
        dims = TN
    tm, tn, tk = min(tm, m), min(tn, n), min(tk, kdim)
    assert m % tm == 0 and n % tn == 0 and kdim % tk == 0 and b_col0 % tn == 0, (name, m, n, kdim)
    if mode == "nn":
        a_spec = pl.BlockSpec((tm, tk), lambda i, j, k: (i, k))
        b_spec = pl.BlockSpec((tk, tn), lambda i, j, k: (k, j + b_col0 // tn))
    elif mode == "nt":
        a_spec = pl.BlockSpec((tm, tk), lambda i, j, k: (i, k))
        b_spec = pl.BlockSpec((tn, tk), lambda i, j, k: (j, k))
    else:
        a_spec = pl.BlockSpec((tk, tm), lambda i, j, k: (k, i))
        b_spec = pl.BlockSpec((tk, tn), lambda i, j, k: (k, j))
    nk = kdim // tk
    n_extra = len(extras)
    n_outs = len(out_dtypes)
    o_spec = pl.BlockSpec((tm, tn), lambda i, j, k: (i, j))
    x_arrays, x_shapes, x_specs, x_sems = _riding_exchange(exchange)
    nx = len(x_arrays)
    grid = (m // tm, n // tn, nk)

    def body(a_ref, b_ref, *refs):
        extra_refs = refs[:n_extra]
        out_refs = refs[n_extra + nx:n_extra + nx + n_outs]
        scratch = refs[n_extra + 2 * nx + n_outs:]
        ids = [pl.program_id(d) for d in range(3)]
        k = ids[2]
        end_exchange = _ride(exchange, refs[n_extra:n_extra + nx], refs[n_extra + nx + n_outs:n_extra + 2 * nx + n_outs],
                             scratch[1:] if nk > 1 else scratch,
                             functools.reduce(jnp.logical_and, [ids[d] == 0 for d in range(3)]),
                             functools.reduce(jnp.logical_and, [ids[d] == grid[d] - 1 for d in range(3)]))
        part = _dot(a_ref[...].astype(MX), b_ref[...].astype(MX), dims)

        def finish(acc):
            outs = (acc,) if epilogue is None else epilogue(acc, *[e[...] for e in extra_refs])
            for o_ref, val in zip(out_refs, outs):
                o_ref[...] = val.astype(o_ref.dtype)

        if nk == 1:
            finish(part)
        else:
            acc_ref = scratch[0]

            @pl.when(k == 0)
            def _():
                acc_ref[...] = part

            @pl.when(jnp.logical_and(k > 0, k < nk - 1))
            def _():
                acc_ref[...] += part

            @pl.when(k == nk - 1)
            def _():
                finish(acc_ref[...] + part)

        end_exchange()

    outs = pl.pallas_call(
        body, name=name if exchange is None else name + "_ride",
        grid=grid,
        in_specs=[a_spec, b_spec] + [o_spec] * n_extra + x_specs,
        out_specs=[o_spec] * n_outs + x_specs,
        out_shape=[jax.ShapeDtypeStruct((m, n), dt) for dt in out_dtypes] + x_shapes,
        scratch_shapes=([pltpu.VMEM((tm, tn), F32)] if nk > 1 else []) + x_sems,
        compiler_params=_params(("parallel", "parallel", "arbitrary") if exchange is None
                                else ("arbitrary", "arbitrary", "arbitrary")),
    )(a, b, *extras, *x_arrays)
    if exchange is not None:
        return tuple(outs[:n_outs]) + (dict(zip(exchange[3], outs[n_outs:])),)
    return outs[0] if n_outs == 1 else outs


ROW_TILE = 1024


def _rms_fwd(x, g, *, res, out_dtype, name):
    s, d = x.shape
    tr = min(ROW_TILE, s)
    has_res = res is not None

    def body(x_ref, g_ref, *refs):
        out_ref = refs[-1]
        xv = x_ref[...]
        y = xv * lax.rsqrt(jnp.mean(xv * xv, axis=-1, keepdims=True) + RMS_EPS) * g_ref[...]
        if has_res:
            y = refs[0][...] + y
        out_ref[...] = y.astype(out_ref.dtype)

    row = pl.BlockSpec((tr, d), lambda i: (i, 0))
    vec = pl.BlockSpec((1, d), lambda i: (0, 0))
    return pl.pallas_call(
        body, name=name, grid=(s // tr,),
        in_specs=[row, vec] + ([row] if has_res else []),
        out_specs=row, out_shape=jax.ShapeDtypeStruct((s, d), out_dtype),
        compiler_params=_params(("parallel",)),
    )(x, g, *([res] if has_res else []))


def _rms_bwd(x, g, dout, *, res, out_dtype, name):
    s, d = x.shape
    tr = min(ROW_TILE, s)
    has_res = res is not None

    def body(x_ref, g_ref, do_ref, *refs):
        dx_ref, dg_ref = refs[-2], refs[-1]
        i = pl.program_id(0)
        xv = x_ref[...]
        do = do_ref[...]
        r = lax.rsqrt(jnp.mean(xv * xv, axis=-1, keepdims=True) + RMS_EPS)
        xhat = xv * r
        dxhat = do * g_ref[...]
        dx = r * (dxhat - xhat * jnp.mean(dxhat * xhat, axis=-1, keepdims=True))
        if has_res:
            dx = refs[0][...] + dx
        dx_ref[...] = dx.astype(dx_ref.dtype)

        @pl.when(i == 0)
        def _():
            dg_ref[...] = jnp.zeros_like(dg_ref)

        dg_ref[...] += jnp.sum(do * xhat, axis=0, keepdims=True)

    row = pl.BlockSpec((tr, d), lambda i: (i, 0))
    vec = pl.BlockSpec((1, d), lambda i: (0, 0))
    return pl.pallas_call(
        body, name=name, grid=(s // tr,),
        in_specs=[row, vec, row] + ([row] if has_res else []),
        out_specs=[row, vec],
        out_shape=[jax.ShapeDtypeStruct((s, d), out_dtype), jax.ShapeDtypeStruct((1, d), F32)],
        compiler_params=_params(("arbitrary",)),
    )(x, g, dout, *([res] if has_res else []))


def _loss_head(y, target):
    s, d = y.shape
    tr = min(ROW_TILE, s)

    def body(y_ref, t_ref, dy_ref, sq_ref):
        i = pl.program_id(0)
        err = y_ref[...] - t_ref[...]
        dy_ref[...] = err * (1.0 / d)

        @pl.when(i == 0)
        def _():
            sq_ref[...] = jnp.zeros_like(sq_ref)

        sq_ref[...] += jnp.sum(err * err, axis=0, keepdims=True)

    row = pl.BlockSpec((tr, d), lambda i: (i, 0))
    vec = pl.BlockSpec((1, d), lambda i: (0, 0))
    return pl.pallas_call(
        body, name="loss_head", grid=(s // tr,),
        in_specs=[row, row], out_specs=[row, vec],
        out_shape=[jax.ShapeDtypeStruct((s, d), F32), jax.ShapeDtypeStruct((1, d), F32)],
        compiler_params=_params(("arbitrary",)),
    )(y, target)


def _sb_masks():
    row = lax.broadcasted_iota(jnp.int32, (CHUNK, CHUNK), 0)
    col = lax.broadcasted_iota(jnp.int32, (CHUNK, CHUNK), 1)
    return row, col


SB_WINDOW = 2 * CHUNK
SB_ROWS = CHUNK // 2
SB_FWD_QUERIES = 256
SB_BWD_QUERIES = 256

def _sb_consts():
    row = lax.broadcasted_iota(jnp.int32, (CHUNK, SB_WINDOW), 0)
    col = lax.broadcasted_iota(jnp.int32, (CHUNK, SB_WINDOW), 1)
    lane_hi = lax.broadcasted_iota(jnp.int32, (SB_ROWS, HEAD_PAIR), 1) >= GROUP
    return col - jnp.bitwise_and(row, SB_ROWS - 1), col, lane_hi


def _sb_window(t0, m, dcol, col):
    hi = t0 + SB_ROWS - SB_WINDOW * m
    start = jnp.maximum(hi - SB_WINDOW, 0)
    keep = jnp.logical_and(dcol < t0 - start, col < hi - start)
    return pl.multiple_of(start, SB_ROWS), keep


def _sb_stack(x, lane_hi):
    zero = jnp.zeros_like(x)
    return jnp.concatenate([jnp.where(lane_hi, zero, x), jnp.where(lane_hi, x, zero)], axis=0)


def _sb_unstack(y, lane_hi):
    return jnp.where(lane_hi, y[SB_ROWS:], y[:SB_ROWS])


def _sb_logs(z, keep):
    t = jnp.log(1.0 + jnp.exp(-jnp.abs(z)))
    ln = -(jnp.maximum(z, 0.0) + t)
    lb = ln + z
    ln = jnp.where(keep, ln, 0.0)
    return ln, lb


def _sb_scores(qm, kw, keep):
    return _sb_logs(_dot(qm, kw, NT), keep)


def _sb_tri(upper):
    r = np.arange(SB_WINDOW)
    m = (r[:, None] > r[None, :]) if upper else (r[:, None] < r[None, :])
    return jnp.asarray(m, jnp.bfloat16)


def _riding_exchange(exchange):
    if exchange is None:
        return [], [], [], []
    arrays, shard_shapes, scatter, names = exchange
    shapes = [jax.ShapeDtypeStruct(s, a.dtype)
              for s, a in zip(_exchange_out_shapes(shard_shapes, scatter, names), arrays)]
    return list(arrays), shapes, [pl.BlockSpec(memory_space=pl.ANY)] * len(arrays), _exchange_sems(len(arrays))


def _ride(exchange, ins, outs, sems, first, last):
    if exchange is None:
        return lambda: None
    start, finish = _exchange_plan(ins, outs, sems, exchange[1], exchange[2], exchange[3])
    pl.when(first)(start)
    return lambda: pl.when(last)(finish)


def _attn_fwd(qkv, exchange=None):
    s = qkv.shape[0]
    qb = min(SB_FWD_QUERIES, s)
    nblk = s // qb
    n_pairs = SB_W // HEAD_PAIR
    x_arrays, x_shapes, x_specs, x_sems = _riding_exchange(exchange)
    nx = len(x_arrays)

    def body(q_ref, k_ref, v_ref, up_ref, *refs):
        o_ref = refs[nx]
        i = pl.program_id(1)
        pair = pl.program_id(0)
        end_exchange = _ride(exchange, refs[:nx], refs[nx + 1:2 * nx + 1], refs[2 * nx + 1:],
                             jnp.logical_and(pair == 0, i == 0),
                             jnp.logical_and(pair == n_pairs - 1, i == nblk - 1))
        dcol, col, lane_hi = _sb_consts()
        halves = range(qb // SB_ROWS)
        t0s = [i * qb + r * SB_ROWS for r in halves]
        qs = [_sb_stack(q_ref[r * SB_ROWS:(r + 1) * SB_ROWS, :], lane_hi) * 0.125 for r in halves]

        def cond(st):
            return jnp.logical_and(st[0] * SB_WINDOW < (i + 1) * qb, st[1])

        def loop(st):
            m = st[0]
            wins = [_sb_window(t0s[r], m, dcol, col) for r in halves]
            kws = [k_ref[pl.ds(wins[r][0], SB_WINDOW), :] for r in halves]
            vws = [v_ref[pl.ds(wins[r][0], SB_WINDOW), :] for r in halves]
            keeps = [wins[r][1] for r in halves]
            zs = [_dot(qs[r], kws[r], NT) for r in halves]
            lnlb = [_sb_logs(zs[r], keeps[r]) for r in halves]
            his = [lnlb[r][0].astype(jnp.bfloat16) for r in halves]
            los = [(lnlb[r][0] - his[r].astype(F32)).astype(jnp.bfloat16) for r in halves]
            s_hi = [_dot(his[r], up_ref[...], NN) for r in halves]
            s_lo = [_dot(los[r], up_ref[...], NN) for r in halves]
            a = [jnp.where(keeps[r], jnp.exp(lnlb[r][1] + (s_hi[r] + s_lo[r] + st[2 + 2 * r])), 0.0).astype(MX)
                 for r in halves]
            new = []
            for r in halves:
                new += [st[2 + 2 * r] + jnp.sum(lnlb[r][0], axis=1, keepdims=True),
                        st[3 + 2 * r] + _dot(a[r], vws[r], NN)]
            active = functools.reduce(jnp.maximum, [jnp.max(c) for c in new[0::2]]) > SKIP_LOG
            return (m + 1, active, *new)

        c0 = jnp.zeros((CHUNK, 1), F32)
        a0 = jnp.zeros((CHUNK, HEAD_PAIR), F32)
        st = lax.while_loop(cond, loop, (i * 0, i >= 0) + (c0, a0) * len(halves))
        for r in halves:
            o_ref[r * SB_ROWS:(r + 1) * SB_ROWS, :] = _sb_unstack(st[3 + 2 * r], lane_hi).astype(o_ref.dtype)
        end_exchange()

    outs = pl.pallas_call(
        body, name="attn_fwd" if exchange is None else "attn_fwd_gather", grid=(n_pairs, nblk),
        in_specs=[pl.BlockSpec((qb, HEAD_PAIR), lambda p, i: (i, p)),
                  pl.BlockSpec((s, HEAD_PAIR), lambda p, i: (0, n_pairs + p)),
                  pl.BlockSpec((s, HEAD_PAIR), lambda p, i: (0, 2 * n_pairs + p)),
                  pl.BlockSpec((SB_WINDOW, SB_WINDOW), lambda p, i: (0, 0))] + x_specs,
        out_specs=[pl.BlockSpec((qb, HEAD_PAIR), lambda p, i: (i, p))] + x_specs,
        out_shape=[jax.ShapeDtypeStruct((s, SB_W), MX)] + x_shapes,
        scratch_shapes=x_sems,
        compiler_params=_params(("arbitrary", "arbitrary")),
    )(qkv, qkv, qkv, _sb_tri(True), *x_arrays)
    return outs[0], dict(zip(exchange[3] if exchange else (), outs[1:]))


def _attn_bwd(qkv, d_o, exchange=None):
    s = qkv.shape[0]
    qb = min(SB_BWD_QUERIES, s)
    nblk = s // qb
    n_pairs = SB_W // HEAD_PAIR
    n_win = s // SB_WINDOW
    x_arrays, x_shapes, x_specs, x_sems = _riding_exchange(exchange)
    nx = len(x_arrays)

    def body(q_ref, k_ref, v_ref, do_ref, up_ref, lo_ref, *refs):
        dq_ref, dk_ref, dv_ref = refs[nx:nx + 3]
        e_s = refs[2 * nx + 3]
        i = pl.program_id(1)
        pair = pl.program_id(0)
        end_exchange = _ride(exchange, refs[:nx], refs[nx + 3:2 * nx + 3], refs[2 * nx + 4:],
                             jnp.logical_and(pair == 0, i == 0),
                             jnp.logical_and(pair == n_pairs - 1, i == nblk - 1))
        dcol, col, lane_hi = _sb_consts()
        halves = range(qb // SB_ROWS)
        t0s = [i * qb + r * SB_ROWS for r in halves]
        qs = [_sb_stack(q_ref[r * SB_ROWS:(r + 1) * SB_ROWS, :], lane_hi) * 0.125 for r in halves]
        dos =[_sb_stack(do_ref[r * SB_ROWS:(r + 1) * SB_ROWS, :], lane_hi) for r in halves]

        @pl.when(i == 0)
        def _():
            dk_ref[...] = jnp.zeros_like(dk_ref)
            dv_ref[...] = jnp.zeros_like(dv_ref)

        def cond(st):
            return jnp.logical_and(st[0] * SB_WINDOW < (i + 1) * qb, st[1])

        def loop(st):
            m = st[0]
            wins = [_sb_window(t0s[r], m, dcol, col) for r in halves]
            kws = [k_ref[pl.ds(wins[r][0], SB_WINDOW), :] for r in halves]
            vws = [v_ref[pl.ds(wins[r][0], SB_WINDOW), :] for r in halves]
            keeps = [wins[r][1] for r in halves]
            zs = [_dot(qs[r], kws[r], NT) for r in halves]
            das = [_dot(dos[r], vws[r], NT) for r in halves]
            lnlb = [_sb_logs(zs[r], keeps[r]) for r in halves]
            his = [lnlb[r][0].astype(jnp.bfloat16) for r in halves]
            los = [(lnlb[r][0] - his[r].astype(F32)).astype(jnp.bfloat16) for r in halves]
            s_hi = [_dot(his[r], up_ref[...], NN) for r in halves]
            s_lo = [_dot(los[r], up_ref[...], NN) for r in halves]
            a = [jnp.where(keeps[r], jnp.exp(lnlb[r][1] + (s_hi[r] + s_lo[r] + st[2 + r])), 0.0) for r in halves]
            for r in halves:
                e_s[r, m] = a[r] * das[r]
            parts = [_dot(a[r].astype(MX), dos[r], TN) for r in halves]
            for r in halves:
                dv_ref[pl.ds(wins[r][0], SB_WINDOW), :] += parts[r]
            cs = [st[2 + r] + jnp.sum(lnlb[r][0], axis=1, keepdims=True) for r in halves]
            active = functools.reduce(jnp.maximum, [jnp.max(c) for c in cs]) > SKIP_LOG
            return (m + 1, active, *cs)

        c0 = jnp.zeros((CHUNK, 1), F32)
        n_seen = lax.while_loop(cond, loop, (i * 0, i >= 0) + (c0,) * len(halves))[0]

        def up(t, st):
            m = n_seen - 1 - t
            wins = [_sb_window(t0s[r], m, dcol, col) for r in halves]
            kws = [k_ref[pl.ds(wins[r][0], SB_WINDOW), :] for r in halves]
            es = [e_s[r, m] for r in halves]
            zs = [_dot(qs[r], kws[r], NT) for r in halves]
            his = [es[r].astype(jnp.bfloat16) for r in halves]
            los = [(es[r] - his[r].astype(F32)).astype(jnp.bfloat16) for r in halves]
            p_hi = [_dot(his[r], lo_ref[...], NN) for r in halves]
            p_lo = [_dot(los[r], lo_ref[...], NN) for r in halves]
            dzs = []
            for r in halves:
                u = jnp.exp(-jnp.abs(zs[r]))
                big = 1.0 / (1.0 + u)
                small = u * big
                pos = zs[r] >= 0.0
                prefix = p_hi[r] + p_lo[r] + st[2 * r]
                dz = es[r] * jnp.where(pos, small, big) - prefix * jnp.where(pos, big, small)
                dzs.append(jnp.where(wins[r][1], dz, 0.0).astype(MX))
            parts = [_dot(dzs[r], qs[r], TN) for r in halves]
            for r in halves:
                dk_ref[pl.ds(wins[r][0], SB_WINDOW), :] += parts[r]
            new = []
            for r in halves:
                new += [st[2 * r] + jnp.sum(es[r], axis=1, keepdims=True),
                        st[2 * r + 1] + _dot(dzs[r], kws[r], NN)]
            return tuple(new)

        a0 = jnp.zeros((CHUNK, HEAD_PAIR), F32)
        st = lax.fori_loop(0, n_seen, up, (c0, a0) * len(halves))
        for r in halves:
            dq_ref[r * SB_ROWS:(r + 1) * SB_ROWS, :] = (_sb_unstack(st[2 * r + 1], lane_hi) * 0.125).astype(dq_ref.dtype)
        end_exchange()

    blk = pl.BlockSpec((qb, HEAD_PAIR), lambda p, i: (i, p))
    full = pl.BlockSpec((s, HEAD_PAIR), lambda p, i: (0, p))
    tri = pl.BlockSpec((SB_WINDOW, SB_WINDOW), lambda p, i: (0, 0))
    outs = pl.pallas_call(
        body, name="attn_bwd" if exchange is None else "attn_bwd_scatter", grid=(n_pairs, nblk),
        in_specs=[blk,
                  pl.BlockSpec((s, HEAD_PAIR), lambda p, i: (0, n_pairs + p)),
                  pl.BlockSpec((s, HEAD_PAIR), lambda p, i: (0, 2 * n_pairs + p)),
                  blk, tri, tri] + x_specs,
        out_specs=[blk, full, full] + x_specs,
        out_shape=[jax.ShapeDtypeStruct((s, SB_W), MX), jax.ShapeDtypeStruct((s, SB_W), F32),
                   jax.ShapeDtypeStruct((s, SB_W), F32)] + x_shapes,
        scratch_shapes=[pltpu.VMEM((qb // SB_ROWS, n_win, CHUNK, SB_WINDOW), F32)] + x_sems,
        compiler_params=_params(("arbitrary", "arbitrary"), 56 * 1024 * 1024),
    )(qkv, qkv, qkv, d_o, _sb_tri(True), _sb_tri(False), *x_arrays)
    return outs[0], outs[1], outs[2], dict(zip(exchange[3] if exchange else (), outs[3:]))


POOL_TILE = 256
POOL_HALO = 128


def _pool_bands(tile, transpose):
    cur = np.zeros((N_GROUPS, tile, tile), np.float32)
    halo = np.zeros((N_GROUPS, tile, POOL_HALO), np.float32)
    t = np.arange(tile)[:, None]
    for g, w in enumerate(POOL_WINDOWS):
        if not transpose:
            p = np.arange(tile)[None, :]
            cur[g] = ((t - p >= 0) & (t - p < w))
            ph = np.arange(POOL_HALO)[None, :] - POOL_HALO
            halo[g] = (t - ph < w)
        else:
            p = np.arange(tile)[None, :]
            cur[g] = ((p - t >= 0) & (p - t < w))
            ph = np.arange(POOL_HALO)[None, :] + tile
            halo[g] = (ph - t < w)
    return jnp.asarray(cur, jnp.bfloat16), jnp.asarray(halo, jnp.bfloat16)


def _pool_count(i, tile, rows, row0):
    pos = (i * tile + row0 + lax.broadcasted_iota(jnp.int32, (rows, PW), 0)).astype(F32)
    grp = lax.broadcasted_iota(jnp.int32, (rows, PW), 1) // GROUP
    win = jnp.where(grp == 0, float(POOL_WINDOWS[0]),
                    jnp.where(grp == 1, float(POOL_WINDOWS[1]),
                              jnp.where(grp == 2, float(POOL_WINDOWS[2]), float(POOL_WINDOWS[3]))))
    return jnp.minimum(pos + 1.0, win), grp


def _pool_fwd(rest, wbd, scale):
    s = rest.shape[0]
    tile = min(POOL_TILE, s)
    halo_per_tile = tile // POOL_HALO
    bcur, bhalo = _pool_bands(tile, False)

    def body(cur_ref, prev_ref, bcur_ref, bhalo_ref, wbd_ref, scale_ref, pooled_ref, out_ref):
        i = pl.program_id(0)
        cur = cur_ref[...]
        prev = jnp.where(i > 0, prev_ref[...], 0.0)
        count, grp = _pool_count(i, tile, tile, 0)
        win = jnp.zeros((tile, PW), F32)
        for g in range(N_GROUPS):
            wsum = _split_dot(cur, bcur_ref[g], True, 3) + _split_dot(prev, bhalo_ref[g], True, 3)
            win = jnp.where(grp == g, wsum, win)
        pooled = (win / count - cur).astype(MX)
        pooled_ref[...] = pooled
        out_ref[...] = (_dot(pooled, wbd_ref[...], NN) * scale_ref[...]).astype(out_ref.dtype)

    tile_spec = pl.BlockSpec((tile, PW), lambda i: (i, 0))
    const3 = lambda shape: pl.BlockSpec(shape, lambda i: (0, 0, 0))
    const2 = lambda shape: pl.BlockSpec(shape, lambda i: (0, 0))
    return pl.pallas_call(
        body, name="pool_fwd", grid=(s // tile,),
        in_specs=[tile_spec,
                  pl.BlockSpec((POOL_HALO, PW), lambda i: (jnp.maximum(i * halo_per_tile - 1, 0), 0)),
                  const3(bcur.shape), const3(bhalo.shape), const2((PW, PW)), const2((1, PW))],
        out_specs=[tile_spec, tile_spec],
        out_shape=[jax.ShapeDtypeStruct((s, PW), MX), jax.ShapeDtypeStruct((s, PW), MX)],
        compiler_params=_params(("parallel",)),
    )(rest, rest, bcur, bhalo, wbd, scale)


def _pool_bwd(d_o, pooled, wbd, scale):
    s = d_o.shape[0]
    tile = min(POOL_TILE, s)
    halo_per_tile = tile // POOL_HALO
    n_halo = s // POOL_HALO
    n_tiles = s // tile
    ccur, chalo = _pool_bands(tile, True)

    def body(do_ref, nxt_ref, pooled_ref, ccur_ref, chalo_ref, wbd_ref, scale_ref, dp_ref, dw_ref, ds_ref):
        i = pl.program_id(0)
        do = do_ref[...]
        w = wbd_ref[...]
        sc = scale_ref[...]
        pooled_v = pooled_ref[...]

        @pl.when(i == 0)
        def _():
            dw_ref[...] = jnp.zeros_like(dw_ref)
            ds_ref[...] = jnp.zeros_like(ds_ref)

        ds_ref[...] += jnp.sum(do * _dot(pooled_v, w, NN), axis=0, keepdims=True)
        dmixed = (do * sc).astype(MX)
        dw_ref[...] += _dot(pooled_v, dmixed, TN)
        dpooled = _dot(dmixed, w, NT)
        nxt = jnp.where(i < n_tiles - 1, nxt_ref[...], 0.0)
        dpooled_n = _dot((nxt * sc).astype(MX), w, NT)
        count, grp = _pool_count(i, tile, tile, 0)
        count_n, _ = _pool_count(i, tile, POOL_HALO, tile)
        dq = dpooled / count
        dq_n = dpooled_n / count_n
        acc = jnp.zeros((tile, PW), F32)
        for g in range(N_GROUPS):
            wsum = _split_dot(dq, ccur_ref[g], True, 3) + _split_dot(dq_n, chalo_ref[g], True, 3)
            acc = jnp.where(grp == g, wsum, acc)
        dp_ref[...] = (acc - dpooled).astype(dp_ref.dtype)

    tile_spec = pl.BlockSpec((tile, PW), lambda i: (i, 0))
    const3 = lambda shape: pl.BlockSpec(shape, lambda i: (0, 0, 0))
    const2 = lambda shape: pl.BlockSpec(shape, lambda i: (0, 0))
    return pl.pallas_call(
        body, name="pool_bwd", grid=(n_tiles,),
        in_specs=[tile_spec,
                  pl.BlockSpec((POOL_HALO, PW), lambda i: (jnp.minimum((i + 1) * halo_per_tile, n_halo - 1), 0)),
                  tile_spec, const3(ccur.shape), const3(chalo.shape), const2((PW, PW)), const2((1, PW))],
        out_specs=[tile_spec, const2((PW, PW)), const2((1, PW))],
        out_shape=[jax.ShapeDtypeStruct((s, PW), MX), jax.ShapeDtypeStruct((PW, PW), F32),
                   jax.ShapeDtypeStruct((1, PW), F32)],
        compiler_params=_params(("arbitrary",)),
    )(d_o, d_o, pooled, ccur, chalo, wbd, scale)


GM_TILE = 512
GELU_C = 0.7978845608028654
GELU_A = 0.044715


def _gelu(x):
    return 0.5 * x * (1.0 + jnp.tanh(GELU_C * (x + GELU_A * (x * x * x))))


def _gelu_grad(x):
    th = jnp.tanh(GELU_C * (x + GELU_A * (x * x * x)))
    return 0.5 * (1.0 + th) + 0.5 * x * (1.0 - th * th) * (GELU_C * (1.0 + 3.0 * GELU_A * (x * x)))


def _gm_common(ws_ref):
    row, col = _sb_masks()
    tril = row >= col
    wsm = [jnp.where(tril, ws_ref[g], 0.0).astype(MX) for g in range(N_GROUPS)]
    grp = lax.broadcasted_iota(jnp.int32, (CHUNK, PW), 1) // GROUP
    return tril, wsm, grp


def _gm_mix(wsm, grp, vn_c):
    mixed = jnp.zeros((CHUNK, PW), F32)
    for g in range(N_GROUPS):
        mixed = jnp.where(grp == g, _dot(wsm[g], vn_c, NN), mixed)
    return mixed


def _gm_fwd(rest, gain, ws, bfull):
    s = rest.shape[0]
    tile = min(GM_TILE, s)

    def body(u_ref, v_ref, gain_ref, ws_ref, b_ref, out_ref):
        _, wsm, grp = _gm_common(ws_ref)
        bias = b_ref[...]
        for n in range(tile // CHUNK):
            rows = slice(n * CHUNK, (n + 1) * CHUNK)
            gu = _gelu(u_ref[rows, :])
            gv = _gelu(v_ref[rows, :])
            r = lax.rsqrt(jnp.mean(gv * gv, axis=-1, keepdims=True) + RMS_EPS)
            vn = (gv * r * gain_ref[...]).astype(MX)
            out_ref[rows, :] = (gu * (_gm_mix(wsm, grp, vn) + bias)).astype(out_ref.dtype)

    const2 = lambda shape: pl.BlockSpec(shape, lambda i: (0, 0))
    return pl.pallas_call(
        body, name="gm_fwd", grid=(s // tile,),
        in_specs=[pl.BlockSpec((tile, PW), lambda i: (i, 1)), pl.BlockSpec((tile, PW), lambda i: (i, 2)),
                  const2((1, PW)), pl.BlockSpec((N_GROUPS, CHUNK, CHUNK), lambda i: (0, 0, 0)),
                  const2((CHUNK, PW))],
        out_specs=pl.BlockSpec((tile, PW), lambda i: (i, 0)),
        out_shape=jax.ShapeDtypeStruct((s, PW), MX),
        compiler_params=_params(("parallel",)),
    )(rest, rest, gain, ws, bfull)


def _gm_bwd(d_o, rest, gain, ws, bfull):
    s = rest.shape[0]
    tile = min(GM_TILE, s)

    def body(do_ref, u_ref, v_ref, gain_ref, ws_ref, b_ref, du_ref, dv_ref, dws_ref, db_ref, dgain_ref):
        i = pl.program_id(0)
        tril, wsm, grp = _gm_common(ws_ref)
        bias = b_ref[...]
        gain_v = gain_ref[...]

        @pl.when(i == 0)
        def _():
            dws_ref[...] = jnp.zeros_like(dws_ref)
            db_ref[...] = jnp.zeros_like(db_ref)
            dgain_ref[...] = jnp.zeros_like(dgain_ref)

        for n in range(tile // CHUNK):
            rows = slice(n * CHUNK, (n + 1) * CHUNK)
            u = u_ref[rows, :]
            v = v_ref[rows, :]
            do = do_ref[rows, :]
            gu = _gelu(u)
            gv = _gelu(v)
            r = lax.rsqrt(jnp.mean(gv * gv, axis=-1, keepdims=True) + RMS_EPS)
            vhat = gv * r
            vn = (vhat * gain_v).astype(MX)
            mixed = _gm_mix(wsm, grp, vn)
            du_ref[rows, :] = (do * (mixed + bias) * _gelu_grad(u)).astype(du_ref.dtype)
            dmix = do * gu
            db_ref[...] += dmix
            dmix_mx = dmix.astype(MX)
            dvn = jnp.zeros((CHUNK, PW), F32)
            for g in range(N_GROUPS):
                dmg = jnp.where(grp == g, dmix_mx, jnp.zeros_like(dmix_mx))
                dws_ref[g] += jnp.where(tril, _dot(dmg, vn, NT), 0.0)
                dvn = jnp.where(grp == g, _dot(wsm[g], dmix_mx, TN), dvn)
            dgain_ref[...] += jnp.sum(dvn * vhat, axis=0, keepdims=True)
            dvhat = dvn * gain_v
            dgv = r * (dvhat - vhat * jnp.mean(dvhat * vhat, axis=-1, keepdims=True))
            dv_ref[rows, :] = (dgv * _gelu_grad(v)).astype(dv_ref.dtype)

    const2 = lambda shape: pl.BlockSpec(shape, lambda i: (0, 0))
    ws_spec = pl.BlockSpec((N_GROUPS, CHUNK, CHUNK), lambda i: (0, 0, 0))
    tile_spec = pl.BlockSpec((tile, PW), lambda i: (i, 0))
    return pl.pallas_call(
        body, name="gm_bwd", grid=(s // tile,),
        in_specs=[tile_spec, pl.BlockSpec((tile, PW), lambda i: (i, 1)), pl.BlockSpec((tile, PW), lambda i: (i, 2)),
                  const2((1, PW)), ws_spec, const2((CHUNK, PW))],
        out_specs=[tile_spec, tile_spec, ws_spec, const2((CHUNK, PW)), const2((1, PW))],
        out_shape=[jax.ShapeDtypeStruct((s, PW), MX), jax.ShapeDtypeStruct((s, PW), MX),
                   jax.ShapeDtypeStruct((N_GROUPS, CHUNK, CHUNK), F32),
                   jax.ShapeDtypeStruct((CHUNK, PW), F32), jax.ShapeDtypeStruct((1, PW), F32)],
        compiler_params=_params(("arbitrary",)),
    )(d_o, rest, rest, gain, ws, bfull)


GATE_TM = 2048
GATE_TN = 256


def _gate_specs(s):
    tm = min(GATE_TM, s)
    tn = GATE_TN
    gate0 = GATE_COL0 // tn
    per = D // tn
    ins = [pl.BlockSpec((tm, SB_W), lambda i, j: (i, 0)),
           pl.BlockSpec((tm, PW), lambda i, j: (i, 0)),
           pl.BlockSpec((tm, PW), lambda i, j: (i, 0)),
           pl.BlockSpec((SB_W, tn), lambda i, j: (0, j)),
           pl.BlockSpec((PW, tn), lambda i, j: (0, j)),
           pl.BlockSpec((PW, tn), lambda i, j: (0, j)),
           pl.BlockSpec((tm, tn), lambda i, j: (i, gate0 + j)),
           pl.BlockSpec((tm, tn), lambda i, j: (i, gate0 + per + j)),
           pl.BlockSpec((tm, tn), lambda i, j: (i, gate0 + 2 * per + j))]
    return tm, tn, ins, pl.BlockSpec((tm, tn), lambda i, j: (i, j))


def _gate_fwd(o_sb, o_pool, o_gm, w_sb, w_pool, w_gm, rest):
    s = rest.shape[0]
    tm, tn, ins, out = _gate_specs(s)

    def body(o0, o1, o2, w0, w1, w2, g0, g1, g2, merged_ref):
        acc = jax.nn.sigmoid(g0[...]) * _dot(o0[...], w0[...], NN)
        acc += jax.nn.sigmoid(g1[...]) * _dot(o1[...], w1[...], NN)
        acc += jax.nn.sigmoid(g2[...]) * _dot(o2[...], w2[...], NN)
        merged_ref[...] = acc.astype(merged_ref.dtype)

    return pl.pallas_call(
        body, name="gate_fwd", grid=(s // tm, D // tn),
        in_specs=ins, out_specs=out, out_shape=jax.ShapeDtypeStruct((s, D), MX),
        compiler_params=_params(("parallel", "parallel")),
    )(o_sb, o_pool, o_gm, w_sb, w_pool, w_gm, rest, rest, rest)


def _gate_bwd(o_sb, o_pool, o_gm, w_sb, w_pool, w_gm, rest, dmerged):
    s = rest.shape[0]
    tm, tn, ins, out = _gate_specs(s)

    def body(o0, o1, o2, w0, w1, w2, g0, g1, g2, dm_ref, db0, db1, db2, dg0, dg1, dg2):
        dm = dm_ref[...]
        for o, w, g, db, dg in ((o0, w0, g0, db0, dg0), (o1, w1, g1, db1, dg1), (o2, w2, g2, db2, dg2)):
            sg = jax.nn.sigmoid(g[...])
            db[...] = (dm * sg).astype(db.dtype)
            dg[...] = (dm * _dot(o[...], w[...], NN) * (sg * (1.0 - sg))).astype(dg.dtype)

    return pl.pallas_call(
        body, name="gate_bwd", grid=(s // tm, D // tn),
        in_specs=ins + [out], out_specs=[out] * 6,
        out_shape=[jax.ShapeDtypeStruct((s, D), MX)] * 6,
        compiler_params=_params(("parallel", "parallel")),
    )(o_sb, o_pool, o_gm, w_sb, w_pool, w_gm, rest, rest, rest, dmerged)


def _relu2(acc):
    r = jnp.maximum(acc, 0.0)
    return r * r, 2.0 * r


def _relu2_bwd(acc, slope):
    return (acc * slope.astype(F32),)


TILES = {
    "proj_qkv": (2048, W_IN_TILE, 1024), "proj_rest": (2048, W_IN_TILE, 1024),
    "out_proj": (2048, 1024, 1024), "ff_in": (2048, 1024, 1024), "ff_out": (1024, 1024, 4096),
    "ff_out_dx": (2048, 1024, 1024), "ff_out_dw": (1024, 1024, 2048),
    "ff_in_dx": (1024, 1024, 4096), "ff_in_dw": (1024, 2048, 1024),
    "out_proj_dx": (2048, 1024, 1024), "out_proj_dw": (1024, 1024, 2048),
    "br_sb_dx": (2048, 512, 1024), "br_sb_dw": (512, 1024, 2048),
    "br_pool_dx": (2048, 256, 1024), "br_pool_dw": (256, 1024, 2048),
    "br_gm_dx": (2048, 256, 1024), "br_gm_dw": (256, 1024, 2048),
    "proj_dx": (512, 1024, 5376), "proj_dw": (1024, W_IN_TILE, 2048),
}


def _mm(name, a, b, mode, out_dtypes, rides=None, got=None, **kw):
    tm, tn, tk = TILES[name]
    ride = rides.get(name) if rides else None
    res = _matmul(a, b, mode=mode, name=name, tm=tm, tn=tn, tk=tk, out_dtypes=out_dtypes, exchange=ride, **kw)
    if ride is None:
        return res
    got.update(res[-1])
    return res[0] if len(res) == 2 else res[:-1]


def _layer_fwd(x, w, rides=None):
    rides = rides or {}
    got = {}
    h = _rms_fwd(x, w["g_mix_pre"], res=None, out_dtype=MX, name="rms_mix_pre")
    qkv = _mm("proj_qkv", h, w["w_in"], "nn", (MX,), n_out=QKV_W)
    rest = _mm("proj_rest", h, w["w_in"], "nn", (F32,), rides, got, b_col0=QKV_W, n_out=REST_W)
    o_sb, arrived = _attn_fwd(qkv, rides.get("attn"))
    got.update(arrived)
    pooled, o_pool = _pool_fwd(rest, w["wbd"], w["pool_scale"])
    o_gm = _gm_fwd(rest, w["gm_gain"], w["w_spatial"], w["bfull"])
    merged = _gate_fwd(o_sb, o_pool, o_gm, w["w_br_sb"], w["w_br_pool"], w["w_br_gm"], rest)
    y = _mm("out_proj", merged, w["w_out"], "nn", (F32,))
    x1 = _rms_fwd(y, w["g_mix_post"], res=x, out_dtype=F32, name="rms_mix_post")
    h2 = _rms_fwd(x1, w["g_ff_pre"], res=None, out_dtype=MX, name="rms_ff_pre")
    r, r_slope = _mm("ff_in", h2, w["w_ff_in"], "nn", (MX, MX), rides, got, epilogue=_relu2)
    ff = _mm("ff_out", r, w["w_ff_out"], "nn", (F32,), rides, got)
    x2 = _rms_fwd(ff, w["g_ff_post"], res=x1, out_dtype=F32, name="rms_ff_post")
    saved = dict(x=x, h=h, qkv=qkv, rest=rest, o_sb=o_sb, pooled=pooled, o_pool=o_pool, o_gm=o_gm,
                 merged=merged, y=y, x1=x1, h2=h2, r=r, r_slope=r_slope, ff=ff)
    return x2, saved, got


def _layer_bwd(dx2, w, sv, rides=None, own_ff=None):
    rides = rides or {}
    got, own = {}, {}
    dff, dg_ff_post = _rms_bwd(sv["ff"], w["g_ff_post"], dx2, res=None, out_dtype=MX, name="rms_ff_post_bwd")
    da = _mm("ff_out_dx", dff, w["w_ff_out"], "nt", (MX,), rides, got, epilogue=_relu2_bwd, extras=(sv["r_slope"],))
    dw_ff_out = _mm("ff_out_dw", sv["r"], dff, "tn", (MX,))
    dh2 = _mm("ff_in_dx", da, w["w_ff_in"], "nt", (F32,), rides, got)
    dw_ff_in = _mm("ff_in_dw", sv["h2"], da, "tn", (MX,))
    dx1, dg_ff_pre = _rms_bwd(sv["x1"], w["g_ff_pre"], dh2, res=dx2, out_dtype=F32, name="rms_ff_pre_bwd")

    dy, dg_mix_post = _rms_bwd(sv["y"], w["g_mix_post"], dx1, res=None, out_dtype=MX, name="rms_mix_post_bwd")
    dmerged = _mm("out_proj_dx", dy, w["w_out"], "nt", (F32,))
    dw_out = _mm("out_proj_dw", sv["merged"], dy, "tn", (MX,))
    db_sb, db_pool, db_gm, dg0, dg1, dg2 = _gate_bwd(
        sv["o_sb"], sv["o_pool"], sv["o_gm"], w["w_br_sb"], w["w_br_pool"], w["w_br_gm"], sv["rest"], dmerged)
    do_sb = _mm("br_sb_dx", db_sb, w["w_br_sb"], "nt", (MX,))
    dw_br_sb = _mm("br_sb_dw", sv["o_sb"], db_sb, "tn", (MX,))
    do_pool = _mm("br_pool_dx", db_pool, w["w_br_pool"], "nt", (F32,))
    dw_br_pool = _mm("br_pool_dw", sv["o_pool"], db_pool, "tn", (MX,))
    do_gm = _mm("br_gm_dx", db_gm, w["w_br_gm"], "nt", (F32,))
    dw_br_gm = _mm("br_gm_dw", sv["o_gm"], db_gm, "tn", (MX,))

    du, dv_gm, dws, dbfull, dgain = _gm_bwd(do_gm, sv["rest"], w["gm_gain"], w["w_spatial"], w["bfull"])
    dp, dwbd, dscale = _pool_bwd(do_pool, sv["pooled"], w["wbd"], w["pool_scale"])
    dq, dk, dv, arrived = _attn_bwd(sv["qkv"], do_sb, rides.get("attn"))
    got.update(arrived)
    dproj = jnp.concatenate([dq, dk.astype(MX), dv.astype(MX), dp, du, dv_gm, dg0, dg1, dg2], axis=1)
    late = {} if own_ff is None else {"proj_dx": ([dw_ff_in], own_ff, True, ("w_ff_in",)),
                                      "proj_dw": ([dw_ff_out], own_ff, True, ("w_ff_out",))}
    dh = _mm("proj_dx", dproj, w["w_in"], "nt", (F32,), late, own)
    dw_in = _mm("proj_dw", sv["h"], dproj, "tn", (MX,), late, own)
    dx, dg_mix_pre = _rms_bwd(sv["x"], w["g_mix_pre"], dh, res=dx1, out_dtype=F32, name="rms_mix_pre_bwd")

    grads = dict(
        w_in=dw_in, w_br_sb=dw_br_sb, w_br_pool=dw_br_pool, w_br_gm=dw_br_gm, w_out=dw_out,
        w_ff_in=dw_ff_in, w_ff_out=dw_ff_out,
        w_pool=jnp.stack([dwbd[g * GROUP:(g + 1) * GROUP, g * GROUP:(g + 1) * GROUP] for g in range(N_GROUPS)]),
        pool_scale=dscale[0], gm_gain=dgain[0], w_spatial=dws,
        b_spatial=dbfull.reshape(CHUNK, N_GROUPS, GROUP).sum(axis=-1).T,
        g_mix_pre=dg_mix_pre[0], g_mix_post=dg_mix_post[0], g_ff_pre=dg_ff_pre[0], g_ff_post=dg_ff_post[0])
    return dx, grads, got, own


def _layer_operands(full_l, small_l):
    wbd = jnp.zeros((PW, PW), F32)
    for g in range(N_GROUPS):
        wbd = wbd.at[g * GROUP:(g + 1) * GROUP, g * GROUP:(g + 1) * GROUP].set(small_l["w_pool"][g])
    w = dict(full_l)
    w["wbd"] = wbd.astype(MX)
    w["bfull"] = jnp.repeat(small_l["b_spatial"].T, GROUP, axis=1)
    w["w_spatial"] = small_l["w_spatial"]
    for n in ("pool_scale", "gm_gain", "g_mix_pre", "g_mix_post", "g_ff_pre", "g_ff_post"):
        w[n] = small_l[n][None, :]
    return w


def _local_step(x, target, layers):
    saved = []
    for w in layers:
        x, sv, _ = _layer_fwd(x, w)
        saved.append(sv)
    dx, sq = _loss_head(x, target)
    loss = 0.5 * jnp.sum(sq) / x.shape[1]
    grads = [None] * len(layers)
    for l in reversed(range(len(layers))):
        dx, grads[l], _, _ = _layer_bwd(dx, layers[l], saved[l])
    return loss, dx, grads


def _mesh_place():
    x, y, c = lax.axis_index("x"), lax.axis_index("y"), lax.axis_index("c")

    def peer(k):
        px = 1 - x if k & 4 else x
        py = 1 - y if k & 2 else y
        pc = 1 - c if k & 1 else c
        return (px, py, pc), 4 * px + 2 * py + pc

    return 4 * x + 2 * y + c, peer


def _shard_window(ref, name, shard_shape, idx):
    if name == "w_in":
        return ref.at[idx]
    a, b = shard_shape
    if BIG_SHARD_AXIS[name] == 2:
        return ref.at[:, pl.ds(pl.multiple_of(idx * b, b), b)]
    return ref.at[pl.ds(pl.multiple_of(idx * a, a), a), :]


def _full_shape(name, shard_shape):
    a, b = shard_shape
    if name == "w_in":
        return (N_DEV, a, b)
    return (a, N_DEV * b) if BIG_SHARD_AXIS[name] == 2 else (N_DEV * a, b)


def _exchange_out_shapes(shard_shapes, scatter, names):
    if scatter:
        return [(N_DEV,) + tuple(shard_shapes[nm]) for nm in names]
    return [_full_shape(nm, shard_shapes[nm]) for nm in names]


def _exchange_sems(n):
    return [pltpu.SemaphoreType.DMA((n, N_DEV - 1)), pltpu.SemaphoreType.DMA((n, N_DEV - 1)),
            pltpu.SemaphoreType.DMA((n,))]


def _exchange_plan(ins, outs, sems, shard_shapes, scatter, names):
    send_sems, recv_sems, local_sems = sems
    me, peer = _mesh_place()
    n = len(names)

    def window(ref, p, idx):
        return _shard_window(ref, names[p], shard_shapes[names[p]], idx)

    def copy(p, k, landing):
        dev, idx = peer(k)
        if scatter:
            src, dst = window(ins[p], p, idx), outs[p].at[idx if landing else me]
        else:
            src, dst = ins[p], window(outs[p], p, idx if landing else me)
        return pltpu.make_async_remote_copy(
            src_ref=src, dst_ref=dst, send_sem=send_sems.at[p, k - 1], recv_sem=recv_sems.at[p, k - 1],
            device_id=dev, device_id_type=pl.DeviceIdType.MESH)

    def local(p):
        if scatter:
            return pltpu.make_async_copy(window(ins[p], p, me), outs[p].at[me], local_sems.at[p])
        return pltpu.make_async_copy(ins[p], window(outs[p], p, me), local_sems.at[p])

    pairs = [(p, k) for p in range(n) for k in range(1, N_DEV)]

    def start():
        for p in range(n):
            local(p).start()
        for p, k in pairs:
            copy(p, k, False).start()

    def finish():
        for p, k in pairs:
            copy(p, k, True).wait_recv()
        for p, k in pairs:
            copy(p, k, False).wait_send()
        for p in range(n):
            local(p).wait()

    return start, finish


def _exchange_big(arrays, shard_shapes, *, scatter, name, names=BIG):
    n = len(names)

    def body(*refs):
        start, finish = _exchange_plan(refs[:n], refs[n:2 * n], refs[2 * n:], shard_shapes, scatter, names)
        start()
        finish()

    outs = pl.pallas_call(
        body, name=name,
        in_specs=[pl.BlockSpec(memory_space=pl.ANY)] * n,
        out_specs=[pl.BlockSpec(memory_space=pl.ANY)] * n,
        out_shape=[jax.ShapeDtypeStruct(s, a.dtype)
                   for s, a in zip(_exchange_out_shapes(shard_shapes, scatter, names), arrays)],
        scratch_shapes=_exchange_sems(n),
    )(*arrays)
    return dict(zip(names, outs))


def _gather_rows(inp, *, name):
    def body(in_ref, out_ref, send_sems, recv_sems, local_sem):
        me, peer = _mesh_place()

        def copy(k, landing):
            dev, idx = peer(k)
            return pltpu.make_async_remote_copy(
                src_ref=in_ref, dst_ref=out_ref.at[idx if landing else me],
                send_sem=send_sems.at[k - 1], recv_sem=recv_sems.at[k - 1],
                device_id=dev, device_id_type=pl.DeviceIdType.MESH)

        mine = pltpu.make_async_copy(in_ref, out_ref.at[me], local_sem)
        mine.start()
        sends = [copy(k, False) for k in range(1, N_DEV)]
        for cp in sends:
            cp.start()
        for k in range(1, N_DEV):
            copy(k, True).wait_recv()
        for cp in sends:
            cp.wait_send()
        mine.wait()

    return pl.pallas_call(
        body, name=name,
        in_specs=[pl.BlockSpec(memory_space=pl.ANY)],
        out_specs=pl.BlockSpec(memory_space=pl.ANY),
        out_shape=jax.ShapeDtypeStruct((N_DEV,) + inp.shape, inp.dtype),
        scratch_shapes=[pltpu.SemaphoreType.DMA((N_DEV - 1,)), pltpu.SemaphoreType.DMA((N_DEV - 1,)),
                        pltpu.SemaphoreType.DMA],
    )(inp)


def _sum_devices(recv, name):
    _, rows, cols = recv.shape
    tr = rows
    for cand in (512, 256, 128, 64, 32, 16, 8):
        if rows % cand == 0:
            tr = cand
            break

    def body(r_ref, out_ref):
        acc = r_ref[0].astype(F32)
        for d in range(1, N_DEV):
            acc = acc + r_ref[d].astype(F32)
        out_ref[...] = acc

    return pl.pallas_call(
        body, name=name, grid=(rows // tr,),
        in_specs=[pl.BlockSpec((N_DEV, tr, cols), lambda i: (0, i, 0))],
        out_specs=pl.BlockSpec((tr, cols), lambda i: (i, 0)),
        out_shape=jax.ShapeDtypeStruct((rows, cols), F32),
        compiler_params=_params(("parallel",)),
    )(recv)


def _adamw(w, g, m, v, name):
    rows, cols = w.shape
    tr = rows
    for cand in (512, 256, 128, 64, 32, 16, 8):
        if rows % cand == 0:
            tr = cand
            break
    c1 = 1.0 - ADAM_B1 ** ADAM_STEP
    c2 = 1.0 - ADAM_B2 ** ADAM_STEP

    def body(w_ref, g_ref, m_ref, v_ref, d_ref, nm_ref, nv_ref):
        gv = g_ref[...]
        nm = ADAM_B1 * m_ref[...] + (1.0 - ADAM_B1) * gv
        nv = ADAM_B2 * v_ref[...] + (1.0 - ADAM_B2) * (gv * gv)
        nm_ref[...] = nm
        nv_ref[...] = nv
        d_ref[...] = -ADAM_LR * ((nm / c1) / (jnp.sqrt(nv / c2) + ADAM_EPS) + ADAM_WD * w_ref[...])

    spec = pl.BlockSpec((tr, cols), lambda i: (i, 0))
    return pl.pallas_call(
        body, name=name, grid=(rows // tr,),
        in_specs=[spec] * 4, out_specs=[spec] * 3,
        out_shape=[jax.ShapeDtypeStruct((rows, cols), F32)] * 3,
        compiler_params=_params(("parallel",)),
    )(w, g, m, v)


def _pack_rows(flat, dtype):
    n = flat.shape[-1]
    rows = -(-n // PACK_COLS)
    rows = -(-rows // 16) * 16
    pad = rows * PACK_COLS - n
    flat = jnp.pad(flat, [(0, 0)] * (flat.ndim - 1) + [(0, pad)])
    return flat.reshape(flat.shape[:-1] + (rows, PACK_COLS)).astype(dtype)


def kernel(x, w_in, w_pool, pool_scale, gm_gain, w_spatial, b_spatial, w_br_sb, w_br_pool, w_br_gm, w_out, g_mix_pre, g_mix_post, g_ff_pre, g_ff_post, w_ff_in, w_ff_out, loss_target, m_w_in, m_w_pool, m_pool_scale, m_gm_gain, m_w_spatial, m_b_spatial, m_w_br_sb, m_w_br_pool, m_w_br_gm, m_w_out, m_g_mix_pre, m_g_mix_post, m_g_ff_pre, m_g_ff_post, m_w_ff_in, m_w_ff_out, v_w_in, v_w_pool, v_pool_scale, v_gm_gain, v_w_spatial, v_b_spatial, v_w_br_sb, v_w_br_pool, v_w_br_gm, v_w_out, v_g_mix_pre, v_g_mix_post, v_g_ff_pre, v_g_ff_post, v_w_ff_in, v_w_ff_out):
    args = dict(locals())
    weights = {n: args[n] for n in WEIGHTS}
    mom_m = {n: args["m_" + n] for n in WEIGHTS}
    mom_v = {n: args["v_" + n] for n in WEIGHTS}

    shard_shapes = {n: weights[n].shape[1:] for n in BIG}
    depth, d_model, w_in_cols = weights["w_in"].shape
    with_attn = ("w_in", "w_br_sb", "w_br_pool", "w_br_gm", "w_out")

    def gather_of(l, names):
        return [weights[n][l].astype(MX) for n in names], shard_shapes, False, names

    def as_operands(full, l):
        full = dict(full)
        full["w_in"] = jnp.transpose(full["w_in"], (1, 0, 2)).reshape(d_model, N_DEV * w_in_cols)
        return _layer_operands(full, {n: weights[n][l] for n in SMALL})

    full = _exchange_big(gather_of(0, BIG)[0], shard_shapes, scatter=False, name="gather_weights")
    xc = x[0]
    layers, saved = [], []
    for l in range(depth):
        layers.append(as_operands(full, l))
        rides = None
        if l + 1 < depth:
            rides = {"attn": gather_of(l + 1, with_attn), "ff_in": gather_of(l + 1, ("w_ff_in",)),
                     "ff_out": gather_of(l + 1, ("w_ff_out",))}
        xc, sv, full = _layer_fwd(xc, layers[l], rides)
        saved.append(sv)
    dx, sq = _loss_head(xc, loss_target[0])
    loss = lax.psum(0.5 * jnp.sum(sq) / d_model, ("x", "y", "c"))

    def scatter_of(g, names):
        arrays = [jnp.transpose(g[n].reshape(d_model, N_DEV, w_in_cols), (1, 0, 2)) if n == "w_in" else g[n]
                  for n in names]
        return arrays, shard_shapes, True, names

    layer_grads, recv = [None] * depth, [None] * depth
    for l in reversed(range(depth)):
        rides = None
        if l + 1 < depth:
            above = layer_grads[l + 1]
            rides = {"ff_out_dx": scatter_of(above, ("w_ff_in",)), "ff_in_dx": scatter_of(above, ("w_ff_out",)),
                     "attn": scatter_of(above, with_attn)}
        dx, layer_grads[l], arrived, own = _layer_bwd(dx, layers[l], saved[l], rides, shard_shapes if l == 0 else None)
        if rides is not None:
            recv[l + 1] = arrived
    recv[0] = dict(own)
    recv[0].update(_exchange_big(scatter_of(layer_grads[0], with_attn)[0], shard_shapes, scatter=True,
                                 name="scatter_grads", names=with_attn))
    grad_w = {n: jnp.stack([_sum_devices(recv[l][n], "sum_grads_" + n) for l in range(depth)]) for n in BIG}
    small_sizes = [int(np.prod(weights[n].shape)) for n in SMALL]
    small_off = np.concatenate([[0], np.cumsum(small_sizes)])
    small_flat = jnp.concatenate([jnp.stack([g[n] for g in layer_grads]).reshape(-1) for n in SMALL])
    g_small = _sum_devices(_gather_rows(_pack_rows(small_flat, F32), name="gather_small_grads"),
                           "sum_small_grads").reshape(-1)
    for n, lo, hi in zip(SMALL, small_off[:-1], small_off[1:]):
        grad_w[n] = g_small[lo:hi].reshape(weights[n].shape)

    delta, new_m, new_v = {}, {}, {}
    for n in WEIGHTS:
        shape = weights[n].shape
        view = (-1, shape[-1])
        d, nm, nv = _adamw(weights[n].reshape(view), grad_w[n].reshape(view), mom_m[n].reshape(view),
                           mom_v[n].reshape(view), "adamw_" + n)
        delta[n], new_m[n], new_v[n] = d.reshape(shape), nm.reshape(shape), nv.reshape(shape)

    return (loss, dx[None], *[grad_w[n] for n in WEIGHTS], *[delta[n] for n in WEIGHTS],
            *[new_m[n] for n in WEIGHTS], *[new_v[n] for n in WEIGHTS])
```

```python
import functools

import numpy as np
import jax
import jax.numpy as jnp
from jax import lax
from jax.experimental import pallas as pl
from jax.experimental.pallas import tpu as pltpu

F32 = jnp.float32
MX = jnp.bfloat16

D = 1024
SB_W = 512
HEAD_PAIR = 128
PW = 256
GROUP = 64
N_GROUPS = 4
CHUNK = 128
POOL_WINDOWS = (2, 4, 8, 16)
D_FF = 4096
D_IN = 5376
QKV_W = 3 * SB_W
REST_W = D_IN - QKV_W
GATE_COL0 = 3 * PW
RMS_EPS = 1e-6
N_DEV = 8
DEPTH = 4

ADAM_LR = 0.001
ADAM_B1 = 0.9
ADAM_B2 = 0.999
ADAM_EPS = 1e-08
ADAM_WD = 0.01
ADAM_STEP = 10

SKIP_LOG = -120.0

VMEM_LIMIT = 48 * 1024 * 1024
W_IN_TILE = 768
PACK_COLS = 1024

BIG = ("w_in", "w_br_sb", "w_br_pool", "w_br_gm", "w_out", "w_ff_in", "w_ff_out")
SMALL_GRADS = "small_grads"
BIG_SHARD_AXIS = {"w_in": 2, "w_br_sb": 2, "w_br_pool": 2, "w_br_gm": 2, "w_out": 1, "w_ff_in": 2, "w_ff_out": 1}
SMALL = ("w_pool", "pool_scale", "gm_gain", "w_spatial", "b_spatial",
         "g_mix_pre", "g_mix_post", "g_ff_pre", "g_ff_post")
WEIGHTS = ("w_in", "w_pool", "pool_scale", "gm_gain", "w_spatial", "b_spatial", "w_br_sb", "w_br_pool",
           "w_br_gm", "w_out", "g_mix_pre", "g_mix_post", "g_ff_pre", "g_ff_post", "w_ff_in", "w_ff_out")


def _params(sem, vmem=VMEM_LIMIT):
    return pltpu.CompilerParams(dimension_semantics=sem, vmem_limit_bytes=vmem)


def _dot(a, b, dims):
    return lax.dot_general(a, b, (dims, ((), ())), preferred_element_type=F32)


NN = ((1,), (0,))
NT = ((1,), (1,))
TN = ((0,), (0,))


def _split_dot(x, m, left, nsplit):
    acc = None
    r = x
    for s in range(nsplit):
        p = r.astype(jnp.bfloat16)
        d = _dot(m, p, NN) if left else _dot(p, m, NN)
        acc = d if acc is None else acc + d
        if s + 1 < nsplit:
            r = r - p.astype(F32)
    return acc


def _matmul(a, b, *, mode, name, tm, tn, tk, out_dtypes, b_col0=0, n_out=None, epilogue=None, extras=(),
            exchange=None):
    if mode == "nn":
        m, kdim = a.shape
        n = b.shape[1] if n_out is None else n_out
        dims = NN
    elif mode == "nt":
        m, kdim = a.shape
        n = b.shape[0]
        dims = NT
    else:
        kdim, m = a.shape
        n = b.shape[1]
        dims = TN
    tm, tn, tk = min(tm, m), min(tn, n), min(tk, kdim)
    assert m % tm == 0 and n % tn == 0 and kdim % tk == 0 and b_col0 % tn == 0, (name, m, n, kdim)
    if mode == "nn":
        a_spec = pl.BlockSpec((tm, tk), lambda i, j, k: (i, k))
        b_spec = pl.BlockSpec((tk, tn), lambda i, j, k: (k, j + b_col0 // tn))
    elif mode == "nt":
        a_spec = pl.BlockSpec((tm, tk), lambda i, j, k: (i, k))
        b_spec = pl.BlockSpec((tn, tk), lambda i, j, k: (j, k))
    else:
        a_spec = pl.BlockSpec((tk, tm), lambda i, j, k: (k, i))
        b_spec = pl.BlockSpec((tk, tn), lambda i, j, k: (k, j))
    nk = kdim // tk
    n_extra = len(extras)
    n_outs = len(out_dtypes)
    o_spec = pl.BlockSpec((tm, tn), lambda i, j, k: (i, j))
    x_arrays, x_shapes, x_specs, x_sems = _riding_exchange(exchange)
    nx = len(x_arrays)
    grid = (m // tm, n // tn, nk)

    def body(a_ref, b_ref, *refs):
        extra_refs = refs[:n_extra]
        out_refs = refs[n_extra + nx:n_extra + nx + n_outs]
        scratch = refs[n_extra + 2 * nx + n_outs:]
        ids = [pl.program_id(d) for d in range(3)]
        k = ids[2]
        end_exchange = _ride(exchange, refs[n_extra:n_extra + nx], refs[n_extra + nx + n_outs:n_extra + 2 * nx + n_outs],
                             scratch[1:] if nk > 1 else scratch,
                             functools.reduce(jnp.logical_and, [ids[d] == 0 for d in range(3)]),
                             functools.reduce(jnp.logical_and, [ids[d] == grid[d] - 1 for d in range(3)]))
        part = _dot(a_ref[...].astype(MX), b_ref[...].astype(MX), dims)

        def finish(acc):
            outs = (acc,) if epilogue is None else epilogue(acc, *[e[...] for e in extra_refs])
            for o_ref, val in zip(out_refs, outs):
                o_ref[...] = val.astype(o_ref.dtype)

        if nk == 1:
            finish(part)
        else:
            acc_ref = scratch[0]

            @pl.when(k == 0)
            def _():
                acc_ref[...] = part

            @pl.when(jnp.logical_and(k > 0, k < nk - 1))
            def _():
                acc_ref[...] += part

            @pl.when(k == nk - 1)
            def _():
                finish(acc_ref[...] + part)

        end_exchange()

    outs = pl.pallas_call(
        body, name=name if exchange is None else name + "_ride",
        grid=grid,
        in_specs=[a_spec, b_spec] + [o_spec] * n_extra + x_specs,
        out_specs=[o_spec] * n_outs + x_specs,
        out_shape=[jax.ShapeDtypeStruct((m, n), dt) for dt in out_dtypes] + x_shapes,
        scratch_shapes=([pltpu.VMEM((tm, tn), F32)] if nk > 1 else []) + x_sems,
        compiler_params=_params(("parallel", "parallel", "arbitrary") if exchange is None
                                else ("arbitrary", "arbitrary", "arbitrary")),
    )(a, b, *extras, *x_arrays)
    if exchange is not None:
        return tuple(outs[:n_outs]) + (dict(zip(exchange[3], outs[n_outs:])),)
    return outs[0] if n_outs == 1 else outs


ROW_TILE = 1024


def _rms_fwd(x, g, *, res, out_dtype, name):
    s, d = x.shape
    tr = min(ROW_TILE, s)
    has_res = res is not None

    def body(x_ref, g_ref, *refs):
        out_ref = refs[-1]
        xv = x_ref[...]
        y = xv * lax.rsqrt(jnp.mean(xv * xv, axis=-1, keepdims=True) + RMS_EPS) * g_ref[...]
        if has_res:
            y = refs[0][...] + y
        out_ref[...] = y.astype(out_ref.dtype)

    row = pl.BlockSpec((tr, d), lambda i: (i, 0))
    vec = pl.BlockSpec((1, d), lambda i: (0, 0))
    return pl.pallas_call(
        body, name=name, grid=(s // tr,),
        in_specs=[row, vec] + ([row] if has_res else []),
        out_specs=row, out_shape=jax.ShapeDtypeStruct((s, d), out_dtype),
        compiler_params=_params(("parallel",)),
    )(x, g, *([res] if has_res else []))


def _rms_bwd(x, g, dout, *, res, out_dtype, name):
    s, d = x.shape
    tr = min(ROW_TILE, s)
    has_res = res is not None

    def body(x_ref, g_ref, do_ref, *refs):
        dx_ref, dg_ref = refs[-2], refs[-1]
        i = pl.program_id(0)
        xv = x_ref[...]
        do = do_ref[...]
        r = lax.rsqrt(jnp.mean(xv * xv, axis=-1, keepdims=True) + RMS_EPS)
        xhat = xv * r
        dxhat = do * g_ref[...]
        dx = r * (dxhat - xhat * jnp.mean(dxhat * xhat, axis=-1, keepdims=True))
        if has_res:
            dx = refs[0][...] + dx
        dx_ref[...] = dx.astype(dx_ref.dtype)

        @pl.when(i == 0)
        def _():
            dg_ref[...] = jnp.zeros_like(dg_ref)

        dg_ref[...] += jnp.sum(do * xhat, axis=0, keepdims=True)

    row = pl.BlockSpec((tr, d), lambda i: (i, 0))
    vec = pl.BlockSpec((1, d), lambda i: (0, 0))
    return pl.pallas_call(
        body, name=name, grid=(s // tr,),
        in_specs=[row, vec, row] + ([row] if has_res else []),
        out_specs=[row, vec],
        out_shape=[jax.ShapeDtypeStruct((s, d), out_dtype), jax.ShapeDtypeStruct((1, d), F32)],
        compiler_params=_params(("arbitrary",)),
    )(x, g, dout, *([res] if has_res else []))


def _loss_head(y, target):
    s, d = y.shape
    tr = min(ROW_TILE, s)

    def body(y_ref, t_ref, dy_ref, sq_ref):
        i = pl.program_id(0)
        err = y_ref[...] - t_ref[...]
        dy_ref[...] = err * (1.0 / d)

        @pl.when(i == 0)
        def _():
            sq_ref[...] = jnp.zeros_like(sq_ref)

        sq_ref[...] += jnp.sum(err * err, axis=0, keepdims=True)

    row = pl.BlockSpec((tr, d), lambda i: (i, 0))
    vec = pl.BlockSpec((1, d), lambda i: (0, 0))
    return pl.pallas_call(
        body, name="loss_head", grid=(s // tr,),
        in_specs=[row, row], out_specs=[row, vec],
        out_shape=[jax.ShapeDtypeStruct((s, d), F32), jax.ShapeDtypeStruct((1, d), F32)],
        compiler_params=_params(("arbitrary",)),
    )(y, target)


def _sb_masks():
    row = lax.broadcasted_iota(jnp.int32, (CHUNK, CHUNK), 0)
    col = lax.broadcasted_iota(jnp.int32, (CHUNK, CHUNK), 1)
    return row, col


SB_WINDOW = 2 * CHUNK
SB_ROWS = CHUNK // 2
SB_FWD_QUERIES = 512
SB_BWD_QUERIES = 256

def _sb_consts():
    row = lax.broadcasted_iota(jnp.int32, (CHUNK, SB_WINDOW), 0)
    col = lax.broadcasted_iota(jnp.int32, (CHUNK, SB_WINDOW), 1)
    lane_hi = lax.broadcasted_iota(jnp.int32, (SB_ROWS, HEAD_PAIR), 1) >= GROUP
    return col - jnp.bitwise_and(row, SB_ROWS - 1), col, lane_hi


def _sb_window(t0, m, dcol, col):
    hi = t0 + SB_ROWS - SB_WINDOW * m
    start = jnp.maximum(hi - SB_WINDOW, 0)
    keep = jnp.logical_and(dcol < t0 - start, col < hi - start)
    return pl.multiple_of(start, SB_ROWS), keep


def _sb_stack(x, lane_hi):
    zero = jnp.zeros_like(x)
    return jnp.concatenate([jnp.where(lane_hi, zero, x), jnp.where(lane_hi, x, zero)], axis=0)


def _sb_unstack(y, lane_hi):
    return jnp.where(lane_hi, y[SB_ROWS:], y[:SB_ROWS])


def _sb_logs(z, keep):
    t = jnp.log(1.0 + jnp.exp(-jnp.abs(z)))
    ln = -(jnp.maximum(z, 0.0) + t)
    lb = ln + z
    ln = jnp.where(keep, ln, 0.0)
    return ln, lb


def _sb_scores(qm, kw, keep):
    return _sb_logs(_dot(qm, kw, NT), keep)


def _sb_tri(upper):
    r = np.arange(SB_WINDOW)
    m = (r[:, None] > r[None, :]) if upper else (r[:, None] < r[None, :])
    return jnp.asarray(m, jnp.bfloat16)


def _riding_exchange(exchange):
    if exchange is None:
        return [], [], [], []
    arrays, shard_shapes, scatter, names = exchange
    shapes = [jax.ShapeDtypeStruct(s, a.dtype)
              for s, a in zip(_exchange_out_shapes(shard_shapes, scatter, names), arrays)]
    return list(arrays), shapes, [pl.BlockSpec(memory_space=pl.ANY)] * len(arrays), _exchange_sems(len(arrays))


def _ride(exchange, ins, outs, sems, first, last):
    if exchange is None:
        return lambda: None
    start, finish = _exchange_plan(ins, outs, sems, exchange[1], exchange[2], exchange[3])
    pl.when(first)(start)
    return lambda: pl.when(last)(finish)


def _attn_fwd(qkv, exchange=None):
    s = qkv.shape[0]
    qb = min(SB_FWD_QUERIES, s)
    nblk = s // qb
    n_pairs = SB_W // HEAD_PAIR
    x_arrays, x_shapes, x_specs, x_sems = _riding_exchange(exchange)
    nx = len(x_arrays)

    def body(q_ref, k_ref, v_ref, up_ref, *refs):
        o_ref = refs[nx]
        i = pl.program_id(1)
        pair = pl.program_id(0)
        end_exchange = _ride(exchange, refs[:nx], refs[nx + 1:2 * nx + 1], refs[2 * nx + 1:],
                             jnp.logical_and(pair == 0, i == 0),
                             jnp.logical_and(pair == n_pairs - 1, i == nblk - 1))
        dcol, col, lane_hi = _sb_consts()
        halves = range(qb // SB_ROWS)
        t0s = [i * qb + r * SB_ROWS for r in halves]
        qs = [_sb_stack(q_ref[r * SB_ROWS:(r + 1) * SB_ROWS, :], lane_hi) * 0.125 for r in halves]

        def cond(st):
            return jnp.logical_and(st[0] * SB_WINDOW < (i + 1) * qb, st[1])

        def loop(st):
            m = st[0]
            wins = [_sb_window(t0s[r], m, dcol, col) for r in halves]
            kws = [k_ref[pl.ds(wins[r][0], SB_WINDOW), :] for r in halves]
            vws = [v_ref[pl.ds(wins[r][0], SB_WINDOW), :] for r in halves]
            keeps = [wins[r][1] for r in halves]
            zs = [_dot(qs[r], kws[r], NT) for r in halves]
            lnlb = [_sb_logs(zs[r], keeps[r]) for r in halves]
            his = [lnlb[r][0].astype(jnp.bfloat16) for r in halves]
            los = [(lnlb[r][0] - his[r].astype(F32)).astype(jnp.bfloat16) for r in halves]
            s_hi = [_dot(his[r], up_ref[...], NN) for r in halves]
            s_lo = [_dot(los[r], up_ref[...], NN) for r in halves]
            a = [jnp.where(keeps[r], jnp.exp(lnlb[r][1] + (s_hi[r] + s_lo[r] + st[2 + 2 * r])), 0.0).astype(MX)
                 for r in halves]
            new = []
            for r in halves:
                new += [st[2 + 2 * r] + jnp.sum(lnlb[r][0], axis=1, keepdims=True),
                        st[3 + 2 * r] + _dot(a[r], vws[r], NN)]
            active = functools.reduce(jnp.maximum, [jnp.max(c) for c in new[0::2]]) > SKIP_LOG
            return (m + 1, active, *new)

        c0 = jnp.zeros((CHUNK, 1), F32)
        a0 = jnp.zeros((CHUNK, HEAD_PAIR), F32)
        st = lax.while_loop(cond, loop, (i * 0, i >= 0) + (c0, a0) * len(halves))
        for r in halves:
            o_ref[r * SB_ROWS:(r + 1) * SB_ROWS, :] = _sb_unstack(st[3 + 2 * r], lane_hi).astype(o_ref.dtype)
        end_exchange()

    outs = pl.pallas_call(
        body, name="attn_fwd" if exchange is None else "attn_fwd_gather", grid=(n_pairs, nblk),
        in_specs=[pl.BlockSpec((qb, HEAD_PAIR), lambda p, i: (i, p)),
                  pl.BlockSpec((s, HEAD_PAIR), lambda p, i: (0, n_pairs + p)),
                  pl.BlockSpec((s, HEAD_PAIR), lambda p, i: (0, 2 * n_pairs + p)),
                  pl.BlockSpec((SB_WINDOW, SB_WINDOW), lambda p, i: (0, 0))] + x_specs,
        out_specs=[pl.BlockSpec((qb, HEAD_PAIR), lambda p, i: (i, p))] + x_specs,
        out_shape=[jax.ShapeDtypeStruct((s, SB_W), MX)] + x_shapes,
        scratch_shapes=x_sems,
        compiler_params=_params(("arbitrary", "arbitrary")),
    )(qkv, qkv, qkv, _sb_tri(True), *x_arrays)
    return outs[0], dict(zip(exchange[3] if exchange else (), outs[1:]))


def _attn_bwd(qkv, d_o, exchange=None):
    s = qkv.shape[0]
    qb = min(SB_BWD_QUERIES, s)
    nblk = s // qb
    n_pairs = SB_W // HEAD_PAIR
    n_win = s // SB_WINDOW
    x_arrays, x_shapes, x_specs, x_sems = _riding_exchange(exchange)
    nx = len(x_arrays)

    def body(q_ref, k_ref, v_ref, do_ref, up_ref, lo_ref, *refs):
        dq_ref, dk_ref, dv_ref = refs[nx:nx + 3]
        e_s = refs[2 * nx + 3]
        i = pl.program_id(1)
        pair = pl.program_id(0)
        end_exchange = _ride(exchange, refs[:nx], refs[nx + 3:2 * nx + 3], refs[2 * nx + 4:],
                             jnp.logical_and(pair == 0, i == 0),
                             jnp.logical_and(pair == n_pairs - 1, i == nblk - 1))
        dcol, col, lane_hi = _sb_consts()
        halves = range(qb // SB_ROWS)
        t0s = [i * qb + r * SB_ROWS for r in halves]
        qs = [_sb_stack(q_ref[r * SB_ROWS:(r + 1) * SB_ROWS, :], lane_hi) * 0.125 for r in halves]
        dos =[_sb_stack(do_ref[r * SB_ROWS:(r + 1) * SB_ROWS, :], lane_hi) for r in halves]

        @pl.when(i == 0)
        def _():
            dk_ref[...] = jnp.zeros_like(dk_ref)
            dv_ref[...] = jnp.zeros_like(dv_ref)

        def cond(st):
            return jnp.logical_and(st[0] * SB_WINDOW < (i + 1) * qb, st[1])

        def loop(st):
            m = st[0]
            wins = [_sb_window(t0s[r], m, dcol, col) for r in halves]
            kws = [k_ref[pl.ds(wins[r][0], SB_WINDOW), :] for r in halves]
            vws = [v_ref[pl.ds(wins[r][0], SB_WINDOW), :] for r in halves]
            keeps = [wins[r][1] for r in halves]
            zs = [_dot(qs[r], kws[r], NT) for r in halves]
            das = [_dot(dos[r], vws[r], NT) for r in halves]
            lnlb = [_sb_logs(zs[r], keeps[r]) for r in halves]
            his = [lnlb[r][0].astype(jnp.bfloat16) for r in halves]
            los = [(lnlb[r][0] - his[r].astype(F32)).astype(jnp.bfloat16) for r in halves]
            s_hi = [_dot(his[r], up_ref[...], NN) for r in halves]
            s_lo = [_dot(los[r], up_ref[...], NN) for r in halves]
            a = [jnp.where(keeps[r], jnp.exp(lnlb[r][1] + (s_hi[r] + s_lo[r] + st[2 + r])), 0.0) for r in halves]
            for r in halves:
                e_s[r, m] = a[r] * das[r]
            parts = [_dot(a[r].astype(MX), dos[r], TN) for r in halves]
            for r in halves:
                dv_ref[pl.ds(wins[r][0], SB_WINDOW), :] += parts[r]
            cs = [st[2 + r] + jnp.sum(lnlb[r][0], axis=1, keepdims=True) for r in halves]
            active = functools.reduce(jnp.maximum, [jnp.max(c) for c in cs]) > SKIP_LOG
            return (m + 1, active, *cs)

        c0 = jnp.zeros((CHUNK, 1), F32)
        n_seen = lax.while_loop(cond, loop, (i * 0, i >= 0) + (c0,) * len(halves))[0]

        def up(t, st):
            m = n_seen - 1 - t
            wins = [_sb_window(t0s[r], m, dcol, col) for r in halves]
            kws = [k_ref[pl.ds(wins[r][0], SB_WINDOW), :] for r in halves]
            es = [e_s[r, m] for r in halves]
            zs = [_dot(qs[r], kws[r], NT) for r in halves]
            his = [es[r].astype(jnp.bfloat16) for r in halves]
            los = [(es[r] - his[r].astype(F32)).astype(jnp.bfloat16) for r in halves]
            p_hi = [_dot(his[r], lo_ref[...], NN) for r in halves]
            p_lo = [_dot(los[r], lo_ref[...], NN) for r in halves]
            dzs = []
            for r in halves:
                u = jnp.exp(-jnp.abs(zs[r]))
                big = 1.0 / (1.0 + u)
                small = u * big
                pos = zs[r] >= 0.0
                prefix = p_hi[r] + p_lo[r] + st[2 * r]
                dz = es[r] * jnp.where(pos, small, big) - prefix * jnp.where(pos, big, small)
                dzs.append(jnp.where(wins[r][1], dz, 0.0).astype(MX))
            parts = [_dot(dzs[r], qs[r], TN) for r in halves]
            for r in halves:
                dk_ref[pl.ds(wins[r][0], SB_WINDOW), :] += parts[r]
            new = []
            for r in halves:
                new += [st[2 * r] + jnp.sum(es[r], axis=1, keepdims=True),
                        st[2 * r + 1] + _dot(dzs[r], kws[r], NN)]
            return tuple(new)

        a0 = jnp.zeros((CHUNK, HEAD_PAIR), F32)
        st = lax.fori_loop(0, n_seen, up, (c0, a0) * len(halves))
        for r in halves:
            dq_ref[r * SB_ROWS:(r + 1) * SB_ROWS, :] = (_sb_unstack(st[2 * r + 1], lane_hi) * 0.125).astype(dq_ref.dtype)
        end_exchange()

    blk = pl.BlockSpec((qb, HEAD_PAIR), lambda p, i: (i, p))
    full = pl.BlockSpec((s, HEAD_PAIR), lambda p, i: (0, p))
    tri = pl.BlockSpec((SB_WINDOW, SB_WINDOW), lambda p, i: (0, 0))
    outs = pl.pallas_call(
        body, name="attn_bwd" if exchange is None else "attn_bwd_scatter", grid=(n_pairs, nblk),
        in_specs=[blk,
                  pl.BlockSpec((s, HEAD_PAIR), lambda p, i: (0, n_pairs + p)),
                  pl.BlockSpec((s, HEAD_PAIR), lambda p, i: (0, 2 * n_pairs + p)),
                  blk, tri, tri] + x_specs,
        out_specs=[blk, full, full] + x_specs,
        out_shape=[jax.ShapeDtypeStruct((s, SB_W), MX), jax.ShapeDtypeStruct((s, SB_W), F32),
                   jax.ShapeDtypeStruct((s, SB_W), F32)] + x_shapes,
        scratch_shapes=[pltpu.VMEM((qb // SB_ROWS, n_win, CHUNK, SB_WINDOW), F32)] + x_sems,
        compiler_params=_params(("arbitrary", "arbitrary"), 56 * 1024 * 1024),
    )(qkv, qkv, qkv, d_o, _sb_tri(True), _sb_tri(False), *x_arrays)
    return outs[0], outs[1], outs[2], dict(zip(exchange[3] if exchange else (), outs[3:]))


POOL_TILE = 256
POOL_HALO = 128


def _pool_bands(tile, transpose):
    cur = np.zeros((N_GROUPS, tile, tile), np.float32)
    halo = np.zeros((N_GROUPS, tile, POOL_HALO), np.float32)
    t = np.arange(tile)[:, None]
    for g, w in enumerate(POOL_WINDOWS):
        if not transpose:
            p = np.arange(tile)[None, :]
            cur[g] = ((t - p >= 0) & (t - p < w))
            ph = np.arange(POOL_HALO)[None, :] - POOL_HALO
            halo[g] = (t - ph < w)
        else:
            p = np.arange(tile)[None, :]
            cur[g] = ((p - t >= 0) & (p - t < w))
            ph = np.arange(POOL_HALO)[None, :] + tile
            halo[g] = (ph - t < w)
    return jnp.asarray(cur, jnp.bfloat16), jnp.asarray(halo, jnp.bfloat16)


def _pool_count(i, tile, rows, row0):
    pos = (i * tile + row0 + lax.broadcasted_iota(jnp.int32, (rows, PW), 0)).astype(F32)
    grp = lax.broadcasted_iota(jnp.int32, (rows, PW), 1) // GROUP
    win = jnp.where(grp == 0, float(POOL_WINDOWS[0]),
                    jnp.where(grp == 1, float(POOL_WINDOWS[1]),
                              jnp.where(grp == 2, float(POOL_WINDOWS[2]), float(POOL_WINDOWS[3]))))
    return jnp.minimum(pos + 1.0, win), grp


def _pool_fwd(rest, wbd, scale):
    s = rest.shape[0]
    tile = min(POOL_TILE, s)
    halo_per_tile = tile // POOL_HALO
    bcur, bhalo = _pool_bands(tile, False)

    def body(cur_ref, prev_ref, bcur_ref, bhalo_ref, wbd_ref, scale_ref, pooled_ref, out_ref):
        i = pl.program_id(0)
        cur = cur_ref[...]
        prev = jnp.where(i > 0, prev_ref[...], 0.0)
        count, grp = _pool_count(i, tile, tile, 0)
        win = jnp.zeros((tile, PW), F32)
        for g in range(N_GROUPS):
            wsum = _split_dot(cur, bcur_ref[g], True, 3) + _split_dot(prev, bhalo_ref[g], True, 3)
            win = jnp.where(grp == g, wsum, win)
        pooled = (win / count - cur).astype(MX)
        pooled_ref[...] = pooled
        out_ref[...] = (_dot(pooled, wbd_ref[...], NN) * scale_ref[...]).astype(out_ref.dtype)

    tile_spec = pl.BlockSpec((tile, PW), lambda i: (i, 0))
    const3 = lambda shape: pl.BlockSpec(shape, lambda i: (0, 0, 0))
    const2 = lambda shape: pl.BlockSpec(shape, lambda i: (0, 0))
    return pl.pallas_call(
        body, name="pool_fwd", grid=(s // tile,),
        in_specs=[tile_spec,
                  pl.BlockSpec((POOL_HALO, PW), lambda i: (jnp.maximum(i * halo_per_tile - 1, 0), 0)),
                  const3(bcur.shape), const3(bhalo.shape), const2((PW, PW)), const2((1, PW))],
        out_specs=[tile_spec, tile_spec],
        out_shape=[jax.ShapeDtypeStruct((s, PW), MX), jax.ShapeDtypeStruct((s, PW), MX)],
        compiler_params=_params(("parallel",)),
    )(rest, rest, bcur, bhalo, wbd, scale)


def _pool_bwd(d_o, pooled, wbd, scale):
    s = d_o.shape[0]
    tile = min(POOL_TILE, s)
    halo_per_tile = tile // POOL_HALO
    n_halo = s // POOL_HALO
    n_tiles = s // tile
    ccur, chalo = _pool_bands(tile, True)

    def body(do_ref, nxt_ref, pooled_ref, ccur_ref, chalo_ref, wbd_ref, scale_ref, dp_ref, dw_ref, ds_ref):
        i = pl.program_id(0)
        do = do_ref[...]
        w = wbd_ref[...]
        sc = scale_ref[...]
        pooled_v = pooled_ref[...]

        @pl.when(i == 0)
        def _():
            dw_ref[...] = jnp.zeros_like(dw_ref)
            ds_ref[...] = jnp.zeros_like(ds_ref)

        ds_ref[...] += jnp.sum(do * _dot(pooled_v, w, NN), axis=0, keepdims=True)
        dmixed = (do * sc).astype(MX)
        dw_ref[...] += _dot(pooled_v, dmixed, TN)
        dpooled = _dot(dmixed, w, NT)
        nxt = jnp.where(i < n_tiles - 1, nxt_ref[...], 0.0)
        dpooled_n = _dot((nxt * sc).astype(MX), w, NT)
        count, grp = _pool_count(i, tile, tile, 0)
        count_n, _ = _pool_count(i, tile, POOL_HALO, tile)
        dq = dpooled / count
        dq_n = dpooled_n / count_n
        acc = jnp.zeros((tile, PW), F32)
        for g in range(N_GROUPS):
            wsum = _split_dot(dq, ccur_ref[g], True, 3) + _split_dot(dq_n, chalo_ref[g], True, 3)
            acc = jnp.where(grp == g, wsum, acc)
        dp_ref[...] = (acc - dpooled).astype(dp_ref.dtype)

    tile_spec = pl.BlockSpec((tile, PW), lambda i: (i, 0))
    const3 = lambda shape: pl.BlockSpec(shape, lambda i: (0, 0, 0))
    const2 = lambda shape: pl.BlockSpec(shape, lambda i: (0, 0))
    return pl.pallas_call(
        body, name="pool_bwd", grid=(n_tiles,),
        in_specs=[tile_spec,
                  pl.BlockSpec((POOL_HALO, PW), lambda i: (jnp.minimum((i + 1) * halo_per_tile, n_halo - 1), 0)),
                  tile_spec, const3(ccur.shape), const3(chalo.shape), const2((PW, PW)), const2((1, PW))],
        out_specs=[tile_spec, const2((PW, PW)), const2((1, PW))],
        out_shape=[jax.ShapeDtypeStruct((s, PW), MX), jax.ShapeDtypeStruct((PW, PW), F32),
                   jax.ShapeDtypeStruct((1, PW), F32)],
        compiler_params=_params(("arbitrary",)),
    )(d_o, d_o, pooled, ccur, chalo, wbd, scale)


GM_TILE = 512
GELU_C = 0.7978845608028654
GELU_A = 0.044715


def _gelu(x):
    return 0.5 * x * (1.0 + jnp.tanh(GELU_C * (x + GELU_A * (x * x * x))))


def _gelu_grad(x):
    th = jnp.tanh(GELU_C * (x + GELU_A * (x * x * x)))
    return 0.5 * (1.0 + th) + 0.5 * x * (1.0 - th * th) * (GELU_C * (1.0 + 3.0 * GELU_A * (x * x)))


def _gm_common(ws_ref):
    row, col = _sb_masks()
    tril = row >= col
    wsm = [jnp.where(tril, ws_ref[g], 0.0).astype(MX) for g in range(N_GROUPS)]
    grp = lax.broadcasted_iota(jnp.int32, (CHUNK, PW), 1) // GROUP
    return tril, wsm, grp


def _gm_mix(wsm, grp, vn_c):
    mixed = jnp.zeros((CHUNK, PW), F32)
    for g in range(N_GROUPS):
        mixed = jnp.where(grp == g, _dot(wsm[g], vn_c, NN), mixed)
    return mixed


def _gm_fwd(rest, gain, ws, bfull):
    s = rest.shape[0]
    tile = min(GM_TILE, s)

    def body(u_ref, v_ref, gain_ref, ws_ref, b_ref, out_ref):
        _, wsm, grp = _gm_common(ws_ref)
        bias = b_ref[...]
        for n in range(tile // CHUNK):
            rows = slice(n * CHUNK, (n + 1) * CHUNK)
            gu = _gelu(u_ref[rows, :])
            gv = _gelu(v_ref[rows, :])
            r = lax.rsqrt(jnp.mean(gv * gv, axis=-1, keepdims=True) + RMS_EPS)
            vn = (gv * r * gain_ref[...]).astype(MX)
            out_ref[rows, :] = (gu * (_gm_mix(wsm, grp, vn) + bias)).astype(out_ref.dtype)

    const2 = lambda shape: pl.BlockSpec(shape, lambda i: (0, 0))
    return pl.pallas_call(
        body, name="gm_fwd", grid=(s // tile,),
        in_specs=[pl.BlockSpec((tile, PW), lambda i: (i, 1)), pl.BlockSpec((tile, PW), lambda i: (i, 2)),
                  const2((1, PW)), pl.BlockSpec((N_GROUPS, CHUNK, CHUNK), lambda i: (0, 0, 0)),
                  const2((CHUNK, PW))],
        out_specs=pl.BlockSpec((tile, PW), lambda i: (i, 0)),
        out_shape=jax.ShapeDtypeStruct((s, PW), MX),
        compiler_params=_params(("parallel",)),
    )(rest, rest, gain, ws, bfull)


def _gm_bwd(d_o, rest, gain, ws, bfull):
    s = rest.shape[0]
    tile = min(GM_TILE, s)

    def body(do_ref, u_ref, v_ref, gain_ref, ws_ref, b_ref, du_ref, dv_ref, dws_ref, db_ref, dgain_ref):
        i = pl.program_id(0)
        tril, wsm, grp = _gm_common(ws_ref)
        bias = b_ref[...]
        gain_v = gain_ref[...]

        @pl.when(i == 0)
        def _():
            dws_ref[...] = jnp.zeros_like(dws_ref)
            db_ref[...] = jnp.zeros_like(db_ref)
            dgain_ref[...] = jnp.zeros_like(dgain_ref)

        for n in range(tile // CHUNK):
            rows = slice(n * CHUNK, (n + 1) * CHUNK)
            u = u_ref[rows, :]
            v = v_ref[rows, :]
            do = do_ref[rows, :]
            gu = _gelu(u)
            gv = _gelu(v)
            r = lax.rsqrt(jnp.mean(gv * gv, axis=-1, keepdims=True) + RMS_EPS)
            vhat = gv * r
            vn = (vhat * gain_v).astype(MX)
            mixed = _gm_mix(wsm, grp, vn)
            du_ref[rows, :] = (do * (mixed + bias) * _gelu_grad(u)).astype(du_ref.dtype)
            dmix = do * gu
            db_ref[...] += dmix
            dmix_mx = dmix.astype(MX)
            dvn = jnp.zeros((CHUNK, PW), F32)
            for g in range(N_GROUPS):
                dmg = jnp.where(grp == g, dmix_mx, jnp.zeros_like(dmix_mx))
                dws_ref[g] += jnp.where(tril, _dot(dmg, vn, NT), 0.0)
                dvn = jnp.where(grp == g, _dot(wsm[g], dmix_mx, TN), dvn)
            dgain_ref[...] += jnp.sum(dvn * vhat, axis=0, keepdims=True)
            dvhat = dvn * gain_v
            dgv = r * (dvhat - vhat * jnp.mean(dvhat * vhat, axis=-1, keepdims=True))
            dv_ref[rows, :] = (dgv * _gelu_grad(v)).astype(dv_ref.dtype)

    const2 = lambda shape: pl.BlockSpec(shape, lambda i: (0, 0))
    ws_spec = pl.BlockSpec((N_GROUPS, CHUNK, CHUNK), lambda i: (0, 0, 0))
    tile_spec = pl.BlockSpec((tile, PW), lambda i: (i, 0))
    return pl.pallas_call(
        body, name="gm_bwd", grid=(s // tile,),
        in_specs=[tile_spec, pl.BlockSpec((tile, PW), lambda i: (i, 1)), pl.BlockSpec((tile, PW), lambda i: (i, 2)),
                  const2((1, PW)), ws_spec, const2((CHUNK, PW))],
        out_specs=[tile_spec, tile_spec, ws_spec, const2((CHUNK, PW)), const2((1, PW))],
        out_shape=[jax.ShapeDtypeStruct((s, PW), MX), jax.ShapeDtypeStruct((s, PW), MX),
                   jax.ShapeDtypeStruct((N_GROUPS, CHUNK, CHUNK), F32),
                   jax.ShapeDtypeStruct((CHUNK, PW), F32), jax.ShapeDtypeStruct((1, PW), F32)],
        compiler_params=_params(("arbitrary",)),
    )(d_o, rest, rest, gain, ws, bfull)


GATE_TM = 2048
GATE_TN = 256


def _gate_specs(s):
    tm = min(GATE_TM, s)
    tn = GATE_TN
    gate0 = GATE_COL0 // tn
    per = D // tn
    ins = [pl.BlockSpec((tm, SB_W), lambda i, j: (i, 0)),
           pl.BlockSpec((tm, PW), lambda i, j: (i, 0)),
           pl.BlockSpec((tm, PW), lambda i, j: (i, 0)),
           pl.BlockSpec((SB_W, tn), lambda i, j: (0, j)),
           pl.BlockSpec((PW, tn), lambda i, j: (0, j)),
           pl.BlockSpec((PW, tn), lambda i, j: (0, j)),
           pl.BlockSpec((tm, tn), lambda i, j: (i, gate0 + j)),
           pl.BlockSpec((tm, tn), lambda i, j: (i, gate0 + per + j)),
           pl.BlockSpec((tm, tn), lambda i, j: (i, gate0 + 2 * per + j))]
    return tm, tn, ins, pl.BlockSpec((tm, tn), lambda i, j: (i, j))


def _gate_fwd(o_sb, o_pool, o_gm, w_sb, w_pool, w_gm, rest):
    s = rest.shape[0]
    tm, tn, ins, out = _gate_specs(s)

    def body(o0, o1, o2, w0, w1, w2, g0, g1, g2, merged_ref):
        acc = jax.nn.sigmoid(g0[...]) * _dot(o0[...], w0[...], NN)
        acc += jax.nn.sigmoid(g1[...]) * _dot(o1[...], w1[...], NN)
        acc += jax.nn.sigmoid(g2[...]) * _dot(o2[...], w2[...], NN)
        merged_ref[...] = acc.astype(merged_ref.dtype)

    return pl.pallas_call(
        body, name="gate_fwd", grid=(s // tm, D // tn),
        in_specs=ins, out_specs=out, out_shape=jax.ShapeDtypeStruct((s, D), MX),
        compiler_params=_params(("parallel", "parallel")),
    )(o_sb, o_pool, o_gm, w_sb, w_pool, w_gm, rest, rest, rest)


def _gate_bwd(o_sb, o_pool, o_gm, w_sb, w_pool, w_gm, rest, dmerged):
    s = rest.shape[0]
    tm, tn, ins, out = _gate_specs(s)

    def body(o0, o1, o2, w0, w1, w2, g0, g1, g2, dm_ref, db0, db1, db2, dg0, dg1, dg2):
        dm = dm_ref[...]
        for o, w, g, db, dg in ((o0, w0, g0, db0, dg0), (o1, w1, g1, db1, dg1), (o2, w2, g2, db2, dg2)):
            sg = jax.nn.sigmoid(g[...])
            db[...] = (dm * sg).astype(db.dtype)
            dg[...] = (dm * _dot(o[...], w[...], NN) * (sg * (1.0 - sg))).astype(dg.dtype)

    return pl.pallas_call(
        body, name="gate_bwd", grid=(s // tm, D // tn),
        in_specs=ins + [out], out_specs=[out] * 6,
        out_shape=[jax.ShapeDtypeStruct((s, D), MX)] * 6,
        compiler_params=_params(("parallel", "parallel")),
    )(o_sb, o_pool, o_gm, w_sb, w_pool, w_gm, rest, rest, rest, dmerged)


def _relu2(acc):
    r = jnp.maximum(acc, 0.0)
    return r * r, 2.0 * r


def _relu2_bwd(acc, slope):
    return (acc * slope.astype(F32),)


TILES = {
    "proj_qkv": (2048, W_IN_TILE, 1024), "proj_rest": (2048, W_IN_TILE, 1024),
    "out_proj": (2048, 1024, 1024), "ff_in": (2048, 1024, 1024), "ff_out": (1024, 1024, 4096),
    "ff_out_dx": (2048, 1024, 1024), "ff_out_dw": (1024, 1024, 2048),
    "ff_in_dx": (1024, 1024, 4096), "ff_in_dw": (1024, 2048, 1024),
    "out_proj_dx": (2048, 1024, 1024), "out_proj_dw": (1024, 1024, 2048),
    "br_sb_dx": (2048, 512, 1024), "br_sb_dw": (512, 1024, 2048),
    "br_pool_dx": (2048, 256, 1024), "br_pool_dw": (256, 1024, 2048),
    "br_gm_dx": (2048, 256, 1024), "br_gm_dw": (256, 1024, 2048),
    "proj_dx": (512, 1024, 5376), "proj_dw": (1024, W_IN_TILE, 2048),
}


def _mm(name, a, b, mode, out_dtypes, rides=None, got=None, **kw):
    tm, tn, tk = TILES[name]
    ride = rides.get(name) if rides else None
    res = _matmul(a, b, mode=mode, name=name, tm=tm, tn=tn, tk=tk, out_dtypes=out_dtypes, exchange=ride, **kw)
    if ride is None:
        return res
    got.update(res[-1])
    return res[0] if len(res) == 2 else res[:-1]


def _layer_fwd(x, w, rides=None):
    rides = rides or {}
    got = {}
    h = _rms_fwd(x, w["g_mix_pre"], res=None, out_dtype=MX, name="rms_mix_pre")
    own = {}
    qkv = _mm("proj_qkv", h, w["w_in"], "nn", (MX,), rides, own, n_out=QKV_W)
    rest = _mm("proj_rest", h, w["w_in"], "nn", (F32,), rides, own, b_col0=QKV_W, n_out=REST_W)
    w = {**w, **own}
    o_sb, arrived = _attn_fwd(qkv, rides.get("attn"))
    got.update(arrived)
    pooled, o_pool = _pool_fwd(rest, w["wbd"], w["pool_scale"])
    o_gm = _gm_fwd(rest, w["gm_gain"], w["w_spatial"], w["bfull"])
    merged = _gate_fwd(o_sb, o_pool, o_gm, w["w_br_sb"], w["w_br_pool"], w["w_br_gm"], rest)
    y = _mm("out_proj", merged, w["w_out"], "nn", (F32,))
    x1 = _rms_fwd(y, w["g_mix_post"], res=x, out_dtype=F32, name="rms_mix_post")
    h2 = _rms_fwd(x1, w["g_ff_pre"], res=None, out_dtype=MX, name="rms_ff_pre")
    r, r_slope = _mm("ff_in", h2, w["w_ff_in"], "nn", (MX, MX), rides, got, epilogue=_relu2)
    ff = _mm("ff_out", r, w["w_ff_out"], "nn", (F32,), rides, got)
    x2 = _rms_fwd(ff, w["g_ff_post"], res=x1, out_dtype=F32, name="rms_ff_post")
    saved = dict(x=x, h=h, qkv=qkv, rest=rest, o_sb=o_sb, pooled=pooled, o_pool=o_pool, o_gm=o_gm,
                 merged=merged, y=y, x1=x1, h2=h2, r=r, r_slope=r_slope, ff=ff)
    return x2, saved, got, own


def _layer_bwd(dx2, w, sv, rides=None, own_ff=None):
    rides = rides or {}
    got, own = {}, {}
    dff, dg_ff_post = _rms_bwd(sv["ff"], w["g_ff_post"], dx2, res=None, out_dtype=MX, name="rms_ff_post_bwd")
    da = _mm("ff_out_dx", dff, w["w_ff_out"], "nt", (MX,), rides, got, epilogue=_relu2_bwd, extras=(sv["r_slope"],))
    dw_ff_out = _mm("ff_out_dw", sv["r"], dff, "tn", (MX,))
    dh2 = _mm("ff_in_dx", da, w["w_ff_in"], "nt", (F32,), rides, got)
    dw_ff_in = _mm("ff_in_dw", sv["h2"], da, "tn", (MX,))
    dx1, dg_ff_pre = _rms_bwd(sv["x1"], w["g_ff_pre"], dh2, res=dx2, out_dtype=F32, name="rms_ff_pre_bwd")

    dy, dg_mix_post = _rms_bwd(sv["y"], w["g_mix_post"], dx1, res=None, out_dtype=MX, name="rms_mix_post_bwd")
    dmerged = _mm("out_proj_dx", dy, w["w_out"], "nt", (F32,))
    dw_out = _mm("out_proj_dw", sv["merged"], dy, "tn", (MX,))
    db_sb, db_pool, db_gm, dg0, dg1, dg2 = _gate_bwd(
        sv["o_sb"], sv["o_pool"], sv["o_gm"], w["w_br_sb"], w["w_br_pool"], w["w_br_gm"], sv["rest"], dmerged)
    do_sb = _mm("br_sb_dx", db_sb, w["w_br_sb"], "nt", (MX,))
    dw_br_sb = _mm("br_sb_dw", sv["o_sb"], db_sb, "tn", (MX,))
    do_pool = _mm("br_pool_dx", db_pool, w["w_br_pool"], "nt", (F32,))
    dw_br_pool = _mm("br_pool_dw", sv["o_pool"], db_pool, "tn", (MX,))
    do_gm = _mm("br_gm_dx", db_gm, w["w_br_gm"], "nt", (F32,))
    dw_br_gm = _mm("br_gm_dw", sv["o_gm"], db_gm, "tn", (MX,))

    du, dv_gm, dws, dbfull, dgain = _gm_bwd(do_gm, sv["rest"], w["gm_gain"], w["w_spatial"], w["bfull"])
    dp, dwbd, dscale = _pool_bwd(do_pool, sv["pooled"], w["wbd"], w["pool_scale"])
    dq, dk, dv, arrived = _attn_bwd(sv["qkv"], do_sb, rides.get("attn"))
    got.update(arrived)
    dproj = jnp.concatenate([dq, dk.astype(MX), dv.astype(MX), dp, du, dv_gm, dg0, dg1, dg2], axis=1)
    late = {} if own_ff is None else {"proj_dx": ([dw_ff_in], own_ff, True, ("w_ff_in",)),
                                      "proj_dw": ([dw_ff_out], own_ff, True, ("w_ff_out",))}
    dh = _mm("proj_dx", dproj, w["w_in"], "nt", (F32,), late, own)
    dw_in = _mm("proj_dw", sv["h"], dproj, "tn", (MX,), late, own)
    dx, dg_mix_pre = _rms_bwd(sv["x"], w["g_mix_pre"], dh, res=dx1, out_dtype=F32, name="rms_mix_pre_bwd")

    grads = dict(
        w_in=dw_in, w_br_sb=dw_br_sb, w_br_pool=dw_br_pool, w_br_gm=dw_br_gm, w_out=dw_out,
        w_ff_in=dw_ff_in, w_ff_out=dw_ff_out,
        w_pool=jnp.stack([dwbd[g * GROUP:(g + 1) * GROUP, g * GROUP:(g + 1) * GROUP] for g in range(N_GROUPS)]),
        pool_scale=dscale[0], gm_gain=dgain[0], w_spatial=dws,
        b_spatial=dbfull.reshape(CHUNK, N_GROUPS, GROUP).sum(axis=-1).T,
        g_mix_pre=dg_mix_pre[0], g_mix_post=dg_mix_post[0], g_ff_pre=dg_ff_pre[0], g_ff_post=dg_ff_post[0])
    return dx, grads, got, own


def _layer_operands(full_l, small_l):
    wbd = jnp.zeros((PW, PW), F32)
    for g in range(N_GROUPS):
        wbd = wbd.at[g * GROUP:(g + 1) * GROUP, g * GROUP:(g + 1) * GROUP].set(small_l["w_pool"][g])
    w = dict(full_l)
    w["wbd"] = wbd.astype(MX)
    w["bfull"] = jnp.repeat(small_l["b_spatial"].T, GROUP, axis=1)
    w["w_spatial"] = small_l["w_spatial"]
    for n in ("pool_scale", "gm_gain", "g_mix_pre", "g_mix_post", "g_ff_pre", "g_ff_post"):
        w[n] = small_l[n][None, :]
    return w


def _local_step(x, target, layers):
    saved = []
    for w in layers:
        x, sv, _, _ = _layer_fwd(x, w)
        saved.append(sv)
    dx, sq = _loss_head(x, target)
    loss = 0.5 * jnp.sum(sq) / x.shape[1]
    grads = [None] * len(layers)
    for l in reversed(range(len(layers))):
        dx, grads[l], _, _ = _layer_bwd(dx, layers[l], saved[l])
    return loss, dx, grads


def _mesh_place():
    x, y, c = lax.axis_index("x"), lax.axis_index("y"), lax.axis_index("c")

    def peer(k):
        px = 1 - x if k & 4 else x
        py = 1 - y if k & 2 else y
        pc = 1 - c if k & 1 else c
        return (px, py, pc), 4 * px + 2 * py + pc

    return 4 * x + 2 * y + c, peer


def _shard_window(ref, name, shard_shape, idx):
    if name in ("w_in", SMALL_GRADS):
        return ref.at[idx]
    a, b = shard_shape
    if BIG_SHARD_AXIS[name] == 2:
        return ref.at[:, pl.ds(pl.multiple_of(idx * b, b), b)]
    return ref.at[pl.ds(pl.multiple_of(idx * a, a), a), :]


def _full_shape(name, shard_shape):
    a, b = shard_shape
    if name == "w_in":
        return (N_DEV, a, b)
    return (a, N_DEV * b) if BIG_SHARD_AXIS[name] == 2 else (N_DEV * a, b)


def _exchange_out_shapes(shard_shapes, scatter, names):
    if scatter:
        return [(N_DEV,) + tuple(shard_shapes[nm]) for nm in names]
    return [_full_shape(nm, shard_shapes[nm]) for nm in names]


def _exchange_sems(n):
    return [pltpu.SemaphoreType.DMA((n, N_DEV - 1)), pltpu.SemaphoreType.DMA((n, N_DEV - 1)),
            pltpu.SemaphoreType.DMA((n,))]


def _exchange_plan(ins, outs, sems, shard_shapes, scatter, names):
    send_sems, recv_sems, local_sems = sems
    me, peer = _mesh_place()
    n = len(names)

    def window(ref, p, idx):
        return _shard_window(ref, names[p], shard_shapes[names[p]], idx)

    def copy(p, k, landing):
        dev, idx = peer(k)
        if names[p] == SMALL_GRADS:
            src, dst = ins[p], outs[p].at[idx if landing else me]
        elif scatter:
            src, dst = window(ins[p], p, idx), outs[p].at[idx if landing else me]
        else:
            src, dst = ins[p], window(outs[p], p, idx if landing else me)
        return pltpu.make_async_remote_copy(
            src_ref=src, dst_ref=dst, send_sem=send_sems.at[p, k - 1], recv_sem=recv_sems.at[p, k - 1],
            device_id=dev, device_id_type=pl.DeviceIdType.MESH)

    def local(p):
        if names[p] == SMALL_GRADS:
            return pltpu.make_async_copy(ins[p], outs[p].at[me], local_sems.at[p])
        if scatter:
            return pltpu.make_async_copy(window(ins[p], p, me), outs[p].at[me], local_sems.at[p])
        return pltpu.make_async_copy(ins[p], window(outs[p], p, me), local_sems.at[p])

    pairs = [(p, k) for p in range(n) for k in range(1, N_DEV)]

    def start():
        for p in range(n):
            local(p).start()
        for p, k in pairs:
            copy(p, k, False).start()

    def finish():
        for p, k in pairs:
            copy(p, k, True).wait_recv()
        for p, k in pairs:
            copy(p, k, False).wait_send()
        for p in range(n):
            local(p).wait()

    return start, finish


def _exchange_big(arrays, shard_shapes, *, scatter, name, names=BIG):
    n = len(names)

    def body(*refs):
        start, finish = _exchange_plan(refs[:n], refs[n:2 * n], refs[2 * n:], shard_shapes, scatter, names)
        start()
        finish()

    outs = pl.pallas_call(
        body, name=name,
        in_specs=[pl.BlockSpec(memory_space=pl.ANY)] * n,
        out_specs=[pl.BlockSpec(memory_space=pl.ANY)] * n,
        out_shape=[jax.ShapeDtypeStruct(s, a.dtype)
                   for s, a in zip(_exchange_out_shapes(shard_shapes, scatter, names), arrays)],
        scratch_shapes=_exchange_sems(n),
    )(*arrays)
    return dict(zip(names, outs))


def _sum_devices(recv, name):
    _, rows, cols = recv.shape
    tr = rows
    for cand in (512, 256, 128, 64, 32, 16, 8):
        if rows % cand == 0:
            tr = cand
            break

    def body(r_ref, out_ref):
        acc = r_ref[0].astype(F32)
        for d in range(1, N_DEV):
            acc = acc + r_ref[d].astype(F32)
        out_ref[...] = acc

    return pl.pallas_call(
        body, name=name, grid=(rows // tr,),
        in_specs=[pl.BlockSpec((N_DEV, tr, cols), lambda i: (0, i, 0))],
        out_specs=pl.BlockSpec((tr, cols), lambda i: (i, 0)),
        out_shape=jax.ShapeDtypeStruct((rows, cols), F32),
        compiler_params=_params(("parallel",)),
    )(recv)


def _adamw(w, g, m, v, name):
    rows, cols = w.shape
    tr = rows
    for cand in (512, 256, 128, 64, 32, 16, 8):
        if rows % cand == 0:
            tr = cand
            break
    c1 = 1.0 - ADAM_B1 ** ADAM_STEP
    c2 = 1.0 - ADAM_B2 ** ADAM_STEP

    def body(w_ref, g_ref, m_ref, v_ref, d_ref, nm_ref, nv_ref):
        gv = g_ref[...]
        nm = ADAM_B1 * m_ref[...] + (1.0 - ADAM_B1) * gv
        nv = ADAM_B2 * v_ref[...] + (1.0 - ADAM_B2) * (gv * gv)
        nm_ref[...] = nm
        nv_ref[...] = nv
        d_ref[...] = -ADAM_LR * ((nm / c1) / (jnp.sqrt(nv / c2) + ADAM_EPS) + ADAM_WD * w_ref[...])

    spec = pl.BlockSpec((tr, cols), lambda i: (i, 0))
    return pl.pallas_call(
        body, name=name, grid=(rows // tr,),
        in_specs=[spec] * 4, out_specs=[spec] * 3,
        out_shape=[jax.ShapeDtypeStruct((rows, cols), F32)] * 3,
        compiler_params=_params(("parallel",)),
    )(w, g, m, v)


def _pack_rows(flat, dtype):
    n = flat.shape[-1]
    rows = -(-n // PACK_COLS)
    rows = -(-rows // 16) * 16
    pad = rows * PACK_COLS - n
    flat = jnp.pad(flat, [(0, 0)] * (flat.ndim - 1) + [(0, pad)])
    return flat.reshape(flat.shape[:-1] + (rows, PACK_COLS)).astype(dtype)


def kernel(x, w_in, w_pool, pool_scale, gm_gain, w_spatial, b_spatial, w_br_sb, w_br_pool, w_br_gm, w_out, g_mix_pre, g_mix_post, g_ff_pre, g_ff_post, w_ff_in, w_ff_out, loss_target, m_w_in, m_w_pool, m_pool_scale, m_gm_gain, m_w_spatial, m_b_spatial, m_w_br_sb, m_w_br_pool, m_w_br_gm, m_w_out, m_g_mix_pre, m_g_mix_post, m_g_ff_pre, m_g_ff_post, m_w_ff_in, m_w_ff_out, v_w_in, v_w_pool, v_pool_scale, v_gm_gain, v_w_spatial, v_b_spatial, v_w_br_sb, v_w_br_pool, v_w_br_gm, v_w_out, v_g_mix_pre, v_g_mix_post, v_g_ff_pre, v_g_ff_post, v_w_ff_in, v_w_ff_out):
    args = dict(locals())
    weights = {n: args[n] for n in WEIGHTS}
    mom_m = {n: args["m_" + n] for n in WEIGHTS}
    mom_v = {n: args["v_" + n] for n in WEIGHTS}

    shard_shapes = {n: weights[n].shape[1:] for n in BIG}
    depth, d_model, w_in_cols = weights["w_in"].shape
    with_attn = ("w_in", "w_br_sb", "w_br_pool", "w_br_gm", "w_out")

    def gather_of(l, names):
        return [weights[n][l].astype(MX) for n in names], shard_shapes, False, names

    def as_operands(full, l):
        full = dict(full)
        full["w_in"] = jnp.transpose(full["w_in"], (1, 0, 2)).reshape(d_model, N_DEV * w_in_cols)
        return _layer_operands(full, {n: weights[n][l] for n in SMALL})

    full = _exchange_big(gather_of(0, with_attn)[0], shard_shapes, scatter=False, name="gather_weights",
                         names=with_attn)
    xc = x[0]
    layers, saved = [], []
    for l in range(depth):
        layers.append(as_operands(full, l))
        rides = {}
        if l == 0:
            rides.update({"proj_qkv": gather_of(0, ("w_ff_out",)), "proj_rest": gather_of(0, ("w_ff_in",))})
        if l + 1 < depth:
            rides.update({"attn": gather_of(l + 1, with_attn), "ff_in": gather_of(l + 1, ("w_ff_in",)),
                          "ff_out": gather_of(l + 1, ("w_ff_out",))})
        xc, sv, full, own = _layer_fwd(xc, layers[l], rides)
        layers[l].update(own)
        saved.append(sv)
    dx, sq = _loss_head(xc, loss_target[0])
    loss = lax.psum(0.5 * jnp.sum(sq) / d_model, ("x", "y", "c"))

    def scatter_of(g, names):
        arrays = [jnp.transpose(g[n].reshape(d_model, N_DEV, w_in_cols), (1, 0, 2)) if n == "w_in" else g[n]
                  for n in names]
        return arrays, shard_shapes, True, names

    layer_grads, recv = [None] * depth, [None] * depth
    for l in reversed(range(depth)):
        rides = None
        if l + 1 < depth:
            above = layer_grads[l + 1]
            rides = {"ff_out_dx": scatter_of(above, ("w_ff_in",)), "ff_in_dx": scatter_of(above, ("w_ff_out",)),
                     "attn": scatter_of(above, with_attn)}
        dx, layer_grads[l], arrived, own = _layer_bwd(dx, layers[l], saved[l], rides, shard_shapes if l == 0 else None)
        if rides is not None:
            recv[l + 1] = arrived
    small_sizes = [int(np.prod(weights[n].shape)) for n in SMALL]
    small_off = np.concatenate([[0], np.cumsum(small_sizes)])
    small_flat = jnp.concatenate([jnp.stack([g[n] for g in layer_grads]).reshape(-1) for n in SMALL])
    small_packed = _pack_rows(small_flat, F32)
    last = _exchange_big(scatter_of(layer_grads[0], with_attn)[0] + [small_packed],
                         {**shard_shapes, SMALL_GRADS: small_packed.shape}, scatter=True,
                         name="scatter_grads", names=with_attn + (SMALL_GRADS,))
    g_small = _sum_devices(last.pop(SMALL_GRADS), "sum_small_grads").reshape(-1)
    recv[0] = {**own, **last}
    grad_w = {n: jnp.stack([_sum_devices(recv[l][n], "sum_grads_" + n) for l in range(depth)]) for n in BIG}
    for n, lo, hi in zip(SMALL, small_off[:-1], small_off[1:]):
        grad_w[n] = g_small[lo:hi].reshape(weights[n].shape)

    delta, new_m, new_v = {}, {}, {}
    for n in WEIGHTS:
        shape = weights[n].shape
        view = (-1, shape[-1])
        d, nm, nv = _adamw(weights[n].reshape(view), grad_w[n].reshape(view), mom_m[n].reshape(view),
                           mom_v[n].reshape(view), "adamw_" + n)
        delta[n], new_m[n], new_v[n] = d.reshape(shape), nm.reshape(shape), nv.reshape(shape)

    return (loss, dx[None], *[grad_w[n] for n in WEIGHTS], *[delta[n] for n in WEIGHTS],
            *[new_m[n] for n in WEIGHTS], *[new_v[n] for n in WEIGHTS])
```

```python
import functools

import numpy as np
import jax
import jax.numpy as jnp
from jax import lax
from jax.experimental import pallas as pl
from jax.experimental.pallas import tpu as pltpu

F32 = jnp.float32
MX = jnp.bfloat16

D = 1024
SB_W = 512
HEAD_PAIR = 128
PW = 256
GROUP = 64
N_GROUPS = 4
CHUNK = 128
POOL_WINDOWS = (2, 4, 8, 16)
D_FF = 4096
D_IN = 5376
QKV_W = 3 * SB_W
REST_W = D_IN - QKV_W
GATE_COL0 = 3 * PW
RMS_EPS = 1e-6
N_DEV = 8
DEPTH = 4

ADAM_LR = 0.001
ADAM_B1 = 0.9
ADAM_B2 = 0.999
ADAM_EPS = 1e-08
ADAM_WD = 0.01
ADAM_STEP = 10

SKIP_LOG = -120.0

VMEM_LIMIT = 48 * 1024 * 1024
W_IN_TILE = 768
PACK_COLS = 1024

BIG = ("w_in", "w_br_sb", "w_br_pool", "w_br_gm", "w_out", "w_ff_in", "w_ff_out")
SMALL_GRADS = "small_grads"
BIG_SHARD_AXIS = {"w_in": 2, "w_br_sb": 2, "w_br_pool": 2, "w_br_gm": 2, "w_out": 1, "w_ff_in": 2, "w_ff_out": 1}
SMALL = ("w_pool", "pool_scale", "gm_gain", "w_spatial", "b_spatial",
         "g_mix_pre", "g_mix_post", "g_ff_pre", "g_ff_post")
WEIGHTS = ("w_in", "w_pool", "pool_scale", "gm_gain", "w_spatial", "b_spatial", "w_br_sb", "w_br_pool",
           "w_br_gm", "w_out", "g_mix_pre", "g_mix_post", "g_ff_pre", "g_ff_post", "w_ff_in", "w_ff_out")


def _params(sem, vmem=VMEM_LIMIT):
    return pltpu.CompilerParams(dimension_semantics=sem, vmem_limit_bytes=vmem)


def _dot(a, b, dims):
    return lax.dot_general(a, b, (dims, ((), ())), preferred_element_type=F32)


NN = ((1,), (0,))
NT = ((1,), (1,))
TN = ((0,), (0,))


def _split_dot(x, m, left, nsplit):
    acc = None
    r = x
    for s in range(nsplit):
        p = r.astype(jnp.bfloat16)
        d = _dot(m, p, NN) if left else _dot(p, m, NN)
        acc = d if acc is None else acc + d
        if s + 1 < nsplit:
            r = r - p.astype(F32)
    return acc


def _matmul(a, b, *, mode, name, tm, tn, tk, out_dtypes, b_col0=0, n_out=None, epilogue=None, extras=(),
            exchange=None):
    if mode == "nn":
        m, kdim = a.shape
        n = b.shape[1] if n_out is None else n_out
        dims = NN
    elif mode == "nt":
        m, kdim = a.shape
        n = b.shape[0]
        dims = NT
    else:
        kdim, m = a.shape
        n = b.shape[1]
        dims = TN
    tm, tn, tk = min(tm, m), min(tn, n), min(tk, kdim)
    assert m % tm == 0 and n % tn == 0 and kdim % tk == 0 and b_col0 % tn == 0, (name, m, n, kdim)
    if mode == "nn":
        a_spec = pl.BlockSpec((tm, tk), lambda i, j, k: (i, k))
        b_spec = pl.BlockSpec((tk, tn), lambda i, j, k: (k, j + b_col0 // tn))
    elif mode == "nt":
        a_spec = pl.BlockSpec((tm, tk), lambda i, j, k: (i, k))
        b_spec = pl.BlockSpec((tn, tk), lambda i, j, k: (j, k))
    else:
        a_spec = pl.BlockSpec((tk, tm), lambda i, j, k: (k, i))
        b_spec = pl.BlockSpec((tk, tn), lambda i, j, k: (k, j))
    nk = kdim // tk
    n_extra = len(extras)
    n_outs = len(out_dtypes)
    o_spec = pl.BlockSpec((tm, tn), lambda i, j, k: (i, j))
    x_arrays, x_shapes, x_specs, x_sems = _riding_exchange(exchange)
    nx = len(x_arrays)
    grid = (m // tm, n // tn, nk)

    def body(a_ref, b_ref, *refs):
        extra_refs = refs[:n_extra]
        out_refs = refs[n_extra + nx:n_extra + nx + n_outs]
        scratch = refs[n_extra + 2 * nx + n_outs:]
        ids = [pl.program_id(d) for d in range(3)]
        k = ids[2]
        end_exchange = _ride(exchange, refs[n_extra:n_extra + nx], refs[n_extra + nx + n_outs:n_extra + 2 * nx + n_outs],
                             scratch[1:] if nk > 1 else scratch,
                             functools.reduce(jnp.logical_and, [ids[d] == 0 for d in range(3)]),
                             functools.reduce(jnp.logical_and, [ids[d] == grid[d] - 1 for d in range(3)]))
        part = _dot(a_ref[...].astype(MX), b_ref[...].astype(MX), dims)

        def finish(acc):
            outs = (acc,) if epilogue is None else epilogue(acc, *[e[...] for e in extra_refs])
            for o_ref, val in zip(out_refs, outs):
                o_ref[...] = val.astype(o_ref.dtype)

        if nk == 1:
            finish(part)
        else:
            acc_ref = scratch[0]

            @pl.when(k == 0)
            def _():
                acc_ref[...] = part

            @pl.when(jnp.logical_and(k > 0, k < nk - 1))
            def _():
                acc_ref[...] += part

            @pl.when(k == nk - 1)
            def _():
                finish(acc_ref[...] + part)

        end_exchange()

    outs = pl.pallas_call(
        body, name=name if exchange is None else name + "_ride",
        grid=grid,
        in_specs=[a_spec, b_spec] + [o_spec] * n_extra + x_specs,
        out_specs=[o_spec] * n_outs + x_specs,
        out_shape=[jax.ShapeDtypeStruct((m, n), dt) for dt in out_dtypes] + x_shapes,
        scratch_shapes=([pltpu.VMEM((tm, tn), F32)] if nk > 1 else []) + x_sems,
        compiler_params=_params(("parallel", "parallel", "arbitrary") if exchange is None
                                else ("arbitrary", "arbitrary", "arbitrary")),
    )(a, b, *extras, *x_arrays)
    if exchange is not None:
        return tuple(outs[:n_outs]) + (dict(zip(exchange[3], outs[n_outs:])),)
    return outs[0] if n_outs == 1 else outs


ROW_TILE = 1024


def _rms_fwd(x, g, *, res, out_dtype, name):
    s, d = x.shape
    tr = min(ROW_TILE, s)
    has_res = res is not None

    def body(x_ref, g_ref, *refs):
        out_ref = refs[-1]
        xv = x_ref[...]
        y = xv * lax.rsqrt(jnp.mean(xv * xv, axis=-1, keepdims=True) + RMS_EPS) * g_ref[...]
        if has_res:
            y = refs[0][...] + y
        out_ref[...] = y.astype(out_ref.dtype)

    row = pl.BlockSpec((tr, d), lambda i: (i, 0))
    vec = pl.BlockSpec((1, d), lambda i: (0, 0))
    return pl.pallas_call(
        body, name=name, grid=(s // tr,),
        in_specs=[row, vec] + ([row] if has_res else []),
        out_specs=row, out_shape=jax.ShapeDtypeStruct((s, d), out_dtype),
        compiler_params=_params(("parallel",)),
    )(x, g, *([res] if has_res else []))


def _rms_bwd(x, g, dout, *, res, out_dtype, name):
    s, d = x.shape
    tr = min(ROW_TILE, s)
    has_res = res is not None

    def body(x_ref, g_ref, do_ref, *refs):
        dx_ref, dg_ref = refs[-2], refs[-1]
        i = pl.program_id(0)
        xv = x_ref[...]
        do = do_ref[...]
        r = lax.rsqrt(jnp.mean(xv * xv, axis=-1, keepdims=True) + RMS_EPS)
        xhat = xv * r
        dxhat = do * g_ref[...]
        dx = r * (dxhat - xhat * jnp.mean(dxhat * xhat, axis=-1, keepdims=True))
        if has_res:
            dx = refs[0][...] + dx
        dx_ref[...] = dx.astype(dx_ref.dtype)

        @pl.when(i == 0)
        def _():
            dg_ref[...] = jnp.zeros_like(dg_ref)

        dg_ref[...] += jnp.sum(do * xhat, axis=0, keepdims=True)

    row = pl.BlockSpec((tr, d), lambda i: (i, 0))
    vec = pl.BlockSpec((1, d), lambda i: (0, 0))
    return pl.pallas_call(
        body, name=name, grid=(s // tr,),
        in_specs=[row, vec, row] + ([row] if has_res else []),
        out_specs=[row, vec],
        out_shape=[jax.ShapeDtypeStruct((s, d), out_dtype), jax.ShapeDtypeStruct((1, d), F32)],
        compiler_params=_params(("arbitrary",)),
    )(x, g, dout, *([res] if has_res else []))


def _loss_head(y, target):
    s, d = y.shape
    tr = min(ROW_TILE, s)

    def body(y_ref, t_ref, dy_ref, sq_ref):
        i = pl.program_id(0)
        err = y_ref[...] - t_ref[...]
        dy_ref[...] = err * (1.0 / d)

        @pl.when(i == 0)
        def _():
            sq_ref[...] = jnp.zeros_like(sq_ref)

        sq_ref[...] += jnp.sum(err * err, axis=0, keepdims=True)

    row = pl.BlockSpec((tr, d), lambda i: (i, 0))
    vec = pl.BlockSpec((1, d), lambda i: (0, 0))
    return pl.pallas_call(
        body, name="loss_head", grid=(s // tr,),
        in_specs=[row, row], out_specs=[row, vec],
        out_shape=[jax.ShapeDtypeStruct((s, d), F32), jax.ShapeDtypeStruct((1, d), F32)],
        compiler_params=_params(("arbitrary",)),
    )(y, target)


def _sb_masks():
    row = lax.broadcasted_iota(jnp.int32, (CHUNK, CHUNK), 0)
    col = lax.broadcasted_iota(jnp.int32, (CHUNK, CHUNK), 1)
    return row, col


SB_WINDOW = 2 * CHUNK
SB_ROWS = CHUNK // 2
SB_FWD_QUERIES = 512
SB_BWD_QUERIES = 256

def _sb_consts():
    row = lax.broadcasted_iota(jnp.int32, (CHUNK, SB_WINDOW), 0)
    col = lax.broadcasted_iota(jnp.int32, (CHUNK, SB_WINDOW), 1)
    lane_hi = lax.broadcasted_iota(jnp.int32, (SB_ROWS, HEAD_PAIR), 1) >= GROUP
    return col - jnp.bitwise_and(row, SB_ROWS - 1), col, lane_hi


def _sb_window(t0, m, dcol, col):
    hi = t0 + SB_ROWS - SB_WINDOW * m
    start = jnp.maximum(hi - SB_WINDOW, 0)
    keep = jnp.logical_and(dcol < t0 - start, col < hi - start)
    return pl.multiple_of(start, SB_ROWS), keep


def _sb_stack(x, lane_hi):
    zero = jnp.zeros_like(x)
    return jnp.concatenate([jnp.where(lane_hi, zero, x), jnp.where(lane_hi, x, zero)], axis=0)


def _sb_unstack(y, lane_hi):
    return jnp.where(lane_hi, y[SB_ROWS:], y[:SB_ROWS])


def _sb_logs(z, keep):
    t = jnp.log(1.0 + jnp.exp(-jnp.abs(z)))
    ln = -(jnp.maximum(z, 0.0) + t)
    lb = ln + z
    ln = jnp.where(keep, ln, 0.0)
    return ln, lb


def _sb_scores(qm, kw, keep):
    return _sb_logs(_dot(qm, kw, NT), keep)


def _sb_tri(upper):
    r = np.arange(SB_WINDOW)
    m = (r[:, None] > r[None, :]) if upper else (r[:, None] < r[None, :])
    return jnp.asarray(m, jnp.bfloat16)


def _riding_exchange(exchange):
    if exchange is None:
        return [], [], [], []
    arrays, shard_shapes, scatter, names = exchange
    shapes = [jax.ShapeDtypeStruct(s, a.dtype)
              for s, a in zip(_exchange_out_shapes(shard_shapes, scatter, names), arrays)]
    return list(arrays), shapes, [pl.BlockSpec(memory_space=pl.ANY)] * len(arrays), _exchange_sems(len(arrays))


def _ride(exchange, ins, outs, sems, first, last):
    if exchange is None:
        return lambda: None
    start, finish = _exchange_plan(ins, outs, sems, exchange[1], exchange[2], exchange[3])
    pl.when(first)(start)
    return lambda: pl.when(last)(finish)


def _attn_fwd(qkv, exchange=None):
    s = qkv.shape[0]
    qb = min(SB_FWD_QUERIES, s)
    nblk = s // qb
    n_pairs = SB_W // HEAD_PAIR
    x_arrays, x_shapes, x_specs, x_sems = _riding_exchange(exchange)
    nx = len(x_arrays)

    def body(q_ref, k_ref, v_ref, up_ref, *refs):
        o_ref = refs[nx]
        i = pl.program_id(1)
        pair = pl.program_id(0)
        end_exchange = _ride(exchange, refs[:nx], refs[nx + 1:2 * nx + 1], refs[2 * nx + 1:],
                             jnp.logical_and(pair == 0, i == 0),
                             jnp.logical_and(pair == n_pairs - 1, i == nblk - 1))
        dcol, col, lane_hi = _sb_consts()
        halves = range(qb // SB_ROWS)
        t0s = [i * qb + r * SB_ROWS for r in halves]
        qs = [_sb_stack(q_ref[r * SB_ROWS:(r + 1) * SB_ROWS, :], lane_hi) * 0.125 for r in halves]

        def cond(st):
            return jnp.logical_and(st[0] * SB_WINDOW < (i + 1) * qb, st[1])

        def loop(st):
            m = st[0]
            wins = [_sb_window(t0s[r], m, dcol, col) for r in halves]
            kws = [k_ref[pl.ds(wins[r][0], SB_WINDOW), :] for r in halves]
            vws = [v_ref[pl.ds(wins[r][0], SB_WINDOW), :] for r in halves]
            keeps = [wins[r][1] for r in halves]
            zs = [_dot(qs[r], kws[r], NT) for r in halves]
            lnlb = [_sb_logs(zs[r], keeps[r]) for r in halves]
            his = [lnlb[r][0].astype(jnp.bfloat16) for r in halves]
            los = [(lnlb[r][0] - his[r].astype(F32)).astype(jnp.bfloat16) for r in halves]
            s_hi = [_dot(his[r], up_ref[...], NN) for r in halves]
            s_lo = [_dot(los[r], up_ref[...], NN) for r in halves]
            a = [jnp.where(keeps[r], jnp.exp(lnlb[r][1] + (s_hi[r] + s_lo[r] + st[2 + 2 * r])), 0.0).astype(MX)
                 for r in halves]
            new = []
            for r in halves:
                new += [st[2 + 2 * r] + jnp.sum(lnlb[r][0], axis=1, keepdims=True),
                        st[3 + 2 * r] + _dot(a[r], vws[r], NN)]
            active = functools.reduce(jnp.maximum, [jnp.max(c) for c in new[0::2]]) > SKIP_LOG
            return (m + 1, active, *new)

        c0 = jnp.zeros((CHUNK, 1), F32)
        a0 = jnp.zeros((CHUNK, HEAD_PAIR), F32)
        st = lax.while_loop(cond, loop, (i * 0, i >= 0) + (c0, a0) * len(halves))
        for r in halves:
            o_ref[r * SB_ROWS:(r + 1) * SB_ROWS, :] = _sb_unstack(st[3 + 2 * r], lane_hi).astype(o_ref.dtype)
        end_exchange()

    outs = pl.pallas_call(
        body, name="attn_fwd" if exchange is None else "attn_fwd_gather", grid=(n_pairs, nblk),
        in_specs=[pl.BlockSpec((qb, HEAD_PAIR), lambda p, i: (i, p)),
                  pl.BlockSpec((s, HEAD_PAIR), lambda p, i: (0, n_pairs + p)),
                  pl.BlockSpec((s, HEAD_PAIR), lambda p, i: (0, 2 * n_pairs + p)),
                  pl.BlockSpec((SB_WINDOW, SB_WINDOW), lambda p, i: (0, 0))] + x_specs,
        out_specs=[pl.BlockSpec((qb, HEAD_PAIR), lambda p, i: (i, p))] + x_specs,
        out_shape=[jax.ShapeDtypeStruct((s, SB_W), MX)] + x_shapes,
        scratch_shapes=x_sems,
        compiler_params=_params(("arbitrary", "arbitrary")),
    )(qkv, qkv, qkv, _sb_tri(True), *x_arrays)
    return outs[0], dict(zip(exchange[3] if exchange else (), outs[1:]))


def _attn_bwd(qkv, d_o, exchange=None):
    s = qkv.shape[0]
    qb = min(SB_BWD_QUERIES, s)
    nblk = s // qb
    n_pairs = SB_W // HEAD_PAIR
    n_win = s // SB_WINDOW
    x_arrays, x_shapes, x_specs, x_sems = _riding_exchange(exchange)
    nx = len(x_arrays)

    def body(q_ref, k_ref, v_ref, do_ref, up_ref, lo_ref, *refs):
        dq_ref, dk_ref, dv_ref = refs[nx:nx + 3]
        e_s = refs[2 * nx + 3]
        i = pl.program_id(1)
        pair = pl.program_id(0)
        end_exchange = _ride(exchange, refs[:nx], refs[nx + 3:2 * nx + 3], refs[2 * nx + 4:],
                             jnp.logical_and(pair == 0, i == 0),
                             jnp.logical_and(pair == n_pairs - 1, i == nblk - 1))
        dcol, col, lane_hi = _sb_consts()
        halves = range(qb // SB_ROWS)
        t0s = [i * qb + r * SB_ROWS for r in halves]
        qs = [_sb_stack(q_ref[r * SB_ROWS:(r + 1) * SB_ROWS, :], lane_hi) * 0.125 for r in halves]
        dos =[_sb_stack(do_ref[r * SB_ROWS:(r + 1) * SB_ROWS, :], lane_hi) for r in halves]

        @pl.when(i == 0)
        def _():
            dk_ref[...] = jnp.zeros_like(dk_ref)
            dv_ref[...] = jnp.zeros_like(dv_ref)

        def cond(st):
            return jnp.logical_and(st[0] * SB_WINDOW < (i + 1) * qb, st[1])

        def loop(st):
            m = st[0]
            wins = [_sb_window(t0s[r], m, dcol, col) for r in halves]
            kws = [k_ref[pl.ds(wins[r][0], SB_WINDOW), :] for r in halves]
            vws = [v_ref[pl.ds(wins[r][0], SB_WINDOW), :] for r in halves]
            keeps = [wins[r][1] for r in halves]
            zs = [_dot(qs[r], kws[r], NT) for r in halves]
            das = [_dot(dos[r], vws[r], NT) for r in halves]
            lnlb = [_sb_logs(zs[r], keeps[r]) for r in halves]
            his = [lnlb[r][0].astype(jnp.bfloat16) for r in halves]
            los = [(lnlb[r][0] - his[r].astype(F32)).astype(jnp.bfloat16) for r in halves]
            s_hi = [_dot(his[r], up_ref[...], NN) for r in halves]
            s_lo = [_dot(los[r], up_ref[...], NN) for r in halves]
            a = [jnp.where(keeps[r], jnp.exp(lnlb[r][1] + (s_hi[r] + s_lo[r] + st[2 + r])), 0.0) for r in halves]
            for r in halves:
                e_s[r, m] = a[r] * das[r]
            parts = [_dot(a[r].astype(MX), dos[r], TN) for r in halves]
            for r in halves:
                dv_ref[pl.ds(wins[r][0], SB_WINDOW), :] += parts[r]
            cs = [st[2 + r] + jnp.sum(lnlb[r][0], axis=1, keepdims=True) for r in halves]
            active = functools.reduce(jnp.maximum, [jnp.max(c) for c in cs]) > SKIP_LOG
            return (m + 1, active, *cs)

        c0 = jnp.zeros((CHUNK, 1), F32)
        n_seen = lax.while_loop(cond, loop, (i * 0, i >= 0) + (c0,) * len(halves))[0]

        def up(t, st):
            m = n_seen - 1 - t
            wins = [_sb_window(t0s[r], m, dcol, col) for r in halves]
            kws = [k_ref[pl.ds(wins[r][0], SB_WINDOW), :] for r in halves]
            es = [e_s[r, m] for r in halves]
            zs = [_dot(qs[r], kws[r], NT) for r in halves]
            his = [es[r].astype(jnp.bfloat16) for r in halves]
            los = [(es[r] - his[r].astype(F32)).astype(jnp.bfloat16) for r in halves]
            p_hi = [_dot(his[r], lo_ref[...], NN) for r in halves]
            p_lo = [_dot(los[r], lo_ref[...], NN) for r in halves]
            dzs = []
            for r in halves:
                u = jnp.exp(-jnp.abs(zs[r]))
                big = 1.0 / (1.0 + u)
                small = u * big
                pos = zs[r] >= 0.0
                prefix = p_hi[r] + p_lo[r] + st[2 * r]
                dz = es[r] * jnp.where(pos, small, big) - prefix * jnp.where(pos, big, small)
                dzs.append(jnp.where(wins[r][1], dz, 0.0).astype(MX))
            parts = [_dot(dzs[r], qs[r], TN) for r in halves]
            for r in halves:
                dk_ref[pl.ds(wins[r][0], SB_WINDOW), :] += parts[r]
            new = []
            for r in halves:
                new += [st[2 * r] + jnp.sum(es[r], axis=1, keepdims=True),
                        st[2 * r + 1] + _dot(dzs[r], kws[r], NN)]
            return tuple(new)

        a0 = jnp.zeros((CHUNK, HEAD_PAIR), F32)
        st = lax.fori_loop(0, n_seen, up, (c0, a0) * len(halves))
        for r in halves:
            dq_ref[r * SB_ROWS:(r + 1) * SB_ROWS, :] = (_sb_unstack(st[2 * r + 1], lane_hi) * 0.125).astype(dq_ref.dtype)
        end_exchange()

    blk = pl.BlockSpec((qb, HEAD_PAIR), lambda p, i: (i, p))
    full = pl.BlockSpec((s, HEAD_PAIR), lambda p, i: (0, p))
    tri = pl.BlockSpec((SB_WINDOW, SB_WINDOW), lambda p, i: (0, 0))
    outs = pl.pallas_call(
        body, name="attn_bwd" if exchange is None else "attn_bwd_scatter", grid=(n_pairs, nblk),
        in_specs=[blk,
                  pl.BlockSpec((s, HEAD_PAIR), lambda p, i: (0, n_pairs + p)),
                  pl.BlockSpec((s, HEAD_PAIR), lambda p, i: (0, 2 * n_pairs + p)),
                  blk, tri, tri] + x_specs,
        out_specs=[blk, full, full] + x_specs,
        out_shape=[jax.ShapeDtypeStruct((s, SB_W), MX), jax.ShapeDtypeStruct((s, SB_W), F32),
                   jax.ShapeDtypeStruct((s, SB_W), F32)] + x_shapes,
        scratch_shapes=[pltpu.VMEM((qb // SB_ROWS, n_win, CHUNK, SB_WINDOW), F32)] + x_sems,
        compiler_params=_params(("arbitrary", "arbitrary"), 56 * 1024 * 1024),
    )(qkv, qkv, qkv, d_o, _sb_tri(True), _sb_tri(False), *x_arrays)
    return outs[0], outs[1], outs[2], dict(zip(exchange[3] if exchange else (), outs[3:]))


POOL_TILE = 256
POOL_HALO = 128


def _pool_bands(tile, transpose):
    cur = np.zeros((N_GROUPS, tile, tile), np.float32)
    halo = np.zeros((N_GROUPS, tile, POOL_HALO), np.float32)
    t = np.arange(tile)[:, None]
    for g, w in enumerate(POOL_WINDOWS):
        if not transpose:
            p = np.arange(tile)[None, :]
            cur[g] = ((t - p >= 0) & (t - p < w))
            ph = np.arange(POOL_HALO)[None, :] - POOL_HALO
            halo[g] = (t - ph < w)
        else:
            p = np.arange(tile)[None, :]
            cur[g] = ((p - t >= 0) & (p - t < w))
            ph = np.arange(POOL_HALO)[None, :] + tile
            halo[g] = (ph - t < w)
    return jnp.asarray(cur, jnp.bfloat16), jnp.asarray(halo, jnp.bfloat16)


def _pool_count(i, tile, rows, row0):
    pos = (i * tile + row0 + lax.broadcasted_iota(jnp.int32, (rows, PW), 0)).astype(F32)
    grp = lax.broadcasted_iota(jnp.int32, (rows, PW), 1) // GROUP
    win = jnp.where(grp == 0, float(POOL_WINDOWS[0]),
                    jnp.where(grp == 1, float(POOL_WINDOWS[1]),
                              jnp.where(grp == 2, float(POOL_WINDOWS[2]), float(POOL_WINDOWS[3]))))
    return jnp.minimum(pos + 1.0, win), grp


def _pool_fwd(rest, wbd, scale):
    s = rest.shape[0]
    tile = min(POOL_TILE, s)
    halo_per_tile = tile // POOL_HALO
    bcur, bhalo = _pool_bands(tile, False)

    def body(cur_ref, prev_ref, bcur_ref, bhalo_ref, wbd_ref, scale_ref, pooled_ref, out_ref):
        i = pl.program_id(0)
        cur = cur_ref[...]
        prev = jnp.where(i > 0, prev_ref[...], 0.0)
        count, grp = _pool_count(i, tile, tile, 0)
        win = jnp.zeros((tile, PW), F32)
        for g in range(N_GROUPS):
            wsum = _split_dot(cur, bcur_ref[g], True, 3) + _split_dot(prev, bhalo_ref[g], True, 3)
            win = jnp.where(grp == g, wsum, win)
        pooled = (win / count - cur).astype(MX)
        pooled_ref[...] = pooled
        out_ref[...] = (_dot(pooled, wbd_ref[...], NN) * scale_ref[...]).astype(out_ref.dtype)

    tile_spec = pl.BlockSpec((tile, PW), lambda i: (i, 0))
    const3 = lambda shape: pl.BlockSpec(shape, lambda i: (0, 0, 0))
    const2 = lambda shape: pl.BlockSpec(shape, lambda i: (0, 0))
    return pl.pallas_call(
        body, name="pool_fwd", grid=(s // tile,),
        in_specs=[tile_spec,
                  pl.BlockSpec((POOL_HALO, PW), lambda i: (jnp.maximum(i * halo_per_tile - 1, 0), 0)),
                  const3(bcur.shape), const3(bhalo.shape), const2((PW, PW)), const2((1, PW))],
        out_specs=[tile_spec, tile_spec],
        out_shape=[jax.ShapeDtypeStruct((s, PW), MX), jax.ShapeDtypeStruct((s, PW), MX)],
        compiler_params=_params(("parallel",)),
    )(rest, rest, bcur, bhalo, wbd, scale)


def _pool_bwd(d_o, pooled, wbd, scale):
    s = d_o.shape[0]
    tile = min(POOL_TILE, s)
    halo_per_tile = tile // POOL_HALO
    n_halo = s // POOL_HALO
    n_tiles = s // tile
    ccur, chalo = _pool_bands(tile, True)

    def body(do_ref, nxt_ref, pooled_ref, ccur_ref, chalo_ref, wbd_ref, scale_ref, dp_ref, dw_ref, ds_ref):
        i = pl.program_id(0)
        do = do_ref[...]
        w = wbd_ref[...]
        sc = scale_ref[...]
        pooled_v = pooled_ref[...]

        @pl.when(i == 0)
        def _():
            dw_ref[...] = jnp.zeros_like(dw_ref)
            ds_ref[...] = jnp.zeros_like(ds_ref)

        ds_ref[...] += jnp.sum(do * _dot(pooled_v, w, NN), axis=0, keepdims=True)
        dmixed = (do * sc).astype(MX)
        dw_ref[...] += _dot(pooled_v, dmixed, TN)
        dpooled = _dot(dmixed, w, NT)
        nxt = jnp.where(i < n_tiles - 1, nxt_ref[...], 0.0)
        dpooled_n = _dot((nxt * sc).astype(MX), w, NT)
        count, grp = _pool_count(i, tile, tile, 0)
        count_n, _ = _pool_count(i, tile, POOL_HALO, tile)
        dq = dpooled / count
        dq_n = dpooled_n / count_n
        acc = jnp.zeros((tile, PW), F32)
        for g in range(N_GROUPS):
            wsum = _split_dot(dq, ccur_ref[g], True, 3) + _split_dot(dq_n, chalo_ref[g], True, 3)
            acc = jnp.where(grp == g, wsum, acc)
        dp_ref[...] = (acc - dpooled).astype(dp_ref.dtype)

    tile_spec = pl.BlockSpec((tile, PW), lambda i: (i, 0))
    const3 = lambda shape: pl.BlockSpec(shape, lambda i: (0, 0, 0))
    const2 = lambda shape: pl.BlockSpec(shape, lambda i: (0, 0))
    return pl.pallas_call(
        body, name="pool_bwd", grid=(n_tiles,),
        in_specs=[tile_spec,
                  pl.BlockSpec((POOL_HALO, PW), lambda i: (jnp.minimum((i + 1) * halo_per_tile, n_halo - 1), 0)),
                  tile_spec, const3(ccur.shape), const3(chalo.shape), const2((PW, PW)), const2((1, PW))],
        out_specs=[tile_spec, const2((PW, PW)), const2((1, PW))],
        out_shape=[jax.ShapeDtypeStruct((s, PW), MX), jax.ShapeDtypeStruct((PW, PW), F32),
                   jax.ShapeDtypeStruct((1, PW), F32)],
        compiler_params=_params(("arbitrary",)),
    )(d_o, d_o, pooled, ccur, chalo, wbd, scale)


GM_TILE = 512
GELU_C = 0.7978845608028654
GELU_A = 0.044715


def _gelu(x):
    return 0.5 * x * (1.0 + jnp.tanh(GELU_C * (x + GELU_A * (x * x * x))))


def _gelu_grad(x):
    th = jnp.tanh(GELU_C * (x + GELU_A * (x * x * x)))
    return 0.5 * (1.0 + th) + 0.5 * x * (1.0 - th * th) * (GELU_C * (1.0 + 3.0 * GELU_A * (x * x)))


def _gm_common(ws_ref):
    row, col = _sb_masks()
    tril = row >= col
    wsm = [jnp.where(tril, ws_ref[g], 0.0).astype(MX) for g in range(N_GROUPS)]
    grp = lax.broadcasted_iota(jnp.int32, (CHUNK, PW), 1) // GROUP
    return tril, wsm, grp


def _gm_mix(wsm, grp, vn_c):
    mixed = jnp.zeros((CHUNK, PW), F32)
    for g in range(N_GROUPS):
        mixed = jnp.where(grp == g, _dot(wsm[g], vn_c, NN), mixed)
    return mixed


def _gm_fwd(rest, gain, ws, bfull):
    s = rest.shape[0]
    tile = min(GM_TILE, s)

    def body(u_ref, v_ref, gain_ref, ws_ref, b_ref, out_ref):
        _, wsm, grp = _gm_common(ws_ref)
        bias = b_ref[...]
        for n in range(tile // CHUNK):
            rows = slice(n * CHUNK, (n + 1) * CHUNK)
            gu = _gelu(u_ref[rows, :])
            gv = _gelu(v_ref[rows, :])
            r = lax.rsqrt(jnp.mean(gv * gv, axis=-1, keepdims=True) + RMS_EPS)
            vn = (gv * r * gain_ref[...]).astype(MX)
            out_ref[rows, :] = (gu * (_gm_mix(wsm, grp, vn) + bias)).astype(out_ref.dtype)

    const2 = lambda shape: pl.BlockSpec(shape, lambda i: (0, 0))
    return pl.pallas_call(
        body, name="gm_fwd", grid=(s // tile,),
        in_specs=[pl.BlockSpec((tile, PW), lambda i: (i, 1)), pl.BlockSpec((tile, PW), lambda i: (i, 2)),
                  const2((1, PW)), pl.BlockSpec((N_GROUPS, CHUNK, CHUNK), lambda i: (0, 0, 0)),
                  const2((CHUNK, PW))],
        out_specs=pl.BlockSpec((tile, PW), lambda i: (i, 0)),
        out_shape=jax.ShapeDtypeStruct((s, PW), MX),
        compiler_params=_params(("parallel",)),
    )(rest, rest, gain, ws, bfull)


def _gm_bwd(d_o, rest, gain, ws, bfull):
    s = rest.shape[0]
    tile = min(GM_TILE, s)

    def body(do_ref, u_ref, v_ref, gain_ref, ws_ref, b_ref, du_ref, dv_ref, dws_ref, db_ref, dgain_ref):
        i = pl.program_id(0)
        tril, wsm, grp = _gm_common(ws_ref)
        bias = b_ref[...]
        gain_v = gain_ref[...]

        @pl.when(i == 0)
        def _():
            dws_ref[...] = jnp.zeros_like(dws_ref)
            db_ref[...] = jnp.zeros_like(db_ref)
            dgain_ref[...] = jnp.zeros_like(dgain_ref)

        for n in range(tile // CHUNK):
            rows = slice(n * CHUNK, (n + 1) * CHUNK)
            u = u_ref[rows, :]
            v = v_ref[rows, :]
            do = do_ref[rows, :]
            gu = _gelu(u)
            gv = _gelu(v)
            r = lax.rsqrt(jnp.mean(gv * gv, axis=-1, keepdims=True) + RMS_EPS)
            vhat = gv * r
            vn = (vhat * gain_v).astype(MX)
            mixed = _gm_mix(wsm, grp, vn)
            du_ref[rows, :] = (do * (mixed + bias) * _gelu_grad(u)).astype(du_ref.dtype)
            dmix = do * gu
            db_ref[...] += dmix
            dmix_mx = dmix.astype(MX)
            dvn = jnp.zeros((CHUNK, PW), F32)
            for g in range(N_GROUPS):
                dmg = jnp.where(grp == g, dmix_mx, jnp.zeros_like(dmix_mx))
                dws_ref[g] += jnp.where(tril, _dot(dmg, vn, NT), 0.0)
                dvn = jnp.where(grp == g, _dot(wsm[g], dmix_mx, TN), dvn)
            dgain_ref[...] += jnp.sum(dvn * vhat, axis=0, keepdims=True)
            dvhat = dvn * gain_v
            dgv = r * (dvhat - vhat * jnp.mean(dvhat * vhat, axis=-1, keepdims=True))
            dv_ref[rows, :] = (dgv * _gelu_grad(v)).astype(dv_ref.dtype)

    const2 = lambda shape: pl.BlockSpec(shape, lambda i: (0, 0))
    ws_spec = pl.BlockSpec((N_GROUPS, CHUNK, CHUNK), lambda i: (0, 0, 0))
    tile_spec = pl.BlockSpec((tile, PW), lambda i: (i, 0))
    return pl.pallas_call(
        body, name="gm_bwd", grid=(s // tile,),
        in_specs=[tile_spec, pl.BlockSpec((tile, PW), lambda i: (i, 1)), pl.BlockSpec((tile, PW), lambda i: (i, 2)),
                  const2((1, PW)), ws_spec, const2((CHUNK, PW))],
        out_specs=[tile_spec, tile_spec, ws_spec, const2((CHUNK, PW)), const2((1, PW))],
        out_shape=[jax.ShapeDtypeStruct((s, PW), MX), jax.ShapeDtypeStruct((s, PW), MX),
                   jax.ShapeDtypeStruct((N_GROUPS, CHUNK, CHUNK), F32),
                   jax.ShapeDtypeStruct((CHUNK, PW), F32), jax.ShapeDtypeStruct((1, PW), F32)],
        compiler_params=_params(("arbitrary",)),
    )(d_o, rest, rest, gain, ws, bfull)


GATE_TM = 2048
GATE_TN = 256


def _gate_specs(s):
    tm = min(GATE_TM, s)
    tn = GATE_TN
    gate0 = 0
    per = D // tn
    ins = [pl.BlockSpec((tm, SB_W), lambda i, j: (i, 0)),
           pl.BlockSpec((tm, PW), lambda i, j: (i, 0)),
           pl.BlockSpec((tm, PW), lambda i, j: (i, 0)),
           pl.BlockSpec((SB_W, tn), lambda i, j: (0, j)),
           pl.BlockSpec((PW, tn), lambda i, j: (0, j)),
           pl.BlockSpec((PW, tn), lambda i, j: (0, j)),
           pl.BlockSpec((tm, tn), lambda i, j: (i, gate0 + j)),
           pl.BlockSpec((tm, tn), lambda i, j: (i, gate0 + per + j)),
           pl.BlockSpec((tm, tn), lambda i, j: (i, gate0 + 2 * per + j))]
    return tm, tn, ins, pl.BlockSpec((tm, tn), lambda i, j: (i, j))


def _gate_fwd(o_sb, o_pool, o_gm, w_sb, w_pool, w_gm, rest):
    s = rest.shape[0]
    tm, tn, ins, out = _gate_specs(s)

    def body(o0, o1, o2, w0, w1, w2, g0, g1, g2, merged_ref):
        acc = jax.nn.sigmoid(g0[...].astype(F32)) * _dot(o0[...], w0[...], NN)
        acc += jax.nn.sigmoid(g1[...].astype(F32)) * _dot(o1[...], w1[...], NN)
        acc += jax.nn.sigmoid(g2[...].astype(F32)) * _dot(o2[...], w2[...], NN)
        merged_ref[...] = acc.astype(merged_ref.dtype)

    return pl.pallas_call(
        body, name="gate_fwd", grid=(s // tm, D // tn),
        in_specs=ins, out_specs=out, out_shape=jax.ShapeDtypeStruct((s, D), MX),
        compiler_params=_params(("parallel", "parallel")),
    )(o_sb, o_pool, o_gm, w_sb, w_pool, w_gm, rest, rest, rest)


def _gate_bwd(o_sb, o_pool, o_gm, w_sb, w_pool, w_gm, rest, dmerged):
    s = rest.shape[0]
    tm, tn, ins, out = _gate_specs(s)

    def body(o0, o1, o2, w0, w1, w2, g0, g1, g2, dm_ref, db0, db1, db2, dg0, dg1, dg2):
        dm = dm_ref[...]
        for o, w, g, db, dg in ((o0, w0, g0, db0, dg0), (o1, w1, g1, db1, dg1), (o2, w2, g2, db2, dg2)):
            sg = jax.nn.sigmoid(g[...].astype(F32))
            db[...] = (dm * sg).astype(db.dtype)
            dg[...] = (dm * _dot(o[...], w[...], NN) * (sg * (1.0 - sg))).astype(dg.dtype)

    return pl.pallas_call(
        body, name="gate_bwd", grid=(s // tm, D // tn),
        in_specs=ins + [out], out_specs=[out] * 6,
        out_shape=[jax.ShapeDtypeStruct((s, D), MX)] * 6,
        compiler_params=_params(("parallel", "parallel")),
    )(o_sb, o_pool, o_gm, w_sb, w_pool, w_gm, rest, rest, rest, dmerged)


def _relu2(acc):
    r = jnp.maximum(acc, 0.0)
    return r * r, 2.0 * r


def _relu2_bwd(acc, slope):
    return (acc * slope.astype(F32),)


TILES = {
    "proj_qkv": (2048, W_IN_TILE, 1024), "proj_rest": (2048, W_IN_TILE, 1024), "proj_gates": (2048, W_IN_TILE, 1024),
    "out_proj": (2048, 1024, 1024), "ff_in": (2048, 1024, 1024), "ff_out": (1024, 1024, 4096),
    "ff_out_dx": (2048, 1024, 1024), "ff_out_dw": (1024, 1024, 2048),
    "ff_in_dx": (1024, 1024, 4096), "ff_in_dw": (1024, 2048, 1024),
    "out_proj_dx": (2048, 1024, 1024), "out_proj_dw": (1024, 1024, 2048),
    "br_sb_dx": (2048, 512, 1024), "br_sb_dw": (512, 1024, 2048),
    "br_pool_dx": (2048, 256, 1024), "br_pool_dw": (256, 1024, 2048),
    "br_gm_dx": (2048, 256, 1024), "br_gm_dw": (256, 1024, 2048),
    "proj_dx": (512, 1024, 5376), "proj_dw": (1024, W_IN_TILE, 2048),
}


def _mm(name, a, b, mode, out_dtypes, rides=None, got=None, **kw):
    tm, tn, tk = TILES[name]
    ride = rides.get(name) if rides else None
    res = _matmul(a, b, mode=mode, name=name, tm=tm, tn=tn, tk=tk, out_dtypes=out_dtypes, exchange=ride, **kw)
    if ride is None:
        return res
    got.update(res[-1])
    return res[0] if len(res) == 2 else res[:-1]


def _layer_fwd(x, w, rides=None):
    rides = rides or {}
    got = {}
    h = _rms_fwd(x, w["g_mix_pre"], res=None, out_dtype=MX, name="rms_mix_pre")
    own = {}
    qkv = _mm("proj_qkv", h, w["w_in"], "nn", (MX,), rides, own, n_out=QKV_W)
    rest = _mm("proj_rest", h, w["w_in"], "nn", (F32,), b_col0=QKV_W, n_out=GATE_COL0)
    gates = _mm("proj_gates", h, w["w_in"], "nn", (MX,), rides, own, b_col0=QKV_W + GATE_COL0, n_out=REST_W - GATE_COL0)
    w = {**w, **own}
    o_sb, arrived = _attn_fwd(qkv, rides.get("attn"))
    got.update(arrived)
    pooled, o_pool = _pool_fwd(rest, w["wbd"], w["pool_scale"])
    o_gm = _gm_fwd(rest, w["gm_gain"], w["w_spatial"], w["bfull"])
    merged = _gate_fwd(o_sb, o_pool, o_gm, w["w_br_sb"], w["w_br_pool"], w["w_br_gm"], gates)
    y = _mm("out_proj", merged, w["w_out"], "nn", (F32,))
    x1 = _rms_fwd(y, w["g_mix_post"], res=x, out_dtype=F32, name="rms_mix_post")
    h2 = _rms_fwd(x1, w["g_ff_pre"], res=None, out_dtype=MX, name="rms_ff_pre")
    r, r_slope = _mm("ff_in", h2, w["w_ff_in"], "nn", (MX, MX), rides, got, epilogue=_relu2)
    ff = _mm("ff_out", r, w["w_ff_out"], "nn", (F32,), rides, got)
    x2 = _rms_fwd(ff, w["g_ff_post"], res=x1, out_dtype=F32, name="rms_ff_post")
    saved = dict(x=x, h=h, qkv=qkv, rest=rest, gates=gates, o_sb=o_sb, pooled=pooled, o_pool=o_pool, o_gm=o_gm,
                 merged=merged, y=y, x1=x1, h2=h2, r=r, r_slope=r_slope, ff=ff)
    return x2, saved, got, own


def _layer_bwd(dx2, w, sv, rides=None, own_ff=None):
    rides = rides or {}
    got, own = {}, {}
    dff, dg_ff_post = _rms_bwd(sv["ff"], w["g_ff_post"], dx2, res=None, out_dtype=MX, name="rms_ff_post_bwd")
    da = _mm("ff_out_dx", dff, w["w_ff_out"], "nt", (MX,), rides, got, epilogue=_relu2_bwd, extras=(sv["r_slope"],))
    dw_ff_out = _mm("ff_out_dw", sv["r"], dff, "tn", (MX,))
    dh2 = _mm("ff_in_dx", da, w["w_ff_in"], "nt", (F32,), rides, got)
    dw_ff_in = _mm("ff_in_dw", sv["h2"], da, "tn", (MX,))
    dx1, dg_ff_pre = _rms_bwd(sv["x1"], w["g_ff_pre"], dh2, res=dx2, out_dtype=F32, name="rms_ff_pre_bwd")

    dy, dg_mix_post = _rms_bwd(sv["y"], w["g_mix_post"], dx1, res=None, out_dtype=MX, name="rms_mix_post_bwd")
    dmerged = _mm("out_proj_dx", dy, w["w_out"], "nt", (F32,))
    dw_out = _mm("out_proj_dw", sv["merged"], dy, "tn", (MX,))
    db_sb, db_pool, db_gm, dg0, dg1, dg2 = _gate_bwd(
        sv["o_sb"], sv["o_pool"], sv["o_gm"], w["w_br_sb"], w["w_br_pool"], w["w_br_gm"], sv["gates"], dmerged)
    do_sb = _mm("br_sb_dx", db_sb, w["w_br_sb"], "nt", (MX,))
    dw_br_sb = _mm("br_sb_dw", sv["o_sb"], db_sb, "tn", (MX,))
    do_pool = _mm("br_pool_dx", db_pool, w["w_br_pool"], "nt", (F32,))
    dw_br_pool = _mm("br_pool_dw", sv["o_pool"], db_pool, "tn", (MX,))
    do_gm = _mm("br_gm_dx", db_gm, w["w_br_gm"], "nt", (F32,))
    dw_br_gm = _mm("br_gm_dw", sv["o_gm"], db_gm, "tn", (MX,))

    du, dv_gm, dws, dbfull, dgain = _gm_bwd(do_gm, sv["rest"], w["gm_gain"], w["w_spatial"], w["bfull"])
    dp, dwbd, dscale = _pool_bwd(do_pool, sv["pooled"], w["wbd"], w["pool_scale"])
    dq, dk, dv, arrived = _attn_bwd(sv["qkv"], do_sb, rides.get("attn"))
    got.update(arrived)
    dproj = jnp.concatenate([dq, dk.astype(MX), dv.astype(MX), dp, du, dv_gm, dg0, dg1, dg2], axis=1)
    late = {} if own_ff is None else {"proj_dx": ([dw_ff_in], own_ff, True, ("w_ff_in",)),
                                      "proj_dw": ([dw_ff_out], own_ff, True, ("w_ff_out",))}
    dh = _mm("proj_dx", dproj, w["w_in"], "nt", (F32,), late, own)
    dw_in = _mm("proj_dw", sv["h"], dproj, "tn", (MX,), late, own)
    dx, dg_mix_pre = _rms_bwd(sv["x"], w["g_mix_pre"], dh, res=dx1, out_dtype=F32, name="rms_mix_pre_bwd")

    grads = dict(
        w_in=dw_in, w_br_sb=dw_br_sb, w_br_pool=dw_br_pool, w_br_gm=dw_br_gm, w_out=dw_out,
        w_ff_in=dw_ff_in, w_ff_out=dw_ff_out,
        w_pool=jnp.stack([dwbd[g * GROUP:(g + 1) * GROUP, g * GROUP:(g + 1) * GROUP] for g in range(N_GROUPS)]),
        pool_scale=dscale[0], gm_gain=dgain[0], w_spatial=dws,
        b_spatial=dbfull.reshape(CHUNK, N_GROUPS, GROUP).sum(axis=-1).T,
        g_mix_pre=dg_mix_pre[0], g_mix_post=dg_mix_post[0], g_ff_pre=dg_ff_pre[0], g_ff_post=dg_ff_post[0])
    return dx, grads, got, own


def _layer_operands(full_l, small_l):
    wbd = jnp.zeros((PW, PW), F32)
    for g in range(N_GROUPS):
        wbd = wbd.at[g * GROUP:(g + 1) * GROUP, g * GROUP:(g + 1) * GROUP].set(small_l["w_pool"][g])
    w = dict(full_l)
    w["wbd"] = wbd.astype(MX)
    w["bfull"] = jnp.repeat(small_l["b_spatial"].T, GROUP, axis=1)
    w["w_spatial"] = small_l["w_spatial"]
    for n in ("pool_scale", "gm_gain", "g_mix_pre", "g_mix_post", "g_ff_pre", "g_ff_post"):
        w[n] = small_l[n][None, :]
    return w


def _local_step(x, target, layers):
    saved = []
    for w in layers:
        x, sv, _, _ = _layer_fwd(x, w)
        saved.append(sv)
    dx, sq = _loss_head(x, target)
    loss = 0.5 * jnp.sum(sq) / x.shape[1]
    grads = [None] * len(layers)
    for l in reversed(range(len(layers))):
        dx, grads[l], _, _ = _layer_bwd(dx, layers[l], saved[l])
    return loss, dx, grads


def _mesh_place():
    x, y, c = lax.axis_index("x"), lax.axis_index("y"), lax.axis_index("c")

    def peer(k):
        px = 1 - x if k & 4 else x
        py = 1 - y if k & 2 else y
        pc = 1 - c if k & 1 else c
        return (px, py, pc), 4 * px + 2 * py + pc

    return 4 * x + 2 * y + c, peer


def _shard_window(ref, name, shard_shape, idx):
    if name in ("w_in", SMALL_GRADS):
        return ref.at[idx]
    a, b = shard_shape
    if BIG_SHARD_AXIS[name] == 2:
        return ref.at[:, pl.ds(pl.multiple_of(idx * b, b), b)]
    return ref.at[pl.ds(pl.multiple_of(idx * a, a), a), :]


def _full_shape(name, shard_shape):
    a, b = shard_shape
    if name == "w_in":
        return (N_DEV, a, b)
    return (a, N_DEV * b) if BIG_SHARD_AXIS[name] == 2 else (N_DEV * a, b)


def _exchange_out_shapes(shard_shapes, scatter, names):
    if scatter:
        return [(N_DEV,) + tuple(shard_shapes[nm]) for nm in names]
    return [_full_shape(nm, shard_shapes[nm]) for nm in names]


def _exchange_sems(n):
    return [pltpu.SemaphoreType.DMA((n, N_DEV - 1)), pltpu.SemaphoreType.DMA((n, N_DEV - 1)),
            pltpu.SemaphoreType.DMA((n,))]


def _exchange_plan(ins, outs, sems, shard_shapes, scatter, names):
    send_sems, recv_sems, local_sems = sems
    me, peer = _mesh_place()
    n = len(names)

    def window(ref, p, idx):
        return _shard_window(ref, names[p], shard_shapes[names[p]], idx)

    def copy(p, k, landing):
        dev, idx = peer(k)
        if names[p] == SMALL_GRADS:
            src, dst = ins[p], outs[p].at[idx if landing else me]
        elif scatter:
            src, dst = window(ins[p], p, idx), outs[p].at[idx if landing else me]
        else:
            src, dst = ins[p], window(outs[p], p, idx if landing else me)
        return pltpu.make_async_remote_copy(
            src_ref=src, dst_ref=dst, send_sem=send_sems.at[p, k - 1], recv_sem=recv_sems.at[p, k - 1],
            device_id=dev, device_id_type=pl.DeviceIdType.MESH)

    def local(p):
        if names[p] == SMALL_GRADS:
            return pltpu.make_async_copy(ins[p], outs[p].at[me], local_sems.at[p])
        if scatter:
            return pltpu.make_async_copy(window(ins[p], p, me), outs[p].at[me], local_sems.at[p])
        return pltpu.make_async_copy(ins[p], window(outs[p], p, me), local_sems.at[p])

    pairs = [(p, k) for p in range(n) for k in range(1, N_DEV)]

    def start():
        for p in range(n):
            local(p).start()
        for p, k in pairs:
            copy(p, k, False).start()

    def finish():
        for p, k in pairs:
            copy(p, k, True).wait_recv()
        for p, k in pairs:
            copy(p, k, False).wait_send()
        for p in range(n):
            local(p).wait()

    return start, finish


def _exchange_big(arrays, shard_shapes, *, scatter, name, names=BIG):
    n = len(names)

    def body(*refs):
        start, finish = _exchange_plan(refs[:n], refs[n:2 * n], refs[2 * n:], shard_shapes, scatter, names)
        start()
        finish()

    outs = pl.pallas_call(
        body, name=name,
        in_specs=[pl.BlockSpec(memory_space=pl.ANY)] * n,
        out_specs=[pl.BlockSpec(memory_space=pl.ANY)] * n,
        out_shape=[jax.ShapeDtypeStruct(s, a.dtype)
                   for s, a in zip(_exchange_out_shapes(shard_shapes, scatter, names), arrays)],
        scratch_shapes=_exchange_sems(n),
    )(*arrays)
    return dict(zip(names, outs))


def _sum_devices(recv, name):
    _, rows, cols = recv.shape
    tr = rows
    for cand in (512, 256, 128, 64, 32, 16, 8):
        if rows % cand == 0:
            tr = cand
            break

    def body(r_ref, out_ref):
        acc = r_ref[0].astype(F32)
        for d in range(1, N_DEV):
            acc = acc + r_ref[d].astype(F32)
        out_ref[...] = acc

    return pl.pallas_call(
        body, name=name, grid=(rows // tr,),
        in_specs=[pl.BlockSpec((N_DEV, tr, cols), lambda i: (0, i, 0))],
        out_specs=pl.BlockSpec((tr, cols), lambda i: (i, 0)),
        out_shape=jax.ShapeDtypeStruct((rows, cols), F32),
        compiler_params=_params(("parallel",)),
    )(recv)


def _adamw(w, g, m, v, name):
    rows, cols = w.shape
    tr = rows
    for cand in (512, 256, 128, 64, 32, 16, 8):
        if rows % cand == 0:
            tr = cand
            break
    c1 = 1.0 - ADAM_B1 ** ADAM_STEP
    c2 = 1.0 - ADAM_B2 ** ADAM_STEP

    def body(w_ref, g_ref, m_ref, v_ref, d_ref, nm_ref, nv_ref):
        gv = g_ref[...]
        nm = ADAM_B1 * m_ref[...] + (1.0 - ADAM_B1) * gv
        nv = ADAM_B2 * v_ref[...] + (1.0 - ADAM_B2) * (gv * gv)
        nm_ref[...] = nm
        nv_ref[...] = nv
        d_ref[...] = -ADAM_LR * ((nm / c1) / (jnp.sqrt(nv / c2) + ADAM_EPS) + ADAM_WD * w_ref[...])

    spec = pl.BlockSpec((tr, cols), lambda i: (i, 0))
    return pl.pallas_call(
        body, name=name, grid=(rows // tr,),
        in_specs=[spec] * 4, out_specs=[spec] * 3,
        out_shape=[jax.ShapeDtypeStruct((rows, cols), F32)] * 3,
        compiler_params=_params(("parallel",)),
    )(w, g, m, v)


def _pack_rows(flat, dtype):
    n = flat.shape[-1]
    rows = -(-n // PACK_COLS)
    rows = -(-rows // 16) * 16
    pad = rows * PACK_COLS - n
    flat = jnp.pad(flat, [(0, 0)] * (flat.ndim - 1) + [(0, pad)])
    return flat.reshape(flat.shape[:-1] + (rows, PACK_COLS)).astype(dtype)


def kernel(x, w_in, w_pool, pool_scale, gm_gain, w_spatial, b_spatial, w_br_sb, w_br_pool, w_br_gm, w_out, g_mix_pre, g_mix_post, g_ff_pre, g_ff_post, w_ff_in, w_ff_out, loss_target, m_w_in, m_w_pool, m_pool_scale, m_gm_gain, m_w_spatial, m_b_spatial, m_w_br_sb, m_w_br_pool, m_w_br_gm, m_w_out, m_g_mix_pre, m_g_mix_post, m_g_ff_pre, m_g_ff_post, m_w_ff_in, m_w_ff_out, v_w_in, v_w_pool, v_pool_scale, v_gm_gain, v_w_spatial, v_b_spatial, v_w_br_sb, v_w_br_pool, v_w_br_gm, v_w_out, v_g_mix_pre, v_g_mix_post, v_g_ff_pre, v_g_ff_post, v_w_ff_in, v_w_ff_out):
    args = dict(locals())
    weights = {n: args[n] for n in WEIGHTS}
    mom_m = {n: args["m_" + n] for n in WEIGHTS}
    mom_v = {n: args["v_" + n] for n in WEIGHTS}

    shard_shapes = {n: weights[n].shape[1:] for n in BIG}
    depth, d_model, w_in_cols = weights["w_in"].shape
    with_attn = ("w_in", "w_br_sb", "w_br_pool", "w_br_gm", "w_out")

    def gather_of(l, names):
        return [weights[n][l].astype(MX) for n in names], shard_shapes, False, names

    def as_operands(full, l):
        full = dict(full)
        full["w_in"] = jnp.transpose(full["w_in"], (1, 0, 2)).reshape(d_model, N_DEV * w_in_cols)
        return _layer_operands(full, {n: weights[n][l] for n in SMALL})

    full = _exchange_big(gather_of(0, with_attn)[0], shard_shapes, scatter=False, name="gather_weights",
                         names=with_attn)
    xc = x[0]
    layers, saved = [], []
    for l in range(depth):
        layers.append(as_operands(full, l))
        rides = {}
        if l == 0:
            rides.update({"proj_qkv": gather_of(0, ("w_ff_out",)), "proj_gates": gather_of(0, ("w_ff_in",))})
        if l + 1 < depth:
            rides.update({"attn": gather_of(l + 1, with_attn), "ff_in": gather_of(l + 1, ("w_ff_in",)),
                          "ff_out": gather_of(l + 1, ("w_ff_out",))})
        xc, sv, full, own = _layer_fwd(xc, layers[l], rides)
        layers[l].update(own)
        saved.append(sv)
    dx, sq = _loss_head(xc, loss_target[0])
    loss = lax.psum(0.5 * jnp.sum(sq) / d_model, ("x", "y", "c"))

    def scatter_of(g, names):
        arrays = [jnp.transpose(g[n].reshape(d_model, N_DEV, w_in_cols), (1, 0, 2)) if n == "w_in" else g[n]
                  for n in names]
        return arrays, shard_shapes, True, names

    layer_grads, recv = [None] * depth, [None] * depth
    for l in reversed(range(depth)):
        rides = None
        if l + 1 < depth:
            above = layer_grads[l + 1]
            rides = {"ff_out_dx": scatter_of(above, ("w_ff_in",)), "ff_in_dx": scatter_of(above, ("w_ff_out",)),
                     "attn": scatter_of(above, with_attn)}
        dx, layer_grads[l], arrived, own = _layer_bwd(dx, layers[l], saved[l], rides, shard_shapes if l == 0 else None)
        if rides is not None:
            recv[l + 1] = arrived
    small_sizes = [int(np.prod(weights[n].shape)) for n in SMALL]
    small_off = np.concatenate([[0], np.cumsum(small_sizes)])
    small_flat = jnp.concatenate([jnp.stack([g[n] for g in layer_grads]).reshape(-1) for n in SMALL])
    small_packed = _pack_rows(small_flat, MX)
    last = _exchange_big(scatter_of(layer_grads[0], with_attn)[0] + [small_packed],
                         {**shard_shapes, SMALL_GRADS: small_packed.shape}, scatter=True,
                         name="scatter_grads", names=with_attn + (SMALL_GRADS,))
    g_small = _sum_devices(last.pop(SMALL_GRADS), "sum_small_grads").reshape(-1)
    recv[0] = {**own, **last}
    grad_w = {n: jnp.stack([_sum_devices(recv[l][n], "sum_grads_" + n) for l in range(depth)]) for n in BIG}
    for n, lo, hi in zip(SMALL, small_off[:-1], small_off[1:]):
        grad_w[n] = g_small[lo:hi].reshape(weights[n].shape)

    delta, new_m, new_v = {}, {}, {}
    for n in WEIGHTS:
        shape = weights[n].shape
        view = (-1, shape[-1])
        d, nm, nv = _adamw(weights[n].reshape(view), grad_w[n].reshape(view), mom_m[n].reshape(view),
                           mom_v[n].reshape(view), "adamw_" + n)
        delta[n], new_m[n], new_v[n] = d.reshape(shape), nm.reshape(shape), nv.reshape(shape)

    return (loss, dx[None], *[grad_w[n] for n in WEIGHTS], *[delta[n] for n in WEIGHTS],
            *[new_m[n] for n in WEIGHTS], *[new_v[n] for n in WEIGHTS])
```

```python
import functools

import numpy as np
import jax
import jax.numpy as jnp
from jax import lax
from jax.experimental import pallas as pl
from jax.experimental.pallas import tpu as pltpu

F32 = jnp.float32
MX = jnp.bfloat16

D = 1024
SB_W = 512
HEAD_PAIR = 128
PW = 256
GROUP = 64
N_GROUPS = 4
CHUNK = 128
POOL_WINDOWS = (2, 4, 8, 16)
D_FF = 4096
D_IN = 5376
QKV_W = 3 * SB_W
REST_W = D_IN - QKV_W
GATE_COL0 = 3 * PW
RMS_EPS = 1e-6
N_DEV = 8
DEPTH = 4

ADAM_LR = 0.001
ADAM_B1 = 0.9
ADAM_B2 = 0.999
ADAM_EPS = 1e-08
ADAM_WD = 0.01
ADAM_STEP = 10

SKIP_LOG = -120.0

VMEM_LIMIT = 48 * 1024 * 1024
W_IN_TILE = 768
PACK_COLS = 1024

BIG = ("w_in", "w_br_sb", "w_br_pool", "w_br_gm", "w_out", "w_ff_in", "w_ff_out")
SMALL_GRADS = "small_grads"
BIG_SHARD_AXIS = {"w_in": 2, "w_br_sb": 2, "w_br_pool": 2, "w_br_gm": 2, "w_out": 1, "w_ff_in": 2, "w_ff_out": 1}
SMALL = ("w_pool", "pool_scale", "gm_gain", "w_spatial", "b_spatial",
         "g_mix_pre", "g_mix_post", "g_ff_pre", "g_ff_post")
WEIGHTS = ("w_in", "w_pool", "pool_scale", "gm_gain", "w_spatial", "b_spatial", "w_br_sb", "w_br_pool",
           "w_br_gm", "w_out", "g_mix_pre", "g_mix_post", "g_ff_pre", "g_ff_post", "w_ff_in", "w_ff_out")


def _params(sem, vmem=VMEM_LIMIT):
    return pltpu.CompilerParams(dimension_semantics=sem, vmem_limit_bytes=vmem)


def _dot(a, b, dims):
    return lax.dot_general(a, b, (dims, ((), ())), preferred_element_type=F32)


NN = ((1,), (0,))
NT = ((1,), (1,))
TN = ((0,), (0,))


def _split_dot(x, m, left, nsplit):
    acc = None
    r = x
    for s in range(nsplit):
        p = r.astype(jnp.bfloat16)
        d = _dot(m, p, NN) if left else _dot(p, m, NN)
        acc = d if acc is None else acc + d
        if s + 1 < nsplit:
            r = r - p.astype(F32)
    return acc


def _matmul(a, b, *, mode, name, tm, tn, tk, out_dtypes, b_col0=0, n_out=None, epilogue=None, extras=(),
            exchange=None):
    if mode == "nn":
        m, kdim = a.shape
        n = b.shape[1] if n_out is None else n_out
        dims = NN
    elif mode == "nt":
        m, kdim = a.shape
        n = b.shape[0]
        dims = NT
    else:
        kdim, m = a.shape
        n = b.shape[1]
        dims = TN
    tm, tn, tk = min(tm, m), min(tn, n), min(tk, kdim)
    assert m % tm == 0 and n % tn == 0 and kdim % tk == 0 and b_col0 % tn == 0, (name, m, n, kdim)
    if mode == "nn":
        a_spec = pl.BlockSpec((tm, tk), lambda i, j, k: (i, k))
        b_spec = pl.BlockSpec((tk, tn), lambda i, j, k: (k, j + b_col0 // tn))
    elif mode == "nt":
        a_spec = pl.BlockSpec((tm, tk), lambda i, j, k: (i, k))
        b_spec = pl.BlockSpec((tn, tk), lambda i, j, k: (j, k))
    else:
        a_spec = pl.BlockSpec((tk, tm), lambda i, j, k: (k, i))
        b_spec = pl.BlockSpec((tk, tn), lambda i, j, k: (k, j))
    nk = kdim // tk
    n_extra = len(extras)
    n_outs = len(out_dtypes)
    o_spec = pl.BlockSpec((tm, tn), lambda i, j, k: (i, j))
    x_arrays, x_shapes, x_specs, x_sems = _riding_exchange(exchange)
    nx = len(x_arrays)
    grid = (m // tm, n // tn, nk)

    def body(a_ref, b_ref, *refs):
        extra_refs = refs[:n_extra]
        out_refs = refs[n_extra + nx:n_extra + nx + n_outs]
        scratch = refs[n_extra + 2 * nx + n_outs:]
        ids = [pl.program_id(d) for d in range(3)]
        k = ids[2]
        end_exchange = _ride(exchange, refs[n_extra:n_extra + nx], refs[n_extra + nx + n_outs:n_extra + 2 * nx + n_outs],
                             scratch[1:] if nk > 1 else scratch,
                             functools.reduce(jnp.logical_and, [ids[d] == 0 for d in range(3)]),
                             functools.reduce(jnp.logical_and, [ids[d] == grid[d] - 1 for d in range(3)]))
        part = _dot(a_ref[...].astype(MX), b_ref[...].astype(MX), dims)

        def finish(acc):
            outs = (acc,) if epilogue is None else epilogue(acc, *[e[...] for e in extra_refs])
            for o_ref, val in zip(out_refs, outs):
                o_ref[...] = val.astype(o_ref.dtype)

        if nk == 1:
            finish(part)
        else:
            acc_ref = scratch[0]

            @pl.when(k == 0)
            def _():
                acc_ref[...] = part

            @pl.when(jnp.logical_and(k > 0, k < nk - 1))
            def _():
                acc_ref[...] += part

            @pl.when(k == nk - 1)
            def _():
                finish(acc_ref[...] + part)

        end_exchange()

    outs = pl.pallas_call(
        body, name=name if exchange is None else name + "_ride",
        grid=grid,
        in_specs=[a_spec, b_spec] + [o_spec] * n_extra + x_specs,
        out_specs=[o_spec] * n_outs + x_specs,
        out_shape=[jax.ShapeDtypeStruct((m, n), dt) for dt in out_dtypes] + x_shapes,
        scratch_shapes=([pltpu.VMEM((tm, tn), F32)] if nk > 1 else []) + x_sems,
        compiler_params=_params(("parallel", "parallel", "arbitrary") if exchange is None
                                else ("arbitrary", "arbitrary", "arbitrary")),
    )(a, b, *extras, *x_arrays)
    if exchange is not None:
        return tuple(outs[:n_outs]) + (dict(zip(exchange[3], outs[n_outs:])),)
    return outs[0] if n_outs == 1 else outs


ROW_TILE = 1024


def _rms_fwd(x, g, *, res, out_dtype, name, then_gain=None):
    s, d = x.shape
    tr = min(ROW_TILE, s)
    has_res = res is not None
    chained = then_gain is not None

    def body(x_ref, g_ref, *refs):
        xv = x_ref[...]
        y = xv * lax.rsqrt(jnp.mean(xv * xv, axis=-1, keepdims=True) + RMS_EPS) * g_ref[...]
        if has_res:
            y = refs[0][...] + y
        if chained:
            y = y.astype(out_dtype).astype(F32)
            refs[-2][...] = y.astype(out_dtype)
            nxt = y * lax.rsqrt(jnp.mean(y * y, axis=-1, keepdims=True) + RMS_EPS) * refs[-3][...]
            refs[-1][...] = nxt.astype(MX)
        else:
            refs[-1][...] = y.astype(out_dtype)

    row = pl.BlockSpec((tr, d), lambda i: (i, 0))
    vec = pl.BlockSpec((1, d), lambda i: (0, 0))
    outs = pl.pallas_call(
        body, name=name, grid=(s // tr,),
        in_specs=[row, vec] + ([row] if has_res else []) + ([vec] if chained else []),
        out_specs=[row, row] if chained else row,
        out_shape=([jax.ShapeDtypeStruct((s, d), out_dtype), jax.ShapeDtypeStruct((s, d), MX)] if chained
                   else jax.ShapeDtypeStruct((s, d), out_dtype)),
        compiler_params=_params(("parallel",)),
    )(x, g, *([res] if has_res else []), *([then_gain] if chained else []))
    return outs


def _rms_bwd(x, g, dout, *, res, out_dtype, name):
    s, d = x.shape
    tr = min(ROW_TILE, s)
    has_res = res is not None

    def body(x_ref, g_ref, do_ref, *refs):
        dx_ref, dg_ref = refs[-2], refs[-1]
        i = pl.program_id(0)
        xv = x_ref[...]
        do = do_ref[...]
        r = lax.rsqrt(jnp.mean(xv * xv, axis=-1, keepdims=True) + RMS_EPS)
        xhat = xv * r
        dxhat = do * g_ref[...]
        dx = r * (dxhat - xhat * jnp.mean(dxhat * xhat, axis=-1, keepdims=True))
        if has_res:
            dx = refs[0][...] + dx
        dx_ref[...] = dx.astype(dx_ref.dtype)

        @pl.when(i == 0)
        def _():
            dg_ref[...] = jnp.zeros_like(dg_ref)

        dg_ref[...] += jnp.sum(do * xhat, axis=0, keepdims=True)

    row = pl.BlockSpec((tr, d), lambda i: (i, 0))
    vec = pl.BlockSpec((1, d), lambda i: (0, 0))
    return pl.pallas_call(
        body, name=name, grid=(s // tr,),
        in_specs=[row, vec, row] + ([row] if has_res else []),
        out_specs=[row, vec],
        out_shape=[jax.ShapeDtypeStruct((s, d), out_dtype), jax.ShapeDtypeStruct((1, d), F32)],
        compiler_params=_params(("arbitrary",)),
    )(x, g, dout, *([res] if has_res else []))


def _loss_head(y, target):
    s, d = y.shape
    tr = min(ROW_TILE, s)

    def body(y_ref, t_ref, dy_ref, sq_ref):
        i = pl.program_id(0)
        err = y_ref[...] - t_ref[...]
        dy_ref[...] = err * (1.0 / d)

        @pl.when(i == 0)
        def _():
            sq_ref[...] = jnp.zeros_like(sq_ref)

        sq_ref[...] += jnp.sum(err * err, axis=0, keepdims=True)

    row = pl.BlockSpec((tr, d), lambda i: (i, 0))
    vec = pl.BlockSpec((1, d), lambda i: (0, 0))
    return pl.pallas_call(
        body, name="loss_head", grid=(s // tr,),
        in_specs=[row, row], out_specs=[row, vec],
        out_shape=[jax.ShapeDtypeStruct((s, d), F32), jax.ShapeDtypeStruct((1, d), F32)],
        compiler_params=_params(("arbitrary",)),
    )(y, target)


def _sb_masks():
    row = lax.broadcasted_iota(jnp.int32, (CHUNK, CHUNK), 0)
    col = lax.broadcasted_iota(jnp.int32, (CHUNK, CHUNK), 1)
    return row, col


SB_WINDOW = 2 * CHUNK
SB_ROWS = CHUNK // 2
SB_FWD_QUERIES = 512
SB_BWD_QUERIES = 256

def _sb_consts():
    row = lax.broadcasted_iota(jnp.int32, (CHUNK, SB_WINDOW), 0)
    col = lax.broadcasted_iota(jnp.int32, (CHUNK, SB_WINDOW), 1)
    lane_hi = lax.broadcasted_iota(jnp.int32, (SB_ROWS, HEAD_PAIR), 1) >= GROUP
    return col - jnp.bitwise_and(row, SB_ROWS - 1), col, lane_hi


def _sb_window(t0, m, dcol, col):
    hi = t0 + SB_ROWS - SB_WINDOW * m
    start = jnp.maximum(hi - SB_WINDOW, 0)
    keep = jnp.logical_and(dcol < t0 - start, col < hi - start)
    return pl.multiple_of(start, SB_ROWS), keep


def _sb_stack(x, lane_hi):
    zero = jnp.zeros_like(x)
    return jnp.concatenate([jnp.where(lane_hi, zero, x), jnp.where(lane_hi, x, zero)], axis=0)


def _sb_unstack(y, lane_hi):
    return jnp.where(lane_hi, y[SB_ROWS:], y[:SB_ROWS])


def _sb_logs(z, keep):
    t = jnp.log(1.0 + jnp.exp(-jnp.abs(z)))
    ln = -(jnp.maximum(z, 0.0) + t)
    lb = ln + z
    ln = jnp.where(keep, ln, 0.0)
    return ln, lb


def _sb_scores(qm, kw, keep):
    return _sb_logs(_dot(qm, kw, NT), keep)


def _sb_tri(upper):
    r = np.arange(SB_WINDOW)
    m = (r[:, None] > r[None, :]) if upper else (r[:, None] < r[None, :])
    return jnp.asarray(m, jnp.bfloat16)


def _riding_exchange(exchange):
    if exchange is None:
        return [], [], [], []
    arrays, shard_shapes, scatter, names = exchange
    shapes = [jax.ShapeDtypeStruct(s, a.dtype)
              for s, a in zip(_exchange_out_shapes(shard_shapes, scatter, names), arrays)]
    return list(arrays), shapes, [pl.BlockSpec(memory_space=pl.ANY)] * len(arrays), _exchange_sems(len(arrays))


def _ride(exchange, ins, outs, sems, first, last):
    if exchange is None:
        return lambda: None
    start, finish = _exchange_plan(ins, outs, sems, exchange[1], exchange[2], exchange[3])
    pl.when(first)(start)
    return lambda: pl.when(last)(finish)


def _attn_fwd(qkv, exchange=None):
    s = qkv.shape[0]
    qb = min(SB_FWD_QUERIES, s)
    nblk = s // qb
    n_pairs = SB_W // HEAD_PAIR
    x_arrays, x_shapes, x_specs, x_sems = _riding_exchange(exchange)
    nx = len(x_arrays)

    def body(q_ref, k_ref, v_ref, up_ref, *refs):
        o_ref = refs[nx]
        i = pl.program_id(1)
        pair = pl.program_id(0)
        end_exchange = _ride(exchange, refs[:nx], refs[nx + 1:2 * nx + 1], refs[2 * nx + 1:],
                             jnp.logical_and(pair == 0, i == 0),
                             jnp.logical_and(pair == n_pairs - 1, i == nblk - 1))
        dcol, col, lane_hi = _sb_consts()
        halves = range(qb // SB_ROWS)
        t0s = [i * qb + r * SB_ROWS for r in halves]
        qs = [_sb_stack(q_ref[r * SB_ROWS:(r + 1) * SB_ROWS, :], lane_hi) * 0.125 for r in halves]

        def cond(st):
            return jnp.logical_and(st[0] * SB_WINDOW < (i + 1) * qb, st[1])

        def loop(st):
            m = st[0]
            wins = [_sb_window(t0s[r], m, dcol, col) for r in halves]
            kws = [k_ref[pl.ds(wins[r][0], SB_WINDOW), :] for r in halves]
            vws = [v_ref[pl.ds(wins[r][0], SB_WINDOW), :] for r in halves]
            keeps = [wins[r][1] for r in halves]
            zs = [_dot(qs[r], kws[r], NT) for r in halves]
            lnlb = [_sb_logs(zs[r], keeps[r]) for r in halves]
            his = [lnlb[r][0].astype(jnp.bfloat16) for r in halves]
            los = [(lnlb[r][0] - his[r].astype(F32)).astype(jnp.bfloat16) for r in halves]
            s_hi = [_dot(his[r], up_ref[...], NN) for r in halves]
            s_lo = [_dot(los[r], up_ref[...], NN) for r in halves]
            a = [jnp.where(keeps[r], jnp.exp(lnlb[r][1] + (s_hi[r] + s_lo[r] + st[2 + 2 * r])), 0.0).astype(MX)
                 for r in halves]
            new = []
            for r in halves:
                new += [st[2 + 2 * r] + jnp.sum(lnlb[r][0], axis=1, keepdims=True),
                        st[3 + 2 * r] + _dot(a[r], vws[r], NN)]
            active = functools.reduce(jnp.maximum, [jnp.max(c) for c in new[0::2]]) > SKIP_LOG
            return (m + 1, active, *new)

        c0 = jnp.zeros((CHUNK, 1), F32)
        a0 = jnp.zeros((CHUNK, HEAD_PAIR), F32)
        st = lax.while_loop(cond, loop, (i * 0, i >= 0) + (c0, a0) * len(halves))
        for r in halves:
            o_ref[r * SB_ROWS:(r + 1) * SB_ROWS, :] = _sb_unstack(st[3 + 2 * r], lane_hi).astype(o_ref.dtype)
        end_exchange()

    outs = pl.pallas_call(
        body, name="attn_fwd" if exchange is None else "attn_fwd_gather", grid=(n_pairs, nblk),
        in_specs=[pl.BlockSpec((qb, HEAD_PAIR), lambda p, i: (i, p)),
                  pl.BlockSpec((s, HEAD_PAIR), lambda p, i: (0, n_pairs + p)),
                  pl.BlockSpec((s, HEAD_PAIR), lambda p, i: (0, 2 * n_pairs + p)),
                  pl.BlockSpec((SB_WINDOW, SB_WINDOW), lambda p, i: (0, 0))] + x_specs,
        out_specs=[pl.BlockSpec((qb, HEAD_PAIR), lambda p, i: (i, p))] + x_specs,
        out_shape=[jax.ShapeDtypeStruct((s, SB_W), MX)] + x_shapes,
        scratch_shapes=x_sems,
        compiler_params=_params(("arbitrary", "arbitrary")),
    )(qkv, qkv, qkv, _sb_tri(True), *x_arrays)
    return outs[0], dict(zip(exchange[3] if exchange else (), outs[1:]))


def _attn_bwd(qkv, d_o, exchange=None):
    s = qkv.shape[0]
    qb = min(SB_BWD_QUERIES, s)
    nblk = s // qb
    n_pairs = SB_W // HEAD_PAIR
    n_win = s // SB_WINDOW
    x_arrays, x_shapes, x_specs, x_sems = _riding_exchange(exchange)
    nx = len(x_arrays)

    def body(q_ref, k_ref, v_ref, do_ref, up_ref, lo_ref, *refs):
        dq_ref, dk_ref, dv_ref = refs[nx:nx + 3]
        e_s = refs[2 * nx + 3]
        i = pl.program_id(1)
        pair = pl.program_id(0)
        end_exchange = _ride(exchange, refs[:nx], refs[nx + 3:2 * nx + 3], refs[2 * nx + 4:],
                             jnp.logical_and(pair == 0, i == 0),
                             jnp.logical_and(pair == n_pairs - 1, i == nblk - 1))
        dcol, col, lane_hi = _sb_consts()
        halves = range(qb // SB_ROWS)
        t0s = [i * qb + r * SB_ROWS for r in halves]
        qs = [_sb_stack(q_ref[r * SB_ROWS:(r + 1) * SB_ROWS, :], lane_hi) * 0.125 for r in halves]
        dos =[_sb_stack(do_ref[r * SB_ROWS:(r + 1) * SB_ROWS, :], lane_hi) for r in halves]

        @pl.when(i == 0)
        def _():
            dk_ref[...] = jnp.zeros_like(dk_ref)
            dv_ref[...] = jnp.zeros_like(dv_ref)

        def cond(st):
            return jnp.logical_and(st[0] * SB_WINDOW < (i + 1) * qb, st[1])

        def loop(st):
            m = st[0]
            wins = [_sb_window(t0s[r], m, dcol, col) for r in halves]
            kws = [k_ref[pl.ds(wins[r][0], SB_WINDOW), :] for r in halves]
            vws = [v_ref[pl.ds(wins[r][0], SB_WINDOW), :] for r in halves]
            keeps = [wins[r][1] for r in halves]
            zs = [_dot(qs[r], kws[r], NT) for r in halves]
            das = [_dot(dos[r], vws[r], NT) for r in halves]
            lnlb = [_sb_logs(zs[r], keeps[r]) for r in halves]
            his = [lnlb[r][0].astype(jnp.bfloat16) for r in halves]
            los = [(lnlb[r][0] - his[r].astype(F32)).astype(jnp.bfloat16) for r in halves]
            s_hi = [_dot(his[r], up_ref[...], NN) for r in halves]
            s_lo = [_dot(los[r], up_ref[...], NN) for r in halves]
            a = [jnp.where(keeps[r], jnp.exp(lnlb[r][1] + (s_hi[r] + s_lo[r] + st[2 + r])), 0.0) for r in halves]
            for r in halves:
                e_s[r, m] = a[r] * das[r]
            parts = [_dot(a[r].astype(MX), dos[r], TN) for r in halves]
            for r in halves:
                dv_ref[pl.ds(wins[r][0], SB_WINDOW), :] += parts[r]
            cs = [st[2 + r] + jnp.sum(lnlb[r][0], axis=1, keepdims=True) for r in halves]
            active = functools.reduce(jnp.maximum, [jnp.max(c) for c in cs]) > SKIP_LOG
            return (m + 1, active, *cs)

        c0 = jnp.zeros((CHUNK, 1), F32)
        n_seen = lax.while_loop(cond, loop, (i * 0, i >= 0) + (c0,) * len(halves))[0]

        def up(t, st):
            m = n_seen - 1 - t
            wins = [_sb_window(t0s[r], m, dcol, col) for r in halves]
            kws = [k_ref[pl.ds(wins[r][0], SB_WINDOW), :] for r in halves]
            es = [e_s[r, m] for r in halves]
            zs = [_dot(qs[r], kws[r], NT) for r in halves]
            his = [es[r].astype(jnp.bfloat16) for r in halves]
            los = [(es[r] - his[r].astype(F32)).astype(jnp.bfloat16) for r in halves]
            p_hi = [_dot(his[r], lo_ref[...], NN) for r in halves]
            p_lo = [_dot(los[r], lo_ref[...], NN) for r in halves]
            dzs = []
            for r in halves:
                u = jnp.exp(-jnp.abs(zs[r]))
                big = 1.0 / (1.0 + u)
                small = u * big
                pos = zs[r] >= 0.0
                prefix = p_hi[r] + p_lo[r] + st[2 * r]
                dz = es[r] * jnp.where(pos, small, big) - prefix * jnp.where(pos, big, small)
                dzs.append(jnp.where(wins[r][1], dz, 0.0).astype(MX))
            parts = [_dot(dzs[r], qs[r], TN) for r in halves]
            for r in halves:
                dk_ref[pl.ds(wins[r][0], SB_WINDOW), :] += parts[r]
            new = []
            for r in halves:
                new += [st[2 * r] + jnp.sum(es[r], axis=1, keepdims=True),
                        st[2 * r + 1] + _dot(dzs[r], kws[r], NN)]
            return tuple(new)

        a0 = jnp.zeros((CHUNK, HEAD_PAIR), F32)
        st = lax.fori_loop(0, n_seen, up, (c0, a0) * len(halves))
        for r in halves:
            dq_ref[r * SB_ROWS:(r + 1) * SB_ROWS, :] = (_sb_unstack(st[2 * r + 1], lane_hi) * 0.125).astype(dq_ref.dtype)
        end_exchange()

    blk = pl.BlockSpec((qb, HEAD_PAIR), lambda p, i: (i, p))
    full = pl.BlockSpec((s, HEAD_PAIR), lambda p, i: (0, p))
    tri = pl.BlockSpec((SB_WINDOW, SB_WINDOW), lambda p, i: (0, 0))
    outs = pl.pallas_call(
        body, name="attn_bwd" if exchange is None else "attn_bwd_scatter", grid=(n_pairs, nblk),
        in_specs=[blk,
                  pl.BlockSpec((s, HEAD_PAIR), lambda p, i: (0, n_pairs + p)),
                  pl.BlockSpec((s, HEAD_PAIR), lambda p, i: (0, 2 * n_pairs + p)),
                  blk, tri, tri] + x_specs,
        out_specs=[blk, full, full] + x_specs,
        out_shape=[jax.ShapeDtypeStruct((s, SB_W), MX), jax.ShapeDtypeStruct((s, SB_W), F32),
                   jax.ShapeDtypeStruct((s, SB_W), F32)] + x_shapes,
        scratch_shapes=[pltpu.VMEM((qb // SB_ROWS, n_win, CHUNK, SB_WINDOW), F32)] + x_sems,
        compiler_params=_params(("arbitrary", "arbitrary"), 56 * 1024 * 1024),
    )(qkv, qkv, qkv, d_o, _sb_tri(True), _sb_tri(False), *x_arrays)
    return outs[0], outs[1], outs[2], dict(zip(exchange[3] if exchange else (), outs[3:]))


POOL_TILE = 256
POOL_HALO = 128


def _pool_bands(tile, transpose):
    cur = np.zeros((N_GROUPS, tile, tile), np.float32)
    halo = np.zeros((N_GROUPS, tile, POOL_HALO), np.float32)
    t = np.arange(tile)[:, None]
    for g, w in enumerate(POOL_WINDOWS):
        if not transpose:
            p = np.arange(tile)[None, :]
            cur[g] = ((t - p >= 0) & (t - p < w))
            ph = np.arange(POOL_HALO)[None, :] - POOL_HALO
            halo[g] = (t - ph < w)
        else:
            p = np.arange(tile)[None, :]
            cur[g] = ((p - t >= 0) & (p - t < w))
            ph = np.arange(POOL_HALO)[None, :] + tile
            halo[g] = (ph - t < w)
    return jnp.asarray(cur, jnp.bfloat16), jnp.asarray(halo, jnp.bfloat16)


def _pool_count(i, tile, rows, row0):
    pos = (i * tile + row0 + lax.broadcasted_iota(jnp.int32, (rows, PW), 0)).astype(F32)
    grp = lax.broadcasted_iota(jnp.int32, (rows, PW), 1) // GROUP
    win = jnp.where(grp == 0, float(POOL_WINDOWS[0]),
                    jnp.where(grp == 1, float(POOL_WINDOWS[1]),
                              jnp.where(grp == 2, float(POOL_WINDOWS[2]), float(POOL_WINDOWS[3]))))
    return jnp.minimum(pos + 1.0, win), grp


def _pool_fwd(rest, wbd, scale):
    s = rest.shape[0]
    tile = min(POOL_TILE, s)
    halo_per_tile = tile // POOL_HALO
    bcur, bhalo = _pool_bands(tile, False)

    def body(cur_ref, prev_ref, bcur_ref, bhalo_ref, wbd_ref, scale_ref, pooled_ref, out_ref):
        i = pl.program_id(0)
        cur = cur_ref[...]
        prev = jnp.where(i > 0, prev_ref[...], 0.0)
        count, grp = _pool_count(i, tile, tile, 0)
        win = jnp.zeros((tile, PW), F32)
        for g in range(N_GROUPS):
            wsum = _split_dot(cur, bcur_ref[g], True, 3) + _split_dot(prev, bhalo_ref[g], True, 3)
            win = jnp.where(grp == g, wsum, win)
        pooled = (win / count - cur).astype(MX)
        pooled_ref[...] = pooled
        out_ref[...] = (_dot(pooled, wbd_ref[...], NN) * scale_ref[...]).astype(out_ref.dtype)

    tile_spec = pl.BlockSpec((tile, PW), lambda i: (i, 0))
    const3 = lambda shape: pl.BlockSpec(shape, lambda i: (0, 0, 0))
    const2 = lambda shape: pl.BlockSpec(shape, lambda i: (0, 0))
    return pl.pallas_call(
        body, name="pool_fwd", grid=(s // tile,),
        in_specs=[tile_spec,
                  pl.BlockSpec((POOL_HALO, PW), lambda i: (jnp.maximum(i * halo_per_tile - 1, 0), 0)),
                  const3(bcur.shape), const3(bhalo.shape), const2((PW, PW)), const2((1, PW))],
        out_specs=[tile_spec, tile_spec],
        out_shape=[jax.ShapeDtypeStruct((s, PW), MX), jax.ShapeDtypeStruct((s, PW), MX)],
        compiler_params=_params(("parallel",)),
    )(rest, rest, bcur, bhalo, wbd, scale)


def _pool_bwd(d_o, pooled, wbd, scale):
    s = d_o.shape[0]
    tile = min(POOL_TILE, s)
    halo_per_tile = tile // POOL_HALO
    n_halo = s // POOL_HALO
    n_tiles = s // tile
    ccur, chalo = _pool_bands(tile, True)

    def body(do_ref, nxt_ref, pooled_ref, ccur_ref, chalo_ref, wbd_ref, scale_ref, dp_ref, dw_ref, ds_ref):
        i = pl.program_id(0)
        do = do_ref[...]
        w = wbd_ref[...]
        sc = scale_ref[...]
        pooled_v = pooled_ref[...]

        @pl.when(i == 0)
        def _():
            dw_ref[...] = jnp.zeros_like(dw_ref)
            ds_ref[...] = jnp.zeros_like(ds_ref)

        ds_ref[...] += jnp.sum(do * _dot(pooled_v, w, NN), axis=0, keepdims=True)
        dmixed = (do * sc).astype(MX)
        dw_ref[...] += _dot(pooled_v, dmixed, TN)
        dpooled = _dot(dmixed, w, NT)
        nxt = jnp.where(i < n_tiles - 1, nxt_ref[...], 0.0)
        dpooled_n = _dot((nxt * sc).astype(MX), w, NT)
        count, grp = _pool_count(i, tile, tile, 0)
        count_n, _ = _pool_count(i, tile, POOL_HALO, tile)
        dq = dpooled / count
        dq_n = dpooled_n / count_n
        acc = jnp.zeros((tile, PW), F32)
        for g in range(N_GROUPS):
            wsum = _split_dot(dq, ccur_ref[g], True, 3) + _split_dot(dq_n, chalo_ref[g], True, 3)
            acc = jnp.where(grp == g, wsum, acc)
        dp_ref[...] = (acc - dpooled).astype(dp_ref.dtype)

    tile_spec = pl.BlockSpec((tile, PW), lambda i: (i, 0))
    const3 = lambda shape: pl.BlockSpec(shape, lambda i: (0, 0, 0))
    const2 = lambda shape: pl.BlockSpec(shape, lambda i: (0, 0))
    return pl.pallas_call(
        body, name="pool_bwd", grid=(n_tiles,),
        in_specs=[tile_spec,
                  pl.BlockSpec((POOL_HALO, PW), lambda i: (jnp.minimum((i + 1) * halo_per_tile, n_halo - 1), 0)),
                  tile_spec, const3(ccur.shape), const3(chalo.shape), const2((PW, PW)), const2((1, PW))],
        out_specs=[tile_spec, const2((PW, PW)), const2((1, PW))],
        out_shape=[jax.ShapeDtypeStruct((s, PW), MX), jax.ShapeDtypeStruct((PW, PW), F32),
                   jax.ShapeDtypeStruct((1, PW), F32)],
        compiler_params=_params(("arbitrary",)),
    )(d_o, d_o, pooled, ccur, chalo, wbd, scale)


GM_TILE = 512
GELU_C = 0.7978845608028654
GELU_A = 0.044715


def _gelu(x):
    return 0.5 * x * (1.0 + jnp.tanh(GELU_C * (x + GELU_A * (x * x * x))))


def _gelu_grad(x):
    th = jnp.tanh(GELU_C * (x + GELU_A * (x * x * x)))
    return 0.5 * (1.0 + th) + 0.5 * x * (1.0 - th * th) * (GELU_C * (1.0 + 3.0 * GELU_A * (x * x)))


def _gm_common(ws_ref):
    row, col = _sb_masks()
    tril = row >= col
    wsm = [jnp.where(tril, ws_ref[g], 0.0).astype(MX) for g in range(N_GROUPS)]
    grp = lax.broadcasted_iota(jnp.int32, (CHUNK, PW), 1) // GROUP
    return tril, wsm, grp


def _gm_mix(wsm, grp, vn_c):
    mixed = jnp.zeros((CHUNK, PW), F32)
    for g in range(N_GROUPS):
        mixed = jnp.where(grp == g, _dot(wsm[g], vn_c, NN), mixed)
    return mixed


def _gm_fwd(rest, gain, ws, bfull):
    s = rest.shape[0]
    tile = min(GM_TILE, s)

    def body(u_ref, v_ref, gain_ref, ws_ref, b_ref, out_ref):
        _, wsm, grp = _gm_common(ws_ref)
        bias = b_ref[...]
        for n in range(tile // CHUNK):
            rows = slice(n * CHUNK, (n + 1) * CHUNK)
            gu = _gelu(u_ref[rows, :])
            gv = _gelu(v_ref[rows, :])
            r = lax.rsqrt(jnp.mean(gv * gv, axis=-1, keepdims=True) + RMS_EPS)
            vn = (gv * r * gain_ref[...]).astype(MX)
            out_ref[rows, :] = (gu * (_gm_mix(wsm, grp, vn) + bias)).astype(out_ref.dtype)

    const2 = lambda shape: pl.BlockSpec(shape, lambda i: (0, 0))
    return pl.pallas_call(
        body, name="gm_fwd", grid=(s // tile,),
        in_specs=[pl.BlockSpec((tile, PW), lambda i: (i, 1)), pl.BlockSpec((tile, PW), lambda i: (i, 2)),
                  const2((1, PW)), pl.BlockSpec((N_GROUPS, CHUNK, CHUNK), lambda i: (0, 0, 0)),
                  const2((CHUNK, PW))],
        out_specs=pl.BlockSpec((tile, PW), lambda i: (i, 0)),
        out_shape=jax.ShapeDtypeStruct((s, PW), MX),
        compiler_params=_params(("parallel",)),
    )(rest, rest, gain, ws, bfull)


def _gm_bwd(d_o, rest, gain, ws, bfull):
    s = rest.shape[0]
    tile = min(GM_TILE, s)

    def body(do_ref, u_ref, v_ref, gain_ref, ws_ref, b_ref, du_ref, dv_ref, dws_ref, db_ref, dgain_ref):
        i = pl.program_id(0)
        tril, wsm, grp = _gm_common(ws_ref)
        bias = b_ref[...]
        gain_v = gain_ref[...]

        @pl.when(i == 0)
        def _():
            dws_ref[...] = jnp.zeros_like(dws_ref)
            db_ref[...] = jnp.zeros_like(db_ref)
            dgain_ref[...] = jnp.zeros_like(dgain_ref)

        for n in range(tile // CHUNK):
            rows = slice(n * CHUNK, (n + 1) * CHUNK)
            u = u_ref[rows, :]
            v = v_ref[rows, :]
            do = do_ref[rows, :]
            gu = _gelu(u)
            gv = _gelu(v)
            r = lax.rsqrt(jnp.mean(gv * gv, axis=-1, keepdims=True) + RMS_EPS)
            vhat = gv * r
            vn = (vhat * gain_v).astype(MX)
            mixed = _gm_mix(wsm, grp, vn)
            du_ref[rows, :] = (do * (mixed + bias) * _gelu_grad(u)).astype(du_ref.dtype)
            dmix = do * gu
            db_ref[...] += dmix
            dmix_mx = dmix.astype(MX)
            dvn = jnp.zeros((CHUNK, PW), F32)
            for g in range(N_GROUPS):
                dmg = jnp.where(grp == g, dmix_mx, jnp.zeros_like(dmix_mx))
                dws_ref[g] += jnp.where(tril, _dot(dmg, vn, NT), 0.0)
                dvn = jnp.where(grp == g, _dot(wsm[g], dmix_mx, TN), dvn)
            dgain_ref[...] += jnp.sum(dvn * vhat, axis=0, keepdims=True)
            dvhat = dvn * gain_v
            dgv = r * (dvhat - vhat * jnp.mean(dvhat * vhat, axis=-1, keepdims=True))
            dv_ref[rows, :] = (dgv * _gelu_grad(v)).astype(dv_ref.dtype)

    const2 = lambda shape: pl.BlockSpec(shape, lambda i: (0, 0))
    ws_spec = pl.BlockSpec((N_GROUPS, CHUNK, CHUNK), lambda i: (0, 0, 0))
    tile_spec = pl.BlockSpec((tile, PW), lambda i: (i, 0))
    return pl.pallas_call(
        body, name="gm_bwd", grid=(s // tile,),
        in_specs=[tile_spec, pl.BlockSpec((tile, PW), lambda i: (i, 1)), pl.BlockSpec((tile, PW), lambda i: (i, 2)),
                  const2((1, PW)), ws_spec, const2((CHUNK, PW))],
        out_specs=[tile_spec, tile_spec, ws_spec, const2((CHUNK, PW)), const2((1, PW))],
        out_shape=[jax.ShapeDtypeStruct((s, PW), MX), jax.ShapeDtypeStruct((s, PW), MX),
                   jax.ShapeDtypeStruct((N_GROUPS, CHUNK, CHUNK), F32),
                   jax.ShapeDtypeStruct((CHUNK, PW), F32), jax.ShapeDtypeStruct((1, PW), F32)],
        compiler_params=_params(("arbitrary",)),
    )(d_o, rest, rest, gain, ws, bfull)


GATE_TM = 2048
GATE_TN = 256


def _gate_specs(s):
    tm = min(GATE_TM, s)
    tn = GATE_TN
    gate0 = 0
    per = D // tn
    ins = [pl.BlockSpec((tm, SB_W), lambda i, j: (i, 0)),
           pl.BlockSpec((tm, PW), lambda i, j: (i, 0)),
           pl.BlockSpec((tm, PW), lambda i, j: (i, 0)),
           pl.BlockSpec((SB_W, tn), lambda i, j: (0, j)),
           pl.BlockSpec((PW, tn), lambda i, j: (0, j)),
           pl.BlockSpec((PW, tn), lambda i, j: (0, j)),
           pl.BlockSpec((tm, tn), lambda i, j: (i, gate0 + j)),
           pl.BlockSpec((tm, tn), lambda i, j: (i, gate0 + per + j)),
           pl.BlockSpec((tm, tn), lambda i, j: (i, gate0 + 2 * per + j))]
    return tm, tn, ins, pl.BlockSpec((tm, tn), lambda i, j: (i, j))


def _gate_fwd(o_sb, o_pool, o_gm, w_sb, w_pool, w_gm, rest):
    s = rest.shape[0]
    tm, tn, ins, out = _gate_specs(s)

    def body(o0, o1, o2, w0, w1, w2, g0, g1, g2, merged_ref):
        acc = jax.nn.sigmoid(g0[...].astype(F32)) * _dot(o0[...], w0[...], NN)
        acc += jax.nn.sigmoid(g1[...].astype(F32)) * _dot(o1[...], w1[...], NN)
        acc += jax.nn.sigmoid(g2[...].astype(F32)) * _dot(o2[...], w2[...], NN)
        merged_ref[...] = acc.astype(merged_ref.dtype)

    return pl.pallas_call(
        body, name="gate_fwd", grid=(s // tm, D // tn),
        in_specs=ins, out_specs=out, out_shape=jax.ShapeDtypeStruct((s, D), MX),
        compiler_params=_params(("parallel", "parallel")),
    )(o_sb, o_pool, o_gm, w_sb, w_pool, w_gm, rest, rest, rest)


def _gate_bwd(o_sb, o_pool, o_gm, w_sb, w_pool, w_gm, rest, dmerged):
    s = rest.shape[0]
    tm, tn, ins, out = _gate_specs(s)

    def body(o0, o1, o2, w0, w1, w2, g0, g1, g2, dm_ref, db0, db1, db2, dg0, dg1, dg2):
        dm = dm_ref[...]
        for o, w, g, db, dg in ((o0, w0, g0, db0, dg0), (o1, w1, g1, db1, dg1), (o2, w2, g2, db2, dg2)):
            sg = jax.nn.sigmoid(g[...].astype(F32))
            db[...] = (dm * sg).astype(db.dtype)
            dg[...] = (dm * _dot(o[...], w[...], NN) * (sg * (1.0 - sg))).astype(dg.dtype)

    return pl.pallas_call(
        body, name="gate_bwd", grid=(s // tm, D // tn),
        in_specs=ins + [out], out_specs=[out] * 6,
        out_shape=[jax.ShapeDtypeStruct((s, D), MX)] * 6,
        compiler_params=_params(("parallel", "parallel")),
    )(o_sb, o_pool, o_gm, w_sb, w_pool, w_gm, rest, rest, rest, dmerged)


def _relu2(acc):
    r = jnp.maximum(acc, 0.0)
    return r * r, 2.0 * r


def _relu2_bwd(acc, slope):
    return (acc * slope.astype(F32),)


TILES = {
    "proj_qkv": (2048, W_IN_TILE, 1024), "proj_rest": (2048, W_IN_TILE, 1024), "proj_gates": (2048, W_IN_TILE, 1024),
    "out_proj": (2048, 1024, 1024), "ff_in": (2048, 1024, 1024), "ff_out": (1024, 1024, 4096),
    "ff_out_dx": (2048, 1024, 1024), "ff_out_dw": (1024, 1024, 2048),
    "ff_in_dx": (1024, 1024, 4096), "ff_in_dw": (1024, 2048, 1024),
    "out_proj_dx": (2048, 1024, 1024), "out_proj_dw": (1024, 1024, 2048),
    "br_sb_dx": (2048, 512, 1024), "br_sb_dw": (512, 1024, 2048),
    "br_pool_dx": (2048, 256, 1024), "br_pool_dw": (256, 1024, 2048),
    "br_gm_dx": (2048, 256, 1024), "br_gm_dw": (256, 1024, 2048),
    "proj_dx": (512, 1024, 5376), "proj_dw": (1024, W_IN_TILE, 2048),
}


def _mm(name, a, b, mode, out_dtypes, rides=None, got=None, **kw):
    tm, tn, tk = TILES[name]
    ride = rides.get(name) if rides else None
    res = _matmul(a, b, mode=mode, name=name, tm=tm, tn=tn, tk=tk, out_dtypes=out_dtypes, exchange=ride, **kw)
    if ride is None:
        return res
    got.update(res[-1])
    return res[0] if len(res) == 2 else res[:-1]


def _layer_fwd(x, w, rides=None):
    rides = rides or {}
    got = {}
    h = _rms_fwd(x, w["g_mix_pre"], res=None, out_dtype=MX, name="rms_mix_pre")
    own = {}
    qkv = _mm("proj_qkv", h, w["w_in"], "nn", (MX,), rides, own, n_out=QKV_W)
    rest = _mm("proj_rest", h, w["w_in"], "nn", (F32,), b_col0=QKV_W, n_out=GATE_COL0)
    gates = _mm("proj_gates", h, w["w_in"], "nn", (MX,), rides, own, b_col0=QKV_W + GATE_COL0, n_out=REST_W - GATE_COL0)
    w = {**w, **own}
    o_sb, arrived = _attn_fwd(qkv, rides.get("attn"))
    got.update(arrived)
    pooled, o_pool = _pool_fwd(rest, w["wbd"], w["pool_scale"])
    o_gm = _gm_fwd(rest, w["gm_gain"], w["w_spatial"], w["bfull"])
    merged = _gate_fwd(o_sb, o_pool, o_gm, w["w_br_sb"], w["w_br_pool"], w["w_br_gm"], gates)
    y = _mm("out_proj", merged, w["w_out"], "nn", (F32,))
    x1, h2 = _rms_fwd(y, w["g_mix_post"], res=x, out_dtype=F32, name="rms_mix_post_ff_pre", then_gain=w["g_ff_pre"])
    r, r_slope = _mm("ff_in", h2, w["w_ff_in"], "nn", (MX, MX), rides, got, epilogue=_relu2)
    ff = _mm("ff_out", r, w["w_ff_out"], "nn", (F32,), rides, got)
    x2 = _rms_fwd(ff, w["g_ff_post"], res=x1, out_dtype=F32, name="rms_ff_post")
    saved = dict(x=x, h=h, qkv=qkv, rest=rest, gates=gates, o_sb=o_sb, pooled=pooled, o_pool=o_pool, o_gm=o_gm,
                 merged=merged, y=y, x1=x1, h2=h2, r=r, r_slope=r_slope, ff=ff)
    return x2, saved, got, own


def _layer_bwd(dx2, w, sv, rides=None, own_ff=None):
    rides = rides or {}
    got, own = {}, {}
    dff, dg_ff_post = _rms_bwd(sv["ff"], w["g_ff_post"], dx2, res=None, out_dtype=MX, name="rms_ff_post_bwd")
    da = _mm("ff_out_dx", dff, w["w_ff_out"], "nt", (MX,), rides, got, epilogue=_relu2_bwd, extras=(sv["r_slope"],))
    dw_ff_out = _mm("ff_out_dw", sv["r"], dff, "tn", (MX,))
    dh2 = _mm("ff_in_dx", da, w["w_ff_in"], "nt", (F32,), rides, got)
    dw_ff_in = _mm("ff_in_dw", sv["h2"], da, "tn", (MX,))
    dx1, dg_ff_pre = _rms_bwd(sv["x1"], w["g_ff_pre"], dh2, res=dx2, out_dtype=F32, name="rms_ff_pre_bwd")

    dy, dg_mix_post = _rms_bwd(sv["y"], w["g_mix_post"], dx1, res=None, out_dtype=MX, name="rms_mix_post_bwd")
    dmerged = _mm("out_proj_dx", dy, w["w_out"], "nt", (F32,))
    dw_out = _mm("out_proj_dw", sv["merged"], dy, "tn", (MX,))
    db_sb, db_pool, db_gm, dg0, dg1, dg2 = _gate_bwd(
        sv["o_sb"], sv["o_pool"], sv["o_gm"], w["w_br_sb"], w["w_br_pool"], w["w_br_gm"], sv["gates"], dmerged)
    do_sb = _mm("br_sb_dx", db_sb, w["w_br_sb"], "nt", (MX,))
    dw_br_sb = _mm("br_sb_dw", sv["o_sb"], db_sb, "tn", (MX,))
    do_pool = _mm("br_pool_dx", db_pool, w["w_br_pool"], "nt", (F32,))
    dw_br_pool = _mm("br_pool_dw", sv["o_pool"], db_pool, "tn", (MX,))
    do_gm = _mm("br_gm_dx", db_gm, w["w_br_gm"], "nt", (F32,))
    dw_br_gm = _mm("br_gm_dw", sv["o_gm"], db_gm, "tn", (MX,))

    du, dv_gm, dws, dbfull, dgain = _gm_bwd(do_gm, sv["rest"], w["gm_gain"], w["w_spatial"], w["bfull"])
    dp, dwbd, dscale = _pool_bwd(do_pool, sv["pooled"], w["wbd"], w["pool_scale"])
    dq, dk, dv, arrived = _attn_bwd(sv["qkv"], do_sb, rides.get("attn"))
    got.update(arrived)
    dproj = jnp.concatenate([dq, dk.astype(MX), dv.astype(MX), dp, du, dv_gm, dg0, dg1, dg2], axis=1)
    late = {} if own_ff is None else {"proj_dx": ([dw_ff_in], own_ff, True, ("w_ff_in",)),
                                      "proj_dw": ([dw_ff_out], own_ff, True, ("w_ff_out",))}
    dh = _mm("proj_dx", dproj, w["w_in"], "nt", (F32,), late, own)
    dw_in = _mm("proj_dw", sv["h"], dproj, "tn", (MX,), late, own)
    dx, dg_mix_pre = _rms_bwd(sv["x"], w["g_mix_pre"], dh, res=dx1, out_dtype=F32, name="rms_mix_pre_bwd")

    grads = dict(
        w_in=dw_in, w_br_sb=dw_br_sb, w_br_pool=dw_br_pool, w_br_gm=dw_br_gm, w_out=dw_out,
        w_ff_in=dw_ff_in, w_ff_out=dw_ff_out,
        w_pool=jnp.stack([dwbd[g * GROUP:(g + 1) * GROUP, g * GROUP:(g + 1) * GROUP] for g in range(N_GROUPS)]),
        pool_scale=dscale[0], gm_gain=dgain[0], w_spatial=dws,
        b_spatial=dbfull.reshape(CHUNK, N_GROUPS, GROUP).sum(axis=-1).T,
        g_mix_pre=dg_mix_pre[0], g_mix_post=dg_mix_post[0], g_ff_pre=dg_ff_pre[0], g_ff_post=dg_ff_post[0])
    return dx, grads, got, own


def _layer_operands(full_l, small_l):
    wbd = jnp.zeros((PW, PW), F32)
    for g in range(N_GROUPS):
        wbd = wbd.at[g * GROUP:(g + 1) * GROUP, g * GROUP:(g + 1) * GROUP].set(small_l["w_pool"][g])
    w = dict(full_l)
    w["wbd"] = wbd.astype(MX)
    w["bfull"] = jnp.repeat(small_l["b_spatial"].T, GROUP, axis=1)
    w["w_spatial"] = small_l["w_spatial"]
    for n in ("pool_scale", "gm_gain", "g_mix_pre", "g_mix_post", "g_ff_pre", "g_ff_post"):
        w[n] = small_l[n][None, :]
    return w


def _local_step(x, target, layers):
    saved = []
    for w in layers:
        x, sv, _, _ = _layer_fwd(x, w)
        saved.append(sv)
    dx, sq = _loss_head(x, target)
    loss = 0.5 * jnp.sum(sq) / x.shape[1]
    grads = [None] * len(layers)
    for l in reversed(range(len(layers))):
        dx, grads[l], _, _ = _layer_bwd(dx, layers[l], saved[l])
    return loss, dx, grads


def _mesh_place():
    x, y, c = lax.axis_index("x"), lax.axis_index("y"), lax.axis_index("c")

    def peer(k):
        px = 1 - x if k & 4 else x
        py = 1 - y if k & 2 else y
        pc = 1 - c if k & 1 else c
        return (px, py, pc), 4 * px + 2 * py + pc

    return 4 * x + 2 * y + c, peer


def _shard_window(ref, name, shard_shape, idx):
    if name in ("w_in", SMALL_GRADS):
        return ref.at[idx]
    a, b = shard_shape
    if BIG_SHARD_AXIS[name] == 2:
        return ref.at[:, pl.ds(pl.multiple_of(idx * b, b), b)]
    return ref.at[pl.ds(pl.multiple_of(idx * a, a), a), :]


def _full_shape(name, shard_shape):
    a, b = shard_shape
    if name == "w_in":
        return (N_DEV, a, b)
    return (a, N_DEV * b) if BIG_SHARD_AXIS[name] == 2 else (N_DEV * a, b)


def _exchange_out_shapes(shard_shapes, scatter, names):
    if scatter:
        return [(N_DEV,) + tuple(shard_shapes[nm]) for nm in names]
    return [_full_shape(nm, shard_shapes[nm]) for nm in names]


def _exchange_sems(n):
    return [pltpu.SemaphoreType.DMA((n, N_DEV - 1)), pltpu.SemaphoreType.DMA((n, N_DEV - 1)),
            pltpu.SemaphoreType.DMA((n,))]


def _exchange_plan(ins, outs, sems, shard_shapes, scatter, names):
    send_sems, recv_sems, local_sems = sems
    me, peer = _mesh_place()
    n = len(names)

    def window(ref, p, idx):
        return _shard_window(ref, names[p], shard_shapes[names[p]], idx)

    def copy(p, k, landing):
        dev, idx = peer(k)
        if names[p] == SMALL_GRADS:
            src, dst = ins[p], outs[p].at[idx if landing else me]
        elif scatter:
            src, dst = window(ins[p], p, idx), outs[p].at[idx if landing else me]
        else:
            src, dst = ins[p], window(outs[p], p, idx if landing else me)
        return pltpu.make_async_remote_copy(
            src_ref=src, dst_ref=dst, send_sem=send_sems.at[p, k - 1], recv_sem=recv_sems.at[p, k - 1],
            device_id=dev, device_id_type=pl.DeviceIdType.MESH)

    def local(p):
        if names[p] == SMALL_GRADS:
            return pltpu.make_async_copy(ins[p], outs[p].at[me], local_sems.at[p])
        if scatter:
            return pltpu.make_async_copy(window(ins[p], p, me), outs[p].at[me], local_sems.at[p])
        return pltpu.make_async_copy(ins[p], window(outs[p], p, me), local_sems.at[p])

    pairs = [(p, k) for p in range(n) for k in range(1, N_DEV)]

    def start():
        for p in range(n):
            local(p).start()
        for p, k in pairs:
            copy(p, k, False).start()

    def finish():
        for p, k in pairs:
            copy(p, k, True).wait_recv()
        for p, k in pairs:
            copy(p, k, False).wait_send()
        for p in range(n):
            local(p).wait()

    return start, finish


def _exchange_big(arrays, shard_shapes, *, scatter, name, names=BIG):
    n = len(names)

    def body(*refs):
        start, finish = _exchange_plan(refs[:n], refs[n:2 * n], refs[2 * n:], shard_shapes, scatter, names)
        start()
        finish()

    outs = pl.pallas_call(
        body, name=name,
        in_specs=[pl.BlockSpec(memory_space=pl.ANY)] * n,
        out_specs=[pl.BlockSpec(memory_space=pl.ANY)] * n,
        out_shape=[jax.ShapeDtypeStruct(s, a.dtype)
                   for s, a in zip(_exchange_out_shapes(shard_shapes, scatter, names), arrays)],
        scratch_shapes=_exchange_sems(n),
    )(*arrays)
    return dict(zip(names, outs))


def _sum_devices(recv, name):
    _, rows, cols = recv.shape
    tr = rows
    for cand in (512, 256, 128, 64, 32, 16, 8):
        if rows % cand == 0:
            tr = cand
            break

    def body(r_ref, out_ref):
        acc = r_ref[0].astype(F32)
        for d in range(1, N_DEV):
            acc = acc + r_ref[d].astype(F32)
        out_ref[...] = acc

    return pl.pallas_call(
        body, name=name, grid=(rows // tr,),
        in_specs=[pl.BlockSpec((N_DEV, tr, cols), lambda i: (0, i, 0))],
        out_specs=pl.BlockSpec((tr, cols), lambda i: (i, 0)),
        out_shape=jax.ShapeDtypeStruct((rows, cols), F32),
        compiler_params=_params(("parallel",)),
    )(recv)


def _adamw(w, g, m, v, name):
    rows, cols = w.shape
    tr = rows
    for cand in (512, 256, 128, 64, 32, 16, 8):
        if rows % cand == 0:
            tr = cand
            break
    c1 = 1.0 - ADAM_B1 ** ADAM_STEP
    c2 = 1.0 - ADAM_B2 ** ADAM_STEP

    def body(w_ref, g_ref, m_ref, v_ref, d_ref, nm_ref, nv_ref):
        gv = g_ref[...]
        nm = ADAM_B1 * m_ref[...] + (1.0 - ADAM_B1) * gv
        nv = ADAM_B2 * v_ref[...] + (1.0 - ADAM_B2) * (gv * gv)
        nm_ref[...] = nm
        nv_ref[...] = nv
        d_ref[...] = -ADAM_LR * ((nm / c1) / (jnp.sqrt(nv / c2) + ADAM_EPS) + ADAM_WD * w_ref[...])

    spec = pl.BlockSpec((tr, cols), lambda i: (i, 0))
    return pl.pallas_call(
        body, name=name, grid=(rows // tr,),
        in_specs=[spec] * 4, out_specs=[spec] * 3,
        out_shape=[jax.ShapeDtypeStruct((rows, cols), F32)] * 3,
        compiler_params=_params(("parallel",)),
    )(w, g, m, v)


def _pack_rows(flat, dtype):
    n = flat.shape[-1]
    rows = -(-n // PACK_COLS)
    rows = -(-rows // 16) * 16
    pad = rows * PACK_COLS - n
    flat = jnp.pad(flat, [(0, 0)] * (flat.ndim - 1) + [(0, pad)])
    return flat.reshape(flat.shape[:-1] + (rows, PACK_COLS)).astype(dtype)


def kernel(x, w_in, w_pool, pool_scale, gm_gain, w_spatial, b_spatial, w_br_sb, w_br_pool, w_br_gm, w_out, g_mix_pre, g_mix_post, g_ff_pre, g_ff_post, w_ff_in, w_ff_out, loss_target, m_w_in, m_w_pool, m_pool_scale, m_gm_gain, m_w_spatial, m_b_spatial, m_w_br_sb, m_w_br_pool, m_w_br_gm, m_w_out, m_g_mix_pre, m_g_mix_post, m_g_ff_pre, m_g_ff_post, m_w_ff_in, m_w_ff_out, v_w_in, v_w_pool, v_pool_scale, v_gm_gain, v_w_spatial, v_b_spatial, v_w_br_sb, v_w_br_pool, v_w_br_gm, v_w_out, v_g_mix_pre, v_g_mix_post, v_g_ff_pre, v_g_ff_post, v_w_ff_in, v_w_ff_out):
    args = dict(locals())
    weights = {n: args[n] for n in WEIGHTS}
    mom_m = {n: args["m_" + n] for n in WEIGHTS}
    mom_v = {n: args["v_" + n] for n in WEIGHTS}

    shard_shapes = {n: weights[n].shape[1:] for n in BIG}
    depth, d_model, w_in_cols = weights["w_in"].shape
    with_attn = ("w_in", "w_br_sb", "w_br_pool", "w_br_gm", "w_out")

    def gather_of(l, names):
        return [weights[n][l].astype(MX) for n in names], shard_shapes, False, names

    def as_operands(full, l):
        full = dict(full)
        full["w_in"] = jnp.transpose(full["w_in"], (1, 0, 2)).reshape(d_model, N_DEV * w_in_cols)
        return _layer_operands(full, {n: weights[n][l] for n in SMALL})

    full = _exchange_big(gather_of(0, with_attn)[0], shard_shapes, scatter=False, name="gather_weights",
                         names=with_attn)
    xc = x[0]
    layers, saved = [], []
    for l in range(depth):
        layers.append(as_operands(full, l))
        rides = {}
        if l == 0:
            rides.update({"proj_qkv": gather_of(0, ("w_ff_out",)), "proj_gates": gather_of(0, ("w_ff_in",))})
        if l + 1 < depth:
            rides.update({"attn": gather_of(l + 1, with_attn), "ff_in": gather_of(l + 1, ("w_ff_in",)),
                          "ff_out": gather_of(l + 1, ("w_ff_out",))})
        xc, sv, full, own = _layer_fwd(xc, layers[l], rides)
        layers[l].update(own)
        saved.append(sv)
    dx, sq = _loss_head(xc, loss_target[0])
    loss = lax.psum(0.5 * jnp.sum(sq) / d_model, ("x", "y", "c"))

    def scatter_of(g, names):
        arrays = [jnp.transpose(g[n].reshape(d_model, N_DEV, w_in_cols), (1, 0, 2)) if n == "w_in" else g[n]
                  for n in names]
        return arrays, shard_shapes, True, names

    layer_grads, recv = [None] * depth, [None] * depth
    for l in reversed(range(depth)):
        rides = None
        if l + 1 < depth:
            above = layer_grads[l + 1]
            rides = {"ff_out_dx": scatter_of(above, ("w_ff_in",)), "ff_in_dx": scatter_of(above, ("w_ff_out",)),
                     "attn": scatter_of(above, with_attn)}
        dx, layer_grads[l], arrived, own = _layer_bwd(dx, layers[l], saved[l], rides, shard_shapes if l == 0 else None)
        if rides is not None:
            recv[l + 1] = arrived
    small_sizes = [int(np.prod(weights[n].shape)) for n in SMALL]
    small_off = np.concatenate([[0], np.cumsum(small_sizes)])
    small_flat = jnp.concatenate([jnp.stack([g[n] for g in layer_grads]).reshape(-1) for n in SMALL])
    small_packed = _pack_rows(small_flat, MX)
    last = _exchange_big(scatter_of(layer_grads[0], with_attn)[0] + [small_packed],
                         {**shard_shapes, SMALL_GRADS: small_packed.shape}, scatter=True,
                         name="scatter_grads", names=with_attn + (SMALL_GRADS,))
    g_small = _sum_devices(last.pop(SMALL_GRADS), "sum_small_grads").reshape(-1)
    recv[0] = {**own, **last}
    grad_w = {n: jnp.stack([_sum_devices(recv[l][n], "sum_grads_" + n) for l in range(depth)]) for n in BIG}
    for n, lo, hi in zip(SMALL, small_off[:-1], small_off[1:]):
        grad_w[n] = g_small[lo:hi].reshape(weights[n].shape)

    delta, new_m, new_v = {}, {}, {}
    for n in WEIGHTS:
        shape = weights[n].shape
        view = (-1, shape[-1])
        d, nm, nv = _adamw(weights[n].reshape(view), grad_w[n].reshape(view), mom_m[n].reshape(view),
                           mom_v[n].reshape(view), "adamw_" + n)
        delta[n], new_m[n], new_v[n] = d.reshape(shape), nm.reshape(shape), nv.reshape(shape)

    return (loss, dx[None], *[grad_w[n] for n in WEIGHTS], *[delta[n] for n in WEIGHTS],
            *[new_m[n] for n in WEIGHTS], *[new_v[n] for n in WEIGHTS])
```
